```python
import jax
import jax.numpy as jnp
from jax import lax
import numpy as np

D_MODEL = 1024
BATCH = 32
SEQ = 2048
DEPTH = 2

N_META = 16
CHUNK = 128
PAD = CHUNK - N_META
EPS = 1e-6

RET_HEADS = 4
RET_QK_DIM = D_MODEL // RET_HEADS
RET_V_DIM = 2 * D_MODEL // RET_HEADS
RET_QK = RET_HEADS * RET_QK_DIM
RET_V = RET_HEADS * RET_V_DIM
RET_IN = 2 * RET_QK + 2 * RET_V
RET_ROPE_BASE = 10000.0

SSM_INNER = 2 * D_MODEL
SSM_HEAD_DIM = 64
SSM_HEADS = SSM_INNER // SSM_HEAD_DIM
SSM_GROUPS = 8
SSM_HPG = SSM_HEADS // SSM_GROUPS
SSM_STATE = 128
SSM_CONV = 4
SSM_BC = SSM_GROUPS * SSM_STATE
SSM_CONV_DIM = SSM_INNER + 2 * SSM_BC
SSM_IN = SSM_INNER + SSM_CONV_DIM + SSM_HEADS

MOE_GROUPS = 4
MOE_EXPERTS_PER_GROUP = 8
MOE_EXPERTS = MOE_GROUPS * MOE_EXPERTS_PER_GROUP
MOE_TOP_K = 2
MOE_FF = D_MODEL // 2
MOE_BLOCK = 256

N_RET_LAYERS = (DEPTH + 1) // 2
N_SSM_LAYERS = DEPTH // 2

kernel_name = 'hybrid_retention_ssd_hmoe'


def rmsnorm(x, g):
    xf = x.astype(jnp.float32)
    y = xf * lax.rsqrt(jnp.mean(xf * xf, axis=-1, keepdims=True) + EPS)
    return (y * g.astype(jnp.float32)).astype(x.dtype)


def pad_front(a):
    return jnp.pad(a, [(0, 0), (PAD, 0)] + [(0, 0)] * (a.ndim - 2))


def to_chunks(a):
    b, l = a.shape[0], a.shape[1]
    return a.reshape((b, l // CHUNK, CHUNK) + a.shape[2:]).swapaxes(0, 1)


def from_chunks(a):
    nc, b, c = a.shape[0], a.shape[1], a.shape[2]
    return a.swapaxes(0, 1).reshape((b, nc * c) + a.shape[3:])


def rotary(x, pos):
    half = x.shape[-1] // 2
    inv_freq = 1.0 / (RET_ROPE_BASE ** jnp.linspace(0.0, 1.0, half, dtype=jnp.float32))
    ang = pos.astype(jnp.float32)[:, None] * inv_freq[None, :]
    cos = jnp.cos(ang)[None, :, None, :]
    sin = jnp.sin(ang)[None, :, None, :]
    xf = x.astype(jnp.float32).reshape(x.shape[:-1] + (half, 2))
    xe, xo = xf[..., 0], xf[..., 1]
    out = jnp.stack([xe * cos - xo * sin, xo * cos + xe * sin], axis=-1)
    return out.reshape(x.shape).astype(x.dtype)


def retention(u, w_in, w_out, head_norm):
    b, l, _ = u.shape
    q, k, v, g = jnp.split(u @ w_in, [RET_QK, 2 * RET_QK, 2 * RET_QK + RET_V], axis=-1)
    pos = jnp.arange(l)
    q = rotary(q.reshape(b, l, RET_HEADS, RET_QK_DIM), pos)
    k = rotary(k.reshape(b, l, RET_HEADS, RET_QK_DIM), pos) * (RET_QK_DIM ** -0.5)
    v = v.reshape(b, l, RET_HEADS, RET_V_DIM)
    log_gamma = jnp.log(1.0 - 2.0 ** (-5.0 - jnp.arange(RET_HEADS, dtype=jnp.float32)))
    idx = jnp.arange(CHUNK, dtype=jnp.float32)
    diff = idx[:, None] - idx[None, :]
    intra = jnp.where(diff >= 0, jnp.exp(log_gamma[:, None, None] * jnp.maximum(diff, 0.0)), 0.0)
    q_decay = jnp.exp(log_gamma[None, :] * (idx[:, None] + 1.0))[None, :, :, None]
    k_decay = jnp.exp(log_gamma[None, :] * (CHUNK - 1.0 - idx[:, None]))[None, :, :, None]
    chunk_decay = jnp.exp(log_gamma * CHUNK)[None, :, None, None]

    def step(state, xs):
        qc, kc, vc = xs[0].astype(jnp.float32), xs[1].astype(jnp.float32), xs[2].astype(jnp.float32)
        scores = jnp.einsum('bnhk,bmhk->bhnm', qc, kc) * intra
        y = jnp.einsum('bhnm,bmhv->bnhv', scores, vc)
        y = y + jnp.einsum('bnhk,bhkv->bnhv', qc, state) * q_decay
        state = state * chunk_decay + jnp.einsum('bmhk,bmhv->bhkv', kc * k_decay, vc)
        return state, y.astype(xs[2].dtype)

    state0 = jnp.zeros((b, RET_HEADS, RET_QK_DIM, RET_V_DIM), jnp.float32)
    _, y = lax.scan(step, state0, (to_chunks(pad_front(q)), to_chunks(pad_front(k)), to_chunks(pad_front(v))))
    y = from_chunks(y)[:, PAD:]
    y = rmsnorm(y, head_norm.reshape(RET_HEADS, RET_V_DIM))
    y = jax.nn.silu(g) * y.reshape(b, l, RET_V)
    return y @ w_out


def causal_conv(x, w, bias):
    out = lax.conv_general_dilated(x, w[:, None, :], window_strides=(1,), padding=[(SSM_CONV - 1, 0)],
                                   dimension_numbers=('NWC', 'WIO', 'NWC'), feature_group_count=x.shape[-1])
    return out + bias


def mamba2(u, w_in, conv_w, conv_b, dt_bias, a_log, d_skip, gate_norm, w_out):
    b, l, _ = u.shape
    z, xbc, dt = jnp.split(u @ w_in, [SSM_INNER, SSM_INNER + SSM_CONV_DIM], axis=-1)
    xbc = jax.nn.silu(causal_conv(xbc, conv_w, conv_b))
    xs, bm, cm = jnp.split(xbc, [SSM_INNER, SSM_INNER + SSM_BC], axis=-1)
    xs = xs.reshape(b, l, SSM_GROUPS, SSM_HPG, SSM_HEAD_DIM)
    bm = bm.reshape(b, l, SSM_GROUPS, SSM_STATE).astype(jnp.float32)
    cm = cm.reshape(b, l, SSM_GROUPS, SSM_STATE).astype(jnp.float32)
    dt = jax.nn.softplus(dt.astype(jnp.float32) + dt_bias.astype(jnp.float32))
    a = -jnp.exp(a_log.astype(jnp.float32))
    da = (dt * a).reshape(b, l, SSM_GROUPS, SSM_HPG)
    xdt = xs.astype(jnp.float32) * dt.reshape(b, l, SSM_GROUPS, SSM_HPG)[..., None]
    causal = jnp.tril(jnp.ones((CHUNK, CHUNK), bool))[None, :, :, None, None]

    def step(state, inp):
        xc, bc, cc, ac = inp
        acs = jnp.cumsum(ac, axis=1)
        seg = acs[:, :, None] - acs[:, None, :]
        decay = jnp.exp(jnp.where(causal, seg, -jnp.inf))
        cb = jnp.einsum('blgn,bsgn->blsg', cc, bc)
        y = jnp.einsum('blsgh,bsghp->blghp', cb[..., None] * decay, xc)
        y = y + jnp.einsum('blgn,bghpn->blghp', cc, state) * jnp.exp(acs)[..., None]
        last = acs[:, -1]
        to_end = jnp.exp(last[:, None] - acs)[..., None]
        state = state * jnp.exp(last)[..., None, None] + jnp.einsum('bsgn,bsghp->bghpn', bc, xc * to_end)
        return state, y

    state0 = jnp.zeros((b, SSM_GROUPS, SSM_HPG, SSM_HEAD_DIM, SSM_STATE), jnp.float32)
    _, y = lax.scan(step, state0, (to_chunks(pad_front(xdt)), to_chunks(pad_front(bm)),
                                   to_chunks(pad_front(cm)), to_chunks(pad_front(da))))
    y = from_chunks(y)[:, PAD:]
    y = y + d_skip.astype(jnp.float32).reshape(SSM_GROUPS, SSM_HPG)[..., None] * xs.astype(jnp.float32)
    y = y.reshape(b, l, SSM_INNER) * jax.nn.silu(z.astype(jnp.float32))
    y = rmsnorm(y.reshape(b, l, SSM_GROUPS, SSM_INNER // SSM_GROUPS),
                gate_norm.reshape(SSM_GROUPS, SSM_INNER // SSM_GROUPS))
    return y.reshape(b, l, SSM_INNER).astype(u.dtype) @ w_out


def hier_moe(hn, wg, bg, we, be, w1, w3, w2):
    t, d = hn.shape
    g_logits = (hn @ wg).astype(jnp.float32) + bg.astype(jnp.float32)
    g_sel = jnp.argmax(g_logits, axis=-1)
    p_group = jnp.take_along_axis(jax.nn.softmax(g_logits, axis=-1), g_sel[:, None], axis=1)
    e_logits = jnp.einsum('td,dge->tge', hn, we).astype(jnp.float32) + be.astype(jnp.float32)
    e_logits = jnp.take_along_axis(e_logits, g_sel[:, None, None], axis=1)[:, 0]
    top_p, top_i = lax.top_k(jax.nn.softmax(e_logits, axis=-1), MOE_TOP_K)
    gate = (p_group * top_p / jnp.sum(top_p, axis=-1, keepdims=True)).reshape(-1)
    eid = (g_sel[:, None] * MOE_EXPERTS_PER_GROUP + top_i).reshape(-1).astype(jnp.int32)
    tok = jnp.repeat(jnp.arange(t, dtype=jnp.int32), MOE_TOP_K)
    n_assign = t * MOE_TOP_K
    n_blocks = -(-n_assign // MOE_BLOCK) + MOE_EXPERTS
    order = jnp.argsort(eid)
    se = eid[order]
    counts = jnp.bincount(eid, length=MOE_EXPERTS).astype(jnp.int32)
    padded = (counts + MOE_BLOCK - 1) // MOE_BLOCK * MOE_BLOCK
    start = jnp.cumsum(counts) - counts
    pend = jnp.cumsum(padded)
    pstart = pend - padded
    dest = pstart[se] + jnp.arange(n_assign, dtype=jnp.int32) - start[se]
    buf_tok = jnp.full((n_blocks * MOE_BLOCK,), t, jnp.int32).at[dest].set(tok[order])
    buf_gate = jnp.zeros((n_blocks * MOE_BLOCK,), jnp.float32).at[dest].set(gate[order])
    block_eid = jnp.minimum(jnp.searchsorted(pend, jnp.arange(n_blocks, dtype=jnp.int32) * MOE_BLOCK, side='right'),
                            MOE_EXPERTS - 1)
    h_pad = jnp.concatenate([hn, jnp.zeros((1, d), hn.dtype)], axis=0)

    def expert_block(args):
        tok_b, e, gate_b = args
        xb = h_pad[tok_b]
        hid = jax.nn.silu(xb @ w1[e]) * (xb @ w3[e])
        return ((hid @ w2[e]).astype(jnp.float32) * gate_b[:, None]).astype(hn.dtype)

    y = lax.map(expert_block, (buf_tok.reshape(n_blocks, MOE_BLOCK), block_eid,
                               buf_gate.reshape(n_blocks, MOE_BLOCK)))
    out = jnp.zeros_like(h_pad).at[buf_tok].add(y.reshape(-1, d))
    return out[:t]


def setup_inputs(seed: int = 0) -> dict:
    key = jax.random.key(seed)
    ks = jax.random.split(key, 26)
    f32 = jnp.float32
    nrm = lambda k, s, scale: jax.random.normal(k, s, f32) * scale
    u_dt = jax.random.uniform(ks[12], (N_SSM_LAYERS, SSM_HEADS), f32)
    dt0 = jnp.exp(u_dt * (np.log(0.1) - np.log(0.001)) + np.log(0.001))
    return {
        'x': nrm(ks[0], (BATCH, SEQ, D_MODEL), 1.0),
        'meta': nrm(ks[1], (N_META, D_MODEL), 1.0),
        'norm_mix': 1.0 + nrm(ks[2], (DEPTH, D_MODEL), 0.02),
        'norm_ffn': 1.0 + nrm(ks[3], (DEPTH, D_MODEL), 0.02),
        'norm_final': 1.0 + nrm(ks[4], (D_MODEL,), 0.02),
        'ret_w_in': nrm(ks[5], (N_RET_LAYERS, D_MODEL, RET_IN), D_MODEL ** -0.5),
        'ret_w_out': nrm(ks[6], (N_RET_LAYERS, RET_V, D_MODEL), RET_V ** -0.5),
        'ret_norm': 1.0 + nrm(ks[7], (N_RET_LAYERS, RET_V), 0.02),
        'ssm_w_in': nrm(ks[8], (N_SSM_LAYERS, D_MODEL, SSM_IN), D_MODEL ** -0.5),
        'ssm_conv_w': nrm(ks[9], (N_SSM_LAYERS, SSM_CONV, SSM_CONV_DIM), SSM_CONV ** -0.5),
        'ssm_conv_b': nrm(ks[10], (N_SSM_LAYERS, SSM_CONV_DIM), 0.02),
        'ssm_dt_bias': dt0 + jnp.log(-jnp.expm1(-dt0)),
        'ssm_a_log': jnp.log(jax.random.uniform(ks[13], (N_SSM_LAYERS, SSM_HEADS), f32, 1.0, 16.0)),
        'ssm_d': 1.0 + nrm(ks[14], (N_SSM_LAYERS, SSM_HEADS), 0.02),
        'ssm_norm': 1.0 + nrm(ks[15], (N_SSM_LAYERS, SSM_INNER), 0.02),
        'ssm_w_out': nrm(ks[16], (N_SSM_LAYERS, SSM_INNER, D_MODEL), SSM_INNER ** -0.5),
        'moe_wg': nrm(ks[17], (DEPTH, D_MODEL, MOE_GROUPS), D_MODEL ** -0.5),
        'moe_bg': nrm(ks[18], (DEPTH, MOE_GROUPS), 0.01),
        'moe_we': nrm(ks[19], (DEPTH, D_MODEL, MOE_GROUPS, MOE_EXPERTS_PER_GROUP), D_MODEL ** -0.5),
        'moe_be': nrm(ks[20], (DEPTH, MOE_GROUPS, MOE_EXPERTS_PER_GROUP), 0.01),
        'moe_w1': nrm(ks[21], (DEPTH, MOE_EXPERTS, D_MODEL, MOE_FF), D_MODEL ** -0.5),
        'moe_w3': nrm(ks[22], (DEPTH, MOE_EXPERTS, D_MODEL, MOE_FF), D_MODEL ** -0.5),
        'moe_w2': nrm(ks[23], (DEPTH, MOE_EXPERTS, MOE_FF, D_MODEL), MOE_FF ** -0.5),
    }


def reference(x, meta, norm_mix, norm_ffn, norm_final, ret_w_in, ret_w_out, ret_norm,
              ssm_w_in, ssm_conv_w, ssm_conv_b, ssm_dt_bias, ssm_a_log, ssm_d, ssm_norm, ssm_w_out,
              moe_wg, moe_bg, moe_we, moe_be, moe_w1, moe_w3, moe_w2):
    b = x.shape[0]
    h = jnp.concatenate([jnp.broadcast_to(meta[None].astype(x.dtype), (b, N_META, D_MODEL)), x], axis=1)
    for i in range(DEPTH):
        j = i // 2
        u = rmsnorm(h, norm_mix[i])
        if i % 2 == 0:
            h = h + retention(u, ret_w_in[j], ret_w_out[j], ret_norm[j])
        else:
            h = h + mamba2(u, ssm_w_in[j], ssm_conv_w[j], ssm_conv_b[j], ssm_dt_bias[j], ssm_a_log[j],
                           ssm_d[j], ssm_norm[j], ssm_w_out[j])
        u = rmsnorm(h, norm_ffn[i])
        y = hier_moe(u.reshape(-1, D_MODEL), moe_wg[i], moe_bg[i], moe_we[i], moe_be[i],
                     moe_w1[i], moe_w3[i], moe_w2[i])
        h = h + y.reshape(h.shape)
    return rmsnorm(h, norm_final)[:, N_META:]
```

```python
import functools
import math

import jax
import jax.numpy as jnp
from jax import lax
from jax.experimental import pallas as pl
from jax.experimental.pallas import tpu as pltpu

F32 = jnp.float32
BF16 = jnp.bfloat16
I32 = jnp.int32

D_MODEL = 1024
N_META = 16
CHUNK = 128
PAD = CHUNK - N_META
EPS = 1e-6

RET_HEADS = 4
RET_QK_DIM = 256
RET_V_DIM = 512
RET_QK = RET_HEADS * RET_QK_DIM
RET_V = RET_HEADS * RET_V_DIM
RET_IN = 2 * RET_QK + 2 * RET_V
RET_ROPE_BASE = 10000.0

SSM_INNER = 2048
SSM_HEAD_DIM = 64
SSM_HEADS = 32
SSM_GROUPS = 8
SSM_HPG = 4
SSM_STATE = 128
SSM_CONV = 4
SSM_BC = SSM_GROUPS * SSM_STATE
SSM_CONV_DIM = SSM_INNER + 2 * SSM_BC
SSM_MAIN = SSM_INNER + SSM_CONV_DIM
LANES = 128
SSM_GROUP_W = SSM_HPG * SSM_HEAD_DIM

MOE_GROUPS = 4
MOE_EPG = 8
MOE_EXPERTS = 32
MOE_FF = 512
MOE_BLOCK = 256
ROUTER_ROWS = 40

TOKEN_TILE = 512
VMEM_LIMIT = 56 * 1024 * 1024


def _cparams(*sem):
    return pltpu.CompilerParams(dimension_semantics=sem, vmem_limit_bytes=VMEM_LIMIT)


def _rms(x, gain):
    ms = jnp.mean(x * x, axis=-1, keepdims=True)
    return x * lax.rsqrt(ms + EPS) * gain


def _silu(x):
    return x * (1.0 / (1.0 + jnp.exp(-x)))


def _dot(a, b):
    return jnp.dot(a, b, preferred_element_type=F32)


def _dot_nt(a, b, precision=None):
    return lax.dot_general(a, b, (((1,), (1,)), ((), ())), preferred_element_type=F32, precision=precision)


def _dot_tn(a, b):
    return lax.dot_general(a, b, (((0,), (0,)), ((), ())), preferred_element_type=F32)


def _row_tile(rows, cap=544):
    best = 16
    for t in range(16, cap + 1, 16):
        if rows % t == 0:
            best = t
    return best


def _ret_in_kernel(h_ref, g_ref, cos_ref, sin_ref, w_ref, o_ref):
    xn = _rms(h_ref[0], g_ref[...]).astype(BF16)
    cos = cos_ref[...]
    sin = sin_ref[...]
    half = RET_QK_DIM // 2
    for j in range(2 * RET_HEADS):
        c0 = j * RET_QK_DIM
        acc = _dot(xn, w_ref[:, c0:c0 + RET_QK_DIM])
        if j >= RET_HEADS:
            acc = acc * (RET_QK_DIM ** -0.5)
        e = acc[:, :half]
        o = acc[:, half:]
        o_ref[0, :, c0:c0 + half] = (e * cos - o * sin).astype(BF16)
        o_ref[0, :, c0 + half:c0 + RET_QK_DIM] = (o * cos + e * sin).astype(BF16)
    step = 1024
    for c0 in range(2 * RET_QK, RET_IN, step):
        o_ref[0, :, c0:c0 + step] = _dot(xn, w_ref[:, c0:c0 + step]).astype(BF16)


def _ret_in(h, gain, cos, sin, w):
    b, lp, d = h.shape
    tm = _row_tile(lp)
    return pl.pallas_call(
        _ret_in_kernel,
        grid=(b, lp // tm),
        in_specs=[
            pl.BlockSpec((1, tm, d), lambda i, j: (i, j, 0)),
            pl.BlockSpec((1, d), lambda i, j: (0, 0)),
            pl.BlockSpec((tm, RET_QK_DIM // 2), lambda i, j: (j, 0)),
            pl.BlockSpec((tm, RET_QK_DIM // 2), lambda i, j: (j, 0)),
            pl.BlockSpec((d, RET_IN), lambda i, j: (0, 0)),
        ],
        out_specs=pl.BlockSpec((1, tm, RET_IN), lambda i, j: (i, j, 0)),
        out_shape=jax.ShapeDtypeStruct((b, lp, RET_IN), BF16),
        compiler_params=_cparams("parallel", "parallel"),
        name="ret_in_proj",
    )(h, gain, cos, sin, w)


def _ret_kernel(q_ref, k_ref, v_ref, g_ref, hn_ref, o_ref, state_ref):
    @pl.when(pl.program_id(1) == 0)
    def _():
        state_ref[...] = jnp.zeros_like(state_ref)

    row = lax.broadcasted_iota(I32, (CHUNK, CHUNK), 0).astype(F32)
    col = lax.broadcasted_iota(I32, (CHUNK, CHUNK), 1).astype(F32)
    diff = row - col
    ridx = row[:, :1]
    for hh in range(RET_HEADS):
        lg = math.log(1.0 - 2.0 ** (-5.0 - hh))
        intra = jnp.where(diff >= 0, jnp.exp(lg * jnp.maximum(diff, 0.0)), 0.0)
        qdec = jnp.exp(lg * (ridx + 1.0))
        kdec = jnp.exp(lg * (CHUNK - 1.0 - ridx))
        cdec = math.exp(lg * CHUNK)
        qh = q_ref[0, :, hh * RET_QK_DIM:(hh + 1) * RET_QK_DIM]
        kh = k_ref[0, :, hh * RET_QK_DIM:(hh + 1) * RET_QK_DIM]
        vh = v_ref[0, :, hh * RET_V_DIM:(hh + 1) * RET_V_DIM]
        scores = _dot_nt(qh, kh) * intra
        st = state_ref[hh]
        y = _dot(scores.astype(BF16), vh) + _dot(qh, st.astype(BF16)) * qdec
        kd = (kh.astype(F32) * kdec).astype(BF16)
        state_ref[hh] = st * cdec + _dot_tn(kd, vh)
        yn = _rms(y, hn_ref[:, hh * RET_V_DIM:(hh + 1) * RET_V_DIM])
        gh = g_ref[0, :, hh * RET_V_DIM:(hh + 1) * RET_V_DIM].astype(F32)
        o_ref[0, :, hh * RET_V_DIM:(hh + 1) * RET_V_DIM] = (_silu(gh) * yn).astype(BF16)


def _retention(qkvg, head_norm):
    b, lp, _ = qkvg.shape
    nc = lp // CHUNK
    return pl.pallas_call(
        _ret_kernel,
        grid=(b, nc),
        in_specs=[
            pl.BlockSpec((1, CHUNK, RET_QK), lambda i, c: (i, c, 0)),
            pl.BlockSpec((1, CHUNK, RET_QK), lambda i, c: (i, c, 1)),
            pl.BlockSpec((1, CHUNK, RET_V), lambda i, c: (i, c, 1)),
            pl.BlockSpec((1, CHUNK, RET_V), lambda i, c: (i, c, 2)),
            pl.BlockSpec((1, RET_V), lambda i, c: (0, 0)),
        ],
        out_specs=pl.BlockSpec((1, CHUNK, RET_V), lambda i, c: (i, c, 0)),
        out_shape=jax.ShapeDtypeStruct((b, lp, RET_V), BF16),
        scratch_shapes=[pltpu.VMEM((RET_HEADS, RET_QK_DIM, RET_V_DIM), F32)],
        compiler_params=_cparams("parallel", "arbitrary"),
        name="retention",
    )(qkvg, qkvg, qkvg, qkvg, head_norm)


def _ssm_in_kernel(h_ref, g_ref, w_ref, o_ref, dt_ref):
    xn = _rms(h_ref[...], g_ref[...]).astype(BF16)
    step = 1024
    for c0 in range(0, SSM_MAIN, step):
        o_ref[:, c0:c0 + step] = _dot(xn, w_ref[:, c0:c0 + step]).astype(BF16)
    dt_ref[...] = _dot(xn, w_ref[:, SSM_MAIN:SSM_MAIN + LANES])


def _ssm_in(h, gain, w):
    t, d = h.shape
    tm = TOKEN_TILE
    return pl.pallas_call(
        _ssm_in_kernel,
        grid=(t // tm,),
        in_specs=[
            pl.BlockSpec((tm, d), lambda i: (i, 0)),
            pl.BlockSpec((1, d), lambda i: (0, 0)),
            pl.BlockSpec((d, SSM_MAIN + LANES), lambda i: (0, 0)),
        ],
        out_specs=[
            pl.BlockSpec((tm, SSM_MAIN), lambda i: (i, 0)),
            pl.BlockSpec((tm, LANES), lambda i: (i, 0)),
        ],
        out_shape=[jax.ShapeDtypeStruct((t, SSM_MAIN), BF16), jax.ShapeDtypeStruct((t, LANES), F32)],
        compiler_params=_cparams("parallel"),
        name="ssm_in_proj",
    )(h, gain, w)


def _ssd_kernel(z_ref, x_ref, bc_ref, dt_ref, cw_ref, cb_ref, dtb_ref, alog_ref, dsk_ref, gn_ref,
                o_ref, state_ref, tail_ref):
    c = pl.program_id(1)

    @pl.when(c == 0)
    def _():
        state_ref[...] = jnp.zeros_like(state_ref)
        tail_ref[...] = jnp.zeros_like(tail_ref)

    row = lax.broadcasted_iota(I32, (CHUNK, CHUNK), 0)
    col = lax.broadcasted_iota(I32, (CHUNK, CHUNK), 1)
    causal = row >= col
    low_half = col < SSM_HEAD_DIM
    valid = row[:, :1] >= jnp.where(c > 0, 0, PAD)

    pre = jnp.concatenate([x_ref[0], bc_ref[0]], axis=1).astype(F32)
    pre = jnp.where(valid, pre, 0.0)
    ext = jnp.concatenate([tail_ref[...], pre], axis=0)
    tail_ref[...] = pre[CHUNK - 8:, :]
    conv = cb_ref[...] + cw_ref[0:1, :] * ext[5:5 + CHUNK]
    for j in range(1, SSM_CONV):
        conv = conv + cw_ref[j:j + 1, :] * ext[5 + j:5 + j + CHUNK]
    act = jnp.where(valid, _silu(conv), 0.0)
    xs = act[:, :SSM_INNER]
    bm = act[:, SSM_INNER:SSM_INNER + SSM_BC]
    cm = act[:, SSM_INNER + SSM_BC:]

    dtr = dt_ref[0] + dtb_ref[...]
    dtv = jnp.maximum(dtr, 0.0) + jnp.log(1.0 + jnp.exp(-jnp.abs(dtr)))
    dtv = jnp.where(valid, dtv, 0.0)
    da = dtv * (-jnp.exp(alog_ref[...]))
    tril = jnp.where(causal, 1.0, 0.0)
    acs = jnp.dot(tril, da, preferred_element_type=F32, precision=lax.Precision.HIGHEST)
    last = acs[CHUNK - 1:CHUNK, :]
    eacs = jnp.exp(acs)
    wcol = dtv * jnp.exp(last - acs)
    elast = jnp.broadcast_to(jnp.exp(last), (CHUNK, LANES))
    acs_t = acs.T
    dt_t = dtv.T

    for gi in range(SSM_GROUPS):
        bg = bm[:, gi * SSM_STATE:(gi + 1) * SSM_STATE]
        cg = cm[:, gi * SSM_STATE:(gi + 1) * SSM_STATE].astype(BF16)
        cb = _dot_nt(cg, bg.astype(BF16))
        st = state_ref[gi]
        yoff = _dot(cg, st.astype(BF16))
        y_pairs = []
        st_pairs = []
        for pp in range(SSM_HPG // 2):
            h0 = gi * SSM_HPG + 2 * pp
            xp = xs[:, h0 * SSM_HEAD_DIM:(h0 + 2) * SSM_HEAD_DIM]
            x_sel = (jnp.where(low_half, xp, 0.0).astype(BF16), jnp.where(low_half, 0.0, xp).astype(BF16))
            y = jnp.zeros((CHUNK, 2 * SSM_HEAD_DIM), F32)
            upd = jnp.zeros((SSM_STATE, 2 * SSM_HEAD_DIM), F32)
            for k in range(2):
                hd = h0 + k
                seg = acs[:, hd:hd + 1] - acs_t[hd:hd + 1, :]
                m = cb * jnp.where(causal, jnp.exp(seg), 0.0) * dt_t[hd:hd + 1, :]
                y = y + _dot(m.astype(BF16), x_sel[k])
                bw = (bg * wcol[:, hd:hd + 1]).astype(BF16)
                upd = upd + _dot_tn(bw, x_sel[k])
            sl = slice(2 * pp * SSM_HEAD_DIM, (2 * pp + 2) * SSM_HEAD_DIM)
            y = y + yoff[:, sl] * jnp.where(low_half, eacs[:, h0:h0 + 1], eacs[:, h0 + 1:h0 + 2])
            st_pairs.append(st[:, sl] * jnp.where(low_half, elast[:, h0:h0 + 1], elast[:, h0 + 1:h0 + 2]) + upd)
            y_pairs.append(y)
        state_ref[gi] = jnp.concatenate(st_pairs, axis=1)
        gs = slice(gi * SSM_GROUP_W, (gi + 1) * SSM_GROUP_W)
        y = jnp.concatenate(y_pairs, axis=1) + dsk_ref[:, gs] * xs[:, gs]
        y = y * _silu(z_ref[0, :, gs].astype(F32))
        o_ref[0, :, gs] = _rms(y, gn_ref[:, gs]).astype(BF16)


def _ssd(zxbc, dt, conv_w, conv_b, dt_bias, a_log, d_skip, gate_norm):
    b, lp, _ = zxbc.shape
    nc = lp // CHUNK
    wide = lambda j: pl.BlockSpec((1, CHUNK, SSM_INNER), lambda i, c: (i, c, j))
    const = lambda r, w: pl.BlockSpec((r, w), lambda i, c: (0, 0))
    return pl.pallas_call(
        _ssd_kernel,
        grid=(b, nc),
        in_specs=[
            wide(0), wide(1), wide(2),
            pl.BlockSpec((1, CHUNK, LANES), lambda i, c: (i, c, 0)),
            const(SSM_CONV, SSM_CONV_DIM), const(1, SSM_CONV_DIM), const(1, LANES), const(1, LANES),
            const(1, SSM_INNER), const(1, SSM_INNER),
        ],
        out_specs=pl.BlockSpec((1, CHUNK, SSM_INNER), lambda i, c: (i, c, 0)),
        out_shape=jax.ShapeDtypeStruct((b, lp, SSM_INNER), BF16),
        scratch_shapes=[pltpu.VMEM((SSM_GROUPS, SSM_STATE, SSM_GROUP_W), F32),
                        pltpu.VMEM((8, SSM_CONV_DIM), F32)],
        compiler_params=_cparams("parallel", "arbitrary"),
        name="ssd",
    )(zxbc, zxbc, zxbc, dt, conv_w, conv_b, dt_bias, a_log, d_skip, gate_norm)


def _out_proj_kernel(h_ref, y_ref, w_ref, g_ref, wr_ref, br_ref, ho_ref, lg_ref):
    hn = h_ref[...] + _dot(y_ref[...], w_ref[...])
    ho_ref[...] = hn
    u = _rms(hn, g_ref[...])
    lg_ref[...] = _dot_nt(wr_ref[...], u, precision=lax.Precision.HIGHEST) + br_ref[...]


def _out_proj(h, y, w, gain, wr_t, br):
    t, d = h.shape
    tm = TOKEN_TILE
    kin = y.shape[1]
    return pl.pallas_call(
        _out_proj_kernel,
        grid=(t // tm,),
        in_specs=[
            pl.BlockSpec((tm, d), lambda i: (i, 0)),
            pl.BlockSpec((tm, kin), lambda i: (i, 0)),
            pl.BlockSpec((kin, d), lambda i: (0, 0)),
            pl.BlockSpec((1, d), lambda i: (0, 0)),
            pl.BlockSpec((ROUTER_ROWS, d), lambda i: (0, 0)),
            pl.BlockSpec((ROUTER_ROWS, 1), lambda i: (0, 0)),
        ],
        out_specs=[
            pl.BlockSpec((tm, d), lambda i: (i, 0)),
            pl.BlockSpec((ROUTER_ROWS, tm), lambda i: (0, i)),
        ],
        out_shape=[jax.ShapeDtypeStruct((t, d), F32), jax.ShapeDtypeStruct((ROUTER_ROWS, t), F32)],
        compiler_params=_cparams("parallel"),
        name="out_proj_router",
    )(h, y, w, gain, wr_t, br)


def _first_argmax(v, n):
    ridx = lax.broadcasted_iota(I32, v.shape, 0).astype(F32)
    vmax = jnp.max(v, axis=0, keepdims=True)
    idx = jnp.min(jnp.where(v == vmax, ridx, float(n)), axis=0, keepdims=True)
    return vmax, idx.astype(I32)


def _route_kernel(lg_ref, eid_ref, gate_ref, rank_ref, cnt_ref, carry_ref):
    @pl.when(pl.program_id(0) == 0)
    def _():
        carry_ref[...] = jnp.zeros_like(carry_ref)

    logits = lg_ref[...]
    tm = logits.shape[1]
    gl = logits[0:MOE_GROUPS]
    gmax, gsel = _first_argmax(gl, MOE_GROUPS)
    p_group = 1.0 / jnp.sum(jnp.exp(gl - gmax), axis=0, keepdims=True)
    el = logits[MOE_GROUPS:MOE_GROUPS + MOE_EPG]
    for gg in range(1, MOE_GROUPS):
        el = jnp.where(gsel == gg, logits[MOE_GROUPS + gg * MOE_EPG:MOE_GROUPS + (gg + 1) * MOE_EPG], el)
    ex = jnp.exp(el - jnp.max(el, axis=0, keepdims=True))
    p = ex / jnp.sum(ex, axis=0, keepdims=True)
    p1, i1 = _first_argmax(p, MOE_EPG)
    ridx8 = lax.broadcasted_iota(I32, p.shape, 0)
    p2, i2 = _first_argmax(jnp.where(ridx8 == i1, -1.0, p), MOE_EPG)
    denom = p1 + p2
    e0 = gsel * MOE_EPG + i1
    e1 = gsel * MOE_EPG + i2
    eid_ref[0:1, :] = e0
    eid_ref[1:2, :] = e1
    gate_ref[0:1, :] = p_group * p1 / denom
    gate_ref[1:2, :] = p_group * p2 / denom

    ridx = lax.broadcasted_iota(I32, (MOE_EXPERTS, tm), 0)
    oh0 = jnp.where(ridx == e0, 1.0, 0.0)
    oh1 = jnp.where(ridx == e1, 1.0, 0.0)
    before = jnp.where(lax.broadcasted_iota(I32, (tm, tm), 0) < lax.broadcasted_iota(I32, (tm, tm), 1), 1.0, 0.0)
    before = before.astype(BF16)
    pref0 = _dot(oh0.astype(BF16), before)
    pref1 = _dot(oh1.astype(BF16), before)
    tot0 = jnp.sum(oh0, axis=1, keepdims=True)
    tot1 = jnp.sum(oh1, axis=1, keepdims=True)
    carry = carry_ref[...]
    rank_ref[0:1, :] = jnp.sum(oh0 * (carry + pref0), axis=0, keepdims=True).astype(I32)
    rank_ref[1:2, :] = jnp.sum(oh1 * (carry + tot0 + pref1), axis=0, keepdims=True).astype(I32)
    carry = carry + tot0 + tot1
    carry_ref[...] = carry
    cnt_ref[...] = jnp.broadcast_to(carry, cnt_ref.shape).astype(I32)


def _route(logits_t):
    t = logits_t.shape[1]
    tm = TOKEN_TILE
    pair = lambda: pl.BlockSpec((2, tm), lambda i: (0, i))
    return pl.pallas_call(
        _route_kernel,
        grid=(t // tm,),
        in_specs=[pl.BlockSpec((ROUTER_ROWS, tm), lambda i: (0, i))],
        out_specs=[pair(), pair(), pair(), pl.BlockSpec((MOE_EXPERTS, LANES), lambda i: (0, 0))],
        out_shape=[jax.ShapeDtypeStruct((2, t), I32), jax.ShapeDtypeStruct((2, t), F32),
                   jax.ShapeDtypeStruct((2, t), I32), jax.ShapeDtypeStruct((MOE_EXPERTS, LANES), I32)],
        scratch_shapes=[pltpu.VMEM((MOE_EXPERTS, 1), F32)],
        compiler_params=_cparams("arbitrary"),
        name="route_rank",
    )(logits_t)


def _dispatch_kernel(dest_ref, h_ref, g_ref, xs_in_ref, xs_ref, u_ref, sem):
    del xs_in_ref
    tm = u_ref.shape[0]
    u_ref[...] = _rms(h_ref[...], g_ref[...])

    def issue(t, carry):
        for k in range(2):
            pltpu.make_async_copy(u_ref.at[pl.ds(t, 1)], xs_ref.at[pl.ds(dest_ref[0, k, t], 1)], sem).start()
        return carry

    lax.fori_loop(0, tm, issue, 0)
    for _ in range(2):
        pltpu.make_async_copy(u_ref, xs_ref.at[pl.ds(0, tm)], sem).wait()


def _dispatch(dest, h, gain, xs_init):
    t, d = h.shape
    tm = TOKEN_TILE
    return pl.pallas_call(
        _dispatch_kernel,
        grid=(t // tm,),
        in_specs=[
            pl.BlockSpec((1, 2, tm), lambda i: (i, 0, 0), memory_space=pltpu.SMEM),
            pl.BlockSpec((tm, d), lambda i: (i, 0)),
            pl.BlockSpec((1, d), lambda i: (0, 0)),
            pl.BlockSpec(memory_space=pl.ANY),
        ],
        out_specs=pl.BlockSpec(memory_space=pl.ANY),
        out_shape=jax.ShapeDtypeStruct(xs_init.shape, F32),
        scratch_shapes=[pltpu.VMEM((tm, d), F32), pltpu.SemaphoreType.DMA(())],
        input_output_aliases={3: 0},
        compiler_params=_cparams("arbitrary"),
        name="moe_dispatch",
    )(dest, h, gain, xs_init)


def _expert_kernel(be_ref, nu_ref, xs_ref, w1_ref, w3_ref, w2_ref, ys_ref):
    used = pl.program_id(0) < nu_ref[0]

    @pl.when(used)
    def _():
        x = xs_ref[...].astype(BF16)
        a = _dot(x, w1_ref[0])
        b = _dot(x, w3_ref[0])
        ys_ref[...] = _dot((_silu(a) * b).astype(BF16), w2_ref[0])

    @pl.when(jnp.logical_not(used))
    def _():
        ys_ref[...] = jnp.zeros_like(ys_ref)


def _experts(block_eid, n_used, xs, w1, w3, w2):
    rows, d = xs.shape
    nb = rows // MOE_BLOCK
    blk = lambda i, be, nu: (jnp.minimum(i, nu[0] - 1), 0)
    wsel = lambda i, be, nu: (be[i], 0, 0)
    return pl.pallas_call(
        _expert_kernel,
        grid_spec=pltpu.PrefetchScalarGridSpec(
            num_scalar_prefetch=2,
            grid=(nb,),
            in_specs=[
                pl.BlockSpec((MOE_BLOCK, d), blk),
                pl.BlockSpec((1, d, MOE_FF), wsel),
                pl.BlockSpec((1, d, MOE_FF), wsel),
                pl.BlockSpec((1, MOE_FF, d), wsel),
            ],
            out_specs=pl.BlockSpec((MOE_BLOCK, d), lambda i, be, nu: (i, 0)),
        ),
        out_shape=jax.ShapeDtypeStruct((rows, d), F32),
        compiler_params=_cparams("arbitrary"),
        name="moe_experts",
    )(block_eid, n_used, xs, w1, w3, w2)


def _combine_kernel(dest_ref, gate_ref, h_ref, g_ref, ys_ref, o_ref, y_ref, sem, *, final_norm):
    tm = h_ref.shape[0]

    def issue(t, carry):
        for k in range(2):
            pltpu.make_async_copy(ys_ref.at[pl.ds(dest_ref[0, k, t], 1)], y_ref.at[k, pl.ds(t, 1)], sem).start()
        return carry

    lax.fori_loop(0, tm, issue, 0)
    for k in range(2):
        pltpu.make_async_copy(ys_ref.at[pl.ds(0, tm)], y_ref.at[k], sem).wait()
    gate = gate_ref[...]
    out = h_ref[...] + (y_ref[0] * gate[:, 0:1] + y_ref[1] * gate[:, 1:2])
    if final_norm:
        out = _rms(out, g_ref[...])
    o_ref[...] = out


def _combine(dest, gate_rows, h, gain, ys, final_norm):
    t, d = h.shape
    tm = TOKEN_TILE
    return pl.pallas_call(
        functools.partial(_combine_kernel, final_norm=final_norm),
        grid=(t // tm,),
        in_specs=[
            pl.BlockSpec((1, 2, tm), lambda i: (i, 0, 0), memory_space=pltpu.SMEM),
            pl.BlockSpec((tm, 2), lambda i: (i, 0)),
            pl.BlockSpec((tm, d), lambda i: (i, 0)),
            pl.BlockSpec((1, d), lambda i: (0, 0)),
            pl.BlockSpec(memory_space=pl.ANY),
        ],
        out_specs=pl.BlockSpec((tm, d), lambda i: (i, 0)),
        out_shape=jax.ShapeDtypeStruct((t, d), F32),
        scratch_shapes=[pltpu.VMEM((2, tm, d), F32), pltpu.SemaphoreType.DMA(())],
        compiler_params=_cparams("arbitrary"),
        name="moe_combine",
    )(dest, gate_rows, h, gain, ys)


def _moe_layer(h, y_mix, w_out, norm_gain, wg, bg, we, be, w1, w3, w2, out_gain, final_norm):
    t, d = h.shape
    tm = TOKEN_TILE
    wr_t = jnp.concatenate([wg.T, we.reshape(d, MOE_EXPERTS).T,
                            jnp.zeros((ROUTER_ROWS - MOE_GROUPS - MOE_EXPERTS, d), F32)], axis=0)
    br = jnp.concatenate([bg, be.reshape(-1), jnp.zeros((ROUTER_ROWS - MOE_GROUPS - MOE_EXPERTS,), F32)])
    h, logits_t = _out_proj(h, y_mix, w_out.astype(BF16), norm_gain[None, :], wr_t, br[:, None])
    eid, gate, rank, counts = _route(logits_t)

    counts = counts[:, 0]
    padded = (counts + MOE_BLOCK - 1) // MOE_BLOCK * MOE_BLOCK
    pend = jnp.cumsum(padded)
    pstart = pend - padded
    onehot = eid[None] == jnp.arange(MOE_EXPERTS, dtype=I32)[:, None, None]
    dest = rank + jnp.sum(jnp.where(onehot, pstart[:, None, None], 0), axis=0)
    dest = dest.reshape(2, t // tm, tm).transpose(1, 0, 2)
    n_blocks = -(-(2 * t) // MOE_BLOCK) + MOE_EXPERTS
    block_eid = jnp.minimum(
        jnp.searchsorted(pend, jnp.arange(n_blocks, dtype=I32) * MOE_BLOCK, side="right"), MOE_EXPERTS - 1
    ).astype(I32)
    n_used = (pend[-1:] // MOE_BLOCK).astype(I32)

    xs = _dispatch(dest, h, norm_gain[None, :], jnp.zeros((n_blocks * MOE_BLOCK, d), F32))
    ys = _experts(block_eid, n_used, xs, w1.astype(BF16), w3.astype(BF16), w2.astype(BF16))
    return _combine(dest, gate.T, h, out_gain[None, :], ys, final_norm)


def kernel(x, meta, norm_mix, norm_ffn, norm_final, ret_w_in, ret_w_out, ret_norm, ssm_w_in, ssm_conv_w,
           ssm_conv_b, ssm_dt_bias, ssm_a_log, ssm_d, ssm_norm, ssm_w_out, moe_wg, moe_bg, moe_we, moe_be,
           moe_w1, moe_w3, moe_w2):
    b, seq, d = x.shape
    lp = CHUNK + seq
    t = b * lp
    depth = norm_mix.shape[0]
    h = jnp.concatenate([jnp.zeros((b, PAD, d), F32), jnp.broadcast_to(meta[None], (b, N_META, d)), x], axis=1)

    half = RET_QK_DIM // 2
    inv_freq = 1.0 / (RET_ROPE_BASE ** jnp.linspace(0.0, 1.0, half, dtype=F32))
    ang = (jnp.arange(lp, dtype=F32) - PAD)[:, None] * inv_freq[None, :]
    cos, sin = jnp.cos(ang), jnp.sin(ang)
    perm = jnp.concatenate([jnp.arange(0, RET_QK_DIM, 2), jnp.arange(1, RET_QK_DIM, 2)])
    qk_perm = (jnp.arange(2 * RET_HEADS)[:, None] * RET_QK_DIM + perm[None, :]).reshape(-1)

    for i in range(depth):
        j = i // 2
        if i % 2 == 0:
            w_in = jnp.concatenate([ret_w_in[j][:, qk_perm], ret_w_in[j][:, 2 * RET_QK:]], axis=1).astype(BF16)
            qkvg = _ret_in(h.reshape(b, lp, d), norm_mix[i][None, :], cos, sin, w_in)
            y = _retention(qkvg, ret_norm[j][None, :])
            w_out = ret_w_out[j]
        else:
            w_in = jnp.concatenate([ssm_w_in[j], jnp.zeros((d, LANES - SSM_HEADS), F32)], axis=1).astype(BF16)
            zxbc, dt = _ssm_in(h.reshape(t, d), norm_mix[i][None, :], w_in)
            lane_pad = lambda v: jnp.concatenate([v, jnp.zeros((LANES - SSM_HEADS,), F32)])[None, :]
            y = _ssd(zxbc.reshape(b, lp, SSM_MAIN), dt.reshape(b, lp, LANES), ssm_conv_w[j],
                     ssm_conv_b[j][None, :], lane_pad(ssm_dt_bias[j]), lane_pad(ssm_a_log[j]),
                     jnp.repeat(ssm_d[j], SSM_HEAD_DIM)[None, :], ssm_norm[j][None, :])
            w_out = ssm_w_out[j]
        last = i == depth - 1
        h = _moe_layer(h.reshape(t, d), y.reshape(t, -1), w_out, norm_ffn[i], moe_wg[i], moe_bg[i], moe_we[i],
                       moe_be[i], moe_w1[i], moe_w3[i], moe_w2[i], norm_final if last else norm_ffn[i], last)
    return h.reshape(b, lp, d)[:, CHUNK:]
```

```python
import functools
import math

import jax
import jax.numpy as jnp
from jax import lax
from jax.experimental import pallas as pl
from jax.experimental.pallas import tpu as pltpu

F32 = jnp.float32
BF16 = jnp.bfloat16
I32 = jnp.int32

D_MODEL = 1024
N_META = 16
CHUNK = 128
PAD = CHUNK - N_META
EPS = 1e-6

RET_HEADS = 4
RET_QK_DIM = 256
RET_V_DIM = 512
RET_QK = RET_HEADS * RET_QK_DIM
RET_V = RET_HEADS * RET_V_DIM
RET_IN = 2 * RET_QK + 2 * RET_V
RET_ROPE_BASE = 10000.0

SSM_INNER = 2048
SSM_HEAD_DIM = 64
SSM_HEADS = 32
SSM_GROUPS = 8
SSM_HPG = 4
SSM_STATE = 128
SSM_CONV = 4
SSM_BC = SSM_GROUPS * SSM_STATE
SSM_CONV_DIM = SSM_INNER + 2 * SSM_BC
SSM_MAIN = SSM_INNER + SSM_CONV_DIM
LANES = 128
SSM_GROUP_W = SSM_HPG * SSM_HEAD_DIM

MOE_GROUPS = 4
MOE_EPG = 8
MOE_EXPERTS = 32
MOE_FF = 512
MOE_BLOCK = 256
ROUTER_ROWS = 40

TOKEN_TILE = 512
VMEM_LIMIT = 56 * 1024 * 1024


def _cparams(*sem):
    return pltpu.CompilerParams(dimension_semantics=sem, vmem_limit_bytes=VMEM_LIMIT)


def _rms(x, gain):
    ms = jnp.mean(x * x, axis=-1, keepdims=True)
    return x * lax.rsqrt(ms + EPS) * gain


def _silu(x):
    hx = 0.5 * x
    return hx + hx * jnp.tanh(hx)


def _dot(a, b):
    return jnp.dot(a, b, preferred_element_type=F32)


def _dot_nt(a, b, precision=None):
    return lax.dot_general(a, b, (((1,), (1,)), ((), ())), preferred_element_type=F32, precision=precision)


def _dot_tn(a, b):
    return lax.dot_general(a, b, (((0,), (0,)), ((), ())), preferred_element_type=F32)


def _row_tile(rows, cap=544):
    best = 16
    for t in range(16, cap + 1, 16):
        if rows % t == 0:
            best = t
    return best


def _ret_in_kernel(h_ref, g_ref, cos_ref, sin_ref, w_ref, o_ref):
    xn = _rms(h_ref[0], g_ref[...]).astype(BF16)
    cos = cos_ref[...]
    sin = sin_ref[...]
    half = RET_QK_DIM // 2
    for j in range(2 * RET_HEADS):
        c0 = j * RET_QK_DIM
        acc = _dot(xn, w_ref[:, c0:c0 + RET_QK_DIM])
        if j >= RET_HEADS:
            acc = acc * (RET_QK_DIM ** -0.5)
        e = acc[:, :half]
        o = acc[:, half:]
        o_ref[0, :, c0:c0 + half] = (e * cos - o * sin).astype(BF16)
        o_ref[0, :, c0 + half:c0 + RET_QK_DIM] = (o * cos + e * sin).astype(BF16)
    step = 1024
    for c0 in range(2 * RET_QK, RET_IN, step):
        o_ref[0, :, c0:c0 + step] = _dot(xn, w_ref[:, c0:c0 + step]).astype(BF16)


def _ret_in(h, gain, cos, sin, w):
    b, lp, d = h.shape
    tm = _row_tile(lp)
    return pl.pallas_call(
        _ret_in_kernel,
        grid=(b, lp // tm),
        in_specs=[
            pl.BlockSpec((1, tm, d), lambda i, j: (i, j, 0)),
            pl.BlockSpec((1, d), lambda i, j: (0, 0)),
            pl.BlockSpec((tm, RET_QK_DIM // 2), lambda i, j: (j, 0)),
            pl.BlockSpec((tm, RET_QK_DIM // 2), lambda i, j: (j, 0)),
            pl.BlockSpec((d, RET_IN), lambda i, j: (0, 0)),
        ],
        out_specs=pl.BlockSpec((1, tm, RET_IN), lambda i, j: (i, j, 0)),
        out_shape=jax.ShapeDtypeStruct((b, lp, RET_IN), BF16),
        compiler_params=_cparams("parallel", "parallel"),
        name="ret_in_proj",
    )(h, gain, cos, sin, w)


def _ret_kernel(q_ref, k_ref, v_ref, g_ref, hn_ref, o_ref, state_ref):
    @pl.when(pl.program_id(1) == 0)
    def _():
        state_ref[...] = jnp.zeros_like(state_ref)

    row = lax.broadcasted_iota(I32, (CHUNK, CHUNK), 0).astype(F32)
    col = lax.broadcasted_iota(I32, (CHUNK, CHUNK), 1).astype(F32)
    diff = row - col
    ridx = row[:, :1]
    for hh in range(RET_HEADS):
        lg = math.log(1.0 - 2.0 ** (-5.0 - hh))
        intra = jnp.where(diff >= 0, jnp.exp(lg * jnp.maximum(diff, 0.0)), 0.0)
        qdec = jnp.exp(lg * (ridx + 1.0))
        kdec = jnp.exp(lg * (CHUNK - 1.0 - ridx))
        cdec = math.exp(lg * CHUNK)
        qh = q_ref[0, :, hh * RET_QK_DIM:(hh + 1) * RET_QK_DIM]
        kh = k_ref[0, :, hh * RET_QK_DIM:(hh + 1) * RET_QK_DIM]
        vh = v_ref[0, :, hh * RET_V_DIM:(hh + 1) * RET_V_DIM]
        scores = _dot_nt(qh, kh) * intra
        st = state_ref[hh]
        y = _dot(scores.astype(BF16), vh) + _dot(qh, st.astype(BF16)) * qdec
        kd = (kh.astype(F32) * kdec).astype(BF16)
        state_ref[hh] = st * cdec + _dot_tn(kd, vh)
        yn = _rms(y, hn_ref[:, hh * RET_V_DIM:(hh + 1) * RET_V_DIM])
        gh = g_ref[0, :, hh * RET_V_DIM:(hh + 1) * RET_V_DIM].astype(F32)
        o_ref[0, :, hh * RET_V_DIM:(hh + 1) * RET_V_DIM] = (_silu(gh) * yn).astype(BF16)


def _retention(qkvg, head_norm):
    b, lp, _ = qkvg.shape
    nc = lp // CHUNK
    return pl.pallas_call(
        _ret_kernel,
        grid=(b, nc),
        in_specs=[
            pl.BlockSpec((1, CHUNK, RET_QK), lambda i, c: (i, c, 0)),
            pl.BlockSpec((1, CHUNK, RET_QK), lambda i, c: (i, c, 1)),
            pl.BlockSpec((1, CHUNK, RET_V), lambda i, c: (i, c, 1)),
            pl.BlockSpec((1, CHUNK, RET_V), lambda i, c: (i, c, 2)),
            pl.BlockSpec((1, RET_V), lambda i, c: (0, 0)),
        ],
        out_specs=pl.BlockSpec((1, CHUNK, RET_V), lambda i, c: (i, c, 0)),
        out_shape=jax.ShapeDtypeStruct((b, lp, RET_V), BF16),
        scratch_shapes=[pltpu.VMEM((RET_HEADS, RET_QK_DIM, RET_V_DIM), F32)],
        compiler_params=_cparams("parallel", "arbitrary"),
        name="retention",
    )(qkvg, qkvg, qkvg, qkvg, head_norm)


def _ssm_in_kernel(h_ref, g_ref, w_ref, cw_ref, cb_ref, o_ref, dt_ref, tail_ref, *, rows_per_seq):
    i = pl.program_id(0)
    tm = h_ref.shape[0]

    @pl.when(i == 0)
    def _():
        tail_ref[...] = jnp.zeros_like(tail_ref)

    xn = _rms(h_ref[...], g_ref[...]).astype(BF16)
    step = 1024
    for c0 in range(0, SSM_INNER, step):
        o_ref[:, c0:c0 + step] = _silu(_dot(xn, w_ref[:, c0:c0 + step])).astype(BF16)

    pos = lax.rem(i * tm, rows_per_seq) + lax.broadcasted_iota(I32, (tm, 1), 0)
    is_pad = jnp.logical_or(pos < PAD, jnp.logical_and(pos >= rows_per_seq, pos < rows_per_seq + PAD))
    row8 = lax.broadcasted_iota(I32, (8, 1), 0)
    for c0 in range(0, SSM_CONV_DIM, step):
        pre = jnp.where(is_pad, 0.0, _dot(xn, w_ref[:, SSM_INNER + c0:SSM_INNER + c0 + step]))
        tail = tail_ref[:, c0:c0 + step]
        tail_ref[:, c0:c0 + step] = pre[tm - 8:, :]
        conv = cb_ref[:, c0:c0 + step] + cw_ref[SSM_CONV - 1:SSM_CONV, c0:c0 + step] * pre
        for k in range(1, SSM_CONV):
            rolled = pltpu.roll(pre, k, axis=0)
            top = jnp.where(row8 < k, pltpu.roll(tail, k, axis=0), rolled[:8])
            shifted = jnp.concatenate([top, rolled[8:]], axis=0)
            conv = conv + cw_ref[SSM_CONV - 1 - k:SSM_CONV - k, c0:c0 + step] * shifted
        o_ref[:, SSM_INNER + c0:SSM_INNER + c0 + step] = _silu(conv).astype(BF16)
    dt_ref[...] = _dot(xn, w_ref[:, SSM_MAIN:SSM_MAIN + LANES])


def _ssm_in(h, gain, w, conv_w, conv_b, rows_per_seq):
    t, d = h.shape
    tm = TOKEN_TILE
    assert tm <= rows_per_seq
    const = lambda r, c: pl.BlockSpec((r, c), lambda i: (0, 0))
    return pl.pallas_call(
        functools.partial(_ssm_in_kernel, rows_per_seq=rows_per_seq),
        grid=(t // tm,),
        in_specs=[
            pl.BlockSpec((tm, d), lambda i: (i, 0)),
            const(1, d), const(d, SSM_MAIN + LANES), const(SSM_CONV, SSM_CONV_DIM), const(1, SSM_CONV_DIM),
        ],
        out_specs=[
            pl.BlockSpec((tm, SSM_MAIN), lambda i: (i, 0)),
            pl.BlockSpec((tm, LANES), lambda i: (i, 0)),
        ],
        out_shape=[jax.ShapeDtypeStruct((t, SSM_MAIN), BF16), jax.ShapeDtypeStruct((t, LANES), F32)],
        scratch_shapes=[pltpu.VMEM((8, SSM_CONV_DIM), F32)],
        compiler_params=_cparams("arbitrary"),
        name="ssm_in_proj",
    )(h, gain, w, conv_w, conv_b)


def _ssd_kernel(zg_ref, x_ref, bc_ref, dt_ref, dtb_ref, alog_ref, dsk_ref, gn_ref, o_ref, state_ref):
    c = pl.program_id(1)

    @pl.when(c == 0)
    def _():
        state_ref[...] = jnp.zeros_like(state_ref)

    row = lax.broadcasted_iota(I32, (CHUNK, CHUNK), 0)
    col = lax.broadcasted_iota(I32, (CHUNK, CHUNK), 1)
    causal_bias = jnp.where(row >= col, 0.0, -jnp.inf)
    low_half = col < SSM_HEAD_DIM
    valid = row[:, :1] >= jnp.where(c > 0, 0, PAD)

    dtr = dt_ref[0] + dtb_ref[...]
    dtv = jnp.maximum(dtr, 0.0) + jnp.log(1.0 + jnp.exp(-jnp.abs(dtr)))
    dtv = jnp.where(valid, dtv, 0.0)
    da = dtv * (-jnp.exp(alog_ref[...]))
    tril = jnp.where(row >= col, 1.0, 0.0)
    acs = jnp.dot(tril, da, preferred_element_type=F32, precision=lax.Precision.HIGHEST)
    last = acs[CHUNK - 1:CHUNK, :]
    eacs = jnp.exp(acs)
    wcol = dtv * jnp.exp(last - acs)
    elast = jnp.broadcast_to(jnp.exp(last), (CHUNK, LANES))
    acs_t = acs.T
    dt_t = dtv.T

    def per_head(v, h0):
        return jnp.where(low_half, v[:, h0:h0 + 1], v[:, h0 + 1:h0 + 2])

    for gi in range(SSM_GROUPS):
        heads = [gi * SSM_HPG + 2 * pp for pp in range(SSM_HPG // 2)]
        gs = slice(gi * SSM_GROUP_W, (gi + 1) * SSM_GROUP_W)
        xg = x_ref[0, :, gs].astype(F32)
        bg = bc_ref[0, :, gi * SSM_STATE:(gi + 1) * SSM_STATE]
        cg = bc_ref[0, :, SSM_BC + gi * SSM_STATE:SSM_BC + (gi + 1) * SSM_STATE]
        cb = _dot_nt(cg, bg)
        st = state_ref[gi]
        yoff = _dot(cg, st.astype(BF16))
        xw = xg * jnp.concatenate([per_head(wcol, h0) for h0 in heads], axis=1)
        state_ref[gi] = st * jnp.concatenate([per_head(elast, h0) for h0 in heads], axis=1) + _dot_tn(
            bg, xw.astype(BF16))
        y_pairs = []
        for pp, h0 in enumerate(heads):
            xp = xg[:, pp * LANES:(pp + 1) * LANES]
            rhs = jnp.concatenate([jnp.where(low_half, xp, 0.0), jnp.where(low_half, 0.0, xp)], axis=0)
            ms = []
            for hd in (h0, h0 + 1):
                seg = acs[:, hd:hd + 1] - acs_t[hd:hd + 1, :]
                ms.append(cb * jnp.exp(seg + causal_bias) * dt_t[hd:hd + 1, :])
            y_pairs.append(_dot(jnp.concatenate(ms, axis=1).astype(BF16), rhs.astype(BF16)))
        y = jnp.concatenate(y_pairs, axis=1)
        y = y + yoff * jnp.concatenate([per_head(eacs, h0) for h0 in heads], axis=1)
        y = y + dsk_ref[:, gs] * xg
        y = y * zg_ref[0, :, gs].astype(F32)
        o_ref[0, :, gs] = _rms(y, gn_ref[:, gs]).astype(BF16)


def _ssd(zxbc, dt, dt_bias, a_log, d_skip, gate_norm):
    b, lp, _ = zxbc.shape
    nc = lp // CHUNK
    wide = lambda j: pl.BlockSpec((1, CHUNK, SSM_INNER), lambda i, c: (i, c, j))
    const = lambda r, w: pl.BlockSpec((r, w), lambda i, c: (0, 0))
    return pl.pallas_call(
        _ssd_kernel,
        grid=(b, nc),
        in_specs=[
            wide(0), wide(1), wide(2),
            pl.BlockSpec((1, CHUNK, LANES), lambda i, c: (i, c, 0)),
            const(1, LANES), const(1, LANES), const(1, SSM_INNER), const(1, SSM_INNER),
        ],
        out_specs=pl.BlockSpec((1, CHUNK, SSM_INNER), lambda i, c: (i, c, 0)),
        out_shape=jax.ShapeDtypeStruct((b, lp, SSM_INNER), BF16),
        scratch_shapes=[pltpu.VMEM((SSM_GROUPS, SSM_STATE, SSM_GROUP_W), F32)],
        compiler_params=_cparams("parallel", "arbitrary"),
        name="ssd",
    )(zxbc, zxbc, zxbc, dt, dt_bias, a_log, d_skip, gate_norm)


def _out_proj_kernel(h_ref, y_ref, w_ref, g_ref, wr_ref, br_ref, ho_ref, lg_ref):
    hn = h_ref[...] + _dot(y_ref[...], w_ref[...])
    ho_ref[...] = hn
    u = _rms(hn, g_ref[...])
    lg_ref[...] = _dot_nt(wr_ref[...], u, precision=lax.Precision.HIGHEST) + br_ref[...]


def _out_proj(h, y, w, gain, wr_t, br):
    t, d = h.shape
    tm = TOKEN_TILE
    kin = y.shape[1]
    return pl.pallas_call(
        _out_proj_kernel,
        grid=(t // tm,),
        in_specs=[
            pl.BlockSpec((tm, d), lambda i: (i, 0)),
            pl.BlockSpec((tm, kin), lambda i: (i, 0)),
            pl.BlockSpec((kin, d), lambda i: (0, 0)),
            pl.BlockSpec((1, d), lambda i: (0, 0)),
            pl.BlockSpec((ROUTER_ROWS, d), lambda i: (0, 0)),
            pl.BlockSpec((ROUTER_ROWS, 1), lambda i: (0, 0)),
        ],
        out_specs=[
            pl.BlockSpec((tm, d), lambda i: (i, 0)),
            pl.BlockSpec((ROUTER_ROWS, tm), lambda i: (0, i)),
        ],
        out_shape=[jax.ShapeDtypeStruct((t, d), F32), jax.ShapeDtypeStruct((ROUTER_ROWS, t), F32)],
        compiler_params=_cparams("parallel"),
        name="out_proj_router",
    )(h, y, w, gain, wr_t, br)


def _first_argmax(v, n):
    ridx = lax.broadcasted_iota(I32, v.shape, 0).astype(F32)
    vmax = jnp.max(v, axis=0, keepdims=True)
    idx = jnp.min(jnp.where(v == vmax, ridx, float(n)), axis=0, keepdims=True)
    return vmax, idx.astype(I32)


def _route_kernel(lg_ref, eid_ref, gate_ref, rank_ref, cnt_ref, carry_ref):
    @pl.when(pl.program_id(0) == 0)
    def _():
        carry_ref[...] = jnp.zeros_like(carry_ref)

    logits = lg_ref[...]
    tm = logits.shape[1]
    gl = logits[0:MOE_GROUPS]
    gmax, gsel = _first_argmax(gl, MOE_GROUPS)
    p_group = 1.0 / jnp.sum(jnp.exp(gl - gmax), axis=0, keepdims=True)
    el = logits[MOE_GROUPS:MOE_GROUPS + MOE_EPG]
    for gg in range(1, MOE_GROUPS):
        el = jnp.where(gsel == gg, logits[MOE_GROUPS + gg * MOE_EPG:MOE_GROUPS + (gg + 1) * MOE_EPG], el)
    ex = jnp.exp(el - jnp.max(el, axis=0, keepdims=True))
    p = ex / jnp.sum(ex, axis=0, keepdims=True)
    p1, i1 = _first_argmax(p, MOE_EPG)
    ridx8 = lax.broadcasted_iota(I32, p.shape, 0)
    p2, i2 = _first_argmax(jnp.where(ridx8 == i1, -1.0, p), MOE_EPG)
    denom = p1 + p2
    e0 = gsel * MOE_EPG + i1
    e1 = gsel * MOE_EPG + i2
    eid_ref[0:1, :] = e0
    eid_ref[1:2, :] = e1
    gate_ref[0:1, :] = p_group * p1 / denom
    gate_ref[1:2, :] = p_group * p2 / denom

    ridx = lax.broadcasted_iota(I32, (MOE_EXPERTS, tm), 0)
    oh0 = jnp.where(ridx == e0, 1.0, 0.0)
    oh1 = jnp.where(ridx == e1, 1.0, 0.0)
    before = jnp.where(lax.broadcasted_iota(I32, (tm, tm), 0) < lax.broadcasted_iota(I32, (tm, tm), 1), 1.0, 0.0)
    before = before.astype(BF16)
    pref0 = _dot(oh0.astype(BF16), before)
    pref1 = _dot(oh1.astype(BF16), before)
    tot0 = jnp.sum(oh0, axis=1, keepdims=True)
    tot1 = jnp.sum(oh1, axis=1, keepdims=True)
    carry = carry_ref[...]
    rank_ref[0:1, :] = jnp.sum(oh0 * (carry + pref0), axis=0, keepdims=True).astype(I32)
    rank_ref[1:2, :] = jnp.sum(oh1 * (carry + tot0 + pref1), axis=0, keepdims=True).astype(I32)
    carry = carry + tot0 + tot1
    carry_ref[...] = carry
    cnt_ref[...] = jnp.broadcast_to(carry, cnt_ref.shape).astype(I32)


def _route(logits_t):
    t = logits_t.shape[1]
    tm = TOKEN_TILE
    pair = lambda: pl.BlockSpec((2, tm), lambda i: (0, i))
    return pl.pallas_call(
        _route_kernel,
        grid=(t // tm,),
        in_specs=[pl.BlockSpec((ROUTER_ROWS, tm), lambda i: (0, i))],
        out_specs=[pair(), pair(), pair(), pl.BlockSpec((MOE_EXPERTS, LANES), lambda i: (0, 0))],
        out_shape=[jax.ShapeDtypeStruct((2, t), I32), jax.ShapeDtypeStruct((2, t), F32),
                   jax.ShapeDtypeStruct((2, t), I32), jax.ShapeDtypeStruct((MOE_EXPERTS, LANES), I32)],
        scratch_shapes=[pltpu.VMEM((MOE_EXPERTS, 1), F32)],
        compiler_params=_cparams("arbitrary"),
        name="route_rank",
    )(logits_t)


def _dispatch_kernel(seg_ref, dest_ref, h_ref, g_ref, xs_ref, u_ref, zero_ref, sem):
    tm = u_ref.shape[0]

    @pl.when(pl.program_id(0) == 0)
    def _():
        zero_ref[...] = jnp.zeros_like(zero_ref)
        for e in range(MOE_EXPERTS):
            @pl.when(seg_ref[MOE_EXPERTS + e] > 0)
            def _():
                start = pl.multiple_of(seg_ref[e] - MOE_BLOCK, MOE_BLOCK)
                fill = pltpu.make_async_copy(zero_ref, xs_ref.at[pl.ds(start, MOE_BLOCK)], sem)
                fill.start()
                fill.wait()

    u_ref[...] = _rms(h_ref[...], g_ref[...])

    def issue(t, carry):
        for k in range(2):
            pltpu.make_async_copy(u_ref.at[pl.ds(t, 1)], xs_ref.at[pl.ds(dest_ref[0, k, t], 1)], sem).start()
        return carry

    lax.fori_loop(0, tm, issue, 0)
    for _ in range(2):
        pltpu.make_async_copy(u_ref, xs_ref.at[pl.ds(0, tm)], sem).wait()


def _dispatch(seg, dest, h, gain, rows):
    t, d = h.shape
    tm = TOKEN_TILE
    return pl.pallas_call(
        _dispatch_kernel,
        grid_spec=pltpu.PrefetchScalarGridSpec(
            num_scalar_prefetch=1,
            grid=(t // tm,),
            in_specs=[
                pl.BlockSpec((1, 2, tm), lambda i, seg: (i, 0, 0), memory_space=pltpu.SMEM),
                pl.BlockSpec((tm, d), lambda i, seg: (i, 0)),
                pl.BlockSpec((1, d), lambda i, seg: (0, 0)),
            ],
            out_specs=pl.BlockSpec(memory_space=pl.ANY),
            scratch_shapes=[pltpu.VMEM((tm, d), F32), pltpu.VMEM((MOE_BLOCK, d), F32),
                            pltpu.SemaphoreType.DMA(())],
        ),
        out_shape=jax.ShapeDtypeStruct((rows, d), F32),
        compiler_params=_cparams("arbitrary"),
        name="moe_dispatch",
    )(seg, dest, h, gain)


def _expert_kernel(be_ref, nu_ref, xs_ref, w1_ref, w3_ref, w2_ref, ys_ref):
    used = pl.program_id(0) < nu_ref[0]

    @pl.when(used)
    def _():
        x = xs_ref[...].astype(BF16)
        a = _dot(x, w1_ref[0])
        b = _dot(x, w3_ref[0])
        ys_ref[...] = _dot((_silu(a) * b).astype(BF16), w2_ref[0])

    @pl.when(jnp.logical_not(used))
    def _():
        ys_ref[...] = jnp.zeros_like(ys_ref)


def _experts(block_eid, n_used, xs, w1, w3, w2):
    rows, d = xs.shape
    nb = rows // MOE_BLOCK
    blk = lambda i, be, nu: (jnp.minimum(i, nu[0] - 1), 0)
    wsel = lambda i, be, nu: (be[i], 0, 0)
    return pl.pallas_call(
        _expert_kernel,
        grid_spec=pltpu.PrefetchScalarGridSpec(
            num_scalar_prefetch=2,
            grid=(nb,),
            in_specs=[
                pl.BlockSpec((MOE_BLOCK, d), blk),
                pl.BlockSpec((1, d, MOE_FF), wsel),
                pl.BlockSpec((1, d, MOE_FF), wsel),
                pl.BlockSpec((1, MOE_FF, d), wsel),
            ],
            out_specs=pl.BlockSpec((MOE_BLOCK, d), lambda i, be, nu: (i, 0)),
        ),
        out_shape=jax.ShapeDtypeStruct((rows, d), F32),
        compiler_params=_cparams("arbitrary"),
        name="moe_experts",
    )(block_eid, n_used, xs, w1, w3, w2)


def _combine_kernel(dest_ref, gate_ref, h_ref, g_ref, ys_ref, o_ref, y_ref, sem, *, final_norm):
    tm = h_ref.shape[0]

    def issue(t, carry):
        for k in range(2):
            pltpu.make_async_copy(ys_ref.at[pl.ds(dest_ref[0, k, t], 1)], y_ref.at[k, pl.ds(t, 1)], sem).start()
        return carry

    lax.fori_loop(0, tm, issue, 0)
    for k in range(2):
        pltpu.make_async_copy(ys_ref.at[pl.ds(0, tm)], y_ref.at[k], sem).wait()
    gate = gate_ref[...]
    out = h_ref[...] + (y_ref[0] * gate[:, 0:1] + y_ref[1] * gate[:, 1:2])
    if final_norm:
        out = _rms(out, g_ref[...])
    o_ref[...] = out


def _combine(dest, gate_rows, h, gain, ys, final_norm):
    t, d = h.shape
    tm = TOKEN_TILE
    return pl.pallas_call(
        functools.partial(_combine_kernel, final_norm=final_norm),
        grid=(t // tm,),
        in_specs=[
            pl.BlockSpec((1, 2, tm), lambda i: (i, 0, 0), memory_space=pltpu.SMEM),
            pl.BlockSpec((tm, 2), lambda i: (i, 0)),
            pl.BlockSpec((tm, d), lambda i: (i, 0)),
            pl.BlockSpec((1, d), lambda i: (0, 0)),
            pl.BlockSpec(memory_space=pl.ANY),
        ],
        out_specs=pl.BlockSpec((tm, d), lambda i: (i, 0)),
        out_shape=jax.ShapeDtypeStruct((t, d), F32),
        scratch_shapes=[pltpu.VMEM((2, tm, d), F32), pltpu.SemaphoreType.DMA(())],
        compiler_params=_cparams("arbitrary"),
        name="moe_combine",
    )(dest, gate_rows, h, gain, ys)


def _moe_layer(h, y_mix, w_out, norm_gain, wg, bg, we, be, w1, w3, w2, out_gain, final_norm):
    t, d = h.shape
    tm = TOKEN_TILE
    wr_t = jnp.concatenate([wg.T, we.reshape(d, MOE_EXPERTS).T,
                            jnp.zeros((ROUTER_ROWS - MOE_GROUPS - MOE_EXPERTS, d), F32)], axis=0)
    br = jnp.concatenate([bg, be.reshape(-1), jnp.zeros((ROUTER_ROWS - MOE_GROUPS - MOE_EXPERTS,), F32)])
    h, logits_t = _out_proj(h, y_mix, w_out.astype(BF16), norm_gain[None, :], wr_t, br[:, None])
    eid, gate, rank, counts = _route(logits_t)

    counts = counts[:, 0]
    padded = (counts + MOE_BLOCK - 1) // MOE_BLOCK * MOE_BLOCK
    pend = jnp.cumsum(padded)
    pstart = pend - padded
    onehot = eid[None] == jnp.arange(MOE_EXPERTS, dtype=I32)[:, None, None]
    dest = rank + jnp.sum(jnp.where(onehot, pstart[:, None, None], 0), axis=0)
    dest = dest.reshape(2, t // tm, tm).transpose(1, 0, 2)
    n_blocks = -(-(2 * t) // MOE_BLOCK) + MOE_EXPERTS
    block_start = jnp.arange(n_blocks, dtype=I32) * MOE_BLOCK
    block_eid = jnp.minimum(jnp.sum((pend[None, :] <= block_start[:, None]).astype(I32), axis=1), MOE_EXPERTS - 1)
    n_used = (pend[-1:] // MOE_BLOCK).astype(I32)

    xs = _dispatch(jnp.concatenate([pend, padded]).astype(I32), dest, h, norm_gain[None, :], n_blocks * MOE_BLOCK)
    ys = _experts(block_eid, n_used, xs, w1.astype(BF16), w3.astype(BF16), w2.astype(BF16))
    return _combine(dest, gate.T, h, out_gain[None, :], ys, final_norm)


def kernel(x, meta, norm_mix, norm_ffn, norm_final, ret_w_in, ret_w_out, ret_norm, ssm_w_in, ssm_conv_w,
           ssm_conv_b, ssm_dt_bias, ssm_a_log, ssm_d, ssm_norm, ssm_w_out, moe_wg, moe_bg, moe_we, moe_be,
           moe_w1, moe_w3, moe_w2):
    b, seq, d = x.shape
    lp = CHUNK + seq
    t = b * lp
    depth = norm_mix.shape[0]
    h = jnp.concatenate([jnp.zeros((b, PAD, d), F32), jnp.broadcast_to(meta[None], (b, N_META, d)), x], axis=1)

    half = RET_QK_DIM // 2
    inv_freq = 1.0 / (RET_ROPE_BASE ** jnp.linspace(0.0, 1.0, half, dtype=F32))
    ang = (jnp.arange(lp, dtype=F32) - PAD)[:, None] * inv_freq[None, :]
    cos, sin = jnp.cos(ang), jnp.sin(ang)
    perm = jnp.concatenate([jnp.arange(0, RET_QK_DIM, 2), jnp.arange(1, RET_QK_DIM, 2)])
    qk_perm = (jnp.arange(2 * RET_HEADS)[:, None] * RET_QK_DIM + perm[None, :]).reshape(-1)

    for i in range(depth):
        j = i // 2
        if i % 2 == 0:
            w_in = jnp.concatenate([ret_w_in[j][:, qk_perm], ret_w_in[j][:, 2 * RET_QK:]], axis=1).astype(BF16)
            qkvg = _ret_in(h.reshape(b, lp, d), norm_mix[i][None, :], cos, sin, w_in)
            y = _retention(qkvg, ret_norm[j][None, :])
            w_out = ret_w_out[j]
        else:
            w_in = jnp.concatenate([ssm_w_in[j], jnp.zeros((d, LANES - SSM_HEADS), F32)], axis=1).astype(BF16)
            zxbc, dt = _ssm_in(h.reshape(t, d), norm_mix[i][None, :], w_in, ssm_conv_w[j], ssm_conv_b[j][None, :], lp)
            lane_pad = lambda v: jnp.concatenate([v, jnp.zeros((LANES - SSM_HEADS,), F32)])[None, :]
            y = _ssd(zxbc.reshape(b, lp, SSM_MAIN), dt.reshape(b, lp, LANES), lane_pad(ssm_dt_bias[j]),
                     lane_pad(ssm_a_log[j]), jnp.repeat(ssm_d[j], SSM_HEAD_DIM)[None, :], ssm_norm[j][None, :])
            w_out = ssm_w_out[j]
        last = i == depth - 1
        h = _moe_layer(h.reshape(t, d), y.reshape(t, -1), w_out, norm_ffn[i], moe_wg[i], moe_bg[i], moe_we[i],
                       moe_be[i], moe_w1[i], moe_w3[i], moe_w2[i], norm_final if last else norm_ffn[i], last)
    return h.reshape(b, lp, d)[:, CHUNK:]
```

```python
import functools
import math

import jax
import jax.numpy as jnp
from jax import lax
from jax.experimental import pallas as pl
from jax.experimental.pallas import tpu as pltpu

F32 = jnp.float32
BF16 = jnp.bfloat16
I32 = jnp.int32

D_MODEL = 1024
N_META = 16
CHUNK = 128
PAD = CHUNK - N_META
EPS = 1e-6

RET_HEADS = 4
RET_QK_DIM = 256
RET_V_DIM = 512
RET_QK = RET_HEADS * RET_QK_DIM
RET_V = RET_HEADS * RET_V_DIM
RET_IN = 2 * RET_QK + 2 * RET_V
RET_ROPE_BASE = 10000.0

SSM_INNER = 2048
SSM_HEAD_DIM = 64
SSM_HEADS = 32
SSM_GROUPS = 8
SSM_HPG = 4
SSM_STATE = 128
SSM_CONV = 4
SSM_BC = SSM_GROUPS * SSM_STATE
SSM_CONV_DIM = SSM_INNER + 2 * SSM_BC
SSM_MAIN = SSM_INNER + SSM_CONV_DIM
LANES = 128
SSM_GROUP_W = SSM_HPG * SSM_HEAD_DIM

MOE_GROUPS = 4
MOE_EPG = 8
MOE_EXPERTS = 32
MOE_FF = 512
MOE_BLOCK = 256
ROUTER_ROWS = 40

TOKEN_TILE = 512
VMEM_LIMIT = 56 * 1024 * 1024


def _cparams(*sem):
    return pltpu.CompilerParams(dimension_semantics=sem, vmem_limit_bytes=VMEM_LIMIT)


def _rms(x, gain):
    ms = jnp.mean(x * x, axis=-1, keepdims=True)
    return x * lax.rsqrt(ms + EPS) * gain


def _silu(x):
    hx = 0.5 * x
    return hx + hx * jnp.tanh(hx)


def _dot(a, b):
    return jnp.dot(a, b, preferred_element_type=F32)


def _dot_nt(a, b, precision=None):
    return lax.dot_general(a, b, (((1,), (1,)), ((), ())), preferred_element_type=F32, precision=precision)


def _dot_tn(a, b):
    return lax.dot_general(a, b, (((0,), (0,)), ((), ())), preferred_element_type=F32)


def _row_tile(rows, cap=544):
    best = 16
    for t in range(16, cap + 1, 16):
        if rows % t == 0:
            best = t
    return best


def _ret_in_kernel(h_ref, g_ref, cos_ref, sin_ref, w_ref, o_ref):
    xn = _rms(h_ref[0], g_ref[...]).astype(BF16)
    cos = cos_ref[...]
    sin = sin_ref[...]
    half = RET_QK_DIM // 2
    for j in range(2 * RET_HEADS):
        c0 = j * RET_QK_DIM
        acc = _dot(xn, w_ref[:, c0:c0 + RET_QK_DIM])
        if j >= RET_HEADS:
            acc = acc * (RET_QK_DIM ** -0.5)
        e = acc[:, :half]
        o = acc[:, half:]
        o_ref[0, :, c0:c0 + half] = (e * cos - o * sin).astype(BF16)
        o_ref[0, :, c0 + half:c0 + RET_QK_DIM] = (o * cos + e * sin).astype(BF16)
    step = 1024
    for c0 in range(2 * RET_QK, RET_IN, step):
        o_ref[0, :, c0:c0 + step] = _dot(xn, w_ref[:, c0:c0 + step]).astype(BF16)


def _ret_in(h, gain, cos, sin, w):
    b, lp, d = h.shape
    tm = _row_tile(lp)
    return pl.pallas_call(
        _ret_in_kernel,
        grid=(b, lp // tm),
        in_specs=[
            pl.BlockSpec((1, tm, d), lambda i, j: (i, j, 0)),
            pl.BlockSpec((1, d), lambda i, j: (0, 0)),
            pl.BlockSpec((tm, RET_QK_DIM // 2), lambda i, j: (j, 0)),
            pl.BlockSpec((tm, RET_QK_DIM // 2), lambda i, j: (j, 0)),
            pl.BlockSpec((d, RET_IN), lambda i, j: (0, 0)),
        ],
        out_specs=pl.BlockSpec((1, tm, RET_IN), lambda i, j: (i, j, 0)),
        out_shape=jax.ShapeDtypeStruct((b, lp, RET_IN), BF16),
        compiler_params=_cparams("parallel", "parallel"),
        name="ret_in_proj",
    )(h, gain, cos, sin, w)


def _ret_kernel(q_ref, k_ref, v_ref, g_ref, hn_ref, o_ref, state_ref):
    @pl.when(pl.program_id(1) == 0)
    def _():
        state_ref[...] = jnp.zeros_like(state_ref)

    row = lax.broadcasted_iota(I32, (CHUNK, CHUNK), 0).astype(F32)
    col = lax.broadcasted_iota(I32, (CHUNK, CHUNK), 1).astype(F32)
    diff = row - col
    ridx = row[:, :1]
    for hh in range(RET_HEADS):
        lg = math.log(1.0 - 2.0 ** (-5.0 - hh))
        intra = jnp.where(diff >= 0, jnp.exp(lg * jnp.maximum(diff, 0.0)), 0.0)
        qdec = jnp.exp(lg * (ridx + 1.0))
        kdec = jnp.exp(lg * (CHUNK - 1.0 - ridx))
        cdec = math.exp(lg * CHUNK)
        qh = q_ref[0, :, hh * RET_QK_DIM:(hh + 1) * RET_QK_DIM]
        kh = k_ref[0, :, hh * RET_QK_DIM:(hh + 1) * RET_QK_DIM]
        vh = v_ref[0, :, hh * RET_V_DIM:(hh + 1) * RET_V_DIM]
        scores = _dot_nt(qh, kh) * intra
        st = state_ref[hh]
        y = _dot(scores.astype(BF16), vh) + _dot(qh, st.astype(BF16)) * qdec
        kd = (kh.astype(F32) * kdec).astype(BF16)
        state_ref[hh] = st * cdec + _dot_tn(kd, vh)
        yn = _rms(y, hn_ref[:, hh * RET_V_DIM:(hh + 1) * RET_V_DIM])
        gh = g_ref[0, :, hh * RET_V_DIM:(hh + 1) * RET_V_DIM].astype(F32)
        o_ref[0, :, hh * RET_V_DIM:(hh + 1) * RET_V_DIM] = (_silu(gh) * yn).astype(BF16)


def _retention(qkvg, head_norm):
    b, lp, _ = qkvg.shape
    nc = lp // CHUNK
    return pl.pallas_call(
        _ret_kernel,
        grid=(b, nc),
        in_specs=[
            pl.BlockSpec((1, CHUNK, RET_QK), lambda i, c: (i, c, 0)),
            pl.BlockSpec((1, CHUNK, RET_QK), lambda i, c: (i, c, 1)),
            pl.BlockSpec((1, CHUNK, RET_V), lambda i, c: (i, c, 1)),
            pl.BlockSpec((1, CHUNK, RET_V), lambda i, c: (i, c, 2)),
            pl.BlockSpec((1, RET_V), lambda i, c: (0, 0)),
        ],
        out_specs=pl.BlockSpec((1, CHUNK, RET_V), lambda i, c: (i, c, 0)),
        out_shape=jax.ShapeDtypeStruct((b, lp, RET_V), BF16),
        scratch_shapes=[pltpu.VMEM((RET_HEADS, RET_QK_DIM, RET_V_DIM), F32)],
        compiler_params=_cparams("parallel", "arbitrary"),
        name="retention",
    )(qkvg, qkvg, qkvg, qkvg, head_norm)


def _ssm_in_kernel(h_ref, g_ref, w_ref, cw_ref, cb_ref, o_ref, dt_ref, tail_ref, *, rows_per_seq):
    i = pl.program_id(0)
    tm = h_ref.shape[0]

    @pl.when(i == 0)
    def _():
        tail_ref[...] = jnp.zeros_like(tail_ref)

    xn = _rms(h_ref[...], g_ref[...]).astype(BF16)
    step = 1024
    for c0 in range(0, SSM_INNER, step):
        o_ref[:, c0:c0 + step] = _silu(_dot(xn, w_ref[:, c0:c0 + step])).astype(BF16)

    pos = lax.rem(i * tm, rows_per_seq) + lax.broadcasted_iota(I32, (tm, 1), 0)
    is_pad = jnp.logical_or(pos < PAD, jnp.logical_and(pos >= rows_per_seq, pos < rows_per_seq + PAD))
    row8 = lax.broadcasted_iota(I32, (8, 1), 0)
    for c0 in range(0, SSM_CONV_DIM, step):
        pre = jnp.where(is_pad, 0.0, _dot(xn, w_ref[:, SSM_INNER + c0:SSM_INNER + c0 + step]))
        tail = tail_ref[:, c0:c0 + step]
        tail_ref[:, c0:c0 + step] = pre[tm - 8:, :]
        conv = cb_ref[:, c0:c0 + step] + cw_ref[SSM_CONV - 1:SSM_CONV, c0:c0 + step] * pre
        for k in range(1, SSM_CONV):
            rolled = pltpu.roll(pre, k, axis=0)
            top = jnp.where(row8 < k, pltpu.roll(tail, k, axis=0), rolled[:8])
            shifted = jnp.concatenate([top, rolled[8:]], axis=0)
            conv = conv + cw_ref[SSM_CONV - 1 - k:SSM_CONV - k, c0:c0 + step] * shifted
        o_ref[:, SSM_INNER + c0:SSM_INNER + c0 + step] = _silu(conv).astype(BF16)
    dt_ref[...] = _dot(xn, w_ref[:, SSM_MAIN:SSM_MAIN + LANES])


def _ssm_in(h, gain, w, conv_w, conv_b, rows_per_seq):
    t, d = h.shape
    tm = TOKEN_TILE
    assert tm <= rows_per_seq
    const = lambda r, c: pl.BlockSpec((r, c), lambda i: (0, 0))
    return pl.pallas_call(
        functools.partial(_ssm_in_kernel, rows_per_seq=rows_per_seq),
        grid=(t // tm,),
        in_specs=[
            pl.BlockSpec((tm, d), lambda i: (i, 0)),
            const(1, d), const(d, SSM_MAIN + LANES), const(SSM_CONV, SSM_CONV_DIM), const(1, SSM_CONV_DIM),
        ],
        out_specs=[
            pl.BlockSpec((tm, SSM_MAIN), lambda i: (i, 0)),
            pl.BlockSpec((tm, LANES), lambda i: (i, 0)),
        ],
        out_shape=[jax.ShapeDtypeStruct((t, SSM_MAIN), BF16), jax.ShapeDtypeStruct((t, LANES), F32)],
        scratch_shapes=[pltpu.VMEM((8, SSM_CONV_DIM), F32)],
        compiler_params=_cparams("arbitrary"),
        name="ssm_in_proj",
    )(h, gain, w, conv_w, conv_b)


def _ssd_kernel(zg_ref, x_ref, bc_ref, dt_ref, dtb_ref, alog_ref, dsk_ref, gn_ref, o_ref, state_ref):
    c = pl.program_id(1)

    @pl.when(c == 0)
    def _():
        state_ref[...] = jnp.zeros_like(state_ref)

    row = lax.broadcasted_iota(I32, (CHUNK, CHUNK), 0)
    col = lax.broadcasted_iota(I32, (CHUNK, CHUNK), 1)
    causal_bias = jnp.where(row >= col, 0.0, -jnp.inf)
    low_half = col < SSM_HEAD_DIM
    valid = row[:, :1] >= jnp.where(c > 0, 0, PAD)

    dtr = dt_ref[0] + dtb_ref[...]
    dtv = jnp.maximum(dtr, 0.0) + jnp.log(1.0 + jnp.exp(-jnp.abs(dtr)))
    dtv = jnp.where(valid, dtv, 0.0)
    da = dtv * (-jnp.exp(alog_ref[...]))
    tril = jnp.where(row >= col, 1.0, 0.0)
    acs = jnp.dot(tril, da, preferred_element_type=F32, precision=lax.Precision.HIGHEST)
    last = acs[CHUNK - 1:CHUNK, :]
    eacs = jnp.exp(acs)
    wcol = dtv * jnp.exp(last - acs)
    elast = jnp.broadcast_to(jnp.exp(last), (CHUNK, LANES))
    acs_t = acs.T
    dt_t = dtv.T

    def per_head(v, h0):
        return jnp.where(low_half, v[:, h0:h0 + 1], v[:, h0 + 1:h0 + 2])

    for gi in range(SSM_GROUPS):
        heads = [gi * SSM_HPG + 2 * pp for pp in range(SSM_HPG // 2)]
        gs = slice(gi * SSM_GROUP_W, (gi + 1) * SSM_GROUP_W)
        xg = x_ref[0, :, gs].astype(F32)
        bg = bc_ref[0, :, gi * SSM_STATE:(gi + 1) * SSM_STATE]
        cg = bc_ref[0, :, SSM_BC + gi * SSM_STATE:SSM_BC + (gi + 1) * SSM_STATE]
        cb = _dot_nt(cg, bg)
        st = state_ref[gi]
        yoff = _dot(cg, st.astype(BF16))
        xw = xg * jnp.concatenate([per_head(wcol, h0) for h0 in heads], axis=1)
        state_ref[gi] = st * jnp.concatenate([per_head(elast, h0) for h0 in heads], axis=1) + _dot_tn(
            bg, xw.astype(BF16))
        y_pairs = []
        for pp, h0 in enumerate(heads):
            xp = xg[:, pp * LANES:(pp + 1) * LANES]
            rhs = jnp.concatenate([jnp.where(low_half, xp, 0.0), jnp.where(low_half, 0.0, xp)], axis=0)
            ms = []
            for hd in (h0, h0 + 1):
                seg = acs[:, hd:hd + 1] - acs_t[hd:hd + 1, :]
                ms.append(cb * jnp.exp(seg + causal_bias) * dt_t[hd:hd + 1, :])
            y_pairs.append(_dot(jnp.concatenate(ms, axis=1).astype(BF16), rhs.astype(BF16)))
        y = jnp.concatenate(y_pairs, axis=1)
        y = y + yoff * jnp.concatenate([per_head(eacs, h0) for h0 in heads], axis=1)
        y = y + dsk_ref[:, gs] * xg
        y = y * zg_ref[0, :, gs].astype(F32)
        o_ref[0, :, gs] = _rms(y, gn_ref[:, gs]).astype(BF16)


def _ssd(zxbc, dt, dt_bias, a_log, d_skip, gate_norm):
    b, lp, _ = zxbc.shape
    nc = lp // CHUNK
    wide = lambda j: pl.BlockSpec((1, CHUNK, SSM_INNER), lambda i, c: (i, c, j))
    const = lambda r, w: pl.BlockSpec((r, w), lambda i, c: (0, 0))
    return pl.pallas_call(
        _ssd_kernel,
        grid=(b, nc),
        in_specs=[
            wide(0), wide(1), wide(2),
            pl.BlockSpec((1, CHUNK, LANES), lambda i, c: (i, c, 0)),
            const(1, LANES), const(1, LANES), const(1, SSM_INNER), const(1, SSM_INNER),
        ],
        out_specs=pl.BlockSpec((1, CHUNK, SSM_INNER), lambda i, c: (i, c, 0)),
        out_shape=jax.ShapeDtypeStruct((b, lp, SSM_INNER), BF16),
        scratch_shapes=[pltpu.VMEM((SSM_GROUPS, SSM_STATE, SSM_GROUP_W), F32)],
        compiler_params=_cparams("parallel", "arbitrary"),
        name="ssd",
    )(zxbc, zxbc, zxbc, dt, dt_bias, a_log, d_skip, gate_norm)


def _out_proj_kernel(h_ref, y_ref, w_ref, g_ref, wr_ref, br_ref, ho_ref, lg_ref):
    hn = h_ref[...] + _dot(y_ref[...], w_ref[...])
    ho_ref[...] = hn
    u = _rms(hn, g_ref[...])
    lg_ref[...] = _dot_nt(wr_ref[...], u, precision=lax.Precision.HIGHEST) + br_ref[...]


def _out_proj(h, y, w, gain, wr_t, br):
    t, d = h.shape
    tm = TOKEN_TILE
    kin = y.shape[1]
    return pl.pallas_call(
        _out_proj_kernel,
        grid=(t // tm,),
        in_specs=[
            pl.BlockSpec((tm, d), lambda i: (i, 0)),
            pl.BlockSpec((tm, kin), lambda i: (i, 0)),
            pl.BlockSpec((kin, d), lambda i: (0, 0)),
            pl.BlockSpec((1, d), lambda i: (0, 0)),
            pl.BlockSpec((ROUTER_ROWS, d), lambda i: (0, 0)),
            pl.BlockSpec((ROUTER_ROWS, 1), lambda i: (0, 0)),
        ],
        out_specs=[
            pl.BlockSpec((tm, d), lambda i: (i, 0)),
            pl.BlockSpec((ROUTER_ROWS, tm), lambda i: (0, i)),
        ],
        out_shape=[jax.ShapeDtypeStruct((t, d), F32), jax.ShapeDtypeStruct((ROUTER_ROWS, t), F32)],
        compiler_params=_cparams("parallel"),
        name="out_proj_router",
    )(h, y, w, gain, wr_t, br)


def _first_argmax(v, n):
    ridx = lax.broadcasted_iota(I32, v.shape, 0).astype(F32)
    vmax = jnp.max(v, axis=0, keepdims=True)
    idx = jnp.min(jnp.where(v == vmax, ridx, float(n)), axis=0, keepdims=True)
    return vmax, idx.astype(I32)


def _route_kernel(lg_ref, eid_ref, gate_ref, rank_ref, cnt_ref, carry_ref):
    @pl.when(pl.program_id(0) == 0)
    def _():
        carry_ref[...] = jnp.zeros_like(carry_ref)

    logits = lg_ref[...]
    tm = logits.shape[1]
    gl = logits[0:MOE_GROUPS]
    gmax, gsel = _first_argmax(gl, MOE_GROUPS)
    p_group = 1.0 / jnp.sum(jnp.exp(gl - gmax), axis=0, keepdims=True)
    el = logits[MOE_GROUPS:MOE_GROUPS + MOE_EPG]
    for gg in range(1, MOE_GROUPS):
        el = jnp.where(gsel == gg, logits[MOE_GROUPS + gg * MOE_EPG:MOE_GROUPS + (gg + 1) * MOE_EPG], el)
    ex = jnp.exp(el - jnp.max(el, axis=0, keepdims=True))
    p = ex / jnp.sum(ex, axis=0, keepdims=True)
    p1, i1 = _first_argmax(p, MOE_EPG)
    ridx8 = lax.broadcasted_iota(I32, p.shape, 0)
    p2, i2 = _first_argmax(jnp.where(ridx8 == i1, -1.0, p), MOE_EPG)
    denom = p1 + p2
    e0 = gsel * MOE_EPG + i1
    e1 = gsel * MOE_EPG + i2
    eid_ref[0:1, :] = e0
    eid_ref[1:2, :] = e1
    gate_ref[0:1, :] = p_group * p1 / denom
    gate_ref[1:2, :] = p_group * p2 / denom

    ridx = lax.broadcasted_iota(I32, (MOE_EXPERTS, tm), 0)
    oh0 = jnp.where(ridx == e0, 1.0, 0.0)
    oh1 = jnp.where(ridx == e1, 1.0, 0.0)
    before = jnp.where(lax.broadcasted_iota(I32, (tm, tm), 0) < lax.broadcasted_iota(I32, (tm, tm), 1), 1.0, 0.0)
    before = before.astype(BF16)
    pref0 = _dot(oh0.astype(BF16), before)
    pref1 = _dot(oh1.astype(BF16), before)
    tot0 = jnp.sum(oh0, axis=1, keepdims=True)
    tot1 = jnp.sum(oh1, axis=1, keepdims=True)
    carry = carry_ref[...]
    rank_ref[0:1, :] = jnp.sum(oh0 * (carry + pref0), axis=0, keepdims=True).astype(I32)
    rank_ref[1:2, :] = jnp.sum(oh1 * (carry + tot0 + pref1), axis=0, keepdims=True).astype(I32)
    carry = carry + tot0 + tot1
    carry_ref[...] = carry
    cnt_ref[...] = jnp.broadcast_to(carry, cnt_ref.shape).astype(I32)


def _route(logits_t):
    t = logits_t.shape[1]
    tm = TOKEN_TILE
    pair = lambda: pl.BlockSpec((2, tm), lambda i: (0, i))
    return pl.pallas_call(
        _route_kernel,
        grid=(t // tm,),
        in_specs=[pl.BlockSpec((ROUTER_ROWS, tm), lambda i: (0, i))],
        out_specs=[pair(), pair(), pair(), pl.BlockSpec((MOE_EXPERTS, LANES), lambda i: (0, 0))],
        out_shape=[jax.ShapeDtypeStruct((2, t), I32), jax.ShapeDtypeStruct((2, t), F32),
                   jax.ShapeDtypeStruct((2, t), I32), jax.ShapeDtypeStruct((MOE_EXPERTS, LANES), I32)],
        scratch_shapes=[pltpu.VMEM((MOE_EXPERTS, 1), F32)],
        compiler_params=_cparams("arbitrary"),
        name="route_rank",
    )(logits_t)


def _dispatch_kernel(seg_ref, dest_ref, h_ref, g_ref, xs_ref, u_ref, zero_ref, sem):
    tm = u_ref.shape[0]

    @pl.when(pl.program_id(0) == 0)
    def _():
        zero_ref[...] = jnp.zeros_like(zero_ref)

        def fill_block(start):
            fill = pltpu.make_async_copy(zero_ref, xs_ref.at[pl.ds(pl.multiple_of(start, MOE_BLOCK), MOE_BLOCK)], sem)
            fill.start()
            fill.wait()

        for e in range(MOE_EXPERTS):
            @pl.when(seg_ref[MOE_EXPERTS + e] > 0)
            def _():
                fill_block(seg_ref[e] - MOE_BLOCK)

        def fill_tail(blk, carry):
            fill_block(blk * MOE_BLOCK)
            return carry

        lax.fori_loop(seg_ref[MOE_EXPERTS - 1] // MOE_BLOCK, xs_ref.shape[0] // MOE_BLOCK, fill_tail, 0)

    u_ref[...] = _rms(h_ref[...], g_ref[...]).reshape(u_ref.shape)

    def issue(t, carry):
        for k in range(2):
            pltpu.make_async_copy(u_ref.at[t], xs_ref.at[dest_ref[0, k, t]], sem).start()
        return carry

    lax.fori_loop(0, tm, issue, 0)
    for _ in range(2):
        pltpu.make_async_copy(u_ref, xs_ref.at[pl.ds(0, tm)], sem).wait()


def _dispatch(seg, dest, h, gain, rows):
    t, d = h.shape
    tm = TOKEN_TILE
    return pl.pallas_call(
        _dispatch_kernel,
        grid_spec=pltpu.PrefetchScalarGridSpec(
            num_scalar_prefetch=1,
            grid=(t // tm,),
            in_specs=[
                pl.BlockSpec((1, 2, tm), lambda i, seg: (i, 0, 0), memory_space=pltpu.SMEM),
                pl.BlockSpec((tm, d), lambda i, seg: (i, 0)),
                pl.BlockSpec((1, d), lambda i, seg: (0, 0)),
            ],
            out_specs=pl.BlockSpec(memory_space=pl.ANY),
            scratch_shapes=[pltpu.VMEM((tm, d // LANES, LANES), F32), pltpu.VMEM((MOE_BLOCK, d // LANES, LANES), F32),
                            pltpu.SemaphoreType.DMA(())],
        ),
        out_shape=jax.ShapeDtypeStruct((rows, d // LANES, LANES), F32),
        compiler_params=_cparams("arbitrary"),
        name="moe_dispatch",
    )(seg, dest, h, gain)


def _expert_kernel(be_ref, nu_ref, xs_ref, w1_ref, w3_ref, w2_ref, ys_ref):
    used = pl.program_id(0) < nu_ref[0]

    @pl.when(used)
    def _():
        rows, sub, lanes = xs_ref.shape
        x = xs_ref[...].reshape(rows, sub * lanes).astype(BF16)
        a = _dot(x, w1_ref[0])
        b = _dot(x, w3_ref[0])
        ys_ref[...] = _dot((_silu(a) * b).astype(BF16), w2_ref[0]).reshape(rows, sub, lanes)

    @pl.when(jnp.logical_not(used))
    def _():
        ys_ref[...] = jnp.zeros_like(ys_ref)


def _experts(block_eid, n_used, xs, w1, w3, w2):
    rows, sub, lanes = xs.shape
    d = sub * lanes
    nb = rows // MOE_BLOCK
    blk = lambda i, be, nu: (jnp.minimum(i, nu[0] - 1), 0, 0)
    wsel = lambda i, be, nu: (be[i], 0, 0)
    return pl.pallas_call(
        _expert_kernel,
        grid_spec=pltpu.PrefetchScalarGridSpec(
            num_scalar_prefetch=2,
            grid=(nb,),
            in_specs=[
                pl.BlockSpec((MOE_BLOCK, sub, lanes), blk),
                pl.BlockSpec((1, d, MOE_FF), wsel),
                pl.BlockSpec((1, d, MOE_FF), wsel),
                pl.BlockSpec((1, MOE_FF, d), wsel),
            ],
            out_specs=pl.BlockSpec((MOE_BLOCK, sub, lanes), lambda i, be, nu: (i, 0, 0)),
        ),
        out_shape=jax.ShapeDtypeStruct((rows, sub, lanes), F32),
        compiler_params=_cparams("arbitrary"),
        name="moe_experts",
    )(block_eid, n_used, xs, w1, w3, w2)


def _combine_kernel(dest_ref, gate_ref, h_ref, g_ref, ys_ref, o_ref, y_ref, sem, *, final_norm):
    tm = h_ref.shape[0]

    def issue(t, carry):
        for k in range(2):
            pltpu.make_async_copy(ys_ref.at[dest_ref[0, k, t]], y_ref.at[k, t], sem).start()
        return carry

    lax.fori_loop(0, tm, issue, 0)
    for k in range(2):
        pltpu.make_async_copy(ys_ref.at[pl.ds(0, tm)], y_ref.at[k], sem).wait()
    gate = gate_ref[...]
    y0 = y_ref[0].reshape(h_ref.shape)
    y1 = y_ref[1].reshape(h_ref.shape)
    out = h_ref[...] + (y0 * gate[:, 0:1] + y1 * gate[:, 1:2])
    if final_norm:
        out = _rms(out, g_ref[...])
    o_ref[...] = out


def _combine(dest, gate_rows, h, gain, ys, final_norm):
    t, d = h.shape
    tm = TOKEN_TILE
    return pl.pallas_call(
        functools.partial(_combine_kernel, final_norm=final_norm),
        grid=(t // tm,),
        in_specs=[
            pl.BlockSpec((1, 2, tm), lambda i: (i, 0, 0), memory_space=pltpu.SMEM),
            pl.BlockSpec((tm, 2), lambda i: (i, 0)),
            pl.BlockSpec((tm, d), lambda i: (i, 0)),
            pl.BlockSpec((1, d), lambda i: (0, 0)),
            pl.BlockSpec(memory_space=pl.ANY),
        ],
        out_specs=pl.BlockSpec((tm, d), lambda i: (i, 0)),
        out_shape=jax.ShapeDtypeStruct((t, d), F32),
        scratch_shapes=[pltpu.VMEM((2, tm, d // LANES, LANES), F32), pltpu.SemaphoreType.DMA(())],
        compiler_params=_cparams("arbitrary"),
        name="moe_combine",
    )(dest, gate_rows, h, gain, ys)


def _moe_layer(h, y_mix, w_out, norm_gain, wg, bg, we, be, w1, w3, w2, out_gain, final_norm):
    t, d = h.shape
    tm = TOKEN_TILE
    wr_t = jnp.concatenate([wg.T, we.reshape(d, MOE_EXPERTS).T,
                            jnp.zeros((ROUTER_ROWS - MOE_GROUPS - MOE_EXPERTS, d), F32)], axis=0)
    br = jnp.concatenate([bg, be.reshape(-1), jnp.zeros((ROUTER_ROWS - MOE_GROUPS - MOE_EXPERTS,), F32)])
    h, logits_t = _out_proj(h, y_mix, w_out.astype(BF16), norm_gain[None, :], wr_t, br[:, None])
    eid, gate, rank, counts = _route(logits_t)

    counts = counts[:, 0]
    padded = (counts + MOE_BLOCK - 1) // MOE_BLOCK * MOE_BLOCK
    pend = jnp.cumsum(padded)
    pstart = pend - padded
    onehot = eid[None] == jnp.arange(MOE_EXPERTS, dtype=I32)[:, None, None]
    dest = rank + jnp.sum(jnp.where(onehot, pstart[:, None, None], 0), axis=0)
    dest = dest.reshape(2, t // tm, tm).transpose(1, 0, 2)
    n_blocks = -(-(2 * t) // MOE_BLOCK) + MOE_EXPERTS
    block_start = jnp.arange(n_blocks, dtype=I32) * MOE_BLOCK
    block_eid = jnp.minimum(jnp.sum((pend[None, :] <= block_start[:, None]).astype(I32), axis=1), MOE_EXPERTS - 1)
    n_used = (pend[-1:] // MOE_BLOCK).astype(I32)

    xs = _dispatch(jnp.concatenate([pend, padded]).astype(I32), dest, h, norm_gain[None, :], n_blocks * MOE_BLOCK)
    ys = _experts(block_eid, n_used, xs, w1.astype(BF16), w3.astype(BF16), w2.astype(BF16))
    return _combine(dest, gate.T, h, out_gain[None, :], ys, final_norm)


def kernel(x, meta, norm_mix, norm_ffn, norm_final, ret_w_in, ret_w_out, ret_norm, ssm_w_in, ssm_conv_w,
           ssm_conv_b, ssm_dt_bias, ssm_a_log, ssm_d, ssm_norm, ssm_w_out, moe_wg, moe_bg, moe_we, moe_be,
           moe_w1, moe_w3, moe_w2):
    b, seq, d = x.shape
    lp = CHUNK + seq
    t = b * lp
    depth = norm_mix.shape[0]
    h = jnp.concatenate([jnp.zeros((b, PAD, d), F32), jnp.broadcast_to(meta[None], (b, N_META, d)), x], axis=1)

    half = RET_QK_DIM // 2
    inv_freq = 1.0 / (RET_ROPE_BASE ** jnp.linspace(0.0, 1.0, half, dtype=F32))
    ang = (jnp.arange(lp, dtype=F32) - PAD)[:, None] * inv_freq[None, :]
    cos, sin = jnp.cos(ang), jnp.sin(ang)
    perm = jnp.concatenate([jnp.arange(0, RET_QK_DIM, 2), jnp.arange(1, RET_QK_DIM, 2)])
    qk_perm = (jnp.arange(2 * RET_HEADS)[:, None] * RET_QK_DIM + perm[None, :]).reshape(-1)

    for i in range(depth):
        j = i // 2
        if i % 2 == 0:
            w_in = jnp.concatenate([ret_w_in[j][:, qk_perm], ret_w_in[j][:, 2 * RET_QK:]], axis=1).astype(BF16)
            qkvg = _ret_in(h.reshape(b, lp, d), norm_mix[i][None, :], cos, sin, w_in)
            y = _retention(qkvg, ret_norm[j][None, :])
            w_out = ret_w_out[j]
        else:
            w_in = jnp.concatenate([ssm_w_in[j], jnp.zeros((d, LANES - SSM_HEADS), F32)], axis=1).astype(BF16)
            zxbc, dt = _ssm_in(h.reshape(t, d), norm_mix[i][None, :], w_in, ssm_conv_w[j], ssm_conv_b[j][None, :], lp)
            lane_pad = lambda v: jnp.concatenate([v, jnp.zeros((LANES - SSM_HEADS,), F32)])[None, :]
            y = _ssd(zxbc.reshape(b, lp, SSM_MAIN), dt.reshape(b, lp, LANES), lane_pad(ssm_dt_bias[j]),
                     lane_pad(ssm_a_log[j]), jnp.repeat(ssm_d[j], SSM_HEAD_DIM)[None, :], ssm_norm[j][None, :])
            w_out = ssm_w_out[j]
        last = i == depth - 1
        h = _moe_layer(h.reshape(t, d), y.reshape(t, -1), w_out, norm_ffn[i], moe_wg[i], moe_bg[i], moe_we[i],
                       moe_be[i], moe_w1[i], moe_w3[i], moe_w2[i], norm_final if last else norm_ffn[i], last)
    return h.reshape(b, lp, d)[:, CHUNK:]
```

```python
import functools
import math

import jax
import jax.numpy as jnp
from jax import lax
from jax.experimental import pallas as pl
from jax.experimental.pallas import tpu as pltpu

F32 = jnp.float32
BF16 = jnp.bfloat16
I32 = jnp.int32

D_MODEL = 1024
N_META = 16
CHUNK = 128
PAD = CHUNK - N_META
EPS = 1e-6

RET_HEADS = 4
RET_QK_DIM = 256
RET_V_DIM = 512
RET_QK = RET_HEADS * RET_QK_DIM
RET_V = RET_HEADS * RET_V_DIM
RET_IN = 2 * RET_QK + 2 * RET_V
RET_ROPE_BASE = 10000.0

SSM_INNER = 2048
SSM_HEAD_DIM = 64
SSM_HEADS = 32
SSM_GROUPS = 8
SSM_HPG = 4
SSM_STATE = 128
SSM_CONV = 4
SSM_BC = SSM_GROUPS * SSM_STATE
SSM_CONV_DIM = SSM_INNER + 2 * SSM_BC
SSM_MAIN = SSM_INNER + SSM_CONV_DIM
LANES = 128
SSM_GROUP_W = SSM_HPG * SSM_HEAD_DIM

MOE_GROUPS = 4
MOE_EPG = 8
MOE_EXPERTS = 32
MOE_FF = 512
MOE_BLOCK = 256
ROUTER_ROWS = 40

SSM_IN_STEP = 512
LOG2E = math.log2(math.e)
TOKEN_TILE = 512
VMEM_LIMIT = 56 * 1024 * 1024


def _cparams(*sem):
    return pltpu.CompilerParams(dimension_semantics=sem, vmem_limit_bytes=VMEM_LIMIT)


def _rms(x, gain):
    ms = jnp.mean(x * x, axis=-1, keepdims=True)
    return x * lax.rsqrt(ms + EPS) * gain


def _silu(x):
    hx = 0.5 * x
    return hx + hx * jnp.tanh(hx)


def _dot(a, b):
    return jnp.dot(a, b, preferred_element_type=F32)


def _dot_nt(a, b, precision=None):
    return lax.dot_general(a, b, (((1,), (1,)), ((), ())), preferred_element_type=F32, precision=precision)


def _dot_tn(a, b):
    return lax.dot_general(a, b, (((0,), (0,)), ((), ())), preferred_element_type=F32)


def _row_tile(rows, cap=544):
    best = 16
    for t in range(16, cap + 1, 16):
        if rows % t == 0:
            best = t
    return best


def _ret_in_kernel(h_ref, g_ref, cos_ref, sin_ref, w_ref, o_ref):
    xn = _rms(h_ref[0], g_ref[...]).astype(BF16)
    cos = cos_ref[...]
    sin = sin_ref[...]
    half = RET_QK_DIM // 2
    for j in range(2 * RET_HEADS):
        c0 = j * RET_QK_DIM
        acc = _dot(xn, w_ref[:, c0:c0 + RET_QK_DIM])
        if j >= RET_HEADS:
            acc = acc * (RET_QK_DIM ** -0.5)
        e = acc[:, :half]
        o = acc[:, half:]
        o_ref[0, :, c0:c0 + half] = (e * cos - o * sin).astype(BF16)
        o_ref[0, :, c0 + half:c0 + RET_QK_DIM] = (o * cos + e * sin).astype(BF16)
    step = 1024
    for c0 in range(2 * RET_QK, RET_IN, step):
        o_ref[0, :, c0:c0 + step] = _dot(xn, w_ref[:, c0:c0 + step]).astype(BF16)


def _ret_in(h, gain, cos, sin, w):
    b, lp, d = h.shape
    tm = _row_tile(lp)
    return pl.pallas_call(
        _ret_in_kernel,
        grid=(b, lp // tm),
        in_specs=[
            pl.BlockSpec((1, tm, d), lambda i, j: (i, j, 0)),
            pl.BlockSpec((1, d), lambda i, j: (0, 0)),
            pl.BlockSpec((tm, RET_QK_DIM // 2), lambda i, j: (j, 0)),
            pl.BlockSpec((tm, RET_QK_DIM // 2), lambda i, j: (j, 0)),
            pl.BlockSpec((d, RET_IN), lambda i, j: (0, 0)),
        ],
        out_specs=pl.BlockSpec((1, tm, RET_IN), lambda i, j: (i, j, 0)),
        out_shape=jax.ShapeDtypeStruct((b, lp, RET_IN), BF16),
        compiler_params=_cparams("parallel", "parallel"),
        name="ret_in_proj",
    )(h, gain, cos, sin, w)


def _ret_kernel(q_ref, k_ref, v_ref, g_ref, hn_ref, o_ref, state_ref):
    @pl.when(pl.program_id(1) == 0)
    def _():
        state_ref[...] = jnp.zeros_like(state_ref)

    row = lax.broadcasted_iota(I32, (CHUNK, CHUNK), 0).astype(F32)
    col = lax.broadcasted_iota(I32, (CHUNK, CHUNK), 1).astype(F32)
    diff = row - col
    ridx = row[:, :1]
    for hh in range(RET_HEADS):
        lg = math.log(1.0 - 2.0 ** (-5.0 - hh))
        intra = jnp.where(diff >= 0, jnp.exp(lg * jnp.maximum(diff, 0.0)), 0.0)
        qdec = jnp.exp(lg * (ridx + 1.0))
        kdec = jnp.exp(lg * (CHUNK - 1.0 - ridx))
        cdec = math.exp(lg * CHUNK)
        qh = q_ref[0, :, hh * RET_QK_DIM:(hh + 1) * RET_QK_DIM]
        kh = k_ref[0, :, hh * RET_QK_DIM:(hh + 1) * RET_QK_DIM]
        vh = v_ref[0, :, hh * RET_V_DIM:(hh + 1) * RET_V_DIM]
        scores = _dot_nt(qh, kh) * intra
        st = state_ref[hh]
        y = _dot(scores.astype(BF16), vh) + _dot(qh, st.astype(BF16)) * qdec
        kd = (kh.astype(F32) * kdec).astype(BF16)
        state_ref[hh] = st * cdec + _dot_tn(kd, vh)
        yn = _rms(y, hn_ref[:, hh * RET_V_DIM:(hh + 1) * RET_V_DIM])
        gh = g_ref[0, :, hh * RET_V_DIM:(hh + 1) * RET_V_DIM].astype(F32)
        o_ref[0, :, hh * RET_V_DIM:(hh + 1) * RET_V_DIM] = (_silu(gh) * yn).astype(BF16)


def _retention(qkvg, head_norm):
    b, lp, _ = qkvg.shape
    nc = lp // CHUNK
    return pl.pallas_call(
        _ret_kernel,
        grid=(b, nc),
        in_specs=[
            pl.BlockSpec((1, CHUNK, RET_QK), lambda i, c: (i, c, 0)),
            pl.BlockSpec((1, CHUNK, RET_QK), lambda i, c: (i, c, 1)),
            pl.BlockSpec((1, CHUNK, RET_V), lambda i, c: (i, c, 1)),
            pl.BlockSpec((1, CHUNK, RET_V), lambda i, c: (i, c, 2)),
            pl.BlockSpec((1, RET_V), lambda i, c: (0, 0)),
        ],
        out_specs=pl.BlockSpec((1, CHUNK, RET_V), lambda i, c: (i, c, 0)),
        out_shape=jax.ShapeDtypeStruct((b, lp, RET_V), BF16),
        scratch_shapes=[pltpu.VMEM((RET_HEADS, RET_QK_DIM, RET_V_DIM), F32)],
        compiler_params=_cparams("parallel", "arbitrary"),
        name="retention",
    )(qkvg, qkvg, qkvg, qkvg, head_norm)


def _ssm_in_kernel(h_ref, g_ref, w_ref, cw_ref, cb_ref, o_ref, dt_ref, tail_ref, acc_ref, *, rows_per_seq):
    i = pl.program_id(0)
    tm = h_ref.shape[0]

    @pl.when(i == 0)
    def _():
        tail_ref[...] = jnp.zeros_like(tail_ref)

    xn = _rms(h_ref[...], g_ref[...]).astype(BF16)
    step = acc_ref.shape[2]
    n_steps = SSM_MAIN // step

    pos = lax.rem(i * tm, rows_per_seq) + lax.broadcasted_iota(I32, (tm, 1), 0)
    is_pad = jnp.logical_or(pos < PAD, jnp.logical_and(pos >= rows_per_seq, pos < rows_per_seq + PAD))
    row8 = lax.broadcasted_iota(I32, (8, 1), 0)

    def project(j):
        acc_ref[j % 2] = _dot(xn, w_ref[:, j * step:(j + 1) * step])

    project(0)
    for j in range(n_steps):
        if j + 1 < n_steps:
            project(j + 1)
        c0 = j * step
        acc = acc_ref[j % 2]
        if c0 < SSM_INNER:
            o_ref[:, c0:c0 + step] = _silu(acc).astype(BF16)
            continue
        cc = c0 - SSM_INNER
        pre = jnp.where(is_pad, 0.0, acc)
        tail = tail_ref[:, cc:cc + step]
        tail_ref[:, cc:cc + step] = pre[tm - 8:, :]
        conv = cb_ref[:, cc:cc + step] + cw_ref[SSM_CONV - 1:SSM_CONV, cc:cc + step] * pre
        for k in range(1, SSM_CONV):
            rolled = pltpu.roll(pre, k, axis=0)
            top = jnp.where(row8 < k, pltpu.roll(tail, k, axis=0), rolled[:8])
            shifted = jnp.concatenate([top, rolled[8:]], axis=0)
            conv = conv + cw_ref[SSM_CONV - 1 - k:SSM_CONV - k, cc:cc + step] * shifted
        o_ref[:, c0:c0 + step] = _silu(conv).astype(BF16)
    dt_ref[...] = _dot(xn, w_ref[:, SSM_MAIN:SSM_MAIN + LANES])


def _ssm_in(h, gain, w, conv_w, conv_b, rows_per_seq):
    t, d = h.shape
    tm = TOKEN_TILE
    assert tm <= rows_per_seq
    const = lambda r, c: pl.BlockSpec((r, c), lambda i: (0, 0))
    return pl.pallas_call(
        functools.partial(_ssm_in_kernel, rows_per_seq=rows_per_seq),
        grid=(t // tm,),
        in_specs=[
            pl.BlockSpec((tm, d), lambda i: (i, 0)),
            const(1, d), const(d, SSM_MAIN + LANES), const(SSM_CONV, SSM_CONV_DIM), const(1, SSM_CONV_DIM),
        ],
        out_specs=[
            pl.BlockSpec((tm, SSM_MAIN), lambda i: (i, 0)),
            pl.BlockSpec((tm, LANES), lambda i: (i, 0)),
        ],
        out_shape=[jax.ShapeDtypeStruct((t, SSM_MAIN), BF16), jax.ShapeDtypeStruct((t, LANES), F32)],
        scratch_shapes=[pltpu.VMEM((8, SSM_CONV_DIM), F32), pltpu.VMEM((2, tm, SSM_IN_STEP), F32)],
        compiler_params=_cparams("arbitrary"),
        name="ssm_in_proj",
    )(h, gain, w, conv_w, conv_b)


def _ssd_kernel(zg_ref, x_ref, bc_ref, dt_ref, dtb_ref, alog_ref, dsk_ref, gn_ref, o_ref, state_ref):
    c = pl.program_id(1)

    @pl.when(c == 0)
    def _():
        state_ref[...] = jnp.zeros_like(state_ref)

    row = lax.broadcasted_iota(I32, (CHUNK, CHUNK), 0)
    col = lax.broadcasted_iota(I32, (CHUNK, CHUNK), 1)
    causal_bias = jnp.where(row >= col, 0.0, -jnp.inf)
    low_half = col < SSM_HEAD_DIM
    valid = row[:, :1] >= jnp.where(c > 0, 0, PAD)

    dtr = dt_ref[0] + dtb_ref[...]
    dtv = jnp.maximum(dtr, 0.0) + jnp.log(1.0 + jnp.exp(-jnp.abs(dtr)))
    dtv = jnp.where(valid, dtv, 0.0)
    da = dtv * (-jnp.exp(alog_ref[...]))
    tril = jnp.where(row >= col, 1.0, 0.0)
    acs = jnp.dot(tril, da, preferred_element_type=F32, precision=lax.Precision.HIGHEST)
    last = acs[CHUNK - 1:CHUNK, :]
    acs2 = acs * LOG2E
    src_t = (jnp.log2(dtv) - acs2).T
    w_t = (dtv * jnp.exp(last - acs)).T
    elast = jnp.exp(last)

    def pair_row(v, h0):
        return jnp.where(low_half[:1], v[:, h0:h0 + 1], v[:, h0 + 1:h0 + 2])

    groups = range(SSM_GROUPS)
    gsl = [slice(gi * SSM_GROUP_W, (gi + 1) * SSM_GROUP_W) for gi in groups]
    bgs = [bc_ref[0, :, gi * SSM_STATE:(gi + 1) * SSM_STATE] for gi in groups]
    cgs = [bc_ref[0, :, SSM_BC + gi * SSM_STATE:SSM_BC + (gi + 1) * SSM_STATE] for gi in groups]
    sts = [state_ref[gi] for gi in groups]
    cbs = [_dot_nt(cgs[gi], bgs[gi]) for gi in groups]
    yoffs = [_dot(cgs[gi], sts[gi].astype(BF16)) for gi in groups]
    bts = [bgs[gi].astype(F32).T for gi in groups]

    outs = []
    eexp = []
    for gi in groups:
        for pp in range(SSM_HPG // 2):
            h0 = gi * SSM_HPG + 2 * pp
            top, bot, ecols = [], [], []
            for hd in (h0, h0 + 1):
                acs_b = jnp.broadcast_to(acs2[:, hd:hd + 1], (CHUNK, CHUNK))
                top.append(cbs[gi] * jnp.exp2(acs_b + (src_t[hd:hd + 1, :] + causal_bias)))
                bot.append(bts[gi] * w_t[hd:hd + 1, :])
                ecols.append(jnp.exp2(acs_b))
            lhs = jnp.concatenate([jnp.concatenate(top, axis=1), jnp.concatenate(bot, axis=1)], axis=0)
            xp = x_ref[0, :, h0 * SSM_HEAD_DIM:(h0 + 2) * SSM_HEAD_DIM]
            zero = jnp.zeros_like(xp)
            rhs = jnp.concatenate([jnp.where(low_half, xp, zero), jnp.where(low_half, zero, xp)], axis=0)
            outs.append(_dot(lhs.astype(BF16), rhs))
            eexp.append(jnp.where(low_half, ecols[0], ecols[1]))

    for gi in groups:
        pairs = [gi * (SSM_HPG // 2) + pp for pp in range(SSM_HPG // 2)]
        heads = [gi * SSM_HPG + 2 * pp for pp in range(SSM_HPG // 2)]
        lexp = jnp.concatenate([pair_row(elast, h0) for h0 in heads], axis=1)
        state_ref[gi] = sts[gi] * lexp + jnp.concatenate([outs[p][CHUNK:] for p in pairs], axis=1)
        xg = x_ref[0, :, gsl[gi]].astype(F32)
        y = jnp.concatenate([outs[p][:CHUNK] for p in pairs], axis=1)
        y = y + yoffs[gi] * jnp.concatenate([eexp[p] for p in pairs], axis=1)
        y = y + dsk_ref[:, gsl[gi]] * xg
        y = y * zg_ref[0, :, gsl[gi]].astype(F32)
        o_ref[0, :, gsl[gi]] = _rms(y, gn_ref[:, gsl[gi]]).astype(BF16)


def _ssd(zxbc, dt, dt_bias, a_log, d_skip, gate_norm):
    b, lp, _ = zxbc.shape
    nc = lp // CHUNK
    wide = lambda j: pl.BlockSpec((1, CHUNK, SSM_INNER), lambda i, c: (i, c, j))
    const = lambda r, w: pl.BlockSpec((r, w), lambda i, c: (0, 0))
    return pl.pallas_call(
        _ssd_kernel,
        grid=(b, nc),
        in_specs=[
            wide(0), wide(1), wide(2),
            pl.BlockSpec((1, CHUNK, LANES), lambda i, c: (i, c, 0)),
            const(1, LANES), const(1, LANES), const(1, SSM_INNER), const(1, SSM_INNER),
        ],
        out_specs=pl.BlockSpec((1, CHUNK, SSM_INNER), lambda i, c: (i, c, 0)),
        out_shape=jax.ShapeDtypeStruct((b, lp, SSM_INNER), BF16),
        scratch_shapes=[pltpu.VMEM((SSM_GROUPS, SSM_STATE, SSM_GROUP_W), F32)],
        compiler_params=_cparams("parallel", "arbitrary"),
        name="ssd",
    )(zxbc, zxbc, zxbc, dt, dt_bias, a_log, d_skip, gate_norm)


def _out_proj_kernel(h_ref, y_ref, w_ref, g_ref, wr_ref, br_ref, ho_ref, lg_ref):
    hn = h_ref[...] + _dot(y_ref[...], w_ref[...])
    ho_ref[...] = hn
    u = _rms(hn, g_ref[...])
    lg_ref[...] = _dot_nt(wr_ref[...], u, precision=lax.Precision.HIGHEST) + br_ref[...]


def _out_proj(h, y, w, gain, wr_t, br):
    t, d = h.shape
    tm = TOKEN_TILE
    kin = y.shape[1]
    return pl.pallas_call(
        _out_proj_kernel,
        grid=(t // tm,),
        in_specs=[
            pl.BlockSpec((tm, d), lambda i: (i, 0)),
            pl.BlockSpec((tm, kin), lambda i: (i, 0)),
            pl.BlockSpec((kin, d), lambda i: (0, 0)),
            pl.BlockSpec((1, d), lambda i: (0, 0)),
            pl.BlockSpec((ROUTER_ROWS, d), lambda i: (0, 0)),
            pl.BlockSpec((ROUTER_ROWS, 1), lambda i: (0, 0)),
        ],
        out_specs=[
            pl.BlockSpec((tm, d), lambda i: (i, 0)),
            pl.BlockSpec((ROUTER_ROWS, tm), lambda i: (0, i)),
        ],
        out_shape=[jax.ShapeDtypeStruct((t, d), F32), jax.ShapeDtypeStruct((ROUTER_ROWS, t), F32)],
        compiler_params=_cparams("parallel"),
        name="out_proj_router",
    )(h, y, w, gain, wr_t, br)


def _first_argmax(v, n):
    ridx = lax.broadcasted_iota(I32, v.shape, 0).astype(F32)
    vmax = jnp.max(v, axis=0, keepdims=True)
    idx = jnp.min(jnp.where(v == vmax, ridx, float(n)), axis=0, keepdims=True)
    return vmax, idx.astype(I32)


def _route_kernel(lg_ref, eid_ref, gate_ref, rank_ref, cnt_ref, carry_ref):
    @pl.when(pl.program_id(0) == 0)
    def _():
        carry_ref[...] = jnp.zeros_like(carry_ref)

    logits = lg_ref[...]
    tm = logits.shape[1]
    gl = logits[0:MOE_GROUPS]
    gmax, gsel = _first_argmax(gl, MOE_GROUPS)
    p_group = 1.0 / jnp.sum(jnp.exp(gl - gmax), axis=0, keepdims=True)
    el = logits[MOE_GROUPS:MOE_GROUPS + MOE_EPG]
    for gg in range(1, MOE_GROUPS):
        el = jnp.where(gsel == gg, logits[MOE_GROUPS + gg * MOE_EPG:MOE_GROUPS + (gg + 1) * MOE_EPG], el)
    ex = jnp.exp(el - jnp.max(el, axis=0, keepdims=True))
    p = ex / jnp.sum(ex, axis=0, keepdims=True)
    p1, i1 = _first_argmax(p, MOE_EPG)
    ridx8 = lax.broadcasted_iota(I32, p.shape, 0)
    p2, i2 = _first_argmax(jnp.where(ridx8 == i1, -1.0, p), MOE_EPG)
    denom = p1 + p2
    e0 = gsel * MOE_EPG + i1
    e1 = gsel * MOE_EPG + i2
    eid_ref[0:1, :] = e0
    eid_ref[1:2, :] = e1
    gate_ref[0:1, :] = p_group * p1 / denom
    gate_ref[1:2, :] = p_group * p2 / denom

    ridx = lax.broadcasted_iota(I32, (MOE_EXPERTS, tm), 0)
    oh0 = jnp.where(ridx == e0, 1.0, 0.0)
    oh1 = jnp.where(ridx == e1, 1.0, 0.0)
    before = jnp.where(lax.broadcasted_iota(I32, (tm, tm), 0) < lax.broadcasted_iota(I32, (tm, tm), 1), 1.0, 0.0)
    before = before.astype(BF16)
    pref0 = _dot(oh0.astype(BF16), before)
    pref1 = _dot(oh1.astype(BF16), before)
    tot0 = jnp.sum(oh0, axis=1, keepdims=True)
    tot1 = jnp.sum(oh1, axis=1, keepdims=True)
    carry = carry_ref[...]
    rank_ref[0:1, :] = jnp.sum(oh0 * (carry + pref0), axis=0, keepdims=True).astype(I32)
    rank_ref[1:2, :] = jnp.sum(oh1 * (carry + tot0 + pref1), axis=0, keepdims=True).astype(I32)
    carry = carry + tot0 + tot1
    carry_ref[...] = carry
    cnt_ref[...] = jnp.broadcast_to(carry, cnt_ref.shape).astype(I32)


def _route(logits_t):
    t = logits_t.shape[1]
    tm = TOKEN_TILE
    pair = lambda: pl.BlockSpec((2, tm), lambda i: (0, i))
    return pl.pallas_call(
        _route_kernel,
        grid=(t // tm,),
        in_specs=[pl.BlockSpec((ROUTER_ROWS, tm), lambda i: (0, i))],
        out_specs=[pair(), pair(), pair(), pl.BlockSpec((MOE_EXPERTS, LANES), lambda i: (0, 0))],
        out_shape=[jax.ShapeDtypeStruct((2, t), I32), jax.ShapeDtypeStruct((2, t), F32),
                   jax.ShapeDtypeStruct((2, t), I32), jax.ShapeDtypeStruct((MOE_EXPERTS, LANES), I32)],
        scratch_shapes=[pltpu.VMEM((MOE_EXPERTS, 1), F32)],
        compiler_params=_cparams("arbitrary"),
        name="route_rank",
    )(logits_t)


def _dispatch_kernel(seg_ref, dest_ref, h_ref, g_ref, xs_ref, u_ref, zero_ref, sem):
    tm = u_ref.shape[0]

    @pl.when(pl.program_id(0) == 0)
    def _():
        zero_ref[...] = jnp.zeros_like(zero_ref)

        def fill_block(start):
            fill = pltpu.make_async_copy(zero_ref, xs_ref.at[pl.ds(pl.multiple_of(start, MOE_BLOCK), MOE_BLOCK)], sem)
            fill.start()
            fill.wait()

        for e in range(MOE_EXPERTS):
            @pl.when(seg_ref[MOE_EXPERTS + e] > 0)
            def _():
                fill_block(seg_ref[e] - MOE_BLOCK)

        def fill_tail(blk, carry):
            fill_block(blk * MOE_BLOCK)
            return carry

        lax.fori_loop(seg_ref[MOE_EXPERTS - 1] // MOE_BLOCK, xs_ref.shape[0] // MOE_BLOCK, fill_tail, 0)

    u_ref[...] = _rms(h_ref[...], g_ref[...]).reshape(u_ref.shape)

    def issue(t, carry):
        for k in range(2):
            pltpu.make_async_copy(u_ref.at[t], xs_ref.at[dest_ref[0, k, t]], sem).start(priority=k)
        return carry

    lax.fori_loop(0, tm, issue, 0)
    for _ in range(2):
        pltpu.make_async_copy(u_ref, xs_ref.at[pl.ds(0, tm)], sem).wait()


def _dispatch(seg, dest, h, gain, rows):
    t, d = h.shape
    tm = TOKEN_TILE
    return pl.pallas_call(
        _dispatch_kernel,
        grid_spec=pltpu.PrefetchScalarGridSpec(
            num_scalar_prefetch=1,
            grid=(t // tm,),
            in_specs=[
                pl.BlockSpec((1, 2, tm), lambda i, seg: (i, 0, 0), memory_space=pltpu.SMEM),
                pl.BlockSpec((tm, d), lambda i, seg: (i, 0)),
                pl.BlockSpec((1, d), lambda i, seg: (0, 0)),
            ],
            out_specs=pl.BlockSpec(memory_space=pl.ANY),
            scratch_shapes=[pltpu.VMEM((tm, d // LANES, LANES), F32), pltpu.VMEM((MOE_BLOCK, d // LANES, LANES), F32),
                            pltpu.SemaphoreType.DMA(())],
        ),
        out_shape=jax.ShapeDtypeStruct((rows, d // LANES, LANES), F32),
        compiler_params=_cparams("arbitrary"),
        name="moe_dispatch",
    )(seg, dest, h, gain)


def _expert_kernel(be_ref, nu_ref, xs_ref, w1_ref, w3_ref, w2_ref, ys_ref):
    used = pl.program_id(0) < nu_ref[0]

    @pl.when(used)
    def _():
        rows, sub, lanes = xs_ref.shape
        x = xs_ref[...].reshape(rows, sub * lanes).astype(BF16)
        a = _dot(x, w1_ref[0])
        b = _dot(x, w3_ref[0])
        ys_ref[...] = _dot((_silu(a) * b).astype(BF16), w2_ref[0]).reshape(rows, sub, lanes)

    @pl.when(jnp.logical_not(used))
    def _():
        ys_ref[...] = jnp.zeros_like(ys_ref)


def _experts(block_eid, n_used, xs, w1, w3, w2):
    rows, sub, lanes = xs.shape
    d = sub * lanes
    nb = rows // MOE_BLOCK
    blk = lambda i, be, nu: (jnp.minimum(i, nu[0] - 1), 0, 0)
    wsel = lambda i, be, nu: (be[i], 0, 0)
    return pl.pallas_call(
        _expert_kernel,
        grid_spec=pltpu.PrefetchScalarGridSpec(
            num_scalar_prefetch=2,
            grid=(nb,),
            in_specs=[
                pl.BlockSpec((MOE_BLOCK, sub, lanes), blk),
                pl.BlockSpec((1, d, MOE_FF), wsel),
                pl.BlockSpec((1, d, MOE_FF), wsel),
                pl.BlockSpec((1, MOE_FF, d), wsel),
            ],
            out_specs=pl.BlockSpec((MOE_BLOCK, sub, lanes), lambda i, be, nu: (i, 0, 0)),
        ),
        out_shape=jax.ShapeDtypeStruct((rows, sub, lanes), F32),
        compiler_params=_cparams("arbitrary"),
        name="moe_experts",
    )(block_eid, n_used, xs, w1, w3, w2)


def _combine_kernel(dest_ref, gate_ref, h_ref, g_ref, ys_ref, o_ref, y_ref, sem, *, final_norm):
    tm = h_ref.shape[0]

    def issue(t, carry):
        for k in range(2):
            pltpu.make_async_copy(ys_ref.at[dest_ref[0, k, t]], y_ref.at[k, t], sem).start(priority=k)
        return carry

    lax.fori_loop(0, tm, issue, 0)
    for k in range(2):
        pltpu.make_async_copy(ys_ref.at[pl.ds(0, tm)], y_ref.at[k], sem).wait()
    gate = gate_ref[...]
    y0 = y_ref[0].reshape(h_ref.shape)
    y1 = y_ref[1].reshape(h_ref.shape)
    out = h_ref[...] + (y0 * gate[:, 0:1] + y1 * gate[:, 1:2])
    if final_norm:
        out = _rms(out, g_ref[...])
    o_ref[...] = out


def _combine(dest, gate_rows, h, gain, ys, final_norm):
    t, d = h.shape
    tm = TOKEN_TILE
    return pl.pallas_call(
        functools.partial(_combine_kernel, final_norm=final_norm),
        grid=(t // tm,),
        in_specs=[
            pl.BlockSpec((1, 2, tm), lambda i: (i, 0, 0), memory_space=pltpu.SMEM),
            pl.BlockSpec((tm, 2), lambda i: (i, 0)),
            pl.BlockSpec((tm, d), lambda i: (i, 0)),
            pl.BlockSpec((1, d), lambda i: (0, 0)),
            pl.BlockSpec(memory_space=pl.ANY),
        ],
        out_specs=pl.BlockSpec((tm, d), lambda i: (i, 0)),
        out_shape=jax.ShapeDtypeStruct((t, d), F32),
        scratch_shapes=[pltpu.VMEM((2, tm, d // LANES, LANES), F32), pltpu.SemaphoreType.DMA(())],
        compiler_params=_cparams("arbitrary"),
        name="moe_combine",
    )(dest, gate_rows, h, gain, ys)


def _moe_layer(h, y_mix, w_out, norm_gain, wg, bg, we, be, w1, w3, w2, out_gain, final_norm):
    t, d = h.shape
    tm = TOKEN_TILE
    wr_t = jnp.concatenate([wg.T, we.reshape(d, MOE_EXPERTS).T,
                            jnp.zeros((ROUTER_ROWS - MOE_GROUPS - MOE_EXPERTS, d), F32)], axis=0)
    br = jnp.concatenate([bg, be.reshape(-1), jnp.zeros((ROUTER_ROWS - MOE_GROUPS - MOE_EXPERTS,), F32)])
    h, logits_t = _out_proj(h, y_mix, w_out.astype(BF16), norm_gain[None, :], wr_t, br[:, None])
    eid, gate, rank, counts = _route(logits_t)

    counts = counts[:, 0]
    padded = (counts + MOE_BLOCK - 1) // MOE_BLOCK * MOE_BLOCK
    pend = jnp.cumsum(padded)
    pstart = pend - padded
    onehot = eid[None] == jnp.arange(MOE_EXPERTS, dtype=I32)[:, None, None]
    dest = rank + jnp.sum(jnp.where(onehot, pstart[:, None, None], 0), axis=0)
    dest = dest.reshape(2, t // tm, tm).transpose(1, 0, 2)
    n_blocks = -(-(2 * t) // MOE_BLOCK) + MOE_EXPERTS
    block_start = jnp.arange(n_blocks, dtype=I32) * MOE_BLOCK
    block_eid = jnp.minimum(jnp.sum((pend[None, :] <= block_start[:, None]).astype(I32), axis=1), MOE_EXPERTS - 1)
    n_used = (pend[-1:] // MOE_BLOCK).astype(I32)

    xs = _dispatch(jnp.concatenate([pend, padded]).astype(I32), dest, h, norm_gain[None, :], n_blocks * MOE_BLOCK)
    ys = _experts(block_eid, n_used, xs, w1.astype(BF16), w3.astype(BF16), w2.astype(BF16))
    return _combine(dest, gate.T, h, out_gain[None, :], ys, final_norm)


def kernel(x, meta, norm_mix, norm_ffn, norm_final, ret_w_in, ret_w_out, ret_norm, ssm_w_in, ssm_conv_w,
           ssm_conv_b, ssm_dt_bias, ssm_a_log, ssm_d, ssm_norm, ssm_w_out, moe_wg, moe_bg, moe_we, moe_be,
           moe_w1, moe_w3, moe_w2):
    b, seq, d = x.shape
    lp = CHUNK + seq
    t = b * lp
    depth = norm_mix.shape[0]
    h = jnp.concatenate([jnp.zeros((b, PAD, d), F32), jnp.broadcast_to(meta[None], (b, N_META, d)), x], axis=1)

    half = RET_QK_DIM // 2
    inv_freq = 1.0 / (RET_ROPE_BASE ** jnp.linspace(0.0, 1.0, half, dtype=F32))
    ang = (jnp.arange(lp, dtype=F32) - PAD)[:, None] * inv_freq[None, :]
    cos, sin = jnp.cos(ang), jnp.sin(ang)
    perm = jnp.concatenate([jnp.arange(0, RET_QK_DIM, 2), jnp.arange(1, RET_QK_DIM, 2)])
    qk_perm = (jnp.arange(2 * RET_HEADS)[:, None] * RET_QK_DIM + perm[None, :]).reshape(-1)

    for i in range(depth):
        j = i // 2
        if i % 2 == 0:
            w_in = jnp.concatenate([ret_w_in[j][:, qk_perm], ret_w_in[j][:, 2 * RET_QK:]], axis=1).astype(BF16)
            qkvg = _ret_in(h.reshape(b, lp, d), norm_mix[i][None, :], cos, sin, w_in)
            y = _retention(qkvg, ret_norm[j][None, :])
            w_out = ret_w_out[j]
        else:
            w_in = jnp.concatenate([ssm_w_in[j], jnp.zeros((d, LANES - SSM_HEADS), F32)], axis=1).astype(BF16)
            zxbc, dt = _ssm_in(h.reshape(t, d), norm_mix[i][None, :], w_in, ssm_conv_w[j], ssm_conv_b[j][None, :], lp)
            lane_pad = lambda v: jnp.concatenate([v, jnp.zeros((LANES - SSM_HEADS,), F32)])[None, :]
            y = _ssd(zxbc.reshape(b, lp, SSM_MAIN), dt.reshape(b, lp, LANES), lane_pad(ssm_dt_bias[j]),
                     lane_pad(ssm_a_log[j]), jnp.repeat(ssm_d[j], SSM_HEAD_DIM)[None, :], ssm_norm[j][None, :])
            w_out = ssm_w_out[j]
        last = i == depth - 1
        h = _moe_layer(h.reshape(t, d), y.reshape(t, -1), w_out, norm_ffn[i], moe_wg[i], moe_bg[i], moe_we[i],
                       moe_be[i], moe_w1[i], moe_w3[i], moe_w2[i], norm_final if last else norm_ffn[i], last)
    return h.reshape(b, lp, d)[:, CHUNK:]
```

```python
import functools
import math

import jax
import jax.numpy as jnp
from jax import lax
from jax.experimental import pallas as pl
from jax.experimental.pallas import tpu as pltpu

F32 = jnp.float32
BF16 = jnp.bfloat16
I32 = jnp.int32

D_MODEL = 1024
N_META = 16
CHUNK = 128
PAD = CHUNK - N_META
EPS = 1e-6

RET_HEADS = 4
RET_QK_DIM = 256
RET_V_DIM = 512
RET_QK = RET_HEADS * RET_QK_DIM
RET_V = RET_HEADS * RET_V_DIM
RET_IN = 2 * RET_QK + 2 * RET_V
RET_ROPE_BASE = 10000.0

SSM_INNER = 2048
SSM_HEAD_DIM = 64
SSM_HEADS = 32
SSM_GROUPS = 8
SSM_HPG = 4
SSM_STATE = 128
SSM_CONV = 4
SSM_BC = SSM_GROUPS * SSM_STATE
SSM_CONV_DIM = SSM_INNER + 2 * SSM_BC
SSM_MAIN = SSM_INNER + SSM_CONV_DIM
LANES = 128
SSM_GROUP_W = SSM_HPG * SSM_HEAD_DIM

MOE_GROUPS = 4
MOE_EPG = 8
MOE_EXPERTS = 32
MOE_FF = 512
MOE_BLOCK = 256
ROUTER_ROWS = 40

SSM_IN_STEP = 512
LOG2E = math.log2(math.e)
TOKEN_TILE = 512
VMEM_LIMIT = 56 * 1024 * 1024


def _cparams(*sem):
    return pltpu.CompilerParams(dimension_semantics=sem, vmem_limit_bytes=VMEM_LIMIT)


def _rms(x, gain):
    ms = jnp.mean(x * x, axis=-1, keepdims=True)
    return x * lax.rsqrt(ms + EPS) * gain


def _silu(x):
    hx = 0.5 * x
    return hx + hx * jnp.tanh(hx)


def _dot(a, b):
    return jnp.dot(a, b, preferred_element_type=F32)


def _dot_nt(a, b, precision=None):
    return lax.dot_general(a, b, (((1,), (1,)), ((), ())), preferred_element_type=F32, precision=precision)


def _dot_tn(a, b):
    return lax.dot_general(a, b, (((0,), (0,)), ((), ())), preferred_element_type=F32)


def _row_tile(rows, cap=544):
    best = 16
    for t in range(16, cap + 1, 16):
        if rows % t == 0:
            best = t
    return best


def _ret_in_kernel(h_ref, g_ref, cos_ref, sin_ref, w_ref, o_ref):
    xn = _rms(h_ref[0], g_ref[...]).astype(BF16)
    cos = cos_ref[...]
    sin = sin_ref[...]
    half = RET_QK_DIM // 2
    for j in range(2 * RET_HEADS):
        c0 = j * RET_QK_DIM
        acc = _dot(xn, w_ref[:, c0:c0 + RET_QK_DIM])
        if j >= RET_HEADS:
            acc = acc * (RET_QK_DIM ** -0.5)
        e = acc[:, :half]
        o = acc[:, half:]
        o_ref[0, :, c0:c0 + half] = (e * cos - o * sin).astype(BF16)
        o_ref[0, :, c0 + half:c0 + RET_QK_DIM] = (o * cos + e * sin).astype(BF16)
    step = 1024
    for c0 in range(2 * RET_QK, RET_IN, step):
        o_ref[0, :, c0:c0 + step] = _dot(xn, w_ref[:, c0:c0 + step]).astype(BF16)


def _ret_in(h, gain, cos, sin, w):
    b, lp, d = h.shape
    tm = _row_tile(lp)
    return pl.pallas_call(
        _ret_in_kernel,
        grid=(b, lp // tm),
        in_specs=[
            pl.BlockSpec((1, tm, d), lambda i, j: (i, j, 0)),
            pl.BlockSpec((1, d), lambda i, j: (0, 0)),
            pl.BlockSpec((tm, RET_QK_DIM // 2), lambda i, j: (j, 0)),
            pl.BlockSpec((tm, RET_QK_DIM // 2), lambda i, j: (j, 0)),
            pl.BlockSpec((d, RET_IN), lambda i, j: (0, 0)),
        ],
        out_specs=pl.BlockSpec((1, tm, RET_IN), lambda i, j: (i, j, 0)),
        out_shape=jax.ShapeDtypeStruct((b, lp, RET_IN), BF16),
        compiler_params=_cparams("parallel", "parallel"),
        name="ret_in_proj",
    )(h, gain, cos, sin, w)


def _ret_kernel(q_ref, k_ref, v_ref, g_ref, hn_ref, o_ref, state_ref):
    @pl.when(pl.program_id(1) == 0)
    def _():
        state_ref[...] = jnp.zeros_like(state_ref)

    row = lax.broadcasted_iota(I32, (CHUNK, CHUNK), 0).astype(F32)
    col = lax.broadcasted_iota(I32, (CHUNK, CHUNK), 1).astype(F32)
    diff = row - col
    ridx = row[:, :1]
    for hh in range(RET_HEADS):
        lg = math.log(1.0 - 2.0 ** (-5.0 - hh))
        intra = jnp.where(diff >= 0, jnp.exp(lg * jnp.maximum(diff, 0.0)), 0.0)
        qdec = jnp.exp(lg * (ridx + 1.0))
        kdec = jnp.exp(lg * (CHUNK - 1.0 - ridx))
        cdec = math.exp(lg * CHUNK)
        qh = q_ref[0, :, hh * RET_QK_DIM:(hh + 1) * RET_QK_DIM]
        kh = k_ref[0, :, hh * RET_QK_DIM:(hh + 1) * RET_QK_DIM]
        vh = v_ref[0, :, hh * RET_V_DIM:(hh + 1) * RET_V_DIM]
        scores = _dot_nt(qh, kh) * intra
        st = state_ref[hh]
        y = _dot(scores.astype(BF16), vh) + _dot(qh, st.astype(BF16)) * qdec
        kd = (kh.astype(F32) * kdec).astype(BF16)
        state_ref[hh] = st * cdec + _dot_tn(kd, vh)
        yn = _rms(y, hn_ref[:, hh * RET_V_DIM:(hh + 1) * RET_V_DIM])
        gh = g_ref[0, :, hh * RET_V_DIM:(hh + 1) * RET_V_DIM].astype(F32)
        o_ref[0, :, hh * RET_V_DIM:(hh + 1) * RET_V_DIM] = (_silu(gh) * yn).astype(BF16)


def _retention(qkvg, head_norm):
    b, lp, _ = qkvg.shape
    nc = lp // CHUNK
    return pl.pallas_call(
        _ret_kernel,
        grid=(b, nc),
        in_specs=[
            pl.BlockSpec((1, CHUNK, RET_QK), lambda i, c: (i, c, 0)),
            pl.BlockSpec((1, CHUNK, RET_QK), lambda i, c: (i, c, 1)),
            pl.BlockSpec((1, CHUNK, RET_V), lambda i, c: (i, c, 1)),
            pl.BlockSpec((1, CHUNK, RET_V), lambda i, c: (i, c, 2)),
            pl.BlockSpec((1, RET_V), lambda i, c: (0, 0)),
        ],
        out_specs=pl.BlockSpec((1, CHUNK, RET_V), lambda i, c: (i, c, 0)),
        out_shape=jax.ShapeDtypeStruct((b, lp, RET_V), BF16),
        scratch_shapes=[pltpu.VMEM((RET_HEADS, RET_QK_DIM, RET_V_DIM), F32)],
        compiler_params=_cparams("parallel", "arbitrary"),
        name="retention",
    )(qkvg, qkvg, qkvg, qkvg, head_norm)


def _ssm_in_kernel(h_ref, g_ref, w_ref, cw_ref, cb_ref, o_ref, dt_ref, tail_ref, acc_ref, *, rows_per_seq):
    i = pl.program_id(0)
    tm = h_ref.shape[0]

    @pl.when(i == 0)
    def _():
        tail_ref[...] = jnp.zeros_like(tail_ref)

    xn = _rms(h_ref[...], g_ref[...]).astype(BF16)
    step = acc_ref.shape[2]
    n_steps = SSM_MAIN // step

    pos = lax.rem(i * tm, rows_per_seq) + lax.broadcasted_iota(I32, (tm, 1), 0)
    is_pad = jnp.logical_or(pos < PAD, jnp.logical_and(pos >= rows_per_seq, pos < rows_per_seq + PAD))
    row8 = lax.broadcasted_iota(I32, (8, 1), 0)

    def project(j):
        acc_ref[j % 2] = _dot(xn, w_ref[:, j * step:(j + 1) * step])

    project(0)
    for j in range(n_steps):
        if j + 1 < n_steps:
            project(j + 1)
        c0 = j * step
        acc = acc_ref[j % 2]
        if c0 < SSM_INNER:
            o_ref[:, c0:c0 + step] = _silu(acc).astype(BF16)
            continue
        cc = c0 - SSM_INNER
        pre = jnp.where(is_pad, 0.0, acc)
        tail = tail_ref[:, cc:cc + step]
        tail_ref[:, cc:cc + step] = pre[tm - 8:, :]
        conv = cb_ref[:, cc:cc + step] + cw_ref[SSM_CONV - 1:SSM_CONV, cc:cc + step] * pre
        for k in range(1, SSM_CONV):
            rolled = pltpu.roll(pre, k, axis=0)
            top = jnp.where(row8 < k, pltpu.roll(tail, k, axis=0), rolled[:8])
            shifted = jnp.concatenate([top, rolled[8:]], axis=0)
            conv = conv + cw_ref[SSM_CONV - 1 - k:SSM_CONV - k, cc:cc + step] * shifted
        o_ref[:, c0:c0 + step] = _silu(conv).astype(BF16)
    dt_ref[...] = _dot(xn, w_ref[:, SSM_MAIN:SSM_MAIN + LANES])


def _ssm_in(h, gain, w, conv_w, conv_b, rows_per_seq):
    t, d = h.shape
    tm = TOKEN_TILE
    assert tm <= rows_per_seq
    const = lambda r, c: pl.BlockSpec((r, c), lambda i: (0, 0))
    return pl.pallas_call(
        functools.partial(_ssm_in_kernel, rows_per_seq=rows_per_seq),
        grid=(t // tm,),
        in_specs=[
            pl.BlockSpec((tm, d), lambda i: (i, 0)),
            const(1, d), const(d, SSM_MAIN + LANES), const(SSM_CONV, SSM_CONV_DIM), const(1, SSM_CONV_DIM),
        ],
        out_specs=[
            pl.BlockSpec((tm, SSM_MAIN), lambda i: (i, 0)),
            pl.BlockSpec((tm, LANES), lambda i: (i, 0)),
        ],
        out_shape=[jax.ShapeDtypeStruct((t, SSM_MAIN), BF16), jax.ShapeDtypeStruct((t, LANES), F32)],
        scratch_shapes=[pltpu.VMEM((8, SSM_CONV_DIM), F32), pltpu.VMEM((2, tm, SSM_IN_STEP), F32)],
        compiler_params=_cparams("arbitrary"),
        name="ssm_in_proj",
    )(h, gain, w, conv_w, conv_b)


def _ssd_kernel(zg_ref, x_ref, bc_ref, dt_ref, dtb_ref, alog_ref, dsk_ref, gn_ref, o_ref, state_ref):
    c = pl.program_id(1)

    @pl.when(c == 0)
    def _():
        state_ref[...] = jnp.zeros_like(state_ref)

    row = lax.broadcasted_iota(I32, (CHUNK, CHUNK), 0)
    col = lax.broadcasted_iota(I32, (CHUNK, CHUNK), 1)
    causal_bias = jnp.where(row >= col, 0.0, -jnp.inf)
    low_half = col < SSM_HEAD_DIM
    valid = row[:, :1] >= jnp.where(c > 0, 0, PAD)

    dtr = dt_ref[0] + dtb_ref[...]
    dtv = jnp.maximum(dtr, 0.0) + jnp.log(1.0 + jnp.exp(-jnp.abs(dtr)))
    dtv = jnp.where(valid, dtv, 0.0)
    da = dtv * (-jnp.exp(alog_ref[...]))
    tril = jnp.where(row >= col, 1.0, 0.0)
    acs = jnp.dot(tril, da, preferred_element_type=F32, precision=lax.Precision.HIGHEST)
    last = acs[CHUNK - 1:CHUNK, :]
    acs2 = acs * LOG2E
    src_t = (jnp.log2(dtv) - acs2).T
    w_t = (dtv * jnp.exp(last - acs)).T
    elast = jnp.exp(last)

    def pair_row(v, h0):
        return jnp.where(low_half[:1], v[:, h0:h0 + 1], v[:, h0 + 1:h0 + 2])

    groups = range(SSM_GROUPS)
    gsl = [slice(gi * SSM_GROUP_W, (gi + 1) * SSM_GROUP_W) for gi in groups]
    bgs = [bc_ref[0, :, gi * SSM_STATE:(gi + 1) * SSM_STATE] for gi in groups]
    cgs = [bc_ref[0, :, SSM_BC + gi * SSM_STATE:SSM_BC + (gi + 1) * SSM_STATE] for gi in groups]
    sts = [state_ref[gi] for gi in groups]
    cbs = [_dot_nt(cgs[gi], bgs[gi]) for gi in groups]
    yoffs = [_dot(cgs[gi], sts[gi].astype(BF16)) for gi in groups]
    bts = [bgs[gi].astype(F32).T for gi in groups]

    outs = []
    eexp = []
    for gi in groups:
        for pp in range(SSM_HPG // 2):
            h0 = gi * SSM_HPG + 2 * pp
            top, bot, ecols = [], [], []
            for hd in (h0, h0 + 1):
                acs_b = jnp.broadcast_to(acs2[:, hd:hd + 1], (CHUNK, CHUNK))
                top.append(cbs[gi] * jnp.exp2(acs_b + (src_t[hd:hd + 1, :] + causal_bias)))
                bot.append(bts[gi] * w_t[hd:hd + 1, :])
                ecols.append(jnp.exp2(acs_b))
            lhs = jnp.concatenate([jnp.concatenate(top, axis=1), jnp.concatenate(bot, axis=1)], axis=0)
            xp = x_ref[0, :, h0 * SSM_HEAD_DIM:(h0 + 2) * SSM_HEAD_DIM]
            zero = jnp.zeros_like(xp)
            rhs = jnp.concatenate([jnp.where(low_half, xp, zero), jnp.where(low_half, zero, xp)], axis=0)
            outs.append(_dot(lhs.astype(BF16), rhs))
            eexp.append(jnp.where(low_half, ecols[0], ecols[1]))

    for gi in groups:
        pairs = [gi * (SSM_HPG // 2) + pp for pp in range(SSM_HPG // 2)]
        heads = [gi * SSM_HPG + 2 * pp for pp in range(SSM_HPG // 2)]
        lexp = jnp.concatenate([pair_row(elast, h0) for h0 in heads], axis=1)
        state_ref[gi] = sts[gi] * lexp + jnp.concatenate([outs[p][CHUNK:] for p in pairs], axis=1)
        xg = x_ref[0, :, gsl[gi]].astype(F32)
        y = jnp.concatenate([outs[p][:CHUNK] for p in pairs], axis=1)
        y = y + yoffs[gi] * jnp.concatenate([eexp[p] for p in pairs], axis=1)
        y = y + dsk_ref[:, gsl[gi]] * xg
        y = y * zg_ref[0, :, gsl[gi]].astype(F32)
        o_ref[0, :, gsl[gi]] = _rms(y, gn_ref[:, gsl[gi]]).astype(BF16)


def _ssd(zxbc, dt, dt_bias, a_log, d_skip, gate_norm):
    b, lp, _ = zxbc.shape
    nc = lp // CHUNK
    wide = lambda j: pl.BlockSpec((1, CHUNK, SSM_INNER), lambda i, c: (i, c, j))
    const = lambda r, w: pl.BlockSpec((r, w), lambda i, c: (0, 0))
    return pl.pallas_call(
        _ssd_kernel,
        grid=(b, nc),
        in_specs=[
            wide(0), wide(1), wide(2),
            pl.BlockSpec((1, CHUNK, LANES), lambda i, c: (i, c, 0)),
            const(1, LANES), const(1, LANES), const(1, SSM_INNER), const(1, SSM_INNER),
        ],
        out_specs=pl.BlockSpec((1, CHUNK, SSM_INNER), lambda i, c: (i, c, 0)),
        out_shape=jax.ShapeDtypeStruct((b, lp, SSM_INNER), BF16),
        scratch_shapes=[pltpu.VMEM((SSM_GROUPS, SSM_STATE, SSM_GROUP_W), F32)],
        compiler_params=_cparams("parallel", "arbitrary"),
        name="ssd",
    )(zxbc, zxbc, zxbc, dt, dt_bias, a_log, d_skip, gate_norm)


def _out_proj_kernel(h_ref, y_ref, w_ref, g_ref, wr_ref, br_ref, ho_ref, lg_ref):
    hn = h_ref[...] + _dot(y_ref[...], w_ref[...])
    ho_ref[...] = hn
    u = _rms(hn, g_ref[...])
    u_hi = u.astype(BF16)
    u_lo = (u - u_hi.astype(F32)).astype(BF16)
    p = _dot(u_hi, wr_ref[...])
    logits = p[:, :LANES] + (p[:, LANES:] + _dot(u_lo, wr_ref[:, :LANES])) + br_ref[...]
    lg_ref[...] = logits.T[:ROUTER_ROWS]


def _out_proj(h, y, w, gain, wr_t, br):
    t, d = h.shape
    tm = TOKEN_TILE
    kin = y.shape[1]
    return pl.pallas_call(
        _out_proj_kernel,
        grid=(t // tm,),
        in_specs=[
            pl.BlockSpec((tm, d), lambda i: (i, 0)),
            pl.BlockSpec((tm, kin), lambda i: (i, 0)),
            pl.BlockSpec((kin, d), lambda i: (0, 0)),
            pl.BlockSpec((1, d), lambda i: (0, 0)),
            pl.BlockSpec((d, 2 * LANES), lambda i: (0, 0)),
            pl.BlockSpec((1, LANES), lambda i: (0, 0)),
        ],
        out_specs=[
            pl.BlockSpec((tm, d), lambda i: (i, 0)),
            pl.BlockSpec((ROUTER_ROWS, tm), lambda i: (0, i)),
        ],
        out_shape=[jax.ShapeDtypeStruct((t, d), F32), jax.ShapeDtypeStruct((ROUTER_ROWS, t), F32)],
        compiler_params=_cparams("parallel"),
        name="out_proj_router",
    )(h, y, w, gain, wr_t, br)


def _first_argmax(v, n):
    ridx = lax.broadcasted_iota(I32, v.shape, 0).astype(F32)
    vmax = jnp.max(v, axis=0, keepdims=True)
    idx = jnp.min(jnp.where(v == vmax, ridx, float(n)), axis=0, keepdims=True)
    return vmax, idx.astype(I32)


def _route_kernel(lg_ref, eid_ref, gate_ref, rank_ref, cnt_ref, carry_ref):
    @pl.when(pl.program_id(0) == 0)
    def _():
        carry_ref[...] = jnp.zeros_like(carry_ref)

    logits = lg_ref[...]
    tm = logits.shape[1]
    gl = logits[0:MOE_GROUPS]
    gmax, gsel = _first_argmax(gl, MOE_GROUPS)
    p_group = 1.0 / jnp.sum(jnp.exp(gl - gmax), axis=0, keepdims=True)
    el = logits[MOE_GROUPS:MOE_GROUPS + MOE_EPG]
    for gg in range(1, MOE_GROUPS):
        el = jnp.where(gsel == gg, logits[MOE_GROUPS + gg * MOE_EPG:MOE_GROUPS + (gg + 1) * MOE_EPG], el)
    ex = jnp.exp(el - jnp.max(el, axis=0, keepdims=True))
    p = ex / jnp.sum(ex, axis=0, keepdims=True)
    p1, i1 = _first_argmax(p, MOE_EPG)
    ridx8 = lax.broadcasted_iota(I32, p.shape, 0)
    p2, i2 = _first_argmax(jnp.where(ridx8 == i1, -1.0, p), MOE_EPG)
    denom = p1 + p2
    e0 = gsel * MOE_EPG + i1
    e1 = gsel * MOE_EPG + i2
    eid_ref[0:1, :] = e0
    eid_ref[1:2, :] = e1
    gate_ref[0:1, :] = p_group * p1 / denom
    gate_ref[1:2, :] = p_group * p2 / denom

    ridx = lax.broadcasted_iota(I32, (MOE_EXPERTS, tm), 0)
    oh0 = jnp.where(ridx == e0, 1.0, 0.0)
    oh1 = jnp.where(ridx == e1, 1.0, 0.0)
    before = jnp.where(lax.broadcasted_iota(I32, (tm, tm), 0) < lax.broadcasted_iota(I32, (tm, tm), 1), 1.0, 0.0)
    before = before.astype(BF16)
    pref0 = _dot(oh0.astype(BF16), before)
    pref1 = _dot(oh1.astype(BF16), before)
    tot0 = jnp.sum(oh0, axis=1, keepdims=True)
    tot1 = jnp.sum(oh1, axis=1, keepdims=True)
    carry = carry_ref[...]
    rank_ref[0:1, :] = jnp.sum(oh0 * (carry + pref0), axis=0, keepdims=True).astype(I32)
    rank_ref[1:2, :] = jnp.sum(oh1 * (carry + tot0 + pref1), axis=0, keepdims=True).astype(I32)
    carry = carry + tot0 + tot1
    carry_ref[...] = carry
    cnt_ref[...] = jnp.broadcast_to(carry, cnt_ref.shape).astype(I32)


def _route(logits_t):
    t = logits_t.shape[1]
    tm = TOKEN_TILE
    pair = lambda: pl.BlockSpec((2, tm), lambda i: (0, i))
    return pl.pallas_call(
        _route_kernel,
        grid=(t // tm,),
        in_specs=[pl.BlockSpec((ROUTER_ROWS, tm), lambda i: (0, i))],
        out_specs=[pair(), pair(), pair(), pl.BlockSpec((MOE_EXPERTS, LANES), lambda i: (0, 0))],
        out_shape=[jax.ShapeDtypeStruct((2, t), I32), jax.ShapeDtypeStruct((2, t), F32),
                   jax.ShapeDtypeStruct((2, t), I32), jax.ShapeDtypeStruct((MOE_EXPERTS, LANES), I32)],
        scratch_shapes=[pltpu.VMEM((MOE_EXPERTS, 1), F32)],
        compiler_params=_cparams("arbitrary"),
        name="route_rank",
    )(logits_t)


def _dispatch_kernel(seg_ref, dest_ref, h_ref, g_ref, xs_ref, u_ref, zero_ref, sems):
    i = pl.program_id(0)
    n = pl.num_programs(0)
    slot = lax.rem(i, 2)
    tm = u_ref.shape[1]

    def wait_rows(s):
        for _ in range(2):
            pltpu.make_async_copy(u_ref.at[s], xs_ref.at[pl.ds(0, tm)], sems.at[s]).wait()

    @pl.when(i == 0)
    def _():
        zero_ref[...] = jnp.zeros_like(zero_ref)

        def fill_block(start):
            fill = pltpu.make_async_copy(
                zero_ref, xs_ref.at[pl.ds(pl.multiple_of(start, MOE_BLOCK), MOE_BLOCK)], sems.at[0])
            fill.start()
            fill.wait()

        for e in range(MOE_EXPERTS):
            @pl.when(seg_ref[MOE_EXPERTS + e] > 0)
            def _():
                fill_block(seg_ref[e] - MOE_BLOCK)

        def fill_tail(blk, carry):
            fill_block(blk * MOE_BLOCK)
            return carry

        lax.fori_loop(seg_ref[MOE_EXPERTS - 1] // MOE_BLOCK, xs_ref.shape[0] // MOE_BLOCK, fill_tail, 0)

    @pl.when(i >= 2)
    def _():
        wait_rows(slot)

    u_ref[slot] = _rms(h_ref[...], g_ref[...]).reshape(u_ref.shape[1:])

    def issue(t, carry):
        for k in range(2):
            pltpu.make_async_copy(
                u_ref.at[slot, t], xs_ref.at[dest_ref[0, k, t]], sems.at[slot]).start(priority=k)
        return carry

    lax.fori_loop(0, tm, issue, 0)

    @pl.when(i == n - 1)
    def _():
        wait_rows(slot)

        @pl.when(n >= 2)
        def _():
            wait_rows(1 - slot)


def _dispatch(seg, dest, h, gain, rows):
    t, d = h.shape
    tm = TOKEN_TILE
    return pl.pallas_call(
        _dispatch_kernel,
        grid_spec=pltpu.PrefetchScalarGridSpec(
            num_scalar_prefetch=1,
            grid=(t // tm,),
            in_specs=[
                pl.BlockSpec((1, 2, tm), lambda i, seg: (i, 0, 0), memory_space=pltpu.SMEM),
                pl.BlockSpec((tm, d), lambda i, seg: (i, 0)),
                pl.BlockSpec((1, d), lambda i, seg: (0, 0)),
            ],
            out_specs=pl.BlockSpec(memory_space=pl.ANY),
            scratch_shapes=[pltpu.VMEM((2, tm, d // LANES, LANES), F32),
                            pltpu.VMEM((MOE_BLOCK, d // LANES, LANES), F32), pltpu.SemaphoreType.DMA((2,))],
        ),
        out_shape=jax.ShapeDtypeStruct((rows, d // LANES, LANES), F32),
        compiler_params=_cparams("arbitrary"),
        name="moe_dispatch",
    )(seg, dest, h, gain)


def _expert_kernel(be_ref, nu_ref, xs_ref, w1_ref, w3_ref, w2_ref, ys_ref):
    used = pl.program_id(0) < nu_ref[0]

    @pl.when(used)
    def _():
        rows, sub, lanes = xs_ref.shape
        x = xs_ref[...].reshape(rows, sub * lanes).astype(BF16)
        a = _dot(x, w1_ref[0])
        b = _dot(x, w3_ref[0])
        ys_ref[...] = _dot((_silu(a) * b).astype(BF16), w2_ref[0]).reshape(rows, sub, lanes)

    @pl.when(jnp.logical_not(used))
    def _():
        ys_ref[...] = jnp.zeros_like(ys_ref)


def _experts(block_eid, n_used, xs, w1, w3, w2):
    rows, sub, lanes = xs.shape
    d = sub * lanes
    nb = rows // MOE_BLOCK
    blk = lambda i, be, nu: (jnp.minimum(i, nu[0] - 1), 0, 0)
    wsel = lambda i, be, nu: (be[i], 0, 0)
    return pl.pallas_call(
        _expert_kernel,
        grid_spec=pltpu.PrefetchScalarGridSpec(
            num_scalar_prefetch=2,
            grid=(nb,),
            in_specs=[
                pl.BlockSpec((MOE_BLOCK, sub, lanes), blk),
                pl.BlockSpec((1, d, MOE_FF), wsel),
                pl.BlockSpec((1, d, MOE_FF), wsel),
                pl.BlockSpec((1, MOE_FF, d), wsel),
            ],
            out_specs=pl.BlockSpec((MOE_BLOCK, sub, lanes), lambda i, be, nu: (i, 0, 0)),
        ),
        out_shape=jax.ShapeDtypeStruct((rows, sub, lanes), F32),
        compiler_params=_cparams("arbitrary"),
        name="moe_experts",
    )(block_eid, n_used, xs, w1, w3, w2)


def _combine_kernel(dest_ref, next_ref, gate_ref, h_ref, g_ref, ys_ref, o_ref, y_ref, sems, *, final_norm):
    i = pl.program_id(0)
    n = pl.num_programs(0)
    slot = lax.rem(i, 2)
    tm = h_ref.shape[0]

    def gather(idx_ref, s):
        def issue(t, carry):
            for k in range(2):
                pltpu.make_async_copy(
                    ys_ref.at[idx_ref[0, k, t]], y_ref.at[s, k, t], sems.at[s]).start(priority=k)
            return carry

        lax.fori_loop(0, tm, issue, 0)

    @pl.when(i == 0)
    def _():
        gather(dest_ref, slot)

    @pl.when(i + 1 < n)
    def _():
        gather(next_ref, 1 - slot)

    for k in range(2):
        pltpu.make_async_copy(ys_ref.at[pl.ds(0, tm)], y_ref.at[slot, k], sems.at[slot]).wait()
    gate = gate_ref[...]
    y0 = y_ref[slot, 0].reshape(h_ref.shape)
    y1 = y_ref[slot, 1].reshape(h_ref.shape)
    out = h_ref[...] + (y0 * gate[:, 0:1] + y1 * gate[:, 1:2])
    if final_norm:
        out = _rms(out, g_ref[...])
    o_ref[...] = out


def _combine(dest, gate_rows, h, gain, ys, final_norm):
    t, d = h.shape
    tm = TOKEN_TILE
    n = t // tm
    return pl.pallas_call(
        functools.partial(_combine_kernel, final_norm=final_norm),
        grid=(n,),
        in_specs=[
            pl.BlockSpec((1, 2, tm), lambda i: (i, 0, 0), memory_space=pltpu.SMEM),
            pl.BlockSpec((1, 2, tm), lambda i: (jnp.minimum(i + 1, n - 1), 0, 0), memory_space=pltpu.SMEM),
            pl.BlockSpec((tm, 2), lambda i: (i, 0)),
            pl.BlockSpec((tm, d), lambda i: (i, 0)),
            pl.BlockSpec((1, d), lambda i: (0, 0)),
            pl.BlockSpec(memory_space=pl.ANY),
        ],
        out_specs=pl.BlockSpec((tm, d), lambda i: (i, 0)),
        out_shape=jax.ShapeDtypeStruct((t, d), F32),
        scratch_shapes=[pltpu.VMEM((2, 2, tm, d // LANES, LANES), F32), pltpu.SemaphoreType.DMA((2,))],
        compiler_params=_cparams("arbitrary"),
        name="moe_combine",
    )(dest, dest, gate_rows, h, gain, ys)


def _moe_layer(h, y_mix, w_out, norm_gain, wg, bg, we, be, w1, w3, w2, out_gain, final_norm):
    t, d = h.shape
    tm = TOKEN_TILE
    n_logits = MOE_GROUPS + MOE_EXPERTS
    wr = jnp.concatenate([wg, we.reshape(d, MOE_EXPERTS), jnp.zeros((d, LANES - n_logits), F32)], axis=1)
    br = jnp.concatenate([bg, be.reshape(-1), jnp.zeros((LANES - n_logits,), F32)])
    wr_hi = wr.astype(BF16)
    wr_split = jnp.concatenate([wr_hi, (wr - wr_hi.astype(F32)).astype(BF16)], axis=1)
    h, logits_t = _out_proj(h, y_mix, w_out.astype(BF16), norm_gain[None, :], wr_split, br[None, :])
    eid, gate, rank, counts = _route(logits_t)

    counts = counts[:, 0]
    padded = (counts + MOE_BLOCK - 1) // MOE_BLOCK * MOE_BLOCK
    pend = jnp.cumsum(padded)
    pstart = pend - padded
    onehot = eid[None] == jnp.arange(MOE_EXPERTS, dtype=I32)[:, None, None]
    dest = rank + jnp.sum(jnp.where(onehot, pstart[:, None, None], 0), axis=0)
    dest = dest.reshape(2, t // tm, tm).transpose(1, 0, 2)
    n_blocks = -(-(2 * t) // MOE_BLOCK) + MOE_EXPERTS
    block_start = jnp.arange(n_blocks, dtype=I32) * MOE_BLOCK
    block_eid = jnp.minimum(jnp.sum((pend[None, :] <= block_start[:, None]).astype(I32), axis=1), MOE_EXPERTS - 1)
    n_used = (pend[-1:] // MOE_BLOCK).astype(I32)

    xs = _dispatch(jnp.concatenate([pend, padded]).astype(I32), dest, h, norm_gain[None, :], n_blocks * MOE_BLOCK)
    ys = _experts(block_eid, n_used, xs, w1.astype(BF16), w3.astype(BF16), w2.astype(BF16))
    return _combine(dest, gate.T, h, out_gain[None, :], ys, final_norm)


def kernel(x, meta, norm_mix, norm_ffn, norm_final, ret_w_in, ret_w_out, ret_norm, ssm_w_in, ssm_conv_w,
           ssm_conv_b, ssm_dt_bias, ssm_a_log, ssm_d, ssm_norm, ssm_w_out, moe_wg, moe_bg, moe_we, moe_be,
           moe_w1, moe_w3, moe_w2):
    b, seq, d = x.shape
    lp = CHUNK + seq
    t = b * lp
    depth = norm_mix.shape[0]
    h = jnp.concatenate([jnp.zeros((b, PAD, d), F32), jnp.broadcast_to(meta[None], (b, N_META, d)), x], axis=1)

    half = RET_QK_DIM // 2
    inv_freq = 1.0 / (RET_ROPE_BASE ** jnp.linspace(0.0, 1.0, half, dtype=F32))
    ang = (jnp.arange(lp, dtype=F32) - PAD)[:, None] * inv_freq[None, :]
    cos, sin = jnp.cos(ang), jnp.sin(ang)
    perm = jnp.concatenate([jnp.arange(0, RET_QK_DIM, 2), jnp.arange(1, RET_QK_DIM, 2)])
    qk_perm = (jnp.arange(2 * RET_HEADS)[:, None] * RET_QK_DIM + perm[None, :]).reshape(-1)

    for i in range(depth):
        j = i // 2
        if i % 2 == 0:
            w_in = jnp.concatenate([ret_w_in[j][:, qk_perm], ret_w_in[j][:, 2 * RET_QK:]], axis=1).astype(BF16)
            qkvg = _ret_in(h.reshape(b, lp, d), norm_mix[i][None, :], cos, sin, w_in)
            y = _retention(qkvg, ret_norm[j][None, :])
            w_out = ret_w_out[j]
        else:
            w_in = jnp.concatenate([ssm_w_in[j], jnp.zeros((d, LANES - SSM_HEADS), F32)], axis=1).astype(BF16)
            zxbc, dt = _ssm_in(h.reshape(t, d), norm_mix[i][None, :], w_in, ssm_conv_w[j], ssm_conv_b[j][None, :], lp)
            lane_pad = lambda v: jnp.concatenate([v, jnp.zeros((LANES - SSM_HEADS,), F32)])[None, :]
            y = _ssd(zxbc.reshape(b, lp, SSM_MAIN), dt.reshape(b, lp, LANES), lane_pad(ssm_dt_bias[j]),
                     lane_pad(ssm_a_log[j]), jnp.repeat(ssm_d[j], SSM_HEAD_DIM)[None, :], ssm_norm[j][None, :])
            w_out = ssm_w_out[j]
        last = i == depth - 1
        h = _moe_layer(h.reshape(t, d), y.reshape(t, -1), w_out, norm_ffn[i], moe_wg[i], moe_bg[i], moe_we[i],
                       moe_be[i], moe_w1[i], moe_w3[i], moe_w2[i], norm_final if last else norm_ffn[i], last)
    return h.reshape(b, lp, d)[:, CHUNK:]
```

```python
import functools
import math

import jax
import jax.numpy as jnp
from jax import lax
from jax.experimental import pallas as pl
from jax.experimental.pallas import tpu as pltpu

F32 = jnp.float32
BF16 = jnp.bfloat16
I32 = jnp.int32

D_MODEL = 1024
N_META = 16
CHUNK = 128
PAD = CHUNK - N_META
EPS = 1e-6

RET_HEADS = 4
RET_QK_DIM = 256
RET_V_DIM = 512
RET_QK = RET_HEADS * RET_QK_DIM
RET_V = RET_HEADS * RET_V_DIM
RET_IN = 2 * RET_QK + 2 * RET_V
RET_ROPE_BASE = 10000.0

SSM_INNER = 2048
SSM_HEAD_DIM = 64
SSM_HEADS = 32
SSM_GROUPS = 8
SSM_HPG = 4
SSM_STATE = 128
SSM_CONV = 4
SSM_BC = SSM_GROUPS * SSM_STATE
SSM_CONV_DIM = SSM_INNER + 2 * SSM_BC
SSM_MAIN = SSM_INNER + SSM_CONV_DIM
LANES = 128
SSM_GROUP_W = SSM_HPG * SSM_HEAD_DIM

MOE_GROUPS = 4
MOE_EPG = 8
MOE_EXPERTS = 32
MOE_FF = 512
MOE_BLOCK = 256
MOE_PAIRS = MOE_EPG * (MOE_EPG - 1) // 2
MOE_BUCKETS = MOE_GROUPS * MOE_PAIRS
MOE_BUCKET_ROWS = 128
ROUTER_ROWS = 40

SSM_IN_STEP = 512
LOG2E = math.log2(math.e)
TOKEN_TILE = 512
VMEM_LIMIT = 56 * 1024 * 1024


def _cparams(*sem):
    return pltpu.CompilerParams(dimension_semantics=sem, vmem_limit_bytes=VMEM_LIMIT)


def _rms(x, gain):
    ms = jnp.mean(x * x, axis=-1, keepdims=True)
    return x * lax.rsqrt(ms + EPS) * gain


def _silu(x):
    hx = 0.5 * x
    return hx + hx * jnp.tanh(hx)


def _dot(a, b):
    return jnp.dot(a, b, preferred_element_type=F32)


def _dot_nt(a, b, precision=None):
    return lax.dot_general(a, b, (((1,), (1,)), ((), ())), preferred_element_type=F32, precision=precision)


def _dot_tn(a, b):
    return lax.dot_general(a, b, (((0,), (0,)), ((), ())), preferred_element_type=F32)


def _row_tile(rows, cap=544):
    best = 16
    for t in range(16, cap + 1, 16):
        if rows % t == 0:
            best = t
    return best


def _ret_in_kernel(h_ref, g_ref, cos_ref, sin_ref, w_ref, o_ref):
    xn = _rms(h_ref[0], g_ref[...]).astype(BF16)
    cos = cos_ref[...]
    sin = sin_ref[...]
    half = RET_QK_DIM // 2
    for j in range(2 * RET_HEADS):
        c0 = j * RET_QK_DIM
        acc = _dot(xn, w_ref[:, c0:c0 + RET_QK_DIM])
        if j >= RET_HEADS:
            acc = acc * (RET_QK_DIM ** -0.5)
        e = acc[:, :half]
        o = acc[:, half:]
        o_ref[0, :, c0:c0 + half] = (e * cos - o * sin).astype(BF16)
        o_ref[0, :, c0 + half:c0 + RET_QK_DIM] = (o * cos + e * sin).astype(BF16)
    step = 1024
    for c0 in range(2 * RET_QK, RET_IN, step):
        o_ref[0, :, c0:c0 + step] = _dot(xn, w_ref[:, c0:c0 + step]).astype(BF16)


def _ret_in(h, gain, cos, sin, w):
    b, lp, d = h.shape
    tm = _row_tile(lp)
    return pl.pallas_call(
        _ret_in_kernel,
        grid=(b, lp // tm),
        in_specs=[
            pl.BlockSpec((1, tm, d), lambda i, j: (i, j, 0)),
            pl.BlockSpec((1, d), lambda i, j: (0, 0)),
            pl.BlockSpec((tm, RET_QK_DIM // 2), lambda i, j: (j, 0)),
            pl.BlockSpec((tm, RET_QK_DIM // 2), lambda i, j: (j, 0)),
            pl.BlockSpec((d, RET_IN), lambda i, j: (0, 0)),
        ],
        out_specs=pl.BlockSpec((1, tm, RET_IN), lambda i, j: (i, j, 0)),
        out_shape=jax.ShapeDtypeStruct((b, lp, RET_IN), BF16),
        compiler_params=_cparams("parallel", "parallel"),
        name="ret_in_proj",
    )(h, gain, cos, sin, w)


def _ret_kernel(q_ref, k_ref, v_ref, g_ref, hn_ref, o_ref, state_ref):
    @pl.when(pl.program_id(1) == 0)
    def _():
        state_ref[...] = jnp.zeros_like(state_ref)

    row = lax.broadcasted_iota(I32, (CHUNK, CHUNK), 0).astype(F32)
    col = lax.broadcasted_iota(I32, (CHUNK, CHUNK), 1).astype(F32)
    diff = row - col
    ridx = row[:, :1]
    for hh in range(RET_HEADS):
        lg = math.log(1.0 - 2.0 ** (-5.0 - hh))
        intra = jnp.where(diff >= 0, jnp.exp(lg * jnp.maximum(diff, 0.0)), 0.0)
        qdec = jnp.exp(lg * (ridx + 1.0))
        kdec = jnp.exp(lg * (CHUNK - 1.0 - ridx))
        cdec = math.exp(lg * CHUNK)
        qh = q_ref[0, :, hh * RET_QK_DIM:(hh + 1) * RET_QK_DIM]
        kh = k_ref[0, :, hh * RET_QK_DIM:(hh + 1) * RET_QK_DIM]
        vh = v_ref[0, :, hh * RET_V_DIM:(hh + 1) * RET_V_DIM]
        scores = _dot_nt(qh, kh) * intra
        st = state_ref[hh]
        y = _dot(scores.astype(BF16), vh) + _dot(qh, st.astype(BF16)) * qdec
        kd = (kh.astype(F32) * kdec).astype(BF16)
        state_ref[hh] = st * cdec + _dot_tn(kd, vh)
        yn = _rms(y, hn_ref[:, hh * RET_V_DIM:(hh + 1) * RET_V_DIM])
        gh = g_ref[0, :, hh * RET_V_DIM:(hh + 1) * RET_V_DIM].astype(F32)
        o_ref[0, :, hh * RET_V_DIM:(hh + 1) * RET_V_DIM] = (_silu(gh) * yn).astype(BF16)


def _retention(qkvg, head_norm):
    b, lp, _ = qkvg.shape
    nc = lp // CHUNK
    return pl.pallas_call(
        _ret_kernel,
        grid=(b, nc),
        in_specs=[
            pl.BlockSpec((1, CHUNK, RET_QK), lambda i, c: (i, c, 0)),
            pl.BlockSpec((1, CHUNK, RET_QK), lambda i, c: (i, c, 1)),
            pl.BlockSpec((1, CHUNK, RET_V), lambda i, c: (i, c, 1)),
            pl.BlockSpec((1, CHUNK, RET_V), lambda i, c: (i, c, 2)),
            pl.BlockSpec((1, RET_V), lambda i, c: (0, 0)),
        ],
        out_specs=pl.BlockSpec((1, CHUNK, RET_V), lambda i, c: (i, c, 0)),
        out_shape=jax.ShapeDtypeStruct((b, lp, RET_V), BF16),
        scratch_shapes=[pltpu.VMEM((RET_HEADS, RET_QK_DIM, RET_V_DIM), F32)],
        compiler_params=_cparams("parallel", "arbitrary"),
        name="retention",
    )(qkvg, qkvg, qkvg, qkvg, head_norm)


def _ssm_in_kernel(h_ref, ym_ref, g_ref, w_ref, cw_ref, cb_ref, ho_ref, o_ref, dt_ref, tail_ref, acc_ref, *,
                   rows_per_seq):
    i = pl.program_id(0)
    tm = h_ref.shape[0]

    @pl.when(i == 0)
    def _():
        tail_ref[...] = jnp.zeros_like(tail_ref)

    h = h_ref[...] + ym_ref[...].reshape(h_ref.shape)
    ho_ref[...] = h
    xn = _rms(h, g_ref[...]).astype(BF16)
    step = acc_ref.shape[2]
    n_steps = SSM_MAIN // step

    pos = lax.rem(i * tm, rows_per_seq) + lax.broadcasted_iota(I32, (tm, 1), 0)
    is_pad = jnp.logical_or(pos < PAD, jnp.logical_and(pos >= rows_per_seq, pos < rows_per_seq + PAD))
    row8 = lax.broadcasted_iota(I32, (8, 1), 0)

    def project(j):
        acc_ref[j % 2] = _dot(xn, w_ref[:, j * step:(j + 1) * step])

    project(0)
    for j in range(n_steps):
        if j + 1 < n_steps:
            project(j + 1)
        c0 = j * step
        acc = acc_ref[j % 2]
        if c0 < SSM_INNER:
            o_ref[:, c0:c0 + step] = _silu(acc).astype(BF16)
            continue
        cc = c0 - SSM_INNER
        pre = jnp.where(is_pad, 0.0, acc)
        tail = tail_ref[:, cc:cc + step]
        tail_ref[:, cc:cc + step] = pre[tm - 8:, :]
        conv = cb_ref[:, cc:cc + step] + cw_ref[SSM_CONV - 1:SSM_CONV, cc:cc + step] * pre
        for k in range(1, SSM_CONV):
            rolled = pltpu.roll(pre, k, axis=0)
            top = jnp.where(row8 < k, pltpu.roll(tail, k, axis=0), rolled[:8])
            shifted = jnp.concatenate([top, rolled[8:]], axis=0)
            conv = conv + cw_ref[SSM_CONV - 1 - k:SSM_CONV - k, cc:cc + step] * shifted
        o_ref[:, c0:c0 + step] = _silu(conv).astype(BF16)
    dt_ref[...] = _dot(xn, w_ref[:, SSM_MAIN:SSM_MAIN + LANES])


def _ssm_in(h, y_moe, gain, w, conv_w, conv_b, rows_per_seq):
    t, d = h.shape
    tm = TOKEN_TILE
    assert tm <= rows_per_seq
    const = lambda r, c: pl.BlockSpec((r, c), lambda i: (0, 0))
    return pl.pallas_call(
        functools.partial(_ssm_in_kernel, rows_per_seq=rows_per_seq),
        grid=(t // tm,),
        in_specs=[
            pl.BlockSpec((tm, d), lambda i: (i, 0)),
            pl.BlockSpec((tm, d // LANES, LANES), lambda i: (i, 0, 0)),
            const(1, d), const(d, SSM_MAIN + LANES), const(SSM_CONV, SSM_CONV_DIM), const(1, SSM_CONV_DIM),
        ],
        out_specs=[
            pl.BlockSpec((tm, d), lambda i: (i, 0)),
            pl.BlockSpec((tm, SSM_MAIN), lambda i: (i, 0)),
            pl.BlockSpec((tm, LANES), lambda i: (i, 0)),
        ],
        out_shape=[jax.ShapeDtypeStruct((t, d), F32), jax.ShapeDtypeStruct((t, SSM_MAIN), BF16),
                   jax.ShapeDtypeStruct((t, LANES), F32)],
        scratch_shapes=[pltpu.VMEM((8, SSM_CONV_DIM), F32), pltpu.VMEM((2, tm, SSM_IN_STEP), F32)],
        compiler_params=_cparams("arbitrary"),
        name="ssm_in_proj",
    )(h, y_moe, gain, w, conv_w, conv_b)


def _ssd_kernel(zg_ref, x_ref, bc_ref, dt_ref, dtb_ref, alog_ref, dsk_ref, gn_ref, o_ref, state_ref):
    c = pl.program_id(1)

    @pl.when(c == 0)
    def _():
        state_ref[...] = jnp.zeros_like(state_ref)

    row = lax.broadcasted_iota(I32, (CHUNK, CHUNK), 0)
    col = lax.broadcasted_iota(I32, (CHUNK, CHUNK), 1)
    causal_bias = jnp.where(row >= col, 0.0, -jnp.inf)
    low_half = col < SSM_HEAD_DIM
    valid = row[:, :1] >= jnp.where(c > 0, 0, PAD)

    dtr = dt_ref[0] + dtb_ref[...]
    dtv = jnp.maximum(dtr, 0.0) + jnp.log(1.0 + jnp.exp(-jnp.abs(dtr)))
    dtv = jnp.where(valid, dtv, 0.0)
    da = dtv * (-jnp.exp(alog_ref[...]))
    tril = jnp.where(row >= col, 1.0, 0.0)
    acs = jnp.dot(tril, da, preferred_element_type=F32, precision=lax.Precision.HIGHEST)
    last = acs[CHUNK - 1:CHUNK, :]
    acs2 = acs * LOG2E
    src_t = (jnp.log2(dtv) - acs2).T
    w_t = (dtv * jnp.exp(last - acs)).T
    elast = jnp.exp(last)

    def pair_row(v, h0):
        return jnp.where(low_half[:1], v[:, h0:h0 + 1], v[:, h0 + 1:h0 + 2])

    groups = range(SSM_GROUPS)
    gsl = [slice(gi * SSM_GROUP_W, (gi + 1) * SSM_GROUP_W) for gi in groups]
    bgs = [bc_ref[0, :, gi * SSM_STATE:(gi + 1) * SSM_STATE] for gi in groups]
    cgs = [bc_ref[0, :, SSM_BC + gi * SSM_STATE:SSM_BC + (gi + 1) * SSM_STATE] for gi in groups]
    sts = [state_ref[gi] for gi in groups]
    cbs = [_dot_nt(cgs[gi], bgs[gi]) for gi in groups]
    yoffs = [_dot(cgs[gi], sts[gi].astype(BF16)) for gi in groups]
    bts = [bgs[gi].astype(F32).T for gi in groups]

    outs = []
    eexp = []
    for gi in groups:
        for pp in range(SSM_HPG // 2):
            h0 = gi * SSM_HPG + 2 * pp
            top, bot, ecols = [], [], []
            for hd in (h0, h0 + 1):
                acs_b = jnp.broadcast_to(acs2[:, hd:hd + 1], (CHUNK, CHUNK))
                top.append(cbs[gi] * jnp.exp2(acs_b + (src_t[hd:hd + 1, :] + causal_bias)))
                bot.append(bts[gi] * w_t[hd:hd + 1, :])
                ecols.append(jnp.exp2(acs_b))
            lhs = jnp.concatenate([jnp.concatenate(top, axis=1), jnp.concatenate(bot, axis=1)], axis=0)
            xp = x_ref[0, :, h0 * SSM_HEAD_DIM:(h0 + 2) * SSM_HEAD_DIM]
            zero = jnp.zeros_like(xp)
            rhs = jnp.concatenate([jnp.where(low_half, xp, zero), jnp.where(low_half, zero, xp)], axis=0)
            outs.append(_dot(lhs.astype(BF16), rhs))
            eexp.append(jnp.where(low_half, ecols[0], ecols[1]))

    for gi in groups:
        pairs = [gi * (SSM_HPG // 2) + pp for pp in range(SSM_HPG // 2)]
        heads = [gi * SSM_HPG + 2 * pp for pp in range(SSM_HPG // 2)]
        lexp = jnp.concatenate([pair_row(elast, h0) for h0 in heads], axis=1)
        state_ref[gi] = sts[gi] * lexp + jnp.concatenate([outs[p][CHUNK:] for p in pairs], axis=1)
        xg = x_ref[0, :, gsl[gi]].astype(F32)
        y = jnp.concatenate([outs[p][:CHUNK] for p in pairs], axis=1)
        y = y + yoffs[gi] * jnp.concatenate([eexp[p] for p in pairs], axis=1)
        y = y + dsk_ref[:, gsl[gi]] * xg
        y = y * zg_ref[0, :, gsl[gi]].astype(F32)
        o_ref[0, :, gsl[gi]] = _rms(y, gn_ref[:, gsl[gi]]).astype(BF16)


def _ssd(zxbc, dt, dt_bias, a_log, d_skip, gate_norm):
    b, lp, _ = zxbc.shape
    nc = lp // CHUNK
    wide = lambda j: pl.BlockSpec((1, CHUNK, SSM_INNER), lambda i, c: (i, c, j))
    const = lambda r, w: pl.BlockSpec((r, w), lambda i, c: (0, 0))
    return pl.pallas_call(
        _ssd_kernel,
        grid=(b, nc),
        in_specs=[
            wide(0), wide(1), wide(2),
            pl.BlockSpec((1, CHUNK, LANES), lambda i, c: (i, c, 0)),
            const(1, LANES), const(1, LANES), const(1, SSM_INNER), const(1, SSM_INNER),
        ],
        out_specs=pl.BlockSpec((1, CHUNK, SSM_INNER), lambda i, c: (i, c, 0)),
        out_shape=jax.ShapeDtypeStruct((b, lp, SSM_INNER), BF16),
        scratch_shapes=[pltpu.VMEM((SSM_GROUPS, SSM_STATE, SSM_GROUP_W), F32)],
        compiler_params=_cparams("parallel", "arbitrary"),
        name="ssd",
    )(zxbc, zxbc, zxbc, dt, dt_bias, a_log, d_skip, gate_norm)


def _out_proj_kernel(h_ref, y_ref, w_ref, g_ref, wr_ref, br_ref, ho_ref, u_ref, lg_ref):
    hn = h_ref[...] + _dot(y_ref[...], w_ref[...])
    ho_ref[...] = hn
    u = _rms(hn, g_ref[...])
    u_ref[...] = u.reshape(u_ref.shape)
    u_hi = u.astype(BF16)
    u_lo = (u - u_hi.astype(F32)).astype(BF16)
    p = _dot(u_hi, wr_ref[...])
    logits = p[:, :LANES] + (p[:, LANES:] + _dot(u_lo, wr_ref[:, :LANES])) + br_ref[...]
    lg_ref[...] = logits.T[:ROUTER_ROWS]


def _out_proj(h, y, w, gain, wr_t, br):
    t, d = h.shape
    tm = TOKEN_TILE
    kin = y.shape[1]
    return pl.pallas_call(
        _out_proj_kernel,
        grid=(t // tm,),
        in_specs=[
            pl.BlockSpec((tm, d), lambda i: (i, 0)),
            pl.BlockSpec((tm, kin), lambda i: (i, 0)),
            pl.BlockSpec((kin, d), lambda i: (0, 0)),
            pl.BlockSpec((1, d), lambda i: (0, 0)),
            pl.BlockSpec((d, 2 * LANES), lambda i: (0, 0)),
            pl.BlockSpec((1, LANES), lambda i: (0, 0)),
        ],
        out_specs=[
            pl.BlockSpec((tm, d), lambda i: (i, 0)),
            pl.BlockSpec((tm, d // LANES, LANES), lambda i: (i, 0, 0)),
            pl.BlockSpec((ROUTER_ROWS, tm), lambda i: (0, i)),
        ],
        out_shape=[jax.ShapeDtypeStruct((t, d), F32), jax.ShapeDtypeStruct((t, d // LANES, LANES), F32),
                   jax.ShapeDtypeStruct((ROUTER_ROWS, t), F32)],
        compiler_params=_cparams("parallel"),
        name="out_proj_router",
    )(h, y, w, gain, wr_t, br)


def _first_argmax(v, n):
    ridx = lax.broadcasted_iota(I32, v.shape, 0).astype(F32)
    vmax = jnp.max(v, axis=0, keepdims=True)
    idx = jnp.min(jnp.where(v == vmax, ridx, float(n)), axis=0, keepdims=True)
    return vmax, idx.astype(I32)


def _route_kernel(lg_ref, bkt_ref, gate_ref, rank_ref, cnt_ref, carry_ref):
    @pl.when(pl.program_id(0) == 0)
    def _():
        carry_ref[...] = jnp.zeros_like(carry_ref)

    logits = lg_ref[...]
    tm = logits.shape[1]
    gl = logits[0:MOE_GROUPS]
    gmax, gsel = _first_argmax(gl, MOE_GROUPS)
    p_group = 1.0 / jnp.sum(jnp.exp(gl - gmax), axis=0, keepdims=True)
    el = logits[MOE_GROUPS:MOE_GROUPS + MOE_EPG]
    for gg in range(1, MOE_GROUPS):
        el = jnp.where(gsel == gg, logits[MOE_GROUPS + gg * MOE_EPG:MOE_GROUPS + (gg + 1) * MOE_EPG], el)
    ex = jnp.exp(el - jnp.max(el, axis=0, keepdims=True))
    p = ex / jnp.sum(ex, axis=0, keepdims=True)
    p1, i1 = _first_argmax(p, MOE_EPG)
    ridx8 = lax.broadcasted_iota(I32, p.shape, 0)
    p2, i2 = _first_argmax(jnp.where(ridx8 == i1, -1.0, p), MOE_EPG)
    denom = p1 + p2
    gate1 = p_group * p1 / denom
    gate2 = p_group * p2 / denom
    first_low = i1 < i2
    gate_ref[0:1, :] = jnp.where(first_low, gate1, gate2)
    gate_ref[1:2, :] = jnp.where(first_low, gate2, gate1)
    lo = jnp.minimum(i1, i2).astype(F32)
    hi = jnp.maximum(i1, i2).astype(F32)
    pair = lo * (2.0 * MOE_EPG - 1.0 - lo) * 0.5 + (hi - lo - 1.0)
    bucket = gsel * MOE_PAIRS + pair.astype(I32)
    bkt_ref[...] = bucket

    ridx = lax.broadcasted_iota(I32, (MOE_BUCKET_ROWS, tm), 0)
    onehot = jnp.where(ridx == bucket, 1.0, 0.0)
    before = jnp.where(lax.broadcasted_iota(I32, (tm, tm), 0) < lax.broadcasted_iota(I32, (tm, tm), 1), 1.0, 0.0)
    prefix = _dot(onehot.astype(BF16), before.astype(BF16))
    carry = carry_ref[...]
    rank_ref[...] = jnp.sum(onehot * (carry + prefix), axis=0, keepdims=True).astype(I32)
    carry = carry + jnp.sum(onehot, axis=1, keepdims=True)
    carry_ref[...] = carry
    cnt_ref[...] = jnp.broadcast_to(carry, cnt_ref.shape).astype(I32)


def _route(logits_t):
    t = logits_t.shape[1]
    tm = TOKEN_TILE
    rows = lambda r: pl.BlockSpec((r, tm), lambda i: (0, i))
    return pl.pallas_call(
        _route_kernel,
        grid=(t // tm,),
        in_specs=[rows(ROUTER_ROWS)],
        out_specs=[rows(1), rows(2), rows(1), pl.BlockSpec((MOE_BUCKET_ROWS, LANES), lambda i: (0, 0))],
        out_shape=[jax.ShapeDtypeStruct((1, t), I32), jax.ShapeDtypeStruct((2, t), F32),
                   jax.ShapeDtypeStruct((1, t), I32), jax.ShapeDtypeStruct((MOE_BUCKET_ROWS, LANES), I32)],
        scratch_shapes=[pltpu.VMEM((MOE_BUCKET_ROWS, 1), F32)],
        compiler_params=_cparams("arbitrary"),
        name="route_rank",
    )(logits_t)


def _moe_kernel(ea_ref, eb_ref, nv_ref, nu_ref, src_ref, nxt_ref, gate_ref, w1a_ref, w3a_ref, w2a_ref,
                w1b_ref, w3b_ref, w2b_ref, u_ref, y_ref, xbuf, ybuf, gsem, ssem):
    b = pl.program_id(0)
    n_used = nu_ref[0]
    slot = lax.rem(b, 2)

    def gather(idx_ref, blk, s):
        def issue(r, carry):
            pltpu.make_async_copy(u_ref.at[idx_ref[0, 0, r]], xbuf.at[s, r], gsem.at[s]).start()
            return carry

        lax.fori_loop(0, nv_ref[blk], issue, 0)

    def wait_rows(count, buf, s, sem):
        @pl.when(count > 0)
        def _():
            pltpu.make_async_copy(u_ref.at[pl.ds(0, count)], buf.at[s, pl.ds(0, count)], sem.at[s]).wait()

    @pl.when(b == 0)
    def _():
        xbuf[...] = jnp.zeros_like(xbuf)
        gather(src_ref, 0, 0)

    @pl.when(b < n_used)
    def _():
        @pl.when(b + 1 < n_used)
        def _():
            gather(nxt_ref, b + 1, 1 - slot)

        wait_rows(nv_ref[b], xbuf, slot, gsem)

        @pl.when(b >= 2)
        def _():
            wait_rows(nv_ref[jnp.maximum(b - 2, 0)], ybuf, slot, ssem)

        rows, sub, lanes = xbuf.shape[1:]
        x = xbuf[slot].reshape(rows, sub * lanes).astype(BF16)
        gates = gate_ref[0].T

        def ffn(w1_ref, w3_ref, w2_ref):
            hid = _silu(_dot(x, w1_ref[0])) * _dot(x, w3_ref[0])
            return _dot(hid.astype(BF16), w2_ref[0])

        y = ffn(w1a_ref, w3a_ref, w2a_ref) * gates[:, 0:1] + ffn(w1b_ref, w3b_ref, w2b_ref) * gates[:, 1:2]
        ybuf[slot] = y.reshape(rows, sub, lanes)

        def issue(r, carry):
            pltpu.make_async_copy(ybuf.at[slot, r], y_ref.at[src_ref[0, 0, r]], ssem.at[slot]).start(priority=1)
            return carry

        lax.fori_loop(0, nv_ref[b], issue, 0)

        @pl.when(b == n_used - 1)
        def _():
            wait_rows(nv_ref[b], ybuf, slot, ssem)

            @pl.when(b >= 1)
            def _():
                wait_rows(nv_ref[jnp.maximum(b - 1, 0)], ybuf, 1 - slot, ssem)


def _moe_experts(block_ea, block_eb, n_valid, n_used, src, gates, w1, w3, w2, u):
    nb = src.shape[0]
    t, sub, lanes = u.shape
    d = sub * lanes
    smem_blk = lambda fn: pl.BlockSpec((1, 1, MOE_BLOCK), fn, memory_space=pltpu.SMEM)
    w_in = lambda tbl: pl.BlockSpec((1, d, MOE_FF), lambda b, ea, eb, nv, nu: ((ea, eb)[tbl][b], 0, 0))
    w_out = lambda tbl: pl.BlockSpec((1, MOE_FF, d), lambda b, ea, eb, nv, nu: ((ea, eb)[tbl][b], 0, 0))
    return pl.pallas_call(
        _moe_kernel,
        grid_spec=pltpu.PrefetchScalarGridSpec(
            num_scalar_prefetch=4,
            grid=(nb,),
            in_specs=[
                smem_blk(lambda b, ea, eb, nv, nu: (b, 0, 0)),
                smem_blk(lambda b, ea, eb, nv, nu: (jnp.minimum(b + 1, nb - 1), 0, 0)),
                pl.BlockSpec((1, 8, MOE_BLOCK), lambda b, ea, eb, nv, nu: (b, 0, 0)),
                w_in(0), w_in(0), w_out(0), w_in(1), w_in(1), w_out(1),
                pl.BlockSpec(memory_space=pl.ANY),
            ],
            out_specs=pl.BlockSpec(memory_space=pl.ANY),
            scratch_shapes=[pltpu.VMEM((2, MOE_BLOCK, sub, lanes), F32), pltpu.VMEM((2, MOE_BLOCK, sub, lanes), F32),
                            pltpu.SemaphoreType.DMA((2,)), pltpu.SemaphoreType.DMA((2,))],
        ),
        out_shape=jax.ShapeDtypeStruct((t, sub, lanes), F32),
        compiler_params=_cparams("arbitrary"),
        name="moe_experts",
    )(block_ea, block_eb, n_valid, n_used, src, src, gates, w1, w3, w2, w1, w3, w2, u)


def _final_kernel(h_ref, y_ref, g_ref, o_ref):
    rows, d = h_ref.shape[1:]
    o_ref[0] = _rms(h_ref[0] + y_ref[...].reshape(rows, d), g_ref[...])


def _final_norm(h, y_moe, gain, seq):
    b, lp, d = h.shape
    nc = lp // CHUNK
    return pl.pallas_call(
        _final_kernel,
        grid=(b, seq // CHUNK),
        in_specs=[
            pl.BlockSpec((1, CHUNK, d), lambda i, c: (i, c + 1, 0)),
            pl.BlockSpec((CHUNK, d // LANES, LANES), lambda i, c: (i * nc + c + 1, 0, 0)),
            pl.BlockSpec((1, d), lambda i, c: (0, 0)),
        ],
        out_specs=pl.BlockSpec((1, CHUNK, d), lambda i, c: (i, c, 0)),
        out_shape=jax.ShapeDtypeStruct((b, seq, d), F32),
        compiler_params=_cparams("parallel", "parallel"),
        name="final_norm",
    )(h, y_moe, gain)


def _bucket_experts():
    lo, hi = [], []
    for g in range(MOE_GROUPS):
        for a in range(MOE_EPG):
            for c in range(a + 1, MOE_EPG):
                lo.append(g * MOE_EPG + a)
                hi.append(g * MOE_EPG + c)
    fill = MOE_BUCKET_ROWS - len(lo)
    return jnp.array(lo + [lo[-1]] * fill, I32), jnp.array(hi + [hi[-1]] * fill, I32)


def _moe_layer(h, y_mix, w_out, norm_gain, wg, bg, we, be, w1, w3, w2):
    t, d = h.shape
    n_logits = MOE_GROUPS + MOE_EXPERTS
    wr = jnp.concatenate([wg, we.reshape(d, MOE_EXPERTS), jnp.zeros((d, LANES - n_logits), F32)], axis=1)
    br = jnp.concatenate([bg, be.reshape(-1), jnp.zeros((LANES - n_logits,), F32)])
    wr_hi = wr.astype(BF16)
    wr_split = jnp.concatenate([wr_hi, (wr - wr_hi.astype(F32)).astype(BF16)], axis=1)
    h, u, logits_t = _out_proj(h, y_mix, w_out.astype(BF16), norm_gain[None, :], wr_split, br[None, :])
    bucket, gate, rank, counts = _route(logits_t)

    counts = counts[:, 0]
    padded = (counts + MOE_BLOCK - 1) // MOE_BLOCK * MOE_BLOCK
    pend = jnp.cumsum(padded)
    pstart = pend - padded
    ids = jnp.arange(MOE_BUCKET_ROWS, dtype=I32)
    dest = rank[0] + jnp.sum(jnp.where(bucket == ids[:, None], pstart[:, None], 0), axis=0)
    n_blocks = -(-t // MOE_BLOCK) + MOE_BUCKETS
    block_start = jnp.arange(n_blocks, dtype=I32) * MOE_BLOCK
    block_bucket = jnp.minimum(jnp.sum((pend[None, :] <= block_start[:, None]).astype(I32), axis=1),
                               MOE_BUCKET_ROWS - 1)
    n_used = (pend[-1:] // MOE_BLOCK).astype(I32)
    n_valid = jnp.clip((pstart + counts)[block_bucket] - block_start, 0, MOE_BLOCK).astype(I32)
    lo, hi = _bucket_experts()
    src = jnp.zeros((n_blocks * MOE_BLOCK,), I32).at[dest].set(jnp.arange(t, dtype=I32))
    gates = jnp.concatenate([gate[:, src].reshape(2, n_blocks, MOE_BLOCK),
                             jnp.zeros((6, n_blocks, MOE_BLOCK), F32)], axis=0).transpose(1, 0, 2)

    y_moe = _moe_experts(lo[block_bucket], hi[block_bucket], n_valid, n_used, src.reshape(n_blocks, 1, MOE_BLOCK),
                         gates, w1.astype(BF16), w3.astype(BF16), w2.astype(BF16), u)
    return h, y_moe


def kernel(x, meta, norm_mix, norm_ffn, norm_final, ret_w_in, ret_w_out, ret_norm, ssm_w_in, ssm_conv_w,
           ssm_conv_b, ssm_dt_bias, ssm_a_log, ssm_d, ssm_norm, ssm_w_out, moe_wg, moe_bg, moe_we, moe_be,
           moe_w1, moe_w3, moe_w2):
    b, seq, d = x.shape
    lp = CHUNK + seq
    t = b * lp
    depth = norm_mix.shape[0]
    h = jnp.concatenate([jnp.zeros((b, PAD, d), F32), jnp.broadcast_to(meta[None], (b, N_META, d)), x], axis=1)

    half = RET_QK_DIM // 2
    inv_freq = 1.0 / (RET_ROPE_BASE ** jnp.linspace(0.0, 1.0, half, dtype=F32))
    ang = (jnp.arange(lp, dtype=F32) - PAD)[:, None] * inv_freq[None, :]
    cos, sin = jnp.cos(ang), jnp.sin(ang)
    perm = jnp.concatenate([jnp.arange(0, RET_QK_DIM, 2), jnp.arange(1, RET_QK_DIM, 2)])
    qk_perm = (jnp.arange(2 * RET_HEADS)[:, None] * RET_QK_DIM + perm[None, :]).reshape(-1)

    h = h.reshape(t, d)
    y_moe = None
    for i in range(depth):
        j = i // 2
        if i % 2 == 0:
            if y_moe is not None:
                h = h + y_moe.reshape(t, d)
            w_in = jnp.concatenate([ret_w_in[j][:, qk_perm], ret_w_in[j][:, 2 * RET_QK:]], axis=1).astype(BF16)
            qkvg = _ret_in(h.reshape(b, lp, d), norm_mix[i][None, :], cos, sin, w_in)
            y = _retention(qkvg, ret_norm[j][None, :])
            w_out = ret_w_out[j]
        else:
            if y_moe is None:
                y_moe = jnp.zeros((t, d // LANES, LANES), F32)
            w_in = jnp.concatenate([ssm_w_in[j], jnp.zeros((d, LANES - SSM_HEADS), F32)], axis=1).astype(BF16)
            h, zxbc, dt = _ssm_in(h, y_moe, norm_mix[i][None, :], w_in, ssm_conv_w[j], ssm_conv_b[j][None, :], lp)
            lane_pad = lambda v: jnp.concatenate([v, jnp.zeros((LANES - SSM_HEADS,), F32)])[None, :]
            y = _ssd(zxbc.reshape(b, lp, SSM_MAIN), dt.reshape(b, lp, LANES), lane_pad(ssm_dt_bias[j]),
                     lane_pad(ssm_a_log[j]), jnp.repeat(ssm_d[j], SSM_HEAD_DIM)[None, :], ssm_norm[j][None, :])
            w_out = ssm_w_out[j]
        h, y_moe = _moe_layer(h, y.reshape(t, -1), w_out, norm_ffn[i], moe_wg[i], moe_bg[i], moe_we[i],
                              moe_be[i], moe_w1[i], moe_w3[i], moe_w2[i])
    return _final_norm(h.reshape(b, lp, d), y_moe, norm_final[None, :], seq)
```

```python
import functools
import math

import jax
import jax.numpy as jnp
from jax import lax
from jax.experimental import pallas as pl
from jax.experimental.pallas import tpu as pltpu

F32 = jnp.float32
BF16 = jnp.bfloat16
I32 = jnp.int32

D_MODEL = 1024
N_META = 16
CHUNK = 128
PAD = CHUNK - N_META
EPS = 1e-6

RET_HEADS = 4
RET_QK_DIM = 256
RET_V_DIM = 512
RET_QK = RET_HEADS * RET_QK_DIM
RET_V = RET_HEADS * RET_V_DIM
RET_IN = 2 * RET_QK + 2 * RET_V
RET_ROPE_BASE = 10000.0

SSM_INNER = 2048
SSM_HEAD_DIM = 64
SSM_HEADS = 32
SSM_GROUPS = 8
SSM_HPG = 4
SSM_STATE = 128
SSM_CONV = 4
SSM_BC = SSM_GROUPS * SSM_STATE
SSM_CONV_DIM = SSM_INNER + 2 * SSM_BC
SSM_MAIN = SSM_INNER + SSM_CONV_DIM
LANES = 128
SSM_GROUP_W = SSM_HPG * SSM_HEAD_DIM

MOE_GROUPS = 4
MOE_EPG = 8
MOE_EXPERTS = 32
MOE_FF = 512
MOE_BLOCK = 256
MOE_PAIRS = MOE_EPG * (MOE_EPG - 1) // 2
MOE_BUCKETS = MOE_GROUPS * MOE_PAIRS
MOE_BUCKET_ROWS = 128
ROUTER_ROWS = 40

SSM_IN_STEP = 512
LOG2E = math.log2(math.e)
TOKEN_TILE = 512
VMEM_LIMIT = 56 * 1024 * 1024


def _cparams(*sem):
    return pltpu.CompilerParams(dimension_semantics=sem, vmem_limit_bytes=VMEM_LIMIT)


def _rms(x, gain):
    ms = jnp.mean(x * x, axis=-1, keepdims=True)
    return x * lax.rsqrt(ms + EPS) * gain


def _silu(x):
    hx = 0.5 * x
    return hx + hx * jnp.tanh(hx)


def _dot(a, b):
    return jnp.dot(a, b, preferred_element_type=F32)


def _dot_nt(a, b, precision=None):
    return lax.dot_general(a, b, (((1,), (1,)), ((), ())), preferred_element_type=F32, precision=precision)


def _dot_tn(a, b):
    return lax.dot_general(a, b, (((0,), (0,)), ((), ())), preferred_element_type=F32)


def _row_tile(rows, cap=544):
    best = 16
    for t in range(16, cap + 1, 16):
        if rows % t == 0:
            best = t
    return best


def _ret_in_kernel(h_ref, g_ref, cos_ref, sin_ref, w_ref, o_ref):
    xn = _rms(h_ref[0], g_ref[...]).astype(BF16)
    cos = cos_ref[...]
    sin = sin_ref[...]
    half = RET_QK_DIM // 2
    for j in range(2 * RET_HEADS):
        c0 = j * RET_QK_DIM
        acc = _dot(xn, w_ref[:, c0:c0 + RET_QK_DIM])
        if j >= RET_HEADS:
            acc = acc * (RET_QK_DIM ** -0.5)
        e = acc[:, :half]
        o = acc[:, half:]
        o_ref[0, :, c0:c0 + half] = (e * cos - o * sin).astype(BF16)
        o_ref[0, :, c0 + half:c0 + RET_QK_DIM] = (o * cos + e * sin).astype(BF16)
    step = 1024
    for c0 in range(2 * RET_QK, RET_IN, step):
        o_ref[0, :, c0:c0 + step] = _dot(xn, w_ref[:, c0:c0 + step]).astype(BF16)


def _ret_in(h, gain, cos, sin, w):
    b, lp, d = h.shape
    tm = _row_tile(lp)
    return pl.pallas_call(
        _ret_in_kernel,
        grid=(b, lp // tm),
        in_specs=[
            pl.BlockSpec((1, tm, d), lambda i, j: (i, j, 0)),
            pl.BlockSpec((1, d), lambda i, j: (0, 0)),
            pl.BlockSpec((tm, RET_QK_DIM // 2), lambda i, j: (j, 0)),
            pl.BlockSpec((tm, RET_QK_DIM // 2), lambda i, j: (j, 0)),
            pl.BlockSpec((d, RET_IN), lambda i, j: (0, 0)),
        ],
        out_specs=pl.BlockSpec((1, tm, RET_IN), lambda i, j: (i, j, 0)),
        out_shape=jax.ShapeDtypeStruct((b, lp, RET_IN), BF16),
        compiler_params=_cparams("parallel", "parallel"),
        name="ret_in_proj",
    )(h, gain, cos, sin, w)


def _ret_kernel(q_ref, k_ref, v_ref, g_ref, hn_ref, o_ref, state_ref):
    @pl.when(pl.program_id(1) == 0)
    def _():
        state_ref[...] = jnp.zeros_like(state_ref)

    row = lax.broadcasted_iota(I32, (CHUNK, CHUNK), 0).astype(F32)
    col = lax.broadcasted_iota(I32, (CHUNK, CHUNK), 1).astype(F32)
    diff = row - col
    ridx = row[:, :1]
    for hh in range(RET_HEADS):
        lg = math.log(1.0 - 2.0 ** (-5.0 - hh))
        intra = jnp.where(diff >= 0, jnp.exp(lg * jnp.maximum(diff, 0.0)), 0.0)
        qdec = jnp.exp(lg * (ridx + 1.0))
        kdec = jnp.exp(lg * (CHUNK - 1.0 - ridx))
        cdec = math.exp(lg * CHUNK)
        qh = q_ref[0, :, hh * RET_QK_DIM:(hh + 1) * RET_QK_DIM]
        kh = k_ref[0, :, hh * RET_QK_DIM:(hh + 1) * RET_QK_DIM]
        vh = v_ref[0, :, hh * RET_V_DIM:(hh + 1) * RET_V_DIM]
        scores = _dot_nt(qh, kh) * intra
        st = state_ref[hh]
        y = _dot(scores.astype(BF16), vh) + _dot(qh, st.astype(BF16)) * qdec
        kd = (kh.astype(F32) * kdec).astype(BF16)
        state_ref[hh] = st * cdec + _dot_tn(kd, vh)
        yn = _rms(y, hn_ref[:, hh * RET_V_DIM:(hh + 1) * RET_V_DIM])
        gh = g_ref[0, :, hh * RET_V_DIM:(hh + 1) * RET_V_DIM].astype(F32)
        o_ref[0, :, hh * RET_V_DIM:(hh + 1) * RET_V_DIM] = (_silu(gh) * yn).astype(BF16)


def _retention(qkvg, head_norm):
    b, lp, _ = qkvg.shape
    nc = lp // CHUNK
    return pl.pallas_call(
        _ret_kernel,
        grid=(b, nc),
        in_specs=[
            pl.BlockSpec((1, CHUNK, RET_QK), lambda i, c: (i, c, 0)),
            pl.BlockSpec((1, CHUNK, RET_QK), lambda i, c: (i, c, 1)),
            pl.BlockSpec((1, CHUNK, RET_V), lambda i, c: (i, c, 1)),
            pl.BlockSpec((1, CHUNK, RET_V), lambda i, c: (i, c, 2)),
            pl.BlockSpec((1, RET_V), lambda i, c: (0, 0)),
        ],
        out_specs=pl.BlockSpec((1, CHUNK, RET_V), lambda i, c: (i, c, 0)),
        out_shape=jax.ShapeDtypeStruct((b, lp, RET_V), BF16),
        scratch_shapes=[pltpu.VMEM((RET_HEADS, RET_QK_DIM, RET_V_DIM), F32)],
        compiler_params=_cparams("parallel", "arbitrary"),
        name="retention",
    )(qkvg, qkvg, qkvg, qkvg, head_norm)


def _ssm_in_kernel(h_ref, ym_ref, g_ref, w_ref, cw_ref, cb_ref, ho_ref, o_ref, dt_ref, tail_ref, acc_ref, *,
                   rows_per_seq):
    i = pl.program_id(0)
    tm = h_ref.shape[0]

    @pl.when(i == 0)
    def _():
        tail_ref[...] = jnp.zeros_like(tail_ref)

    h = h_ref[...] + ym_ref[...].reshape(h_ref.shape)
    ho_ref[...] = h
    xn = _rms(h, g_ref[...]).astype(BF16)
    step = acc_ref.shape[2]
    n_steps = SSM_MAIN // step

    pos = lax.rem(i * tm, rows_per_seq) + lax.broadcasted_iota(I32, (tm, 1), 0)
    is_pad = jnp.logical_or(pos < PAD, jnp.logical_and(pos >= rows_per_seq, pos < rows_per_seq + PAD))
    row8 = lax.broadcasted_iota(I32, (8, 1), 0)

    def project(j):
        acc_ref[j % 2] = _dot(xn, w_ref[:, j * step:(j + 1) * step])

    project(0)
    for j in range(n_steps):
        if j + 1 < n_steps:
            project(j + 1)
        c0 = j * step
        acc = acc_ref[j % 2]
        if c0 < SSM_INNER:
            o_ref[:, c0:c0 + step] = _silu(acc).astype(BF16)
            continue
        cc = c0 - SSM_INNER
        pre = jnp.where(is_pad, 0.0, acc)
        tail = tail_ref[:, cc:cc + step]
        tail_ref[:, cc:cc + step] = pre[tm - 8:, :]
        conv = cb_ref[:, cc:cc + step] + cw_ref[SSM_CONV - 1:SSM_CONV, cc:cc + step] * pre
        for k in range(1, SSM_CONV):
            rolled = pltpu.roll(pre, k, axis=0)
            top = jnp.where(row8 < k, pltpu.roll(tail, k, axis=0), rolled[:8])
            shifted = jnp.concatenate([top, rolled[8:]], axis=0)
            conv = conv + cw_ref[SSM_CONV - 1 - k:SSM_CONV - k, cc:cc + step] * shifted
        o_ref[:, c0:c0 + step] = _silu(conv).astype(BF16)
    dt_ref[...] = _dot(xn, w_ref[:, SSM_MAIN:SSM_MAIN + LANES])


def _ssm_in(h, y_moe, gain, w, conv_w, conv_b, rows_per_seq):
    t, d = h.shape
    tm = TOKEN_TILE
    assert tm <= rows_per_seq
    const = lambda r, c: pl.BlockSpec((r, c), lambda i: (0, 0))
    return pl.pallas_call(
        functools.partial(_ssm_in_kernel, rows_per_seq=rows_per_seq),
        grid=(t // tm,),
        in_specs=[
            pl.BlockSpec((tm, d), lambda i: (i, 0)),
            pl.BlockSpec((tm, d // LANES, LANES), lambda i: (i, 0, 0)),
            const(1, d), const(d, SSM_MAIN + LANES), const(SSM_CONV, SSM_CONV_DIM), const(1, SSM_CONV_DIM),
        ],
        out_specs=[
            pl.BlockSpec((tm, d), lambda i: (i, 0)),
            pl.BlockSpec((tm, SSM_MAIN), lambda i: (i, 0)),
            pl.BlockSpec((tm, LANES), lambda i: (i, 0)),
        ],
        out_shape=[jax.ShapeDtypeStruct((t, d), F32), jax.ShapeDtypeStruct((t, SSM_MAIN), BF16),
                   jax.ShapeDtypeStruct((t, LANES), F32)],
        scratch_shapes=[pltpu.VMEM((8, SSM_CONV_DIM), F32), pltpu.VMEM((2, tm, SSM_IN_STEP), F32)],
        compiler_params=_cparams("arbitrary"),
        name="ssm_in_proj",
    )(h, y_moe, gain, w, conv_w, conv_b)


def _ssd_kernel(zg_ref, x_ref, bc_ref, dt_ref, dtb_ref, alog_ref, dsk_ref, gn_ref, o_ref, state_ref):
    c = pl.program_id(1)

    @pl.when(c == 0)
    def _():
        state_ref[...] = jnp.zeros_like(state_ref)

    row = lax.broadcasted_iota(I32, (CHUNK, CHUNK), 0)
    col = lax.broadcasted_iota(I32, (CHUNK, CHUNK), 1)
    causal_bias = jnp.where(row >= col, 0.0, -jnp.inf)
    low_half = col < SSM_HEAD_DIM
    valid = row[:, :1] >= jnp.where(c > 0, 0, PAD)

    dtr = dt_ref[0] + dtb_ref[...]
    dtv = jnp.maximum(dtr, 0.0) + jnp.log(1.0 + jnp.exp(-jnp.abs(dtr)))
    dtv = jnp.where(valid, dtv, 0.0)
    da = dtv * (-jnp.exp(alog_ref[...]))
    tril = jnp.where(row >= col, 1.0, 0.0)
    acs = jnp.dot(tril, da, preferred_element_type=F32, precision=lax.Precision.HIGHEST)
    last = acs[CHUNK - 1:CHUNK, :]
    acs2 = acs * LOG2E
    src_t = (jnp.log2(dtv) - acs2).T
    w_t = (dtv * jnp.exp(last - acs)).T
    elast = jnp.exp(last)

    def pair_row(v, h0):
        return jnp.where(low_half[:1], v[:, h0:h0 + 1], v[:, h0 + 1:h0 + 2])

    groups = range(SSM_GROUPS)
    gsl = [slice(gi * SSM_GROUP_W, (gi + 1) * SSM_GROUP_W) for gi in groups]
    bgs = [bc_ref[0, :, gi * SSM_STATE:(gi + 1) * SSM_STATE] for gi in groups]
    cgs = [bc_ref[0, :, SSM_BC + gi * SSM_STATE:SSM_BC + (gi + 1) * SSM_STATE] for gi in groups]
    sts = [state_ref[gi] for gi in groups]
    cbs = [_dot_nt(cgs[gi], bgs[gi]) for gi in groups]
    yoffs = [_dot(cgs[gi], sts[gi].astype(BF16)) for gi in groups]
    bts = [bgs[gi].astype(F32).T for gi in groups]

    outs = []
    eexp = []
    for gi in groups:
        for pp in range(SSM_HPG // 2):
            h0 = gi * SSM_HPG + 2 * pp
            top, bot, ecols = [], [], []
            for hd in (h0, h0 + 1):
                acs_b = jnp.broadcast_to(acs2[:, hd:hd + 1], (CHUNK, CHUNK))
                top.append(cbs[gi] * jnp.exp2(acs_b + (src_t[hd:hd + 1, :] + causal_bias)))
                bot.append(bts[gi] * w_t[hd:hd + 1, :])
                ecols.append(jnp.exp2(acs_b))
            lhs = jnp.concatenate([jnp.concatenate(top, axis=1), jnp.concatenate(bot, axis=1)], axis=0)
            xp = x_ref[0, :, h0 * SSM_HEAD_DIM:(h0 + 2) * SSM_HEAD_DIM]
            zero = jnp.zeros_like(xp)
            rhs = jnp.concatenate([jnp.where(low_half, xp, zero), jnp.where(low_half, zero, xp)], axis=0)
            outs.append(_dot(lhs.astype(BF16), rhs))
            eexp.append(jnp.where(low_half, ecols[0], ecols[1]))

    for gi in groups:
        pairs = [gi * (SSM_HPG // 2) + pp for pp in range(SSM_HPG // 2)]
        heads = [gi * SSM_HPG + 2 * pp for pp in range(SSM_HPG // 2)]
        lexp = jnp.concatenate([pair_row(elast, h0) for h0 in heads], axis=1)
        state_ref[gi] = sts[gi] * lexp + jnp.concatenate([outs[p][CHUNK:] for p in pairs], axis=1)
        xg = x_ref[0, :, gsl[gi]].astype(F32)
        y = jnp.concatenate([outs[p][:CHUNK] for p in pairs], axis=1)
        y = y + yoffs[gi] * jnp.concatenate([eexp[p] for p in pairs], axis=1)
        y = y + dsk_ref[:, gsl[gi]] * xg
        y = y * zg_ref[0, :, gsl[gi]].astype(F32)
        o_ref[0, :, gsl[gi]] = _rms(y, gn_ref[:, gsl[gi]]).astype(BF16)


def _ssd(zxbc, dt, dt_bias, a_log, d_skip, gate_norm):
    b, lp, _ = zxbc.shape
    nc = lp // CHUNK
    wide = lambda j: pl.BlockSpec((1, CHUNK, SSM_INNER), lambda i, c: (i, c, j))
    const = lambda r, w: pl.BlockSpec((r, w), lambda i, c: (0, 0))
    return pl.pallas_call(
        _ssd_kernel,
        grid=(b, nc),
        in_specs=[
            wide(0), wide(1), wide(2),
            pl.BlockSpec((1, CHUNK, LANES), lambda i, c: (i, c, 0)),
            const(1, LANES), const(1, LANES), const(1, SSM_INNER), const(1, SSM_INNER),
        ],
        out_specs=pl.BlockSpec((1, CHUNK, SSM_INNER), lambda i, c: (i, c, 0)),
        out_shape=jax.ShapeDtypeStruct((b, lp, SSM_INNER), BF16),
        scratch_shapes=[pltpu.VMEM((SSM_GROUPS, SSM_STATE, SSM_GROUP_W), F32)],
        compiler_params=_cparams("parallel", "arbitrary"),
        name="ssd",
    )(zxbc, zxbc, zxbc, dt, dt_bias, a_log, d_skip, gate_norm)


def _out_proj_kernel(h_ref, y_ref, w_ref, g_ref, wr_ref, br_ref, ho_ref, lg_ref):
    hn = h_ref[...] + _dot(y_ref[...], w_ref[...])
    ho_ref[...] = hn
    u = _rms(hn, g_ref[...])
    u_hi = u.astype(BF16)
    u_lo = (u - u_hi.astype(F32)).astype(BF16)
    p = _dot(u_hi, wr_ref[...])
    logits = p[:, :LANES] + (p[:, LANES:] + _dot(u_lo, wr_ref[:, :LANES])) + br_ref[...]
    lg_ref[...] = logits.T[:ROUTER_ROWS]


def _out_proj(h, y, w, gain, wr_t, br):
    t, d = h.shape
    tm = TOKEN_TILE
    kin = y.shape[1]
    return pl.pallas_call(
        _out_proj_kernel,
        grid=(t // tm,),
        in_specs=[
            pl.BlockSpec((tm, d), lambda i: (i, 0)),
            pl.BlockSpec((tm, kin), lambda i: (i, 0)),
            pl.BlockSpec((kin, d), lambda i: (0, 0)),
            pl.BlockSpec((1, d), lambda i: (0, 0)),
            pl.BlockSpec((d, 2 * LANES), lambda i: (0, 0)),
            pl.BlockSpec((1, LANES), lambda i: (0, 0)),
        ],
        out_specs=[
            pl.BlockSpec((tm, d), lambda i: (i, 0)),
            pl.BlockSpec((ROUTER_ROWS, tm), lambda i: (0, i)),
        ],
        out_shape=[jax.ShapeDtypeStruct((t, d), F32), jax.ShapeDtypeStruct((ROUTER_ROWS, t), F32)],
        compiler_params=_cparams("parallel"),
        name="out_proj_router",
    )(h, y, w, gain, wr_t, br)


def _first_argmax(v, n):
    ridx = lax.broadcasted_iota(I32, v.shape, 0).astype(F32)
    vmax = jnp.max(v, axis=0, keepdims=True)
    idx = jnp.min(jnp.where(v == vmax, ridx, float(n)), axis=0, keepdims=True)
    return vmax, idx.astype(I32)


def _route_kernel(lg_ref, bkt_ref, gate_ref, rank_ref, cnt_ref, carry_ref):
    @pl.when(pl.program_id(0) == 0)
    def _():
        carry_ref[...] = jnp.zeros_like(carry_ref)

    logits = lg_ref[...]
    tm = logits.shape[1]
    gl = logits[0:MOE_GROUPS]
    gmax, gsel = _first_argmax(gl, MOE_GROUPS)
    p_group = 1.0 / jnp.sum(jnp.exp(gl - gmax), axis=0, keepdims=True)
    el = logits[MOE_GROUPS:MOE_GROUPS + MOE_EPG]
    for gg in range(1, MOE_GROUPS):
        el = jnp.where(gsel == gg, logits[MOE_GROUPS + gg * MOE_EPG:MOE_GROUPS + (gg + 1) * MOE_EPG], el)
    ex = jnp.exp(el - jnp.max(el, axis=0, keepdims=True))
    p = ex / jnp.sum(ex, axis=0, keepdims=True)
    p1, i1 = _first_argmax(p, MOE_EPG)
    ridx8 = lax.broadcasted_iota(I32, p.shape, 0)
    p2, i2 = _first_argmax(jnp.where(ridx8 == i1, -1.0, p), MOE_EPG)
    denom = p1 + p2
    gate1 = p_group * p1 / denom
    gate2 = p_group * p2 / denom
    first_low = i1 < i2
    gate_ref[0:1, :] = jnp.where(first_low, gate1, gate2)
    gate_ref[1:2, :] = jnp.where(first_low, gate2, gate1)
    lo = jnp.minimum(i1, i2).astype(F32)
    hi = jnp.maximum(i1, i2).astype(F32)
    pair = lo * (2.0 * MOE_EPG - 1.0 - lo) * 0.5 + (hi - lo - 1.0)
    bucket = gsel * MOE_PAIRS + pair.astype(I32)
    bkt_ref[...] = bucket

    ridx = lax.broadcasted_iota(I32, (MOE_BUCKET_ROWS, tm), 0)
    onehot = jnp.where(ridx == bucket, 1.0, 0.0)
    before = jnp.where(lax.broadcasted_iota(I32, (tm, tm), 0) < lax.broadcasted_iota(I32, (tm, tm), 1), 1.0, 0.0)
    prefix = _dot(onehot.astype(BF16), before.astype(BF16))
    carry = carry_ref[...]
    rank_ref[...] = jnp.sum(onehot * (carry + prefix), axis=0, keepdims=True).astype(I32)
    carry = carry + jnp.sum(onehot, axis=1, keepdims=True)
    carry_ref[...] = carry
    cnt_ref[...] = jnp.broadcast_to(carry, cnt_ref.shape).astype(I32)


def _route(logits_t):
    t = logits_t.shape[1]
    tm = TOKEN_TILE
    rows = lambda r: pl.BlockSpec((r, tm), lambda i: (0, i))
    return pl.pallas_call(
        _route_kernel,
        grid=(t // tm,),
        in_specs=[rows(ROUTER_ROWS)],
        out_specs=[rows(1), rows(2), rows(1), pl.BlockSpec((MOE_BUCKET_ROWS, LANES), lambda i: (0, 0))],
        out_shape=[jax.ShapeDtypeStruct((1, t), I32), jax.ShapeDtypeStruct((2, t), F32),
                   jax.ShapeDtypeStruct((1, t), I32), jax.ShapeDtypeStruct((MOE_BUCKET_ROWS, LANES), I32)],
        scratch_shapes=[pltpu.VMEM((MOE_BUCKET_ROWS, 1), F32)],
        compiler_params=_cparams("arbitrary"),
        name="route_rank",
    )(logits_t)


def _dispatch_kernel(seg_ref, dest_ref, gate_ref, h_ref, g_ref, xs_ref, u_ref, zero_ref, sems):
    i = pl.program_id(0)
    n = pl.num_programs(0)
    slot = lax.rem(i, 2)
    tm, d = h_ref.shape

    def wait_rows(s):
        pltpu.make_async_copy(u_ref.at[s], xs_ref.at[pl.ds(0, tm)], sems.at[s]).wait()

    @pl.when(i == 0)
    def _():
        zero_ref[...] = jnp.zeros_like(zero_ref)

        def fill_block(start):
            fill = pltpu.make_async_copy(
                zero_ref, xs_ref.at[pl.ds(pl.multiple_of(start, MOE_BLOCK), MOE_BLOCK)], sems.at[0])
            fill.start()
            fill.wait()

        def fill_segment(e, carry):
            @pl.when(seg_ref[MOE_BUCKET_ROWS + e] > 0)
            def _():
                fill_block(seg_ref[e] - MOE_BLOCK)
            return carry

        lax.fori_loop(0, MOE_BUCKETS, fill_segment, 0)

        def fill_tail(blk, carry):
            fill_block(blk * MOE_BLOCK)
            return carry

        lax.fori_loop(seg_ref[MOE_BUCKET_ROWS - 1] // MOE_BLOCK, xs_ref.shape[0] // MOE_BLOCK, fill_tail, 0)

    @pl.when(i >= 2)
    def _():
        wait_rows(slot)

    u = _rms(h_ref[...], g_ref[...])
    half = d // 2
    as_bits = lambda v: lax.bitcast_convert_type(v.astype(BF16).astype(F32), jnp.uint32)
    words = (as_bits(u[:, half:]) & jnp.uint32(0xFFFF0000)) | (as_bits(u[:, :half]) >> 16)
    gates = jnp.concatenate([gate_ref[...], jnp.zeros((LANES - gate_ref.shape[0], tm), F32)], axis=0).T
    packed = jnp.concatenate([words, lax.bitcast_convert_type(gates, jnp.uint32),
                              jnp.zeros((tm, half - LANES), jnp.uint32)], axis=1)
    u_ref[slot] = packed.reshape(u_ref.shape[1:])

    def issue(t, carry):
        pltpu.make_async_copy(u_ref.at[slot, t], xs_ref.at[dest_ref[0, 0, t]], sems.at[slot]).start()
        return carry

    lax.fori_loop(0, tm, issue, 0)

    @pl.when(i == n - 1)
    def _():
        wait_rows(slot)

        @pl.when(n >= 2)
        def _():
            wait_rows(1 - slot)


def _dispatch(seg, dest, gate, h, gain, rows):
    t, d = h.shape
    tm = TOKEN_TILE
    sub = d // LANES
    return pl.pallas_call(
        _dispatch_kernel,
        grid_spec=pltpu.PrefetchScalarGridSpec(
            num_scalar_prefetch=1,
            grid=(t // tm,),
            in_specs=[
                pl.BlockSpec((1, 1, tm), lambda i, seg: (i, 0, 0), memory_space=pltpu.SMEM),
                pl.BlockSpec((8, tm), lambda i, seg: (0, i)),
                pl.BlockSpec((tm, d), lambda i, seg: (i, 0)),
                pl.BlockSpec((1, d), lambda i, seg: (0, 0)),
            ],
            out_specs=pl.BlockSpec(memory_space=pl.ANY),
            scratch_shapes=[pltpu.VMEM((2, tm, sub, LANES), jnp.uint32), pltpu.VMEM((MOE_BLOCK, sub, LANES), jnp.uint32),
                            pltpu.SemaphoreType.DMA((2,))],
        ),
        out_shape=jax.ShapeDtypeStruct((rows, sub, LANES), jnp.uint32),
        compiler_params=_cparams("arbitrary"),
        name="moe_dispatch",
    )(seg, dest, gate, h, gain)


def _expert_kernel(ea_ref, eb_ref, nu_ref, xs_ref, w1a_ref, w3a_ref, w2a_ref, w1b_ref, w3b_ref, w2b_ref, ys_ref):
    used = pl.program_id(0) < nu_ref[0]

    @pl.when(used)
    def _():
        rows, sub, lanes = xs_ref.shape
        d = sub * lanes
        packed = xs_ref[...].reshape(rows, d)
        words = packed[:, :d // 2]
        lo = lax.bitcast_convert_type(words << 16, F32).astype(BF16)
        hi = lax.bitcast_convert_type(words & jnp.uint32(0xFFFF0000), F32).astype(BF16)
        x = jnp.concatenate([lo, hi], axis=1)
        gates = lax.bitcast_convert_type(packed[:, d // 2:d // 2 + LANES], F32)

        def ffn(w1_ref, w3_ref, w2_ref):
            hid = _silu(_dot(x, w1_ref[0])) * _dot(x, w3_ref[0])
            return _dot(hid.astype(BF16), w2_ref[0])

        y = ffn(w1a_ref, w3a_ref, w2a_ref) * gates[:, 0:1] + ffn(w1b_ref, w3b_ref, w2b_ref) * gates[:, 1:2]
        ys_ref[...] = y.reshape(rows, sub, lanes)

    @pl.when(jnp.logical_not(used))
    def _():
        ys_ref[...] = jnp.zeros_like(ys_ref)


def _experts(block_ea, block_eb, n_used, xs, w1, w3, w2):
    rows, sub, lanes = xs.shape
    d = sub * lanes
    nb = rows // MOE_BLOCK
    w_in = lambda tbl: pl.BlockSpec((1, d, MOE_FF), lambda i, ea, eb, nu: ((ea, eb)[tbl][i], 0, 0))
    w_out = lambda tbl: pl.BlockSpec((1, MOE_FF, d), lambda i, ea, eb, nu: ((ea, eb)[tbl][i], 0, 0))
    return pl.pallas_call(
        _expert_kernel,
        grid_spec=pltpu.PrefetchScalarGridSpec(
            num_scalar_prefetch=3,
            grid=(nb,),
            in_specs=[
                pl.BlockSpec((MOE_BLOCK, sub, lanes), lambda i, ea, eb, nu: (jnp.minimum(i, nu[0] - 1), 0, 0)),
                w_in(0), w_in(0), w_out(0), w_in(1), w_in(1), w_out(1),
            ],
            out_specs=pl.BlockSpec((MOE_BLOCK, sub, lanes), lambda i, ea, eb, nu: (i, 0, 0)),
        ),
        out_shape=jax.ShapeDtypeStruct((rows, sub, lanes), F32),
        compiler_params=_cparams("arbitrary"),
        name="moe_experts",
    )(block_ea, block_eb, n_used, xs, w1, w3, w2, w1, w3, w2)


def _gather_kernel(dest_ref, ys_ref, o_ref, sem):
    tm = o_ref.shape[0]

    def issue_pair(p, carry):
        for k in range(2):
            t = 2 * p + k
            pltpu.make_async_copy(ys_ref.at[dest_ref[0, 0, t]], o_ref.at[t], sem).start(priority=k)
        return carry

    lax.fori_loop(0, tm // 2, issue_pair, 0)
    pltpu.make_async_copy(ys_ref.at[pl.ds(0, tm)], o_ref, sem).wait()


def _gather_rows(dest, ys):
    nt, _, tm = dest.shape
    _, sub, lanes = ys.shape
    return pl.pallas_call(
        _gather_kernel,
        grid=(nt,),
        in_specs=[
            pl.BlockSpec((1, 1, tm), lambda i: (i, 0, 0), memory_space=pltpu.SMEM),
            pl.BlockSpec(memory_space=pl.ANY),
        ],
        out_specs=pl.BlockSpec((tm, sub, lanes), lambda i: (i, 0, 0)),
        out_shape=jax.ShapeDtypeStruct((nt * tm, sub, lanes), F32),
        scratch_shapes=[pltpu.SemaphoreType.DMA(())],
        compiler_params=_cparams("arbitrary"),
        name="moe_gather",
    )(dest, ys)


def _final_kernel(h_ref, y_ref, g_ref, o_ref):
    rows, d = h_ref.shape[1:]
    o_ref[0] = _rms(h_ref[0] + y_ref[...].reshape(rows, d), g_ref[...])


def _final_norm(h, y_moe, gain, seq):
    b, lp, d = h.shape
    nc = lp // CHUNK
    return pl.pallas_call(
        _final_kernel,
        grid=(b, seq // CHUNK),
        in_specs=[
            pl.BlockSpec((1, CHUNK, d), lambda i, c: (i, c + 1, 0)),
            pl.BlockSpec((CHUNK, d // LANES, LANES), lambda i, c: (i * nc + c + 1, 0, 0)),
            pl.BlockSpec((1, d), lambda i, c: (0, 0)),
        ],
        out_specs=pl.BlockSpec((1, CHUNK, d), lambda i, c: (i, c, 0)),
        out_shape=jax.ShapeDtypeStruct((b, seq, d), F32),
        compiler_params=_cparams("parallel", "parallel"),
        name="final_norm",
    )(h, y_moe, gain)


def _bucket_experts():
    lo, hi = [], []
    for g in range(MOE_GROUPS):
        for a in range(MOE_EPG):
            for c in range(a + 1, MOE_EPG):
                lo.append(g * MOE_EPG + a)
                hi.append(g * MOE_EPG + c)
    fill = MOE_BUCKET_ROWS - len(lo)
    return jnp.array(lo + [lo[-1]] * fill, I32), jnp.array(hi + [hi[-1]] * fill, I32)


def _moe_layer(h, y_mix, w_out, norm_gain, wg, bg, we, be, w1, w3, w2):
    t, d = h.shape
    n_logits = MOE_GROUPS + MOE_EXPERTS
    wr = jnp.concatenate([wg, we.reshape(d, MOE_EXPERTS), jnp.zeros((d, LANES - n_logits), F32)], axis=1)
    br = jnp.concatenate([bg, be.reshape(-1), jnp.zeros((LANES - n_logits,), F32)])
    wr_hi = wr.astype(BF16)
    wr_split = jnp.concatenate([wr_hi, (wr - wr_hi.astype(F32)).astype(BF16)], axis=1)
    h, logits_t = _out_proj(h, y_mix, w_out.astype(BF16), norm_gain[None, :], wr_split, br[None, :])
    bucket, gate, rank, counts = _route(logits_t)

    counts = counts[:, 0]
    padded = (counts + MOE_BLOCK - 1) // MOE_BLOCK * MOE_BLOCK
    pend = jnp.cumsum(padded)
    pstart = pend - padded
    ids = jnp.arange(MOE_BUCKET_ROWS, dtype=I32)
    dest = rank[0] + jnp.sum(jnp.where(bucket == ids[:, None], pstart[:, None], 0), axis=0)
    n_blocks = -(-t // MOE_BLOCK) + MOE_BUCKETS
    block_start = jnp.arange(n_blocks, dtype=I32) * MOE_BLOCK
    block_bucket = jnp.minimum(jnp.sum((pend[None, :] <= block_start[:, None]).astype(I32), axis=1),
                               MOE_BUCKET_ROWS - 1)
    n_used = (pend[-1:] // MOE_BLOCK).astype(I32)
    lo, hi = _bucket_experts()
    dest = dest.reshape(t // TOKEN_TILE, 1, TOKEN_TILE)
    gate8 = jnp.concatenate([gate, jnp.zeros((6, t), F32)], axis=0)

    xs = _dispatch(jnp.concatenate([pend, padded]).astype(I32), dest, gate8, h, norm_gain[None, :],
                   n_blocks * MOE_BLOCK)
    ys = _experts(lo[block_bucket], hi[block_bucket], n_used, xs, w1.astype(BF16), w3.astype(BF16), w2.astype(BF16))
    return h, _gather_rows(dest, ys)


def kernel(x, meta, norm_mix, norm_ffn, norm_final, ret_w_in, ret_w_out, ret_norm, ssm_w_in, ssm_conv_w,
           ssm_conv_b, ssm_dt_bias, ssm_a_log, ssm_d, ssm_norm, ssm_w_out, moe_wg, moe_bg, moe_we, moe_be,
           moe_w1, moe_w3, moe_w2):
    b, seq, d = x.shape
    lp = CHUNK + seq
    t = b * lp
    depth = norm_mix.shape[0]
    h = jnp.concatenate([jnp.zeros((b, PAD, d), F32), jnp.broadcast_to(meta[None], (b, N_META, d)), x], axis=1)

    half = RET_QK_DIM // 2
    inv_freq = 1.0 / (RET_ROPE_BASE ** jnp.linspace(0.0, 1.0, half, dtype=F32))
    ang = (jnp.arange(lp, dtype=F32) - PAD)[:, None] * inv_freq[None, :]
    cos, sin = jnp.cos(ang), jnp.sin(ang)
    perm = jnp.concatenate([jnp.arange(0, RET_QK_DIM, 2), jnp.arange(1, RET_QK_DIM, 2)])
    qk_perm = (jnp.arange(2 * RET_HEADS)[:, None] * RET_QK_DIM + perm[None, :]).reshape(-1)

    h = h.reshape(t, d)
    y_moe = None
    for i in range(depth):
        j = i // 2
        if i % 2 == 0:
            if y_moe is not None:
                h = h + y_moe.reshape(t, d)
            w_in = jnp.concatenate([ret_w_in[j][:, qk_perm], ret_w_in[j][:, 2 * RET_QK:]], axis=1).astype(BF16)
            qkvg = _ret_in(h.reshape(b, lp, d), norm_mix[i][None, :], cos, sin, w_in)
            y = _retention(qkvg, ret_norm[j][None, :])
            w_out = ret_w_out[j]
        else:
            if y_moe is None:
                y_moe = jnp.zeros((t, d // LANES, LANES), F32)
            w_in = jnp.concatenate([ssm_w_in[j], jnp.zeros((d, LANES - SSM_HEADS), F32)], axis=1).astype(BF16)
            h, zxbc, dt = _ssm_in(h, y_moe, norm_mix[i][None, :], w_in, ssm_conv_w[j], ssm_conv_b[j][None, :], lp)
            lane_pad = lambda v: jnp.concatenate([v, jnp.zeros((LANES - SSM_HEADS,), F32)])[None, :]
            y = _ssd(zxbc.reshape(b, lp, SSM_MAIN), dt.reshape(b, lp, LANES), lane_pad(ssm_dt_bias[j]),
                     lane_pad(ssm_a_log[j]), jnp.repeat(ssm_d[j], SSM_HEAD_DIM)[None, :], ssm_norm[j][None, :])
            w_out = ssm_w_out[j]
        h, y_moe = _moe_layer(h, y.reshape(t, -1), w_out, norm_ffn[i], moe_wg[i], moe_bg[i], moe_we[i],
                              moe_be[i], moe_w1[i], moe_w3[i], moe_w2[i])
    return _final_norm(h.reshape(b, lp, d), y_moe, norm_final[None, :], seq)
```

```python
import functools
import math

import jax
import jax.numpy as jnp
from jax import lax
from jax.experimental import pallas as pl
from jax.experimental.pallas import tpu as pltpu

F32 = jnp.float32
BF16 = jnp.bfloat16
I32 = jnp.int32

D_MODEL = 1024
N_META = 16
CHUNK = 128
PAD = CHUNK - N_META
EPS = 1e-6

RET_HEADS = 4
RET_QK_DIM = 256
RET_V_DIM = 512
RET_QK = RET_HEADS * RET_QK_DIM
RET_V = RET_HEADS * RET_V_DIM
RET_IN = 2 * RET_QK + 2 * RET_V
RET_ROPE_BASE = 10000.0

SSM_INNER = 2048
SSM_HEAD_DIM = 64
SSM_HEADS = 32
SSM_GROUPS = 8
SSM_HPG = 4
SSM_STATE = 128
SSM_CONV = 4
SSM_BC = SSM_GROUPS * SSM_STATE
SSM_CONV_DIM = SSM_INNER + 2 * SSM_BC
SSM_MAIN = SSM_INNER + SSM_CONV_DIM
LANES = 128
SSM_GROUP_W = SSM_HPG * SSM_HEAD_DIM

MOE_GROUPS = 4
MOE_EPG = 8
MOE_EXPERTS = 32
MOE_FF = 512
MOE_BLOCK = 256
MOE_PAIRS = MOE_EPG * (MOE_EPG - 1) // 2
MOE_BUCKETS = MOE_GROUPS * MOE_PAIRS
MOE_BUCKET_ROWS = 128
ROUTER_ROWS = 40

ROW_DMA_UNROLL = 4
SSM_IN_STEP = 512
LOG2E = math.log2(math.e)
TOKEN_TILE = 512
VMEM_LIMIT = 56 * 1024 * 1024


def _cparams(*sem):
    return pltpu.CompilerParams(dimension_semantics=sem, vmem_limit_bytes=VMEM_LIMIT)


def _rms(x, gain):
    ms = jnp.mean(x * x, axis=-1, keepdims=True)
    return x * lax.rsqrt(ms + EPS) * gain


def _silu(x):
    hx = 0.5 * x
    return hx + hx * jnp.tanh(hx)


def _dot(a, b):
    return jnp.dot(a, b, preferred_element_type=F32)


def _dot_nt(a, b, precision=None):
    return lax.dot_general(a, b, (((1,), (1,)), ((), ())), preferred_element_type=F32, precision=precision)


def _dot_tn(a, b):
    return lax.dot_general(a, b, (((0,), (0,)), ((), ())), preferred_element_type=F32)


def _row_tile(rows, cap=544):
    best = 16
    for t in range(16, cap + 1, 16):
        if rows % t == 0:
            best = t
    return best


def _ret_in_kernel(h_ref, g_ref, cos_ref, sin_ref, w_ref, o_ref):
    xn = _rms(h_ref[0], g_ref[...]).astype(BF16)
    cos = cos_ref[...]
    sin = sin_ref[...]
    half = RET_QK_DIM // 2
    for j in range(2 * RET_HEADS):
        c0 = j * RET_QK_DIM
        acc = _dot(xn, w_ref[:, c0:c0 + RET_QK_DIM])
        if j >= RET_HEADS:
            acc = acc * (RET_QK_DIM ** -0.5)
        e = acc[:, :half]
        o = acc[:, half:]
        o_ref[0, :, c0:c0 + half] = (e * cos - o * sin).astype(BF16)
        o_ref[0, :, c0 + half:c0 + RET_QK_DIM] = (o * cos + e * sin).astype(BF16)
    step = 1024
    for c0 in range(2 * RET_QK, RET_IN, step):
        o_ref[0, :, c0:c0 + step] = _dot(xn, w_ref[:, c0:c0 + step]).astype(BF16)


def _ret_in(h, gain, cos, sin, w):
    b, lp, d = h.shape
    tm = _row_tile(lp)
    return pl.pallas_call(
        _ret_in_kernel,
        grid=(b, lp // tm),
        in_specs=[
            pl.BlockSpec((1, tm, d), lambda i, j: (i, j, 0)),
            pl.BlockSpec((1, d), lambda i, j: (0, 0)),
            pl.BlockSpec((tm, RET_QK_DIM // 2), lambda i, j: (j, 0)),
            pl.BlockSpec((tm, RET_QK_DIM // 2), lambda i, j: (j, 0)),
            pl.BlockSpec((d, RET_IN), lambda i, j: (0, 0)),
        ],
        out_specs=pl.BlockSpec((1, tm, RET_IN), lambda i, j: (i, j, 0)),
        out_shape=jax.ShapeDtypeStruct((b, lp, RET_IN), BF16),
        compiler_params=_cparams("parallel", "parallel"),
        name="ret_in_proj",
    )(h, gain, cos, sin, w)


def _ret_kernel(q_ref, k_ref, v_ref, g_ref, hn_ref, o_ref, state_ref):
    @pl.when(pl.program_id(1) == 0)
    def _():
        state_ref[...] = jnp.zeros_like(state_ref)

    row = lax.broadcasted_iota(I32, (CHUNK, CHUNK), 0).astype(F32)
    col = lax.broadcasted_iota(I32, (CHUNK, CHUNK), 1).astype(F32)
    diff = row - col
    ridx = row[:, :1]
    for hh in range(RET_HEADS):
        lg = math.log(1.0 - 2.0 ** (-5.0 - hh))
        intra = jnp.where(diff >= 0, jnp.exp(lg * jnp.maximum(diff, 0.0)), 0.0)
        qdec = jnp.exp(lg * (ridx + 1.0))
        kdec = jnp.exp(lg * (CHUNK - 1.0 - ridx))
        cdec = math.exp(lg * CHUNK)
        qh = q_ref[0, :, hh * RET_QK_DIM:(hh + 1) * RET_QK_DIM]
        kh = k_ref[0, :, hh * RET_QK_DIM:(hh + 1) * RET_QK_DIM]
        vh = v_ref[0, :, hh * RET_V_DIM:(hh + 1) * RET_V_DIM]
        scores = _dot_nt(qh, kh) * intra
        st = state_ref[hh]
        y = _dot(scores.astype(BF16), vh) + _dot(qh, st.astype(BF16)) * qdec
        kd = (kh.astype(F32) * kdec).astype(BF16)
        state_ref[hh] = st * cdec + _dot_tn(kd, vh)
        yn = _rms(y, hn_ref[:, hh * RET_V_DIM:(hh + 1) * RET_V_DIM])
        gh = g_ref[0, :, hh * RET_V_DIM:(hh + 1) * RET_V_DIM].astype(F32)
        o_ref[0, :, hh * RET_V_DIM:(hh + 1) * RET_V_DIM] = (_silu(gh) * yn).astype(BF16)


def _retention(qkvg, head_norm):
    b, lp, _ = qkvg.shape
    nc = lp // CHUNK
    return pl.pallas_call(
        _ret_kernel,
        grid=(b, nc),
        in_specs=[
            pl.BlockSpec((1, CHUNK, RET_QK), lambda i, c: (i, c, 0)),
            pl.BlockSpec((1, CHUNK, RET_QK), lambda i, c: (i, c, 1)),
            pl.BlockSpec((1, CHUNK, RET_V), lambda i, c: (i, c, 1)),
            pl.BlockSpec((1, CHUNK, RET_V), lambda i, c: (i, c, 2)),
            pl.BlockSpec((1, RET_V), lambda i, c: (0, 0)),
        ],
        out_specs=pl.BlockSpec((1, CHUNK, RET_V), lambda i, c: (i, c, 0)),
        out_shape=jax.ShapeDtypeStruct((b, lp, RET_V), BF16),
        scratch_shapes=[pltpu.VMEM((RET_HEADS, RET_QK_DIM, RET_V_DIM), F32)],
        compiler_params=_cparams("parallel", "arbitrary"),
        name="retention",
    )(qkvg, qkvg, qkvg, qkvg, head_norm)


def _ssm_in_kernel(h_ref, ym_ref, g_ref, w_ref, cw_ref, cb_ref, ho_ref, o_ref, dt_ref, tail_ref, acc_ref, *,
                   rows_per_seq):
    i = pl.program_id(0)
    tm = h_ref.shape[0]

    @pl.when(i == 0)
    def _():
        tail_ref[...] = jnp.zeros_like(tail_ref)

    h = h_ref[...] + ym_ref[...].reshape(h_ref.shape)
    ho_ref[...] = h
    xn = _rms(h, g_ref[...]).astype(BF16)
    step = acc_ref.shape[2]
    n_steps = SSM_MAIN // step

    pos = lax.rem(i * tm, rows_per_seq) + lax.broadcasted_iota(I32, (tm, 1), 0)
    is_pad = jnp.logical_or(pos < PAD, jnp.logical_and(pos >= rows_per_seq, pos < rows_per_seq + PAD))
    row8 = lax.broadcasted_iota(I32, (8, 1), 0)

    def project(j):
        acc_ref[j % 2] = _dot(xn, w_ref[:, j * step:(j + 1) * step])

    project(0)
    for j in range(n_steps):
        if j + 1 < n_steps:
            project(j + 1)
        c0 = j * step
        acc = acc_ref[j % 2]
        if c0 < SSM_INNER:
            o_ref[:, c0:c0 + step] = _silu(acc).astype(BF16)
            continue
        cc = c0 - SSM_INNER
        pre = jnp.where(is_pad, 0.0, acc)
        tail = tail_ref[:, cc:cc + step]
        tail_ref[:, cc:cc + step] = pre[tm - 8:, :]
        conv = cb_ref[:, cc:cc + step] + cw_ref[SSM_CONV - 1:SSM_CONV, cc:cc + step] * pre
        for k in range(1, SSM_CONV):
            rolled = pltpu.roll(pre, k, axis=0)
            top = jnp.where(row8 < k, pltpu.roll(tail, k, axis=0), rolled[:8])
            shifted = jnp.concatenate([top, rolled[8:]], axis=0)
            conv = conv + cw_ref[SSM_CONV - 1 - k:SSM_CONV - k, cc:cc + step] * shifted
        o_ref[:, c0:c0 + step] = _silu(conv).astype(BF16)
    dt_ref[...] = _dot(xn, w_ref[:, SSM_MAIN:SSM_MAIN + LANES])


def _ssm_in(h, y_moe, gain, w, conv_w, conv_b, rows_per_seq):
    t, d = h.shape
    tm = TOKEN_TILE
    assert tm <= rows_per_seq
    const = lambda r, c: pl.BlockSpec((r, c), lambda i: (0, 0))
    return pl.pallas_call(
        functools.partial(_ssm_in_kernel, rows_per_seq=rows_per_seq),
        grid=(t // tm,),
        in_specs=[
            pl.BlockSpec((tm, d), lambda i: (i, 0)),
            pl.BlockSpec((tm, d // LANES, LANES), lambda i: (i, 0, 0)),
            const(1, d), const(d, SSM_MAIN + LANES), const(SSM_CONV, SSM_CONV_DIM), const(1, SSM_CONV_DIM),
        ],
        out_specs=[
            pl.BlockSpec((tm, d), lambda i: (i, 0)),
            pl.BlockSpec((tm, SSM_MAIN), lambda i: (i, 0)),
            pl.BlockSpec((tm, LANES), lambda i: (i, 0)),
        ],
        out_shape=[jax.ShapeDtypeStruct((t, d), F32), jax.ShapeDtypeStruct((t, SSM_MAIN), BF16),
                   jax.ShapeDtypeStruct((t, LANES), F32)],
        scratch_shapes=[pltpu.VMEM((8, SSM_CONV_DIM), F32), pltpu.VMEM((2, tm, SSM_IN_STEP), F32)],
        compiler_params=_cparams("arbitrary"),
        name="ssm_in_proj",
    )(h, y_moe, gain, w, conv_w, conv_b)


def _ssd_kernel(zg_ref, x_ref, bc_ref, dt_ref, dtb_ref, alog_ref, dsk_ref, gn_ref, o_ref, state_ref):
    c = pl.program_id(1)

    @pl.when(c == 0)
    def _():
        state_ref[...] = jnp.zeros_like(state_ref)

    row = lax.broadcasted_iota(I32, (CHUNK, CHUNK), 0)
    col = lax.broadcasted_iota(I32, (CHUNK, CHUNK), 1)
    causal_bias = jnp.where(row >= col, 0.0, -jnp.inf)
    low_half = col < SSM_HEAD_DIM
    valid = row[:, :1] >= jnp.where(c > 0, 0, PAD)

    dtr = dt_ref[0] + dtb_ref[...]
    dtv = jnp.maximum(dtr, 0.0) + jnp.log(1.0 + jnp.exp(-jnp.abs(dtr)))
    dtv = jnp.where(valid, dtv, 0.0)
    da = dtv * (-jnp.exp(alog_ref[...]))
    tril = jnp.where(row >= col, 1.0, 0.0)
    acs = jnp.dot(tril, da, preferred_element_type=F32, precision=lax.Precision.HIGHEST)
    last = acs[CHUNK - 1:CHUNK, :]
    acs2 = acs * LOG2E
    src_t = (jnp.log2(dtv) - acs2).T
    w_t = (dtv * jnp.exp(last - acs)).T
    elast = jnp.exp(last)

    def pair_row(v, h0):
        return jnp.where(low_half[:1], v[:, h0:h0 + 1], v[:, h0 + 1:h0 + 2])

    groups = range(SSM_GROUPS)
    gsl = [slice(gi * SSM_GROUP_W, (gi + 1) * SSM_GROUP_W) for gi in groups]
    bgs = [bc_ref[0, :, gi * SSM_STATE:(gi + 1) * SSM_STATE] for gi in groups]
    cgs = [bc_ref[0, :, SSM_BC + gi * SSM_STATE:SSM_BC + (gi + 1) * SSM_STATE] for gi in groups]
    sts = [state_ref[gi] for gi in groups]
    cbs = [_dot_nt(cgs[gi], bgs[gi]) for gi in groups]
    yoffs = [_dot(cgs[gi], sts[gi].astype(BF16)) for gi in groups]
    bts = [bgs[gi].astype(F32).T for gi in groups]

    outs = []
    eexp = []
    for gi in groups:
        for pp in range(SSM_HPG // 2):
            h0 = gi * SSM_HPG + 2 * pp
            top, bot, ecols = [], [], []
            for hd in (h0, h0 + 1):
                acs_b = jnp.broadcast_to(acs2[:, hd:hd + 1], (CHUNK, CHUNK))
                top.append(cbs[gi] * jnp.exp2(acs_b + (src_t[hd:hd + 1, :] + causal_bias)))
                bot.append(bts[gi] * w_t[hd:hd + 1, :])
                ecols.append(jnp.exp2(acs_b))
            lhs = jnp.concatenate([jnp.concatenate(top, axis=1), jnp.concatenate(bot, axis=1)], axis=0)
            xp = x_ref[0, :, h0 * SSM_HEAD_DIM:(h0 + 2) * SSM_HEAD_DIM]
            zero = jnp.zeros_like(xp)
            rhs = jnp.concatenate([jnp.where(low_half, xp, zero), jnp.where(low_half, zero, xp)], axis=0)
            outs.append(_dot(lhs.astype(BF16), rhs))
            eexp.append(jnp.where(low_half, ecols[0], ecols[1]))

    for gi in groups:
        pairs = [gi * (SSM_HPG // 2) + pp for pp in range(SSM_HPG // 2)]
        heads = [gi * SSM_HPG + 2 * pp for pp in range(SSM_HPG // 2)]
        lexp = jnp.concatenate([pair_row(elast, h0) for h0 in heads], axis=1)
        state_ref[gi] = sts[gi] * lexp + jnp.concatenate([outs[p][CHUNK:] for p in pairs], axis=1)
        xg = x_ref[0, :, gsl[gi]].astype(F32)
        y = jnp.concatenate([outs[p][:CHUNK] for p in pairs], axis=1)
        y = y + yoffs[gi] * jnp.concatenate([eexp[p] for p in pairs], axis=1)
        y = y + dsk_ref[:, gsl[gi]] * xg
        y = y * zg_ref[0, :, gsl[gi]].astype(F32)
        o_ref[0, :, gsl[gi]] = _rms(y, gn_ref[:, gsl[gi]]).astype(BF16)


def _ssd(zxbc, dt, dt_bias, a_log, d_skip, gate_norm):
    b, lp, _ = zxbc.shape
    nc = lp // CHUNK
    wide = lambda j: pl.BlockSpec((1, CHUNK, SSM_INNER), lambda i, c: (i, c, j))
    const = lambda r, w: pl.BlockSpec((r, w), lambda i, c: (0, 0))
    return pl.pallas_call(
        _ssd_kernel,
        grid=(b, nc),
        in_specs=[
            wide(0), wide(1), wide(2),
            pl.BlockSpec((1, CHUNK, LANES), lambda i, c: (i, c, 0)),
            const(1, LANES), const(1, LANES), const(1, SSM_INNER), const(1, SSM_INNER),
        ],
        out_specs=pl.BlockSpec((1, CHUNK, SSM_INNER), lambda i, c: (i, c, 0)),
        out_shape=jax.ShapeDtypeStruct((b, lp, SSM_INNER), BF16),
        scratch_shapes=[pltpu.VMEM((SSM_GROUPS, SSM_STATE, SSM_GROUP_W), F32)],
        compiler_params=_cparams("parallel", "arbitrary"),
        name="ssd",
    )(zxbc, zxbc, zxbc, dt, dt_bias, a_log, d_skip, gate_norm)


def _out_proj_kernel(h_ref, y_ref, w_ref, g_ref, wr_ref, br_ref, ho_ref, lg_ref):
    hn = h_ref[...] + _dot(y_ref[...], w_ref[...])
    ho_ref[...] = hn
    u = _rms(hn, g_ref[...])
    u_hi = u.astype(BF16)
    u_lo = (u - u_hi.astype(F32)).astype(BF16)
    p = _dot(u_hi, wr_ref[...])
    logits = p[:, :LANES] + (p[:, LANES:] + _dot(u_lo, wr_ref[:, :LANES])) + br_ref[...]
    lg_ref[...] = logits.T[:ROUTER_ROWS]


def _out_proj(h, y, w, gain, wr_t, br):
    t, d = h.shape
    tm = TOKEN_TILE
    kin = y.shape[1]
    return pl.pallas_call(
        _out_proj_kernel,
        grid=(t // tm,),
        in_specs=[
            pl.BlockSpec((tm, d), lambda i: (i, 0)),
            pl.BlockSpec((tm, kin), lambda i: (i, 0)),
            pl.BlockSpec((kin, d), lambda i: (0, 0)),
            pl.BlockSpec((1, d), lambda i: (0, 0)),
            pl.BlockSpec((d, 2 * LANES), lambda i: (0, 0)),
            pl.BlockSpec((1, LANES), lambda i: (0, 0)),
        ],
        out_specs=[
            pl.BlockSpec((tm, d), lambda i: (i, 0)),
            pl.BlockSpec((ROUTER_ROWS, tm), lambda i: (0, i)),
        ],
        out_shape=[jax.ShapeDtypeStruct((t, d), F32), jax.ShapeDtypeStruct((ROUTER_ROWS, t), F32)],
        compiler_params=_cparams("parallel"),
        name="out_proj_router",
    )(h, y, w, gain, wr_t, br)


def _first_argmax(v, n):
    ridx = lax.broadcasted_iota(I32, v.shape, 0).astype(F32)
    vmax = jnp.max(v, axis=0, keepdims=True)
    idx = jnp.min(jnp.where(v == vmax, ridx, float(n)), axis=0, keepdims=True)
    return vmax, idx.astype(I32)


def _route_kernel(lg_ref, bkt_ref, gate_ref, rank_ref, cnt_ref, carry_ref):
    @pl.when(pl.program_id(0) == 0)
    def _():
        carry_ref[...] = jnp.zeros_like(carry_ref)

    logits = lg_ref[...]
    tm = logits.shape[1]
    gl = logits[0:MOE_GROUPS]
    gmax, gsel = _first_argmax(gl, MOE_GROUPS)
    p_group = 1.0 / jnp.sum(jnp.exp(gl - gmax), axis=0, keepdims=True)
    el = logits[MOE_GROUPS:MOE_GROUPS + MOE_EPG]
    for gg in range(1, MOE_GROUPS):
        el = jnp.where(gsel == gg, logits[MOE_GROUPS + gg * MOE_EPG:MOE_GROUPS + (gg + 1) * MOE_EPG], el)
    ex = jnp.exp(el - jnp.max(el, axis=0, keepdims=True))
    p = ex / jnp.sum(ex, axis=0, keepdims=True)
    p1, i1 = _first_argmax(p, MOE_EPG)
    ridx8 = lax.broadcasted_iota(I32, p.shape, 0)
    p2, i2 = _first_argmax(jnp.where(ridx8 == i1, -1.0, p), MOE_EPG)
    denom = p1 + p2
    gate1 = p_group * p1 / denom
    gate2 = p_group * p2 / denom
    first_low = i1 < i2
    gate_ref[0:1, :] = jnp.where(first_low, gate1, gate2)
    gate_ref[1:2, :] = jnp.where(first_low, gate2, gate1)
    lo = jnp.minimum(i1, i2).astype(F32)
    hi = jnp.maximum(i1, i2).astype(F32)
    pair = lo * (2.0 * MOE_EPG - 1.0 - lo) * 0.5 + (hi - lo - 1.0)
    bucket = gsel * MOE_PAIRS + pair.astype(I32)
    bkt_ref[...] = bucket

    ridx = lax.broadcasted_iota(I32, (MOE_BUCKET_ROWS, tm), 0)
    onehot = jnp.where(ridx == bucket, 1.0, 0.0)
    before = jnp.where(lax.broadcasted_iota(I32, (tm, tm), 0) < lax.broadcasted_iota(I32, (tm, tm), 1), 1.0, 0.0)
    prefix = _dot(onehot.astype(BF16), before.astype(BF16))
    carry = carry_ref[...]
    rank_ref[...] = jnp.sum(onehot * (carry + prefix), axis=0, keepdims=True).astype(I32)
    carry = carry + jnp.sum(onehot, axis=1, keepdims=True)
    carry_ref[...] = carry
    cnt_ref[...] = jnp.broadcast_to(carry, cnt_ref.shape).astype(I32)


def _route(logits_t):
    t = logits_t.shape[1]
    tm = TOKEN_TILE
    rows = lambda r: pl.BlockSpec((r, tm), lambda i: (0, i))
    return pl.pallas_call(
        _route_kernel,
        grid=(t // tm,),
        in_specs=[rows(ROUTER_ROWS)],
        out_specs=[rows(1), rows(2), rows(1), pl.BlockSpec((MOE_BUCKET_ROWS, LANES), lambda i: (0, 0))],
        out_shape=[jax.ShapeDtypeStruct((1, t), I32), jax.ShapeDtypeStruct((2, t), F32),
                   jax.ShapeDtypeStruct((1, t), I32), jax.ShapeDtypeStruct((MOE_BUCKET_ROWS, LANES), I32)],
        scratch_shapes=[pltpu.VMEM((MOE_BUCKET_ROWS, 1), F32)],
        compiler_params=_cparams("arbitrary"),
        name="route_rank",
    )(logits_t)


def _dispatch_kernel(seg_ref, dest_ref, gate_ref, h_ref, g_ref, xs_ref, u_ref, zero_ref, sems):
    i = pl.program_id(0)
    n = pl.num_programs(0)
    slot = lax.rem(i, 2)
    tm, d = h_ref.shape

    def wait_rows(s):
        pltpu.make_async_copy(u_ref.at[s], xs_ref.at[pl.ds(0, tm)], sems.at[s]).wait()

    @pl.when(i == 0)
    def _():
        zero_ref[...] = jnp.zeros_like(zero_ref)

        def fill_block(start):
            fill = pltpu.make_async_copy(
                zero_ref, xs_ref.at[pl.ds(pl.multiple_of(start, MOE_BLOCK), MOE_BLOCK)], sems.at[0])
            fill.start()
            fill.wait()

        def fill_segment(e, carry):
            @pl.when(seg_ref[MOE_BUCKET_ROWS + e] > 0)
            def _():
                fill_block(seg_ref[e] - MOE_BLOCK)
            return carry

        lax.fori_loop(0, MOE_BUCKETS, fill_segment, 0)

        def fill_tail(blk, carry):
            fill_block(blk * MOE_BLOCK)
            return carry

        lax.fori_loop(seg_ref[MOE_BUCKET_ROWS - 1] // MOE_BLOCK, xs_ref.shape[0] // MOE_BLOCK, fill_tail, 0)

    @pl.when(i >= 2)
    def _():
        wait_rows(slot)

    u = _rms(h_ref[...], g_ref[...])
    half = d // 2
    as_bits = lambda v: lax.bitcast_convert_type(v.astype(BF16).astype(F32), jnp.uint32)
    words = (as_bits(u[:, half:]) & jnp.uint32(0xFFFF0000)) | (as_bits(u[:, :half]) >> 16)
    gates = jnp.concatenate([gate_ref[...], jnp.zeros((LANES - gate_ref.shape[0], tm), F32)], axis=0).T
    packed = jnp.concatenate([words, lax.bitcast_convert_type(gates, jnp.uint32),
                              jnp.zeros((tm, half - LANES), jnp.uint32)], axis=1)
    u_ref[slot] = packed.reshape(u_ref.shape[1:])

    def issue(q, carry):
        for k in range(ROW_DMA_UNROLL):
            t = q * ROW_DMA_UNROLL + k
            pltpu.make_async_copy(
                u_ref.at[slot, t], xs_ref.at[dest_ref[0, 0, t]], sems.at[slot]).start(priority=k % 2)
        return carry

    lax.fori_loop(0, tm // ROW_DMA_UNROLL, issue, 0)

    @pl.when(i == n - 1)
    def _():
        wait_rows(slot)

        @pl.when(n >= 2)
        def _():
            wait_rows(1 - slot)


def _dispatch(seg, dest, gate, h, gain, rows):
    t, d = h.shape
    tm = TOKEN_TILE
    sub = d // LANES
    return pl.pallas_call(
        _dispatch_kernel,
        grid_spec=pltpu.PrefetchScalarGridSpec(
            num_scalar_prefetch=1,
            grid=(t // tm,),
            in_specs=[
                pl.BlockSpec((1, 1, tm), lambda i, seg: (i, 0, 0), memory_space=pltpu.SMEM),
                pl.BlockSpec((8, tm), lambda i, seg: (0, i)),
                pl.BlockSpec((tm, d), lambda i, seg: (i, 0)),
                pl.BlockSpec((1, d), lambda i, seg: (0, 0)),
            ],
            out_specs=pl.BlockSpec(memory_space=pl.ANY),
            scratch_shapes=[pltpu.VMEM((2, tm, sub, LANES), jnp.uint32), pltpu.VMEM((MOE_BLOCK, sub, LANES), jnp.uint32),
                            pltpu.SemaphoreType.DMA((2,))],
        ),
        out_shape=jax.ShapeDtypeStruct((rows, sub, LANES), jnp.uint32),
        compiler_params=_cparams("arbitrary"),
        name="moe_dispatch",
    )(seg, dest, gate, h, gain)


def _expert_kernel(ea_ref, eb_ref, nu_ref, xs_ref, w1a_ref, w3a_ref, w2a_ref, w1b_ref, w3b_ref, w2b_ref, ys_ref):
    used = pl.program_id(0) < nu_ref[0]

    @pl.when(used)
    def _():
        rows, sub, lanes = xs_ref.shape
        d = sub * lanes
        packed = xs_ref[...].reshape(rows, d)
        words = packed[:, :d // 2]
        lo = lax.bitcast_convert_type(words << 16, F32).astype(BF16)
        hi = lax.bitcast_convert_type(words & jnp.uint32(0xFFFF0000), F32).astype(BF16)
        x = jnp.concatenate([lo, hi], axis=1)
        gates = lax.bitcast_convert_type(packed[:, d // 2:d // 2 + LANES], F32)

        def ffn(w1_ref, w3_ref, w2_ref):
            hid = _silu(_dot(x, w1_ref[0])) * _dot(x, w3_ref[0])
            return _dot(hid.astype(BF16), w2_ref[0])

        y = ffn(w1a_ref, w3a_ref, w2a_ref) * gates[:, 0:1] + ffn(w1b_ref, w3b_ref, w2b_ref) * gates[:, 1:2]
        ys_ref[...] = y.reshape(rows, sub, lanes)

    @pl.when(jnp.logical_not(used))
    def _():
        ys_ref[...] = jnp.zeros_like(ys_ref)


def _experts(block_ea, block_eb, n_used, xs, w1, w3, w2):
    rows, sub, lanes = xs.shape
    d = sub * lanes
    nb = rows // MOE_BLOCK
    w_in = lambda tbl: pl.BlockSpec((1, d, MOE_FF), lambda i, ea, eb, nu: ((ea, eb)[tbl][i], 0, 0))
    w_out = lambda tbl: pl.BlockSpec((1, MOE_FF, d), lambda i, ea, eb, nu: ((ea, eb)[tbl][i], 0, 0))
    return pl.pallas_call(
        _expert_kernel,
        grid_spec=pltpu.PrefetchScalarGridSpec(
            num_scalar_prefetch=3,
            grid=(nb,),
            in_specs=[
                pl.BlockSpec((MOE_BLOCK, sub, lanes), lambda i, ea, eb, nu: (jnp.minimum(i, nu[0] - 1), 0, 0)),
                w_in(0), w_in(0), w_out(0), w_in(1), w_in(1), w_out(1),
            ],
            out_specs=pl.BlockSpec((MOE_BLOCK, sub, lanes), lambda i, ea, eb, nu: (i, 0, 0)),
        ),
        out_shape=jax.ShapeDtypeStruct((rows, sub, lanes), F32),
        compiler_params=_cparams("arbitrary"),
        name="moe_experts",
    )(block_ea, block_eb, n_used, xs, w1, w3, w2, w1, w3, w2)


def _issue_row_gather(idx_ref, src_ref, dst_ref, sem, rows):
    def issue(q, carry):
        for k in range(ROW_DMA_UNROLL):
            r = q * ROW_DMA_UNROLL + k
            pltpu.make_async_copy(src_ref.at[idx_ref[0, 0, r]], dst_ref.at[r], sem).start(priority=k % 2)
        return carry

    lax.fori_loop(0, rows // ROW_DMA_UNROLL, issue, 0)


def _gather_kernel(dest_ref, ys_ref, o_ref, sem):
    tm = o_ref.shape[0]
    _issue_row_gather(dest_ref, ys_ref, o_ref, sem, tm)
    pltpu.make_async_copy(ys_ref.at[pl.ds(0, tm)], o_ref, sem).wait()


def _gather_rows(dest, ys):
    nt, _, tm = dest.shape
    _, sub, lanes = ys.shape
    return pl.pallas_call(
        _gather_kernel,
        grid=(nt,),
        in_specs=[
            pl.BlockSpec((1, 1, tm), lambda i: (i, 0, 0), memory_space=pltpu.SMEM),
            pl.BlockSpec(memory_space=pl.ANY),
        ],
        out_specs=pl.BlockSpec((tm, sub, lanes), lambda i: (i, 0, 0)),
        out_shape=jax.ShapeDtypeStruct((nt * tm, sub, lanes), F32),
        scratch_shapes=[pltpu.SemaphoreType.DMA(())],
        compiler_params=_cparams("arbitrary"),
        name="moe_gather",
    )(dest, ys)


def _final_kernel(dest_ref, next_ref, h_ref, g_ref, ys_ref, o_ref, ybuf, sems):
    step = pl.program_id(0) * pl.num_programs(1) + pl.program_id(1)
    last = pl.num_programs(0) * pl.num_programs(1) - 1
    slot = lax.rem(step, 2)
    rows, d = h_ref.shape[1:]

    @pl.when(step == 0)
    def _():
        _issue_row_gather(dest_ref, ys_ref, ybuf.at[slot], sems.at[slot], rows)

    @pl.when(step < last)
    def _():
        _issue_row_gather(next_ref, ys_ref, ybuf.at[1 - slot], sems.at[1 - slot], rows)

    pltpu.make_async_copy(ys_ref.at[pl.ds(0, rows)], ybuf.at[slot], sems.at[slot]).wait()
    o_ref[0] = _rms(h_ref[0] + ybuf[slot].reshape(rows, d), g_ref[...])


def _final_norm(h, dest, ys, gain, seq):
    b, lp, d = h.shape
    nc = lp // CHUNK
    ns = seq // CHUNK

    def chunk_of(step):
        return (step // ns) * nc + lax.rem(step, ns) + 1

    smem = lambda fn: pl.BlockSpec((1, 1, CHUNK), fn, memory_space=pltpu.SMEM)
    return pl.pallas_call(
        _final_kernel,
        grid=(b, ns),
        in_specs=[
            smem(lambda i, c: (chunk_of(i * ns + c), 0, 0)),
            smem(lambda i, c: (chunk_of(jnp.minimum(i * ns + c + 1, b * ns - 1)), 0, 0)),
            pl.BlockSpec((1, CHUNK, d), lambda i, c: (i, c + 1, 0)),
            pl.BlockSpec((1, d), lambda i, c: (0, 0)),
            pl.BlockSpec(memory_space=pl.ANY),
        ],
        out_specs=pl.BlockSpec((1, CHUNK, d), lambda i, c: (i, c, 0)),
        out_shape=jax.ShapeDtypeStruct((b, seq, d), F32),
        scratch_shapes=[pltpu.VMEM((2, CHUNK, d // LANES, LANES), F32), pltpu.SemaphoreType.DMA((2,))],
        compiler_params=_cparams("arbitrary", "arbitrary"),
        name="final_norm",
    )(dest, dest, h, gain, ys)


def _bucket_experts():
    lo, hi = [], []
    for g in range(MOE_GROUPS):
        for a in range(MOE_EPG):
            for c in range(a + 1, MOE_EPG):
                lo.append(g * MOE_EPG + a)
                hi.append(g * MOE_EPG + c)
    fill = MOE_BUCKET_ROWS - len(lo)
    return jnp.array(lo + [lo[-1]] * fill, I32), jnp.array(hi + [hi[-1]] * fill, I32)


def _moe_layer(h, y_mix, w_out, norm_gain, wg, bg, we, be, w1, w3, w2):
    t, d = h.shape
    n_logits = MOE_GROUPS + MOE_EXPERTS
    wr = jnp.concatenate([wg, we.reshape(d, MOE_EXPERTS), jnp.zeros((d, LANES - n_logits), F32)], axis=1)
    br = jnp.concatenate([bg, be.reshape(-1), jnp.zeros((LANES - n_logits,), F32)])
    wr_hi = wr.astype(BF16)
    wr_split = jnp.concatenate([wr_hi, (wr - wr_hi.astype(F32)).astype(BF16)], axis=1)
    h, logits_t = _out_proj(h, y_mix, w_out.astype(BF16), norm_gain[None, :], wr_split, br[None, :])
    bucket, gate, rank, counts = _route(logits_t)

    counts = counts[:, 0]
    padded = (counts + MOE_BLOCK - 1) // MOE_BLOCK * MOE_BLOCK
    pend = jnp.cumsum(padded)
    pstart = pend - padded
    ids = jnp.arange(MOE_BUCKET_ROWS, dtype=I32)
    dest = rank[0] + jnp.sum(jnp.where(bucket == ids[:, None], pstart[:, None], 0), axis=0)
    n_blocks = -(-t // MOE_BLOCK) + MOE_BUCKETS
    block_start = jnp.arange(n_blocks, dtype=I32) * MOE_BLOCK
    block_bucket = jnp.minimum(jnp.sum((pend[None, :] <= block_start[:, None]).astype(I32), axis=1),
                               MOE_BUCKET_ROWS - 1)
    n_used = (pend[-1:] // MOE_BLOCK).astype(I32)
    lo, hi = _bucket_experts()
    gate8 = jnp.concatenate([gate, jnp.zeros((6, t), F32)], axis=0)

    xs = _dispatch(jnp.concatenate([pend, padded]).astype(I32), dest.reshape(t // TOKEN_TILE, 1, TOKEN_TILE), gate8,
                   h, norm_gain[None, :], n_blocks * MOE_BLOCK)
    ys = _experts(lo[block_bucket], hi[block_bucket], n_used, xs, w1.astype(BF16), w3.astype(BF16), w2.astype(BF16))
    return h, dest, ys


def kernel(x, meta, norm_mix, norm_ffn, norm_final, ret_w_in, ret_w_out, ret_norm, ssm_w_in, ssm_conv_w,
           ssm_conv_b, ssm_dt_bias, ssm_a_log, ssm_d, ssm_norm, ssm_w_out, moe_wg, moe_bg, moe_we, moe_be,
           moe_w1, moe_w3, moe_w2):
    b, seq, d = x.shape
    lp = CHUNK + seq
    t = b * lp
    depth = norm_mix.shape[0]
    h = jnp.concatenate([jnp.zeros((b, PAD, d), F32), jnp.broadcast_to(meta[None], (b, N_META, d)), x], axis=1)

    half = RET_QK_DIM // 2
    inv_freq = 1.0 / (RET_ROPE_BASE ** jnp.linspace(0.0, 1.0, half, dtype=F32))
    ang = (jnp.arange(lp, dtype=F32) - PAD)[:, None] * inv_freq[None, :]
    cos, sin = jnp.cos(ang), jnp.sin(ang)
    perm = jnp.concatenate([jnp.arange(0, RET_QK_DIM, 2), jnp.arange(1, RET_QK_DIM, 2)])
    qk_perm = (jnp.arange(2 * RET_HEADS)[:, None] * RET_QK_DIM + perm[None, :]).reshape(-1)

    h = h.reshape(t, d)
    moe = None
    for i in range(depth):
        j = i // 2
        y_moe = None if moe is None else _gather_rows(moe[0].reshape(t // TOKEN_TILE, 1, TOKEN_TILE), moe[1])
        if i % 2 == 0:
            if y_moe is not None:
                h = h + y_moe.reshape(t, d)
            w_in = jnp.concatenate([ret_w_in[j][:, qk_perm], ret_w_in[j][:, 2 * RET_QK:]], axis=1).astype(BF16)
            qkvg = _ret_in(h.reshape(b, lp, d), norm_mix[i][None, :], cos, sin, w_in)
            y = _retention(qkvg, ret_norm[j][None, :])
            w_out = ret_w_out[j]
        else:
            if y_moe is None:
                y_moe = jnp.zeros((t, d // LANES, LANES), F32)
            w_in = jnp.concatenate([ssm_w_in[j], jnp.zeros((d, LANES - SSM_HEADS), F32)], axis=1).astype(BF16)
            h, zxbc, dt = _ssm_in(h, y_moe, norm_mix[i][None, :], w_in, ssm_conv_w[j], ssm_conv_b[j][None, :], lp)
            lane_pad = lambda v: jnp.concatenate([v, jnp.zeros((LANES - SSM_HEADS,), F32)])[None, :]
            y = _ssd(zxbc.reshape(b, lp, SSM_MAIN), dt.reshape(b, lp, LANES), lane_pad(ssm_dt_bias[j]),
                     lane_pad(ssm_a_log[j]), jnp.repeat(ssm_d[j], SSM_HEAD_DIM)[None, :], ssm_norm[j][None, :])
            w_out = ssm_w_out[j]
        h, *moe = _moe_layer(h, y.reshape(t, -1), w_out, norm_ffn[i], moe_wg[i], moe_bg[i], moe_we[i],
                             moe_be[i], moe_w1[i], moe_w3[i], moe_w2[i])
    return _final_norm(h.reshape(b, lp, d), moe[0].reshape(t // CHUNK, 1, CHUNK), moe[1], norm_final[None, :], seq)
```

```python
import functools
import math

import jax
import jax.numpy as jnp
from jax import lax
from jax.experimental import pallas as pl
from jax.experimental.pallas import tpu as pltpu

F32 = jnp.float32
BF16 = jnp.bfloat16
I32 = jnp.int32

D_MODEL = 1024
N_META = 16
CHUNK = 128
PAD = CHUNK - N_META
EPS = 1e-6

RET_HEADS = 4
RET_QK_DIM = 256
RET_V_DIM = 512
RET_QK = RET_HEADS * RET_QK_DIM
RET_V = RET_HEADS * RET_V_DIM
RET_IN = 2 * RET_QK + 2 * RET_V
RET_ROPE_BASE = 10000.0

SSM_INNER = 2048
SSM_HEAD_DIM = 64
SSM_HEADS = 32
SSM_GROUPS = 8
SSM_HPG = 4
SSM_STATE = 128
SSM_CONV = 4
SSM_BC = SSM_GROUPS * SSM_STATE
SSM_CONV_DIM = SSM_INNER + 2 * SSM_BC
SSM_MAIN = SSM_INNER + SSM_CONV_DIM
LANES = 128
SSM_GROUP_W = SSM_HPG * SSM_HEAD_DIM

MOE_GROUPS = 4
MOE_EPG = 8
MOE_EXPERTS = 32
MOE_FF = 512
MOE_BLOCK = 256
MOE_PAIRS = MOE_EPG * (MOE_EPG - 1) // 2
MOE_BUCKETS = MOE_GROUPS * MOE_PAIRS
MOE_BUCKET_ROWS = 128
ROUTER_ROWS = 40

ROW_DMA_UNROLL = 4
SSM_IN_STEP = 256
SSM_IN_ROWS = 64
LOG2E = math.log2(math.e)
TOKEN_TILE = 512
VMEM_LIMIT = 56 * 1024 * 1024


def _cparams(*sem):
    return pltpu.CompilerParams(dimension_semantics=sem, vmem_limit_bytes=VMEM_LIMIT)


def _rms(x, gain):
    ms = jnp.mean(x * x, axis=-1, keepdims=True)
    return x * lax.rsqrt(ms + EPS) * gain


def _silu(x):
    hx = 0.5 * x
    return hx + hx * jnp.tanh(hx)


def _dot(a, b):
    return jnp.dot(a, b, preferred_element_type=F32)


def _dot_nt(a, b, precision=None):
    return lax.dot_general(a, b, (((1,), (1,)), ((), ())), preferred_element_type=F32, precision=precision)


def _dot_tn(a, b):
    return lax.dot_general(a, b, (((0,), (0,)), ((), ())), preferred_element_type=F32)


def _row_tile(rows, cap=544):
    best = 16
    for t in range(16, cap + 1, 16):
        if rows % t == 0:
            best = t
    return best


def _ret_in_kernel(h_ref, g_ref, cos_ref, sin_ref, w_ref, o_ref):
    xn = _rms(h_ref[0], g_ref[...]).astype(BF16)
    cos = cos_ref[...]
    sin = sin_ref[...]
    half = RET_QK_DIM // 2
    for j in range(2 * RET_HEADS):
        c0 = j * RET_QK_DIM
        acc = _dot(xn, w_ref[:, c0:c0 + RET_QK_DIM])
        if j >= RET_HEADS:
            acc = acc * (RET_QK_DIM ** -0.5)
        e = acc[:, :half]
        o = acc[:, half:]
        o_ref[0, :, c0:c0 + half] = (e * cos - o * sin).astype(BF16)
        o_ref[0, :, c0 + half:c0 + RET_QK_DIM] = (o * cos + e * sin).astype(BF16)
    step = 1024
    for c0 in range(2 * RET_QK, RET_IN, step):
        o_ref[0, :, c0:c0 + step] = _dot(xn, w_ref[:, c0:c0 + step]).astype(BF16)


def _ret_in(h, gain, cos, sin, w):
    b, lp, d = h.shape
    tm = _row_tile(lp)
    return pl.pallas_call(
        _ret_in_kernel,
        grid=(b, lp // tm),
        in_specs=[
            pl.BlockSpec((1, tm, d), lambda i, j: (i, j, 0)),
            pl.BlockSpec((1, d), lambda i, j: (0, 0)),
            pl.BlockSpec((tm, RET_QK_DIM // 2), lambda i, j: (j, 0)),
            pl.BlockSpec((tm, RET_QK_DIM // 2), lambda i, j: (j, 0)),
            pl.BlockSpec((d, RET_IN), lambda i, j: (0, 0)),
        ],
        out_specs=pl.BlockSpec((1, tm, RET_IN), lambda i, j: (i, j, 0)),
        out_shape=jax.ShapeDtypeStruct((b, lp, RET_IN), BF16),
        compiler_params=_cparams("parallel", "parallel"),
        name="ret_in_proj",
    )(h, gain, cos, sin, w)


def _ret_kernel(q_ref, k_ref, v_ref, g_ref, hn_ref, o_ref, state_ref):
    @pl.when(pl.program_id(1) == 0)
    def _():
        state_ref[...] = jnp.zeros_like(state_ref)

    row = lax.broadcasted_iota(I32, (CHUNK, CHUNK), 0).astype(F32)
    col = lax.broadcasted_iota(I32, (CHUNK, CHUNK), 1).astype(F32)
    diff = row - col
    ridx = row[:, :1]
    for hh in range(RET_HEADS):
        lg = math.log(1.0 - 2.0 ** (-5.0 - hh))
        intra = jnp.where(diff >= 0, jnp.exp(lg * jnp.maximum(diff, 0.0)), 0.0)
        qdec = jnp.exp(lg * (ridx + 1.0))
        kdec = jnp.exp(lg * (CHUNK - 1.0 - ridx))
        cdec = math.exp(lg * CHUNK)
        qh = q_ref[0, :, hh * RET_QK_DIM:(hh + 1) * RET_QK_DIM]
        kh = k_ref[0, :, hh * RET_QK_DIM:(hh + 1) * RET_QK_DIM]
        vh = v_ref[0, :, hh * RET_V_DIM:(hh + 1) * RET_V_DIM]
        scores = _dot_nt(qh, kh) * intra
        st = state_ref[hh]
        y = _dot(scores.astype(BF16), vh) + _dot(qh, st.astype(BF16)) * qdec
        kd = (kh.astype(F32) * kdec).astype(BF16)
        state_ref[hh] = st * cdec + _dot_tn(kd, vh)
        yn = _rms(y, hn_ref[:, hh * RET_V_DIM:(hh + 1) * RET_V_DIM])
        gh = g_ref[0, :, hh * RET_V_DIM:(hh + 1) * RET_V_DIM].astype(F32)
        o_ref[0, :, hh * RET_V_DIM:(hh + 1) * RET_V_DIM] = (_silu(gh) * yn).astype(BF16)


def _retention(qkvg, head_norm):
    b, lp, _ = qkvg.shape
    nc = lp // CHUNK
    return pl.pallas_call(
        _ret_kernel,
        grid=(b, nc),
        in_specs=[
            pl.BlockSpec((1, CHUNK, RET_QK), lambda i, c: (i, c, 0)),
            pl.BlockSpec((1, CHUNK, RET_QK), lambda i, c: (i, c, 1)),
            pl.BlockSpec((1, CHUNK, RET_V), lambda i, c: (i, c, 1)),
            pl.BlockSpec((1, CHUNK, RET_V), lambda i, c: (i, c, 2)),
            pl.BlockSpec((1, RET_V), lambda i, c: (0, 0)),
        ],
        out_specs=pl.BlockSpec((1, CHUNK, RET_V), lambda i, c: (i, c, 0)),
        out_shape=jax.ShapeDtypeStruct((b, lp, RET_V), BF16),
        scratch_shapes=[pltpu.VMEM((RET_HEADS, RET_QK_DIM, RET_V_DIM), F32)],
        compiler_params=_cparams("parallel", "arbitrary"),
        name="retention",
    )(qkvg, qkvg, qkvg, qkvg, head_norm)


def _ssm_in_kernel(xn_ref, w_ref, cw_ref, cb_ref, o_ref, dt_ref, tail_ref, acc_ref, *, rows_per_seq):
    i = pl.program_id(0)
    tm = xn_ref.shape[0]

    @pl.when(i == 0)
    def _():
        tail_ref[...] = jnp.zeros_like(tail_ref)

    step = acc_ref.shape[2]
    n_steps = SSM_MAIN // step
    rb = SSM_IN_ROWS

    pos = lax.rem(i * tm, rows_per_seq) + lax.broadcasted_iota(I32, (tm, 1), 0)
    is_pad = jnp.logical_or(pos < PAD, jnp.logical_and(pos >= rows_per_seq, pos < rows_per_seq + PAD))

    def project(j):
        c0 = j * step
        acc = _dot(xn_ref[...], w_ref[:, c0:c0 + step])
        acc_ref[j % 2, 8:, :] = acc if c0 < SSM_INNER else jnp.where(is_pad, 0.0, acc)

    project(0)
    for j in range(n_steps):
        if j + 1 < n_steps:
            project(j + 1)
        c0 = j * step
        s = j % 2
        if c0 < SSM_INNER:
            for r0 in range(0, tm, rb):
                o_ref[r0:r0 + rb, c0:c0 + step] = _silu(acc_ref[s, 8 + r0:8 + r0 + rb, :]).astype(BF16)
            continue
        cc = c0 - SSM_INNER
        acc_ref[s, 0:8, :] = tail_ref[:, cc:cc + step]
        tail_ref[:, cc:cc + step] = acc_ref[s, tm:tm + 8, :]
        taps = [cw_ref[k:k + 1, cc:cc + step] for k in range(SSM_CONV)]
        bias = cb_ref[:, cc:cc + step]
        for r0 in range(0, tm, rb):
            ext = acc_ref[s, r0:r0 + rb + 8, :]
            conv = bias + taps[SSM_CONV - 1] * ext[8:]
            for k in range(1, SSM_CONV):
                conv = conv + taps[SSM_CONV - 1 - k] * pltpu.roll(ext, k, axis=0)[8:]
            o_ref[r0:r0 + rb, c0:c0 + step] = _silu(conv).astype(BF16)
    dt_ref[...] = _dot(xn_ref[...], w_ref[:, SSM_MAIN:SSM_MAIN + LANES])


def _ssm_in(xn, w, conv_w, conv_b, rows_per_seq):
    t, d = xn.shape
    tm = TOKEN_TILE
    assert tm <= rows_per_seq
    const = lambda r, c: pl.BlockSpec((r, c), lambda i: (0, 0))
    return pl.pallas_call(
        functools.partial(_ssm_in_kernel, rows_per_seq=rows_per_seq),
        grid=(t // tm,),
        in_specs=[
            pl.BlockSpec((tm, d), lambda i: (i, 0)),
            const(d, SSM_MAIN + LANES), const(SSM_CONV, SSM_CONV_DIM), const(1, SSM_CONV_DIM),
        ],
        out_specs=[
            pl.BlockSpec((tm, SSM_MAIN), lambda i: (i, 0)),
            pl.BlockSpec((tm, LANES), lambda i: (i, 0)),
        ],
        out_shape=[jax.ShapeDtypeStruct((t, SSM_MAIN), BF16), jax.ShapeDtypeStruct((t, LANES), F32)],
        scratch_shapes=[pltpu.VMEM((8, SSM_CONV_DIM), F32), pltpu.VMEM((2, tm + 8, SSM_IN_STEP), F32)],
        compiler_params=_cparams("arbitrary"),
        name="ssm_in_proj",
    )(xn, w, conv_w, conv_b)


def _ssd_kernel(zg_ref, x_ref, bc_ref, dt_ref, dtb_ref, alog_ref, dsk_ref, gn_ref, o_ref, state_ref):
    c = pl.program_id(1)

    @pl.when(c == 0)
    def _():
        state_ref[...] = jnp.zeros_like(state_ref)

    row = lax.broadcasted_iota(I32, (CHUNK, CHUNK), 0)
    col = lax.broadcasted_iota(I32, (CHUNK, CHUNK), 1)
    causal_bias = jnp.where(row >= col, 0.0, -jnp.inf)
    low_half = col < SSM_HEAD_DIM
    valid = row[:, :1] >= jnp.where(c > 0, 0, PAD)

    dtr = dt_ref[0] + dtb_ref[...]
    dtv = jnp.maximum(dtr, 0.0) + jnp.log(1.0 + jnp.exp(-jnp.abs(dtr)))
    dtv = jnp.where(valid, dtv, 0.0)
    da = dtv * (-jnp.exp(alog_ref[...]))
    tril = jnp.where(row >= col, 1.0, 0.0)
    acs = jnp.dot(tril, da, preferred_element_type=F32, precision=lax.Precision.HIGHEST)
    last = acs[CHUNK - 1:CHUNK, :]
    acs2 = acs * LOG2E
    src_t = (jnp.log2(dtv) - acs2).T
    w_t = (dtv * jnp.exp(last - acs)).T
    elast = jnp.exp(last)

    def pair_row(v, h0):
        return jnp.where(low_half[:1], v[:, h0:h0 + 1], v[:, h0 + 1:h0 + 2])

    groups = range(SSM_GROUPS)
    gsl = [slice(gi * SSM_GROUP_W, (gi + 1) * SSM_GROUP_W) for gi in groups]
    bgs = [bc_ref[0, :, gi * SSM_STATE:(gi + 1) * SSM_STATE] for gi in groups]
    cgs = [bc_ref[0, :, SSM_BC + gi * SSM_STATE:SSM_BC + (gi + 1) * SSM_STATE] for gi in groups]
    sts = [state_ref[gi] for gi in groups]
    cbs = [_dot_nt(cgs[gi], bgs[gi]) for gi in groups]
    yoffs = [_dot(cgs[gi], sts[gi].astype(BF16)) for gi in groups]
    bts = [bgs[gi].astype(F32).T for gi in groups]

    outs = []
    eexp = []
    for gi in groups:
        for pp in range(SSM_HPG // 2):
            h0 = gi * SSM_HPG + 2 * pp
            top, bot, ecols = [], [], []
            for hd in (h0, h0 + 1):
                acs_b = jnp.broadcast_to(acs2[:, hd:hd + 1], (CHUNK, CHUNK))
                top.append(cbs[gi] * jnp.exp2(acs_b + (src_t[hd:hd + 1, :] + causal_bias)))
                bot.append(bts[gi] * w_t[hd:hd + 1, :])
                ecols.append(jnp.exp2(acs_b))
            lhs = jnp.concatenate([jnp.concatenate(top, axis=1), jnp.concatenate(bot, axis=1)], axis=0)
            xp = x_ref[0, :, h0 * SSM_HEAD_DIM:(h0 + 2) * SSM_HEAD_DIM]
            zero = jnp.zeros_like(xp)
            rhs = jnp.concatenate([jnp.where(low_half, xp, zero), jnp.where(low_half, zero, xp)], axis=0)
            outs.append(_dot(lhs.astype(BF16), rhs))
            eexp.append(jnp.where(low_half, ecols[0], ecols[1]))

    for gi in groups:
        pairs = [gi * (SSM_HPG // 2) + pp for pp in range(SSM_HPG // 2)]
        heads = [gi * SSM_HPG + 2 * pp for pp in range(SSM_HPG // 2)]
        lexp = jnp.concatenate([pair_row(elast, h0) for h0 in heads], axis=1)
        state_ref[gi] = sts[gi] * lexp + jnp.concatenate([outs[p][CHUNK:] for p in pairs], axis=1)
        xg = x_ref[0, :, gsl[gi]].astype(F32)
        y = jnp.concatenate([outs[p][:CHUNK] for p in pairs], axis=1)
        y = y + yoffs[gi] * jnp.concatenate([eexp[p] for p in pairs], axis=1)
        y = y + dsk_ref[:, gsl[gi]] * xg
        y = y * zg_ref[0, :, gsl[gi]].astype(F32)
        o_ref[0, :, gsl[gi]] = _rms(y, gn_ref[:, gsl[gi]]).astype(BF16)


def _ssd(zxbc, dt, dt_bias, a_log, d_skip, gate_norm):
    b, lp, _ = zxbc.shape
    nc = lp // CHUNK
    wide = lambda j: pl.BlockSpec((1, CHUNK, SSM_INNER), lambda i, c: (i, c, j))
    const = lambda r, w: pl.BlockSpec((r, w), lambda i, c: (0, 0))
    return pl.pallas_call(
        _ssd_kernel,
        grid=(b, nc),
        in_specs=[
            wide(0), wide(1), wide(2),
            pl.BlockSpec((1, CHUNK, LANES), lambda i, c: (i, c, 0)),
            const(1, LANES), const(1, LANES), const(1, SSM_INNER), const(1, SSM_INNER),
        ],
        out_specs=pl.BlockSpec((1, CHUNK, SSM_INNER), lambda i, c: (i, c, 0)),
        out_shape=jax.ShapeDtypeStruct((b, lp, SSM_INNER), BF16),
        scratch_shapes=[pltpu.VMEM((SSM_GROUPS, SSM_STATE, SSM_GROUP_W), F32)],
        compiler_params=_cparams("parallel", "arbitrary"),
        name="ssd",
    )(zxbc, zxbc, zxbc, dt, dt_bias, a_log, d_skip, gate_norm)


def _out_proj_kernel(h_ref, y_ref, w_ref, g_ref, wr_ref, br_ref, ho_ref, lg_ref):
    hn = h_ref[...] + _dot(y_ref[...], w_ref[...])
    ho_ref[...] = hn
    u = _rms(hn, g_ref[...])
    u_hi = u.astype(BF16)
    u_lo = (u - u_hi.astype(F32)).astype(BF16)
    p = _dot(u_hi, wr_ref[...])
    logits = p[:, :LANES] + (p[:, LANES:] + _dot(u_lo, wr_ref[:, :LANES])) + br_ref[...]
    lg_ref[...] = logits.T[:ROUTER_ROWS]


def _out_proj(h, y, w, gain, wr_t, br):
    t, d = h.shape
    tm = TOKEN_TILE
    kin = y.shape[1]
    return pl.pallas_call(
        _out_proj_kernel,
        grid=(t // tm,),
        in_specs=[
            pl.BlockSpec((tm, d), lambda i: (i, 0)),
            pl.BlockSpec((tm, kin), lambda i: (i, 0)),
            pl.BlockSpec((kin, d), lambda i: (0, 0)),
            pl.BlockSpec((1, d), lambda i: (0, 0)),
            pl.BlockSpec((d, 2 * LANES), lambda i: (0, 0)),
            pl.BlockSpec((1, LANES), lambda i: (0, 0)),
        ],
        out_specs=[
            pl.BlockSpec((tm, d), lambda i: (i, 0)),
            pl.BlockSpec((ROUTER_ROWS, tm), lambda i: (0, i)),
        ],
        out_shape=[jax.ShapeDtypeStruct((t, d), F32), jax.ShapeDtypeStruct((ROUTER_ROWS, t), F32)],
        compiler_params=_cparams("parallel"),
        name="out_proj_router",
    )(h, y, w, gain, wr_t, br)


def _first_argmax(v, n):
    ridx = lax.broadcasted_iota(I32, v.shape, 0).astype(F32)
    vmax = jnp.max(v, axis=0, keepdims=True)
    idx = jnp.min(jnp.where(v == vmax, ridx, float(n)), axis=0, keepdims=True)
    return vmax, idx.astype(I32)


def _route_kernel(lg_ref, bkt_ref, gate_ref, rank_ref, cnt_ref, carry_ref):
    @pl.when(pl.program_id(0) == 0)
    def _():
        carry_ref[...] = jnp.zeros_like(carry_ref)

    logits = lg_ref[...]
    tm = logits.shape[1]
    gl = logits[0:MOE_GROUPS]
    gmax, gsel = _first_argmax(gl, MOE_GROUPS)
    p_group = 1.0 / jnp.sum(jnp.exp(gl - gmax), axis=0, keepdims=True)
    el = logits[MOE_GROUPS:MOE_GROUPS + MOE_EPG]
    for gg in range(1, MOE_GROUPS):
        el = jnp.where(gsel == gg, logits[MOE_GROUPS + gg * MOE_EPG:MOE_GROUPS + (gg + 1) * MOE_EPG], el)
    ex = jnp.exp(el - jnp.max(el, axis=0, keepdims=True))
    p = ex / jnp.sum(ex, axis=0, keepdims=True)
    p1, i1 = _first_argmax(p, MOE_EPG)
    ridx8 = lax.broadcasted_iota(I32, p.shape, 0)
    p2, i2 = _first_argmax(jnp.where(ridx8 == i1, -1.0, p), MOE_EPG)
    denom = p1 + p2
    gate1 = p_group * p1 / denom
    gate2 = p_group * p2 / denom
    first_low = i1 < i2
    gate_ref[0:1, :] = jnp.where(first_low, gate1, gate2)
    gate_ref[1:2, :] = jnp.where(first_low, gate2, gate1)
    lo = jnp.minimum(i1, i2).astype(F32)
    hi = jnp.maximum(i1, i2).astype(F32)
    pair = lo * (2.0 * MOE_EPG - 1.0 - lo) * 0.5 + (hi - lo - 1.0)
    bucket = gsel * MOE_PAIRS + pair.astype(I32)
    bkt_ref[...] = bucket

    ridx = lax.broadcasted_iota(I32, (MOE_BUCKET_ROWS, tm), 0)
    onehot = jnp.where(ridx == bucket, 1.0, 0.0)
    before = jnp.where(lax.broadcasted_iota(I32, (tm, tm), 0) < lax.broadcasted_iota(I32, (tm, tm), 1), 1.0, 0.0)
    prefix = _dot(onehot.astype(BF16), before.astype(BF16))
    carry = carry_ref[...]
    rank_ref[...] = jnp.sum(onehot * (carry + prefix), axis=0, keepdims=True).astype(I32)
    carry = carry + jnp.sum(onehot, axis=1, keepdims=True)
    carry_ref[...] = carry
    cnt_ref[...] = jnp.broadcast_to(carry, cnt_ref.shape).astype(I32)


def _route(logits_t):
    t = logits_t.shape[1]
    tm = TOKEN_TILE
    rows = lambda r: pl.BlockSpec((r, tm), lambda i: (0, i))
    return pl.pallas_call(
        _route_kernel,
        grid=(t // tm,),
        in_specs=[rows(ROUTER_ROWS)],
        out_specs=[rows(1), rows(2), rows(1), pl.BlockSpec((MOE_BUCKET_ROWS, LANES), lambda i: (0, 0))],
        out_shape=[jax.ShapeDtypeStruct((1, t), I32), jax.ShapeDtypeStruct((2, t), F32),
                   jax.ShapeDtypeStruct((1, t), I32), jax.ShapeDtypeStruct((MOE_BUCKET_ROWS, LANES), I32)],
        scratch_shapes=[pltpu.VMEM((MOE_BUCKET_ROWS, 1), F32)],
        compiler_params=_cparams("arbitrary"),
        name="route_rank",
    )(logits_t)


def _dispatch_kernel(seg_ref, dest_ref, gate_ref, h_ref, g_ref, xs_ref, u_ref, zero_ref, sems):
    i = pl.program_id(0)
    n = pl.num_programs(0)
    slot = lax.rem(i, 2)
    tm, d = h_ref.shape

    def wait_rows(s):
        pltpu.make_async_copy(u_ref.at[s], xs_ref.at[pl.ds(0, tm)], sems.at[s]).wait()

    @pl.when(i == 0)
    def _():
        zero_ref[...] = jnp.zeros_like(zero_ref)

        def fill_block(start):
            fill = pltpu.make_async_copy(
                zero_ref, xs_ref.at[pl.ds(pl.multiple_of(start, MOE_BLOCK), MOE_BLOCK)], sems.at[0])
            fill.start()
            fill.wait()

        def fill_segment(e, carry):
            @pl.when(seg_ref[MOE_BUCKET_ROWS + e] > 0)
            def _():
                fill_block(seg_ref[e] - MOE_BLOCK)
            return carry

        lax.fori_loop(0, MOE_BUCKETS, fill_segment, 0)

        def fill_tail(blk, carry):
            fill_block(blk * MOE_BLOCK)
            return carry

        lax.fori_loop(seg_ref[MOE_BUCKET_ROWS - 1] // MOE_BLOCK, xs_ref.shape[0] // MOE_BLOCK, fill_tail, 0)

    @pl.when(i >= 2)
    def _():
        wait_rows(slot)

    u = _rms(h_ref[...], g_ref[...])
    half = d // 2
    as_bits = lambda v: lax.bitcast_convert_type(v.astype(BF16).astype(F32), jnp.uint32)
    words = (as_bits(u[:, half:]) & jnp.uint32(0xFFFF0000)) | (as_bits(u[:, :half]) >> 16)
    gates = jnp.concatenate([gate_ref[...], jnp.zeros((LANES - gate_ref.shape[0], tm), F32)], axis=0).T
    packed = jnp.concatenate([words, lax.bitcast_convert_type(gates, jnp.uint32),
                              jnp.zeros((tm, half - LANES), jnp.uint32)], axis=1)
    u_ref[slot] = packed.reshape(u_ref.shape[1:])

    def issue(q, carry):
        for k in range(ROW_DMA_UNROLL):
            t = q * ROW_DMA_UNROLL + k
            pltpu.make_async_copy(
                u_ref.at[slot, t], xs_ref.at[dest_ref[0, 0, t]], sems.at[slot]).start(priority=k % 2)
        return carry

    lax.fori_loop(0, tm // ROW_DMA_UNROLL, issue, 0)

    @pl.when(i == n - 1)
    def _():
        wait_rows(slot)

        @pl.when(n >= 2)
        def _():
            wait_rows(1 - slot)


def _dispatch(seg, dest, gate, h, gain, rows):
    t, d = h.shape
    tm = TOKEN_TILE
    sub = d // LANES
    return pl.pallas_call(
        _dispatch_kernel,
        grid_spec=pltpu.PrefetchScalarGridSpec(
            num_scalar_prefetch=1,
            grid=(t // tm,),
            in_specs=[
                pl.BlockSpec((1, 1, tm), lambda i, seg: (i, 0, 0), memory_space=pltpu.SMEM),
                pl.BlockSpec((8, tm), lambda i, seg: (0, i)),
                pl.BlockSpec((tm, d), lambda i, seg: (i, 0)),
                pl.BlockSpec((1, d), lambda i, seg: (0, 0)),
            ],
            out_specs=pl.BlockSpec(memory_space=pl.ANY),
            scratch_shapes=[pltpu.VMEM((2, tm, sub, LANES), jnp.uint32), pltpu.VMEM((MOE_BLOCK, sub, LANES), jnp.uint32),
                            pltpu.SemaphoreType.DMA((2,))],
        ),
        out_shape=jax.ShapeDtypeStruct((rows, sub, LANES), jnp.uint32),
        compiler_params=_cparams("arbitrary"),
        name="moe_dispatch",
    )(seg, dest, gate, h, gain)


def _expert_kernel(ea_ref, eb_ref, nu_ref, xs_ref, w1a_ref, w3a_ref, w2a_ref, w1b_ref, w3b_ref, w2b_ref, ys_ref):
    used = pl.program_id(0) < nu_ref[0]

    @pl.when(used)
    def _():
        rows, sub, lanes = xs_ref.shape
        d = sub * lanes
        packed = xs_ref[...].reshape(rows, d)
        words = packed[:, :d // 2]
        lo = lax.bitcast_convert_type(words << 16, F32).astype(BF16)
        hi = lax.bitcast_convert_type(words & jnp.uint32(0xFFFF0000), F32).astype(BF16)
        x = jnp.concatenate([lo, hi], axis=1)
        gates = lax.bitcast_convert_type(packed[:, d // 2:d // 2 + LANES], F32)

        def ffn(w1_ref, w3_ref, w2_ref):
            hid = _silu(_dot(x, w1_ref[0].astype(BF16))) * _dot(x, w3_ref[0].astype(BF16))
            return _dot(hid.astype(BF16), w2_ref[0].astype(BF16))

        y = ffn(w1a_ref, w3a_ref, w2a_ref) * gates[:, 0:1] + ffn(w1b_ref, w3b_ref, w2b_ref) * gates[:, 1:2]
        ys_ref[...] = y.reshape(rows, sub, lanes)

    @pl.when(jnp.logical_not(used))
    def _():
        ys_ref[...] = jnp.zeros_like(ys_ref)


def _experts(block_ea, block_eb, n_used, xs, w1, w3, w2):
    rows, sub, lanes = xs.shape
    d = sub * lanes
    nb = rows // MOE_BLOCK
    w_in = lambda tbl: pl.BlockSpec((1, d, MOE_FF), lambda i, ea, eb, nu: ((ea, eb)[tbl][i], 0, 0))
    w_out = lambda tbl: pl.BlockSpec((1, MOE_FF, d), lambda i, ea, eb, nu: ((ea, eb)[tbl][i], 0, 0))
    return pl.pallas_call(
        _expert_kernel,
        grid_spec=pltpu.PrefetchScalarGridSpec(
            num_scalar_prefetch=3,
            grid=(nb,),
            in_specs=[
                pl.BlockSpec((MOE_BLOCK, sub, lanes), lambda i, ea, eb, nu: (jnp.minimum(i, nu[0] - 1), 0, 0)),
                w_in(0), w_in(0), w_out(0), w_in(1), w_in(1), w_out(1),
            ],
            out_specs=pl.BlockSpec((MOE_BLOCK, sub, lanes), lambda i, ea, eb, nu: (i, 0, 0)),
        ),
        out_shape=jax.ShapeDtypeStruct((rows, sub, lanes), F32),
        compiler_params=_cparams("arbitrary"),
        name="moe_experts",
    )(block_ea, block_eb, n_used, xs, w1, w3, w2, w1, w3, w2)


def _issue_row_gather(idx_ref, src_ref, dst_ref, sem, rows):
    def issue(q, carry):
        for k in range(ROW_DMA_UNROLL):
            r = q * ROW_DMA_UNROLL + k
            pltpu.make_async_copy(src_ref.at[idx_ref[0, 0, r]], dst_ref.at[r], sem).start(priority=k % 2)
        return carry

    lax.fori_loop(0, rows // ROW_DMA_UNROLL, issue, 0)


def _combine_kernel(dest_ref, next_ref, h_ref, g_ref, ys_ref, ho_ref, xn_ref, ybuf, sems):
    i = pl.program_id(0)
    slot = lax.rem(i, 2)
    tm, d = h_ref.shape

    @pl.when(i == 0)
    def _():
        _issue_row_gather(dest_ref, ys_ref, ybuf.at[slot], sems.at[slot], tm)

    @pl.when(i + 1 < pl.num_programs(0))
    def _():
        _issue_row_gather(next_ref, ys_ref, ybuf.at[1 - slot], sems.at[1 - slot], tm)

    pltpu.make_async_copy(ys_ref.at[pl.ds(0, tm)], ybuf.at[slot], sems.at[slot]).wait()
    h = h_ref[...] + ybuf[slot].reshape(tm, d)
    ho_ref[...] = h
    xn_ref[...] = _rms(h, g_ref[...]).astype(BF16)


def _combine_norm(dest, ys, h, gain):
    n, _, tm = dest.shape
    t, d = h.shape
    smem = lambda fn: pl.BlockSpec((1, 1, tm), fn, memory_space=pltpu.SMEM)
    return pl.pallas_call(
        _combine_kernel,
        grid=(n,),
        in_specs=[
            smem(lambda i: (i, 0, 0)),
            smem(lambda i: (jnp.minimum(i + 1, n - 1), 0, 0)),
            pl.BlockSpec((tm, d), lambda i: (i, 0)),
            pl.BlockSpec((1, d), lambda i: (0, 0)),
            pl.BlockSpec(memory_space=pl.ANY),
        ],
        out_specs=[pl.BlockSpec((tm, d), lambda i: (i, 0)), pl.BlockSpec((tm, d), lambda i: (i, 0))],
        out_shape=[jax.ShapeDtypeStruct((t, d), F32), jax.ShapeDtypeStruct((t, d), BF16)],
        scratch_shapes=[pltpu.VMEM((2, tm, d // LANES, LANES), F32), pltpu.SemaphoreType.DMA((2,))],
        compiler_params=_cparams("arbitrary"),
        name="moe_combine",
    )(dest, dest, h, gain, ys)


def _final_kernel(dest_ref, next_ref, h_ref, g_ref, ys_ref, o_ref, ybuf, sems):
    step = pl.program_id(0) * pl.num_programs(1) + pl.program_id(1)
    last = pl.num_programs(0) * pl.num_programs(1) - 1
    slot = lax.rem(step, 2)
    rows, d = h_ref.shape[1:]

    @pl.when(step == 0)
    def _():
        _issue_row_gather(dest_ref, ys_ref, ybuf.at[slot], sems.at[slot], rows)

    @pl.when(step < last)
    def _():
        _issue_row_gather(next_ref, ys_ref, ybuf.at[1 - slot], sems.at[1 - slot], rows)

    pltpu.make_async_copy(ys_ref.at[pl.ds(0, rows)], ybuf.at[slot], sems.at[slot]).wait()
    o_ref[0] = _rms(h_ref[0] + ybuf[slot].reshape(rows, d), g_ref[...])


def _final_norm(h, dest, ys, gain, seq):
    b, lp, d = h.shape
    nc = lp // CHUNK
    ns = seq // CHUNK

    def chunk_of(step):
        return (step // ns) * nc + lax.rem(step, ns) + 1

    smem = lambda fn: pl.BlockSpec((1, 1, CHUNK), fn, memory_space=pltpu.SMEM)
    return pl.pallas_call(
        _final_kernel,
        grid=(b, ns),
        in_specs=[
            smem(lambda i, c: (chunk_of(i * ns + c), 0, 0)),
            smem(lambda i, c: (chunk_of(jnp.minimum(i * ns + c + 1, b * ns - 1)), 0, 0)),
            pl.BlockSpec((1, CHUNK, d), lambda i, c: (i, c + 1, 0)),
            pl.BlockSpec((1, d), lambda i, c: (0, 0)),
            pl.BlockSpec(memory_space=pl.ANY),
        ],
        out_specs=pl.BlockSpec((1, CHUNK, d), lambda i, c: (i, c, 0)),
        out_shape=jax.ShapeDtypeStruct((b, seq, d), F32),
        scratch_shapes=[pltpu.VMEM((2, CHUNK, d // LANES, LANES), F32), pltpu.SemaphoreType.DMA((2,))],
        compiler_params=_cparams("arbitrary", "arbitrary"),
        name="final_norm",
    )(dest, dest, h, gain, ys)


def _bucket_experts():
    lo, hi = [], []
    for g in range(MOE_GROUPS):
        for a in range(MOE_EPG):
            for c in range(a + 1, MOE_EPG):
                lo.append(g * MOE_EPG + a)
                hi.append(g * MOE_EPG + c)
    fill = MOE_BUCKET_ROWS - len(lo)
    return jnp.array(lo + [lo[-1]] * fill, I32), jnp.array(hi + [hi[-1]] * fill, I32)


def _moe_layer(h, y_mix, w_out, norm_gain, wg, bg, we, be, w1, w3, w2):
    t, d = h.shape
    n_logits = MOE_GROUPS + MOE_EXPERTS
    wr = jnp.concatenate([wg, we.reshape(d, MOE_EXPERTS), jnp.zeros((d, LANES - n_logits), F32)], axis=1)
    br = jnp.concatenate([bg, be.reshape(-1), jnp.zeros((LANES - n_logits,), F32)])
    wr_hi = wr.astype(BF16)
    wr_split = jnp.concatenate([wr_hi, (wr - wr_hi.astype(F32)).astype(BF16)], axis=1)
    h, logits_t = _out_proj(h, y_mix, w_out.astype(BF16), norm_gain[None, :], wr_split, br[None, :])
    bucket, gate, rank, counts = _route(logits_t)

    counts = counts[:, 0]
    padded = (counts + MOE_BLOCK - 1) // MOE_BLOCK * MOE_BLOCK
    pend = jnp.cumsum(padded)
    pstart = pend - padded
    ids = jnp.arange(MOE_BUCKET_ROWS, dtype=I32)
    dest = rank[0] + jnp.sum(jnp.where(bucket == ids[:, None], pstart[:, None], 0), axis=0)
    n_blocks = -(-t // MOE_BLOCK) + MOE_BUCKETS
    block_start = jnp.arange(n_blocks, dtype=I32) * MOE_BLOCK
    block_bucket = jnp.minimum(jnp.sum((pend[None, :] <= block_start[:, None]).astype(I32), axis=1),
                               MOE_BUCKET_ROWS - 1)
    n_used = (pend[-1:] // MOE_BLOCK).astype(I32)
    lo, hi = _bucket_experts()
    gate8 = jnp.concatenate([gate, jnp.zeros((6, t), F32)], axis=0)

    xs = _dispatch(jnp.concatenate([pend, padded]).astype(I32), dest.reshape(t // TOKEN_TILE, 1, TOKEN_TILE), gate8,
                   h, norm_gain[None, :], n_blocks * MOE_BLOCK)
    ys = _experts(lo[block_bucket], hi[block_bucket], n_used, xs, w1, w3, w2)
    return h, dest, ys


def kernel(x, meta, norm_mix, norm_ffn, norm_final, ret_w_in, ret_w_out, ret_norm, ssm_w_in, ssm_conv_w,
           ssm_conv_b, ssm_dt_bias, ssm_a_log, ssm_d, ssm_norm, ssm_w_out, moe_wg, moe_bg, moe_we, moe_be,
           moe_w1, moe_w3, moe_w2):
    b, seq, d = x.shape
    lp = CHUNK + seq
    t = b * lp
    depth = norm_mix.shape[0]
    h = jnp.concatenate([jnp.zeros((b, PAD, d), F32), jnp.broadcast_to(meta[None], (b, N_META, d)), x], axis=1)

    half = RET_QK_DIM // 2
    inv_freq = 1.0 / (RET_ROPE_BASE ** jnp.linspace(0.0, 1.0, half, dtype=F32))
    ang = (jnp.arange(lp, dtype=F32) - PAD)[:, None] * inv_freq[None, :]
    cos, sin = jnp.cos(ang), jnp.sin(ang)
    perm = jnp.concatenate([jnp.arange(0, RET_QK_DIM, 2), jnp.arange(1, RET_QK_DIM, 2)])
    qk_perm = (jnp.arange(2 * RET_HEADS)[:, None] * RET_QK_DIM + perm[None, :]).reshape(-1)

    h = h.reshape(t, d)
    moe = None
    for i in range(depth):
        j = i // 2
        if moe is not None:
            h, xn = _combine_norm(moe[0].reshape(t // TOKEN_TILE, 1, TOKEN_TILE), moe[1], h, norm_mix[i][None, :])
        if i % 2 == 0:
            w_in = jnp.concatenate([ret_w_in[j][:, qk_perm], ret_w_in[j][:, 2 * RET_QK:]], axis=1).astype(BF16)
            qkvg = _ret_in(h.reshape(b, lp, d), norm_mix[i][None, :], cos, sin, w_in)
            y = _retention(qkvg, ret_norm[j][None, :])
            w_out = ret_w_out[j]
        else:
            w_in = jnp.concatenate([ssm_w_in[j], jnp.zeros((d, LANES - SSM_HEADS), F32)], axis=1).astype(BF16)
            zxbc, dt = _ssm_in(xn, w_in, ssm_conv_w[j], ssm_conv_b[j][None, :], lp)
            lane_pad = lambda v: jnp.concatenate([v, jnp.zeros((LANES - SSM_HEADS,), F32)])[None, :]
            y = _ssd(zxbc.reshape(b, lp, SSM_MAIN), dt.reshape(b, lp, LANES), lane_pad(ssm_dt_bias[j]),
                     lane_pad(ssm_a_log[j]), jnp.repeat(ssm_d[j], SSM_HEAD_DIM)[None, :], ssm_norm[j][None, :])
            w_out = ssm_w_out[j]
        h, *moe = _moe_layer(h, y.reshape(t, -1), w_out, norm_ffn[i], moe_wg[i], moe_bg[i], moe_we[i],
                             moe_be[i], moe_w1[i], moe_w3[i], moe_w2[i])
    return _final_norm(h.reshape(b, lp, d), moe[0].reshape(t // CHUNK, 1, CHUNK), moe[1], norm_final[None, :], seq)
```

```python
import functools
import math

import jax
import jax.numpy as jnp
from jax import lax
from jax.experimental import pallas as pl
from jax.experimental.pallas import tpu as pltpu

F32 = jnp.float32
BF16 = jnp.bfloat16
I32 = jnp.int32

D_MODEL = 1024
N_META = 16
CHUNK = 128
PAD = CHUNK - N_META
EPS = 1e-6

RET_HEADS = 4
RET_QK_DIM = 256
RET_V_DIM = 512
RET_QK = RET_HEADS * RET_QK_DIM
RET_V = RET_HEADS * RET_V_DIM
RET_IN = 2 * RET_QK + 2 * RET_V
RET_ROPE_BASE = 10000.0

SSM_INNER = 2048
SSM_HEAD_DIM = 64
SSM_HEADS = 32
SSM_GROUPS = 8
SSM_HPG = 4
SSM_STATE = 128
SSM_CONV = 4
SSM_BC = SSM_GROUPS * SSM_STATE
SSM_CONV_DIM = SSM_INNER + 2 * SSM_BC
SSM_MAIN = SSM_INNER + SSM_CONV_DIM
LANES = 128
SSM_GROUP_W = SSM_HPG * SSM_HEAD_DIM

MOE_GROUPS = 4
MOE_EPG = 8
MOE_EXPERTS = 32
MOE_FF = 512
MOE_BLOCK = 256
MOE_PAIRS = MOE_EPG * (MOE_EPG - 1) // 2
MOE_BUCKETS = MOE_GROUPS * MOE_PAIRS
MOE_BUCKET_ROWS = 128
ROUTER_ROWS = 40

ROW_DMA_UNROLL = 4
SSM_IN_STEP = 256
SSM_IN_ROWS = 64
LOG2E = math.log2(math.e)
TOKEN_TILE = 512
VMEM_LIMIT = 56 * 1024 * 1024


def _cparams(*sem):
    return pltpu.CompilerParams(dimension_semantics=sem, vmem_limit_bytes=VMEM_LIMIT)


def _rms(x, gain):
    ms = jnp.mean(x * x, axis=-1, keepdims=True)
    return x * lax.rsqrt(ms + EPS) * gain


def _silu(x):
    hx = 0.5 * x
    return hx + hx * jnp.tanh(hx)


def _dot(a, b):
    return jnp.dot(a, b, preferred_element_type=F32)


def _dot_nt(a, b, precision=None):
    return lax.dot_general(a, b, (((1,), (1,)), ((), ())), preferred_element_type=F32, precision=precision)


def _dot_tn(a, b):
    return lax.dot_general(a, b, (((0,), (0,)), ((), ())), preferred_element_type=F32)


def _row_tile(rows, cap=544):
    best = 16
    for t in range(16, cap + 1, 16):
        if rows % t == 0:
            best = t
    return best


def _ret_in_kernel(h_ref, g_ref, cos_ref, sin_ref, w_ref, o_ref):
    _ret_project(h_ref[0], g_ref, cos_ref, sin_ref, w_ref, o_ref)


def _ret_in_first_kernel(meta_ref, g_ref, cos_ref, sin_ref, w_ref, x_ref, ho_ref, o_ref, hbuf, sems):
    nj = pl.num_programs(1)
    step = pl.program_id(0) * nj + pl.program_id(1)
    last = pl.num_programs(0) * nj - 1
    slot = lax.rem(step, 2)
    tm = hbuf.shape[1]

    def rows_copy(s, slot_):
        i, j = s // nj, lax.rem(s, nj)
        first = pltpu.make_async_copy(
            x_ref.at[i, pl.ds(0, tm - CHUNK)], hbuf.at[slot_, pl.ds(CHUNK, tm - CHUNK)], sems.at[slot_])
        start = pl.multiple_of(jnp.maximum(j * tm - CHUNK, 0), 8)
        other = pltpu.make_async_copy(x_ref.at[i, pl.ds(start, tm)], hbuf.at[slot_], sems.at[slot_])
        return j == 0, first, other

    def fetch(s, slot_):
        is_first, first, other = rows_copy(s, slot_)
        pl.when(is_first)(first.start)
        pl.when(jnp.logical_not(is_first))(other.start)

    @pl.when(step == 0)
    def _():
        fetch(step, slot)

    @pl.when(step < last)
    def _():
        fetch(step + 1, 1 - slot)

    is_first, first, other = rows_copy(step, slot)
    pl.when(is_first)(first.wait)
    pl.when(jnp.logical_not(is_first))(other.wait)

    @pl.when(is_first)
    def _():
        hbuf[slot, 0:PAD, :] = jnp.zeros((PAD, hbuf.shape[2]), F32)
        hbuf[slot, PAD:CHUNK, :] = meta_ref[...]

    h = hbuf[slot]
    ho_ref[0] = h
    _ret_project(h, g_ref, cos_ref, sin_ref, w_ref, o_ref)


def _ret_project(h, g_ref, cos_ref, sin_ref, w_ref, o_ref):
    xn = _rms(h, g_ref[...]).astype(BF16)
    cos = cos_ref[...]
    sin = sin_ref[...]
    half = RET_QK_DIM // 2
    for j in range(2 * RET_HEADS):
        c0 = j * RET_QK_DIM
        acc = _dot(xn, w_ref[:, c0:c0 + RET_QK_DIM])
        if j >= RET_HEADS:
            acc = acc * (RET_QK_DIM ** -0.5)
        e = acc[:, :half]
        o = acc[:, half:]
        o_ref[0, :, c0:c0 + half] = (e * cos - o * sin).astype(BF16)
        o_ref[0, :, c0 + half:c0 + RET_QK_DIM] = (o * cos + e * sin).astype(BF16)
    step = 1024
    for c0 in range(2 * RET_QK, RET_IN, step):
        o_ref[0, :, c0:c0 + step] = _dot(xn, w_ref[:, c0:c0 + step]).astype(BF16)


def _ret_in_first(x, meta, gain, cos, sin, w):
    b, seq, d = x.shape
    lp = CHUNK + seq
    tm = _row_tile(lp)
    assert tm > CHUNK
    const = lambda r, c: pl.BlockSpec((r, c), lambda i, j: (0, 0))
    return pl.pallas_call(
        _ret_in_first_kernel,
        grid=(b, lp // tm),
        in_specs=[
            const(N_META, d), const(1, d),
            pl.BlockSpec((tm, RET_QK_DIM // 2), lambda i, j: (j, 0)),
            pl.BlockSpec((tm, RET_QK_DIM // 2), lambda i, j: (j, 0)),
            const(d, RET_IN),
            pl.BlockSpec(memory_space=pl.ANY),
        ],
        out_specs=[pl.BlockSpec((1, tm, d), lambda i, j: (i, j, 0)), pl.BlockSpec((1, tm, RET_IN), lambda i, j: (i, j, 0))],
        out_shape=[jax.ShapeDtypeStruct((b, lp, d), F32), jax.ShapeDtypeStruct((b, lp, RET_IN), BF16)],
        scratch_shapes=[pltpu.VMEM((2, tm, d), F32), pltpu.SemaphoreType.DMA((2,))],
        compiler_params=_cparams("arbitrary", "arbitrary"),
        name="ret_in_proj",
    )(meta, gain, cos, sin, w, x)


def _ret_in(h, gain, cos, sin, w):
    b, lp, d = h.shape
    tm = _row_tile(lp)
    return pl.pallas_call(
        _ret_in_kernel,
        grid=(b, lp // tm),
        in_specs=[
            pl.BlockSpec((1, tm, d), lambda i, j: (i, j, 0)),
            pl.BlockSpec((1, d), lambda i, j: (0, 0)),
            pl.BlockSpec((tm, RET_QK_DIM // 2), lambda i, j: (j, 0)),
            pl.BlockSpec((tm, RET_QK_DIM // 2), lambda i, j: (j, 0)),
            pl.BlockSpec((d, RET_IN), lambda i, j: (0, 0)),
        ],
        out_specs=pl.BlockSpec((1, tm, RET_IN), lambda i, j: (i, j, 0)),
        out_shape=jax.ShapeDtypeStruct((b, lp, RET_IN), BF16),
        compiler_params=_cparams("parallel", "parallel"),
        name="ret_in_proj",
    )(h, gain, cos, sin, w)


def _ret_kernel(q_ref, k_ref, v_ref, g_ref, hn_ref, o_ref, state_ref):
    @pl.when(pl.program_id(1) == 0)
    def _():
        state_ref[...] = jnp.zeros_like(state_ref)

    row = lax.broadcasted_iota(I32, (CHUNK, CHUNK), 0).astype(F32)
    col = lax.broadcasted_iota(I32, (CHUNK, CHUNK), 1).astype(F32)
    diff = row - col
    ridx = row[:, :1]
    for hh in range(RET_HEADS):
        lg = math.log(1.0 - 2.0 ** (-5.0 - hh))
        intra = jnp.where(diff >= 0, jnp.exp(lg * jnp.maximum(diff, 0.0)), 0.0)
        qdec = jnp.exp(lg * (ridx + 1.0))
        kdec = jnp.exp(lg * (CHUNK - 1.0 - ridx))
        cdec = math.exp(lg * CHUNK)
        qh = q_ref[0, :, hh * RET_QK_DIM:(hh + 1) * RET_QK_DIM]
        kh = k_ref[0, :, hh * RET_QK_DIM:(hh + 1) * RET_QK_DIM]
        vh = v_ref[0, :, hh * RET_V_DIM:(hh + 1) * RET_V_DIM]
        scores = _dot_nt(qh, kh) * intra
        st = state_ref[hh]
        y = _dot(scores.astype(BF16), vh) + _dot(qh, st.astype(BF16)) * qdec
        kd = (kh.astype(F32) * kdec).astype(BF16)
        state_ref[hh] = st * cdec + _dot_tn(kd, vh)
        yn = _rms(y, hn_ref[:, hh * RET_V_DIM:(hh + 1) * RET_V_DIM])
        gh = g_ref[0, :, hh * RET_V_DIM:(hh + 1) * RET_V_DIM].astype(F32)
        o_ref[0, :, hh * RET_V_DIM:(hh + 1) * RET_V_DIM] = (_silu(gh) * yn).astype(BF16)


def _retention(qkvg, head_norm):
    b, lp, _ = qkvg.shape
    nc = lp // CHUNK
    return pl.pallas_call(
        _ret_kernel,
        grid=(b, nc),
        in_specs=[
            pl.BlockSpec((1, CHUNK, RET_QK), lambda i, c: (i, c, 0)),
            pl.BlockSpec((1, CHUNK, RET_QK), lambda i, c: (i, c, 1)),
            pl.BlockSpec((1, CHUNK, RET_V), lambda i, c: (i, c, 1)),
            pl.BlockSpec((1, CHUNK, RET_V), lambda i, c: (i, c, 2)),
            pl.BlockSpec((1, RET_V), lambda i, c: (0, 0)),
        ],
        out_specs=pl.BlockSpec((1, CHUNK, RET_V), lambda i, c: (i, c, 0)),
        out_shape=jax.ShapeDtypeStruct((b, lp, RET_V), BF16),
        scratch_shapes=[pltpu.VMEM((RET_HEADS, RET_QK_DIM, RET_V_DIM), F32)],
        compiler_params=_cparams("parallel", "arbitrary"),
        name="retention",
    )(qkvg, qkvg, qkvg, qkvg, head_norm)


def _ssm_in_kernel(xn_ref, w_ref, cw_ref, cb_ref, o_ref, dt_ref, tail_ref, acc_ref, *, rows_per_seq):
    i = pl.program_id(0)
    tm = xn_ref.shape[0]

    @pl.when(i == 0)
    def _():
        tail_ref[...] = jnp.zeros_like(tail_ref)

    step = acc_ref.shape[2]
    n_steps = SSM_MAIN // step
    rb = SSM_IN_ROWS

    pos = lax.rem(i * tm, rows_per_seq) + lax.broadcasted_iota(I32, (tm, 1), 0)
    is_pad = jnp.logical_or(pos < PAD, jnp.logical_and(pos >= rows_per_seq, pos < rows_per_seq + PAD))

    def project(j):
        c0 = j * step
        acc = _dot(xn_ref[...], w_ref[:, c0:c0 + step])
        acc_ref[j % 2, 8:, :] = acc if c0 < SSM_INNER else jnp.where(is_pad, 0.0, acc)

    project(0)
    for j in range(n_steps):
        if j + 1 < n_steps:
            project(j + 1)
        c0 = j * step
        s = j % 2
        if c0 < SSM_INNER:
            for r0 in range(0, tm, rb):
                o_ref[r0:r0 + rb, c0:c0 + step] = _silu(acc_ref[s, 8 + r0:8 + r0 + rb, :]).astype(BF16)
            continue
        cc = c0 - SSM_INNER
        acc_ref[s, 0:8, :] = tail_ref[:, cc:cc + step]
        tail_ref[:, cc:cc + step] = acc_ref[s, tm:tm + 8, :]
        taps = [cw_ref[k:k + 1, cc:cc + step] for k in range(SSM_CONV)]
        bias = cb_ref[:, cc:cc + step]
        for r0 in range(0, tm, rb):
            ext = acc_ref[s, r0:r0 + rb + 8, :]
            conv = bias + taps[SSM_CONV - 1] * ext[8:]
            for k in range(1, SSM_CONV):
                conv = conv + taps[SSM_CONV - 1 - k] * pltpu.roll(ext, k, axis=0)[8:]
            o_ref[r0:r0 + rb, c0:c0 + step] = _silu(conv).astype(BF16)
    dt_ref[...] = _dot(xn_ref[...], w_ref[:, SSM_MAIN:SSM_MAIN + LANES])


def _ssm_in(xn, w, conv_w, conv_b, rows_per_seq):
    t, d = xn.shape
    tm = TOKEN_TILE
    assert tm <= rows_per_seq
    const = lambda r, c: pl.BlockSpec((r, c), lambda i: (0, 0))
    return pl.pallas_call(
        functools.partial(_ssm_in_kernel, rows_per_seq=rows_per_seq),
        grid=(t // tm,),
        in_specs=[
            pl.BlockSpec((tm, d), lambda i: (i, 0)),
            const(d, SSM_MAIN + LANES), const(SSM_CONV, SSM_CONV_DIM), const(1, SSM_CONV_DIM),
        ],
        out_specs=[
            pl.BlockSpec((tm, SSM_MAIN), lambda i: (i, 0)),
            pl.BlockSpec((tm, LANES), lambda i: (i, 0)),
        ],
        out_shape=[jax.ShapeDtypeStruct((t, SSM_MAIN), BF16), jax.ShapeDtypeStruct((t, LANES), F32)],
        scratch_shapes=[pltpu.VMEM((8, SSM_CONV_DIM), F32), pltpu.VMEM((2, tm + 8, SSM_IN_STEP), F32)],
        compiler_params=_cparams("arbitrary"),
        name="ssm_in_proj",
    )(xn, w, conv_w, conv_b)


def _ssd_kernel(zg_ref, x_ref, bc_ref, dt_ref, dtb_ref, alog_ref, dsk_ref, gn_ref, o_ref, state_ref):
    c = pl.program_id(1)

    @pl.when(c == 0)
    def _():
        state_ref[...] = jnp.zeros_like(state_ref)

    row = lax.broadcasted_iota(I32, (CHUNK, CHUNK), 0)
    col = lax.broadcasted_iota(I32, (CHUNK, CHUNK), 1)
    causal_bias = jnp.where(row >= col, 0.0, -jnp.inf)
    low_half = col < SSM_HEAD_DIM
    valid = row[:, :1] >= jnp.where(c > 0, 0, PAD)

    dtr = dt_ref[0] + dtb_ref[...]
    dtv = jnp.maximum(dtr, 0.0) + jnp.log(1.0 + jnp.exp(-jnp.abs(dtr)))
    dtv = jnp.where(valid, dtv, 0.0)
    da = dtv * (-jnp.exp(alog_ref[...]))
    tril = jnp.where(row >= col, 1.0, 0.0)
    acs = jnp.dot(tril, da, preferred_element_type=F32, precision=lax.Precision.HIGHEST)
    last = acs[CHUNK - 1:CHUNK, :]
    acs2 = acs * LOG2E
    src_t = (jnp.log2(dtv) - acs2).T
    w_t = (dtv * jnp.exp(last - acs)).T
    elast = jnp.exp(last)

    def pair_row(v, h0):
        return jnp.where(low_half[:1], v[:, h0:h0 + 1], v[:, h0 + 1:h0 + 2])

    groups = range(SSM_GROUPS)
    gsl = [slice(gi * SSM_GROUP_W, (gi + 1) * SSM_GROUP_W) for gi in groups]
    bgs = [bc_ref[0, :, gi * SSM_STATE:(gi + 1) * SSM_STATE] for gi in groups]
    cgs = [bc_ref[0, :, SSM_BC + gi * SSM_STATE:SSM_BC + (gi + 1) * SSM_STATE] for gi in groups]
    sts = [state_ref[gi] for gi in groups]
    cbs = [_dot_nt(cgs[gi], bgs[gi]) for gi in groups]
    yoffs = [_dot(cgs[gi], sts[gi].astype(BF16)) for gi in groups]
    bts = [bgs[gi].astype(F32).T for gi in groups]

    outs = []
    eexp = []
    for gi in groups:
        for pp in range(SSM_HPG // 2):
            h0 = gi * SSM_HPG + 2 * pp
            top, bot, ecols = [], [], []
            for hd in (h0, h0 + 1):
                acs_b = jnp.broadcast_to(acs2[:, hd:hd + 1], (CHUNK, CHUNK))
                top.append(cbs[gi] * jnp.exp2(acs_b + (src_t[hd:hd + 1, :] + causal_bias)))
                bot.append(bts[gi] * w_t[hd:hd + 1, :])
                ecols.append(jnp.exp2(acs_b))
            lhs = jnp.concatenate([jnp.concatenate(top, axis=1), jnp.concatenate(bot, axis=1)], axis=0)
            xp = x_ref[0, :, h0 * SSM_HEAD_DIM:(h0 + 2) * SSM_HEAD_DIM]
            zero = jnp.zeros_like(xp)
            rhs = jnp.concatenate([jnp.where(low_half, xp, zero), jnp.where(low_half, zero, xp)], axis=0)
            outs.append(_dot(lhs.astype(BF16), rhs))
            eexp.append(jnp.where(low_half, ecols[0], ecols[1]))

    for gi in groups:
        pairs = [gi * (SSM_HPG // 2) + pp for pp in range(SSM_HPG // 2)]
        heads = [gi * SSM_HPG + 2 * pp for pp in range(SSM_HPG // 2)]
        lexp = jnp.concatenate([pair_row(elast, h0) for h0 in heads], axis=1)
        state_ref[gi] = sts[gi] * lexp + jnp.concatenate([outs[p][CHUNK:] for p in pairs], axis=1)
        xg = x_ref[0, :, gsl[gi]].astype(F32)
        y = jnp.concatenate([outs[p][:CHUNK] for p in pairs], axis=1)
        y = y + yoffs[gi] * jnp.concatenate([eexp[p] for p in pairs], axis=1)
        y = y + dsk_ref[:, gsl[gi]] * xg
        y = y * zg_ref[0, :, gsl[gi]].astype(F32)
        o_ref[0, :, gsl[gi]] = _rms(y, gn_ref[:, gsl[gi]]).astype(BF16)


def _ssd(zxbc, dt, dt_bias, a_log, d_skip, gate_norm):
    b, lp, _ = zxbc.shape
    nc = lp // CHUNK
    wide = lambda j: pl.BlockSpec((1, CHUNK, SSM_INNER), lambda i, c: (i, c, j))
    const = lambda r, w: pl.BlockSpec((r, w), lambda i, c: (0, 0))
    return pl.pallas_call(
        _ssd_kernel,
        grid=(b, nc),
        in_specs=[
            wide(0), wide(1), wide(2),
            pl.BlockSpec((1, CHUNK, LANES), lambda i, c: (i, c, 0)),
            const(1, LANES), const(1, LANES), const(1, SSM_INNER), const(1, SSM_INNER),
        ],
        out_specs=pl.BlockSpec((1, CHUNK, SSM_INNER), lambda i, c: (i, c, 0)),
        out_shape=jax.ShapeDtypeStruct((b, lp, SSM_INNER), BF16),
        scratch_shapes=[pltpu.VMEM((SSM_GROUPS, SSM_STATE, SSM_GROUP_W), F32)],
        compiler_params=_cparams("parallel", "arbitrary"),
        name="ssd",
    )(zxbc, zxbc, zxbc, dt, dt_bias, a_log, d_skip, gate_norm)


def _out_proj_kernel(h_ref, y_ref, w_ref, g_ref, wr_ref, br_ref, ho_ref, lg_ref):
    hn = h_ref[...] + _dot(y_ref[...], w_ref[...])
    ho_ref[...] = hn
    u = _rms(hn, g_ref[...])
    u_hi = u.astype(BF16)
    u_lo = (u - u_hi.astype(F32)).astype(BF16)
    p = _dot(u_hi, wr_ref[...])
    logits = p[:, :LANES] + (p[:, LANES:] + _dot(u_lo, wr_ref[:, :LANES])) + br_ref[...]
    lg_ref[...] = logits.T[:ROUTER_ROWS]


def _out_proj(h, y, w, gain, wr_t, br):
    t, d = h.shape
    tm = TOKEN_TILE
    kin = y.shape[1]
    return pl.pallas_call(
        _out_proj_kernel,
        grid=(t // tm,),
        in_specs=[
            pl.BlockSpec((tm, d), lambda i: (i, 0)),
            pl.BlockSpec((tm, kin), lambda i: (i, 0)),
            pl.BlockSpec((kin, d), lambda i: (0, 0)),
            pl.BlockSpec((1, d), lambda i: (0, 0)),
            pl.BlockSpec((d, 2 * LANES), lambda i: (0, 0)),
            pl.BlockSpec((1, LANES), lambda i: (0, 0)),
        ],
        out_specs=[
            pl.BlockSpec((tm, d), lambda i: (i, 0)),
            pl.BlockSpec((ROUTER_ROWS, tm), lambda i: (0, i)),
        ],
        out_shape=[jax.ShapeDtypeStruct((t, d), F32), jax.ShapeDtypeStruct((ROUTER_ROWS, t), F32)],
        compiler_params=_cparams("parallel"),
        name="out_proj_router",
    )(h, y, w, gain, wr_t, br)


def _first_argmax(v, n):
    ridx = lax.broadcasted_iota(I32, v.shape, 0).astype(F32)
    vmax = jnp.max(v, axis=0, keepdims=True)
    idx = jnp.min(jnp.where(v == vmax, ridx, float(n)), axis=0, keepdims=True)
    return vmax, idx.astype(I32)


def _route_kernel(lg_ref, bkt_ref, gate_ref, rank_ref, cnt_ref, carry_ref):
    @pl.when(pl.program_id(0) == 0)
    def _():
        carry_ref[...] = jnp.zeros_like(carry_ref)

    logits = lg_ref[...]
    tm = logits.shape[1]
    gl = logits[0:MOE_GROUPS]
    gmax, gsel = _first_argmax(gl, MOE_GROUPS)
    p_group = 1.0 / jnp.sum(jnp.exp(gl - gmax), axis=0, keepdims=True)
    el = logits[MOE_GROUPS:MOE_GROUPS + MOE_EPG]
    for gg in range(1, MOE_GROUPS):
        el = jnp.where(gsel == gg, logits[MOE_GROUPS + gg * MOE_EPG:MOE_GROUPS + (gg + 1) * MOE_EPG], el)
    ex = jnp.exp(el - jnp.max(el, axis=0, keepdims=True))
    p = ex / jnp.sum(ex, axis=0, keepdims=True)
    p1, i1 = _first_argmax(p, MOE_EPG)
    ridx8 = lax.broadcasted_iota(I32, p.shape, 0)
    p2, i2 = _first_argmax(jnp.where(ridx8 == i1, -1.0, p), MOE_EPG)
    denom = p1 + p2
    gate1 = p_group * p1 / denom
    gate2 = p_group * p2 / denom
    first_low = i1 < i2
    gate_ref[0:1, :] = jnp.where(first_low, gate1, gate2)
    gate_ref[1:2, :] = jnp.where(first_low, gate2, gate1)
    lo = jnp.minimum(i1, i2).astype(F32)
    hi = jnp.maximum(i1, i2).astype(F32)
    pair = lo * (2.0 * MOE_EPG - 1.0 - lo) * 0.5 + (hi - lo - 1.0)
    bucket = gsel * MOE_PAIRS + pair.astype(I32)
    bkt_ref[...] = bucket

    ridx = lax.broadcasted_iota(I32, (MOE_BUCKET_ROWS, tm), 0)
    onehot = jnp.where(ridx == bucket, 1.0, 0.0)
    before = jnp.where(lax.broadcasted_iota(I32, (tm, tm), 0) < lax.broadcasted_iota(I32, (tm, tm), 1), 1.0, 0.0)
    prefix = _dot(onehot.astype(BF16), before.astype(BF16))
    carry = carry_ref[...]
    rank_ref[...] = jnp.sum(onehot * (carry + prefix), axis=0, keepdims=True).astype(I32)
    carry = carry + jnp.sum(onehot, axis=1, keepdims=True)
    carry_ref[...] = carry
    cnt_ref[...] = jnp.broadcast_to(carry, cnt_ref.shape).astype(I32)


def _route(logits_t):
    t = logits_t.shape[1]
    tm = TOKEN_TILE
    rows = lambda r: pl.BlockSpec((r, tm), lambda i: (0, i))
    return pl.pallas_call(
        _route_kernel,
        grid=(t // tm,),
        in_specs=[rows(ROUTER_ROWS)],
        out_specs=[rows(1), rows(2), rows(1), pl.BlockSpec((MOE_BUCKET_ROWS, LANES), lambda i: (0, 0))],
        out_shape=[jax.ShapeDtypeStruct((1, t), I32), jax.ShapeDtypeStruct((2, t), F32),
                   jax.ShapeDtypeStruct((1, t), I32), jax.ShapeDtypeStruct((MOE_BUCKET_ROWS, LANES), I32)],
        scratch_shapes=[pltpu.VMEM((MOE_BUCKET_ROWS, 1), F32)],
        compiler_params=_cparams("arbitrary"),
        name="route_rank",
    )(logits_t)


def _dispatch_kernel(seg_ref, dest_ref, gate_ref, h_ref, g_ref, xs_ref, u_ref, zero_ref, sems):
    i = pl.program_id(0)
    n = pl.num_programs(0)
    slot = lax.rem(i, 2)
    tm, d = h_ref.shape

    def wait_rows(s):
        pltpu.make_async_copy(u_ref.at[s], xs_ref.at[pl.ds(0, tm)], sems.at[s]).wait()

    @pl.when(i == 0)
    def _():
        zero_ref[...] = jnp.zeros_like(zero_ref)

        def fill_block(start):
            fill = pltpu.make_async_copy(
                zero_ref, xs_ref.at[pl.ds(pl.multiple_of(start, MOE_BLOCK), MOE_BLOCK)], sems.at[0])
            fill.start()
            fill.wait()

        def fill_segment(e, carry):
            @pl.when(seg_ref[MOE_BUCKET_ROWS + e] > 0)
            def _():
                fill_block(seg_ref[e] - MOE_BLOCK)
            return carry

        lax.fori_loop(0, MOE_BUCKETS, fill_segment, 0)

        def fill_tail(blk, carry):
            fill_block(blk * MOE_BLOCK)
            return carry

        lax.fori_loop(seg_ref[MOE_BUCKET_ROWS - 1] // MOE_BLOCK, xs_ref.shape[0] // MOE_BLOCK, fill_tail, 0)

    @pl.when(i >= 2)
    def _():
        wait_rows(slot)

    u = _rms(h_ref[...], g_ref[...])
    half = d // 2
    as_bits = lambda v: lax.bitcast_convert_type(v.astype(BF16).astype(F32), jnp.uint32)
    words = (as_bits(u[:, half:]) & jnp.uint32(0xFFFF0000)) | (as_bits(u[:, :half]) >> 16)
    gates = jnp.concatenate([gate_ref[...], jnp.zeros((LANES - gate_ref.shape[0], tm), F32)], axis=0).T
    packed = jnp.concatenate([words, lax.bitcast_convert_type(gates, jnp.uint32),
                              jnp.zeros((tm, half - LANES), jnp.uint32)], axis=1)
    u_ref[slot] = packed.reshape(u_ref.shape[1:])

    def issue(q, carry):
        for k in range(ROW_DMA_UNROLL):
            t = q * ROW_DMA_UNROLL + k
            pltpu.make_async_copy(
                u_ref.at[slot, t], xs_ref.at[dest_ref[0, 0, t]], sems.at[slot]).start(priority=k % 2)
        return carry

    lax.fori_loop(0, tm // ROW_DMA_UNROLL, issue, 0)

    @pl.when(i == n - 1)
    def _():
        wait_rows(slot)

        @pl.when(n >= 2)
        def _():
            wait_rows(1 - slot)


def _dispatch(seg, dest, gate, h, gain, rows):
    t, d = h.shape
    tm = TOKEN_TILE
    sub = d // LANES
    return pl.pallas_call(
        _dispatch_kernel,
        grid_spec=pltpu.PrefetchScalarGridSpec(
            num_scalar_prefetch=1,
            grid=(t // tm,),
            in_specs=[
                pl.BlockSpec((1, 1, tm), lambda i, seg: (i, 0, 0), memory_space=pltpu.SMEM),
                pl.BlockSpec((8, tm), lambda i, seg: (0, i)),
                pl.BlockSpec((tm, d), lambda i, seg: (i, 0)),
                pl.BlockSpec((1, d), lambda i, seg: (0, 0)),
            ],
            out_specs=pl.BlockSpec(memory_space=pl.ANY),
            scratch_shapes=[pltpu.VMEM((2, tm, sub, LANES), jnp.uint32), pltpu.VMEM((MOE_BLOCK, sub, LANES), jnp.uint32),
                            pltpu.SemaphoreType.DMA((2,))],
        ),
        out_shape=jax.ShapeDtypeStruct((rows, sub, LANES), jnp.uint32),
        compiler_params=_cparams("arbitrary"),
        name="moe_dispatch",
    )(seg, dest, gate, h, gain)


def _expert_kernel(ea_ref, eb_ref, nu_ref, xs_ref, w1a_ref, w3a_ref, w2a_ref, w1b_ref, w3b_ref, w2b_ref, ys_ref):
    used = pl.program_id(0) < nu_ref[0]

    @pl.when(used)
    def _():
        rows, sub, lanes = xs_ref.shape
        d = sub * lanes
        packed = xs_ref[...].reshape(rows, d)
        words = packed[:, :d // 2]
        lo = lax.bitcast_convert_type(words << 16, F32).astype(BF16)
        hi = lax.bitcast_convert_type(words & jnp.uint32(0xFFFF0000), F32).astype(BF16)
        x = jnp.concatenate([lo, hi], axis=1)
        gates = lax.bitcast_convert_type(packed[:, d // 2:d // 2 + LANES], F32)

        def ffn(w1_ref, w3_ref, w2_ref):
            hid = _silu(_dot(x, w1_ref[0, 0].astype(BF16))) * _dot(x, w3_ref[0, 0].astype(BF16))
            return _dot(hid.astype(BF16), w2_ref[0, 0].astype(BF16))

        y = ffn(w1a_ref, w3a_ref, w2a_ref) * gates[:, 0:1] + ffn(w1b_ref, w3b_ref, w2b_ref) * gates[:, 1:2]
        ys_ref[...] = y.reshape(rows, sub, lanes)

    @pl.when(jnp.logical_not(used))
    def _():
        ys_ref[...] = jnp.zeros_like(ys_ref)


def _experts(block_ea, block_eb, n_used, xs, w1, w3, w2, layer):
    rows, sub, lanes = xs.shape
    d = sub * lanes
    nb = rows // MOE_BLOCK
    w_in = lambda tbl: pl.BlockSpec((1, 1, d, MOE_FF), lambda i, ea, eb, nu: (layer, (ea, eb)[tbl][i], 0, 0))
    w_out = lambda tbl: pl.BlockSpec((1, 1, MOE_FF, d), lambda i, ea, eb, nu: (layer, (ea, eb)[tbl][i], 0, 0))
    return pl.pallas_call(
        _expert_kernel,
        grid_spec=pltpu.PrefetchScalarGridSpec(
            num_scalar_prefetch=3,
            grid=(nb,),
            in_specs=[
                pl.BlockSpec((MOE_BLOCK, sub, lanes), lambda i, ea, eb, nu: (jnp.minimum(i, nu[0] - 1), 0, 0)),
                w_in(0), w_in(0), w_out(0), w_in(1), w_in(1), w_out(1),
            ],
            out_specs=pl.BlockSpec((MOE_BLOCK, sub, lanes), lambda i, ea, eb, nu: (i, 0, 0)),
        ),
        out_shape=jax.ShapeDtypeStruct((rows, sub, lanes), F32),
        compiler_params=_cparams("arbitrary"),
        name="moe_experts",
    )(block_ea, block_eb, n_used, xs, w1, w3, w2, w1, w3, w2)


def _issue_row_gather(idx_ref, src_ref, dst_ref, sem, rows):
    def issue(q, carry):
        for k in range(ROW_DMA_UNROLL):
            r = q * ROW_DMA_UNROLL + k
            pltpu.make_async_copy(src_ref.at[idx_ref[0, 0, r]], dst_ref.at[r], sem).start(priority=k % 2)
        return carry

    lax.fori_loop(0, rows // ROW_DMA_UNROLL, issue, 0)


def _combine_kernel(dest_ref, next_ref, h_ref, g_ref, ys_ref, ho_ref, xn_ref, ybuf, sems):
    i = pl.program_id(0)
    slot = lax.rem(i, 2)
    tm, d = h_ref.shape

    @pl.when(i == 0)
    def _():
        _issue_row_gather(dest_ref, ys_ref, ybuf.at[slot], sems.at[slot], tm)

    @pl.when(i + 1 < pl.num_programs(0))
    def _():
        _issue_row_gather(next_ref, ys_ref, ybuf.at[1 - slot], sems.at[1 - slot], tm)

    pltpu.make_async_copy(ys_ref.at[pl.ds(0, tm)], ybuf.at[slot], sems.at[slot]).wait()
    h = h_ref[...] + ybuf[slot].reshape(tm, d)
    ho_ref[...] = h
    xn_ref[...] = _rms(h, g_ref[...]).astype(BF16)


def _combine_norm(dest, ys, h, gain):
    n, _, tm = dest.shape
    t, d = h.shape
    smem = lambda fn: pl.BlockSpec((1, 1, tm), fn, memory_space=pltpu.SMEM)
    return pl.pallas_call(
        _combine_kernel,
        grid=(n,),
        in_specs=[
            smem(lambda i: (i, 0, 0)),
            smem(lambda i: (jnp.minimum(i + 1, n - 1), 0, 0)),
            pl.BlockSpec((tm, d), lambda i: (i, 0)),
            pl.BlockSpec((1, d), lambda i: (0, 0)),
            pl.BlockSpec(memory_space=pl.ANY),
        ],
        out_specs=[pl.BlockSpec((tm, d), lambda i: (i, 0)), pl.BlockSpec((tm, d), lambda i: (i, 0))],
        out_shape=[jax.ShapeDtypeStruct((t, d), F32), jax.ShapeDtypeStruct((t, d), BF16)],
        scratch_shapes=[pltpu.VMEM((2, tm, d // LANES, LANES), F32), pltpu.SemaphoreType.DMA((2,))],
        compiler_params=_cparams("arbitrary"),
        name="moe_combine",
    )(dest, dest, h, gain, ys)


def _final_kernel(dest_ref, next_ref, h_ref, g_ref, ys_ref, o_ref, ybuf, sems):
    step = pl.program_id(0) * pl.num_programs(1) + pl.program_id(1)
    last = pl.num_programs(0) * pl.num_programs(1) - 1
    slot = lax.rem(step, 2)
    rows, d = h_ref.shape[1:]

    @pl.when(step == 0)
    def _():
        _issue_row_gather(dest_ref, ys_ref, ybuf.at[slot], sems.at[slot], rows)

    @pl.when(step < last)
    def _():
        _issue_row_gather(next_ref, ys_ref, ybuf.at[1 - slot], sems.at[1 - slot], rows)

    pltpu.make_async_copy(ys_ref.at[pl.ds(0, rows)], ybuf.at[slot], sems.at[slot]).wait()
    o_ref[0] = _rms(h_ref[0] + ybuf[slot].reshape(rows, d), g_ref[...])


def _final_norm(h, dest, ys, gain, seq):
    b, lp, d = h.shape
    nc = lp // CHUNK
    ns = seq // CHUNK

    def chunk_of(step):
        return (step // ns) * nc + lax.rem(step, ns) + 1

    smem = lambda fn: pl.BlockSpec((1, 1, CHUNK), fn, memory_space=pltpu.SMEM)
    return pl.pallas_call(
        _final_kernel,
        grid=(b, ns),
        in_specs=[
            smem(lambda i, c: (chunk_of(i * ns + c), 0, 0)),
            smem(lambda i, c: (chunk_of(jnp.minimum(i * ns + c + 1, b * ns - 1)), 0, 0)),
            pl.BlockSpec((1, CHUNK, d), lambda i, c: (i, c + 1, 0)),
            pl.BlockSpec((1, d), lambda i, c: (0, 0)),
            pl.BlockSpec(memory_space=pl.ANY),
        ],
        out_specs=pl.BlockSpec((1, CHUNK, d), lambda i, c: (i, c, 0)),
        out_shape=jax.ShapeDtypeStruct((b, seq, d), F32),
        scratch_shapes=[pltpu.VMEM((2, CHUNK, d // LANES, LANES), F32), pltpu.SemaphoreType.DMA((2,))],
        compiler_params=_cparams("arbitrary", "arbitrary"),
        name="final_norm",
    )(dest, dest, h, gain, ys)


def _bucket_experts():
    lo, hi = [], []
    for g in range(MOE_GROUPS):
        for a in range(MOE_EPG):
            for c in range(a + 1, MOE_EPG):
                lo.append(g * MOE_EPG + a)
                hi.append(g * MOE_EPG + c)
    fill = MOE_BUCKET_ROWS - len(lo)
    return jnp.array(lo + [lo[-1]] * fill, I32), jnp.array(hi + [hi[-1]] * fill, I32)


def _moe_layer(h, y_mix, w_out, norm_gain, wg, bg, we, be, w1, w3, w2, layer):
    t, d = h.shape
    n_logits = MOE_GROUPS + MOE_EXPERTS
    wr = jnp.concatenate([wg, we.reshape(d, MOE_EXPERTS), jnp.zeros((d, LANES - n_logits), F32)], axis=1)
    br = jnp.concatenate([bg, be.reshape(-1), jnp.zeros((LANES - n_logits,), F32)])
    wr_hi = wr.astype(BF16)
    wr_split = jnp.concatenate([wr_hi, (wr - wr_hi.astype(F32)).astype(BF16)], axis=1)
    h, logits_t = _out_proj(h, y_mix, w_out.astype(BF16), norm_gain[None, :], wr_split, br[None, :])
    bucket, gate, rank, counts = _route(logits_t)

    counts = counts[:, 0]
    padded = (counts + MOE_BLOCK - 1) // MOE_BLOCK * MOE_BLOCK
    pend = jnp.cumsum(padded)
    pstart = pend - padded
    ids = jnp.arange(MOE_BUCKET_ROWS, dtype=I32)
    dest = rank[0] + jnp.sum(jnp.where(bucket == ids[:, None], pstart[:, None], 0), axis=0)
    n_blocks = -(-t // MOE_BLOCK) + MOE_BUCKETS
    block_start = jnp.arange(n_blocks, dtype=I32) * MOE_BLOCK
    block_bucket = jnp.minimum(jnp.sum((pend[None, :] <= block_start[:, None]).astype(I32), axis=1),
                               MOE_BUCKET_ROWS - 1)
    n_used = (pend[-1:] // MOE_BLOCK).astype(I32)
    lo, hi = _bucket_experts()
    gate8 = jnp.concatenate([gate, jnp.zeros((6, t), F32)], axis=0)

    xs = _dispatch(jnp.concatenate([pend, padded]).astype(I32), dest.reshape(t // TOKEN_TILE, 1, TOKEN_TILE), gate8,
                   h, norm_gain[None, :], n_blocks * MOE_BLOCK)
    ys = _experts(lo[block_bucket], hi[block_bucket], n_used, xs, w1, w3, w2, layer)
    return h, dest, ys


def kernel(x, meta, norm_mix, norm_ffn, norm_final, ret_w_in, ret_w_out, ret_norm, ssm_w_in, ssm_conv_w,
           ssm_conv_b, ssm_dt_bias, ssm_a_log, ssm_d, ssm_norm, ssm_w_out, moe_wg, moe_bg, moe_we, moe_be,
           moe_w1, moe_w3, moe_w2):
    b, seq, d = x.shape
    lp = CHUNK + seq
    t = b * lp
    depth = norm_mix.shape[0]

    half = RET_QK_DIM // 2
    inv_freq = 1.0 / (RET_ROPE_BASE ** jnp.linspace(0.0, 1.0, half, dtype=F32))
    ang = (jnp.arange(lp, dtype=F32) - PAD)[:, None] * inv_freq[None, :]
    cos, sin = jnp.cos(ang), jnp.sin(ang)
    perm = jnp.concatenate([jnp.arange(0, RET_QK_DIM, 2), jnp.arange(1, RET_QK_DIM, 2)])
    qk_perm = (jnp.arange(2 * RET_HEADS)[:, None] * RET_QK_DIM + perm[None, :]).reshape(-1)

    h = None
    moe = None
    for i in range(depth):
        j = i // 2
        if moe is not None:
            h, xn = _combine_norm(moe[0].reshape(t // TOKEN_TILE, 1, TOKEN_TILE), moe[1], h, norm_mix[i][None, :])
        if i % 2 == 0:
            w_in = jnp.concatenate([ret_w_in[j][:, qk_perm], ret_w_in[j][:, 2 * RET_QK:]], axis=1).astype(BF16)
            if i == 0:
                h, qkvg = _ret_in_first(x, meta, norm_mix[i][None, :], cos, sin, w_in)
                h = h.reshape(t, d)
            else:
                qkvg = _ret_in(h.reshape(b, lp, d), norm_mix[i][None, :], cos, sin, w_in)
            y = _retention(qkvg, ret_norm[j][None, :])
            w_out = ret_w_out[j]
        else:
            w_in = jnp.concatenate([ssm_w_in[j], jnp.zeros((d, LANES - SSM_HEADS), F32)], axis=1).astype(BF16)
            zxbc, dt = _ssm_in(xn, w_in, ssm_conv_w[j], ssm_conv_b[j][None, :], lp)
            lane_pad = lambda v: jnp.concatenate([v, jnp.zeros((LANES - SSM_HEADS,), F32)])[None, :]
            y = _ssd(zxbc.reshape(b, lp, SSM_MAIN), dt.reshape(b, lp, LANES), lane_pad(ssm_dt_bias[j]),
                     lane_pad(ssm_a_log[j]), jnp.repeat(ssm_d[j], SSM_HEAD_DIM)[None, :], ssm_norm[j][None, :])
            w_out = ssm_w_out[j]
        h, *moe = _moe_layer(h, y.reshape(t, -1), w_out, norm_ffn[i], moe_wg[i], moe_bg[i], moe_we[i],
                             moe_be[i], moe_w1, moe_w3, moe_w2, i)
    return _final_norm(h.reshape(b, lp, d), moe[0].reshape(t // CHUNK, 1, CHUNK), moe[1], norm_final[None, :], seq)
```

```python
import functools
import math

import jax
import jax.numpy as jnp
from jax import lax
from jax.experimental import pallas as pl
from jax.experimental.pallas import tpu as pltpu

F32 = jnp.float32
BF16 = jnp.bfloat16
I32 = jnp.int32

D_MODEL = 1024
N_META = 16
CHUNK = 128
PAD = CHUNK - N_META
EPS = 1e-6

RET_HEADS = 4
RET_QK_DIM = 256
RET_V_DIM = 512
RET_QK = RET_HEADS * RET_QK_DIM
RET_V = RET_HEADS * RET_V_DIM
RET_IN = 2 * RET_QK + 2 * RET_V
RET_ROPE_BASE = 10000.0

SSM_INNER = 2048
SSM_HEAD_DIM = 64
SSM_HEADS = 32
SSM_GROUPS = 8
SSM_HPG = 4
SSM_STATE = 128
SSM_CONV = 4
SSM_BC = SSM_GROUPS * SSM_STATE
SSM_CONV_DIM = SSM_INNER + 2 * SSM_BC
SSM_MAIN = SSM_INNER + SSM_CONV_DIM
LANES = 128
SSM_GROUP_W = SSM_HPG * SSM_HEAD_DIM

MOE_GROUPS = 4
MOE_EPG = 8
MOE_EXPERTS = 32
MOE_FF = 512
MOE_BLOCK = 256
MOE_PAIRS = MOE_EPG * (MOE_EPG - 1) // 2
MOE_BUCKETS = MOE_GROUPS * MOE_PAIRS
MOE_BUCKET_ROWS = 128
ROUTER_ROWS = 40

ROW_DMA_UNROLL = 4
SSM_IN_STEP = 256
SSM_IN_ROWS = 64
LOG2E = math.log2(math.e)
TOKEN_TILE = 512
VMEM_LIMIT = 56 * 1024 * 1024


def _cparams(*sem):
    return pltpu.CompilerParams(dimension_semantics=sem, vmem_limit_bytes=VMEM_LIMIT)


def _rms(x, gain):
    ms = jnp.mean(x * x, axis=-1, keepdims=True)
    return x * lax.rsqrt(ms + EPS) * gain


def _silu(x):
    hx = 0.5 * x
    return hx + hx * jnp.tanh(hx)


def _dot(a, b):
    return jnp.dot(a, b, preferred_element_type=F32)


def _dot_nt(a, b, precision=None):
    return lax.dot_general(a, b, (((1,), (1,)), ((), ())), preferred_element_type=F32, precision=precision)


def _dot_tn(a, b):
    return lax.dot_general(a, b, (((0,), (0,)), ((), ())), preferred_element_type=F32)


def _row_tile(rows, cap=544):
    best = 16
    for t in range(16, cap + 1, 16):
        if rows % t == 0:
            best = t
    return best


def _ret_in_kernel(h_ref, g_ref, cos_ref, sin_ref, w_ref, o_ref):
    _ret_project(h_ref[0], g_ref, cos_ref, sin_ref, w_ref, o_ref)


def _ret_in_first_kernel(meta_ref, g_ref, cos_ref, sin_ref, w_ref, x_ref, ho_ref, o_ref, hbuf, sems):
    nj = pl.num_programs(1)
    step = pl.program_id(0) * nj + pl.program_id(1)
    last = pl.num_programs(0) * nj - 1
    slot = lax.rem(step, 2)
    tm = hbuf.shape[1]

    def rows_copy(s, slot_):
        i, j = s // nj, lax.rem(s, nj)
        first = pltpu.make_async_copy(
            x_ref.at[i, pl.ds(0, tm - CHUNK)], hbuf.at[slot_, pl.ds(CHUNK, tm - CHUNK)], sems.at[slot_])
        start = pl.multiple_of(jnp.maximum(j * tm - CHUNK, 0), 8)
        other = pltpu.make_async_copy(x_ref.at[i, pl.ds(start, tm)], hbuf.at[slot_], sems.at[slot_])
        return j == 0, first, other

    def fetch(s, slot_):
        is_first, first, other = rows_copy(s, slot_)
        pl.when(is_first)(first.start)
        pl.when(jnp.logical_not(is_first))(other.start)

    @pl.when(step == 0)
    def _():
        fetch(step, slot)

    @pl.when(step < last)
    def _():
        fetch(step + 1, 1 - slot)

    is_first, first, other = rows_copy(step, slot)
    pl.when(is_first)(first.wait)
    pl.when(jnp.logical_not(is_first))(other.wait)

    @pl.when(is_first)
    def _():
        hbuf[slot, 0:PAD, :] = jnp.zeros((PAD, hbuf.shape[2]), F32)
        hbuf[slot, PAD:CHUNK, :] = meta_ref[...]

    h = hbuf[slot]
    ho_ref[0] = h
    _ret_project(h, g_ref, cos_ref, sin_ref, w_ref, o_ref)


def _ret_project(h, g_ref, cos_ref, sin_ref, w_ref, o_ref):
    xn = _rms(h, g_ref[...]).astype(BF16)
    cos = cos_ref[...]
    sin = sin_ref[...]
    half = RET_QK_DIM // 2
    for j in range(2 * RET_HEADS):
        c0 = j * RET_QK_DIM
        acc = _dot(xn, w_ref[:, c0:c0 + RET_QK_DIM])
        if j >= RET_HEADS:
            acc = acc * (RET_QK_DIM ** -0.5)
        e = acc[:, :half]
        o = acc[:, half:]
        o_ref[0, :, c0:c0 + half] = (e * cos - o * sin).astype(BF16)
        o_ref[0, :, c0 + half:c0 + RET_QK_DIM] = (o * cos + e * sin).astype(BF16)
    step = 1024
    for c0 in range(2 * RET_QK, RET_IN, step):
        o_ref[0, :, c0:c0 + step] = _dot(xn, w_ref[:, c0:c0 + step]).astype(BF16)


def _ret_in_first(x, meta, gain, cos, sin, w):
    b, seq, d = x.shape
    lp = CHUNK + seq
    tm = _row_tile(lp)
    assert tm > CHUNK
    const = lambda r, c: pl.BlockSpec((r, c), lambda i, j: (0, 0))
    return pl.pallas_call(
        _ret_in_first_kernel,
        grid=(b, lp // tm),
        in_specs=[
            const(N_META, d), const(1, d),
            pl.BlockSpec((tm, RET_QK_DIM // 2), lambda i, j: (j, 0)),
            pl.BlockSpec((tm, RET_QK_DIM // 2), lambda i, j: (j, 0)),
            const(d, RET_IN),
            pl.BlockSpec(memory_space=pl.ANY),
        ],
        out_specs=[pl.BlockSpec((1, tm, d), lambda i, j: (i, j, 0)), pl.BlockSpec((1, tm, RET_IN), lambda i, j: (i, j, 0))],
        out_shape=[jax.ShapeDtypeStruct((b, lp, d), F32), jax.ShapeDtypeStruct((b, lp, RET_IN), BF16)],
        scratch_shapes=[pltpu.VMEM((2, tm, d), F32), pltpu.SemaphoreType.DMA((2,))],
        compiler_params=_cparams("arbitrary", "arbitrary"),
        name="ret_in_proj",
    )(meta, gain, cos, sin, w, x)


def _ret_in(h, gain, cos, sin, w):
    b, lp, d = h.shape
    tm = _row_tile(lp)
    return pl.pallas_call(
        _ret_in_kernel,
        grid=(b, lp // tm),
        in_specs=[
            pl.BlockSpec((1, tm, d), lambda i, j: (i, j, 0)),
            pl.BlockSpec((1, d), lambda i, j: (0, 0)),
            pl.BlockSpec((tm, RET_QK_DIM // 2), lambda i, j: (j, 0)),
            pl.BlockSpec((tm, RET_QK_DIM // 2), lambda i, j: (j, 0)),
            pl.BlockSpec((d, RET_IN), lambda i, j: (0, 0)),
        ],
        out_specs=pl.BlockSpec((1, tm, RET_IN), lambda i, j: (i, j, 0)),
        out_shape=jax.ShapeDtypeStruct((b, lp, RET_IN), BF16),
        compiler_params=_cparams("parallel", "parallel"),
        name="ret_in_proj",
    )(h, gain, cos, sin, w)


def _ret_kernel(q_ref, k_ref, v_ref, g_ref, hn_ref, o_ref, state_ref):
    @pl.when(pl.program_id(1) == 0)
    def _():
        state_ref[...] = jnp.zeros_like(state_ref)

    row = lax.broadcasted_iota(I32, (CHUNK, CHUNK), 0).astype(F32)
    col = lax.broadcasted_iota(I32, (CHUNK, CHUNK), 1).astype(F32)
    diff = row - col
    ridx = row[:, :1]
    for hh in range(RET_HEADS):
        lg = math.log(1.0 - 2.0 ** (-5.0 - hh))
        intra = jnp.where(diff >= 0, jnp.exp(lg * jnp.maximum(diff, 0.0)), 0.0)
        qdec = jnp.exp(lg * (ridx + 1.0))
        kdec = jnp.exp(lg * (CHUNK - 1.0 - ridx))
        cdec = math.exp(lg * CHUNK)
        qh = q_ref[0, :, hh * RET_QK_DIM:(hh + 1) * RET_QK_DIM]
        kh = k_ref[0, :, hh * RET_QK_DIM:(hh + 1) * RET_QK_DIM]
        vh = v_ref[0, :, hh * RET_V_DIM:(hh + 1) * RET_V_DIM]
        scores = _dot_nt(qh, kh) * intra
        st = state_ref[hh]
        y = _dot(scores.astype(BF16), vh) + _dot(qh, st.astype(BF16)) * qdec
        kd = (kh.astype(F32) * kdec).astype(BF16)
        state_ref[hh] = st * cdec + _dot_tn(kd, vh)
        yn = _rms(y, hn_ref[:, hh * RET_V_DIM:(hh + 1) * RET_V_DIM])
        gh = g_ref[0, :, hh * RET_V_DIM:(hh + 1) * RET_V_DIM].astype(F32)
        o_ref[0, :, hh * RET_V_DIM:(hh + 1) * RET_V_DIM] = (_silu(gh) * yn).astype(BF16)


def _retention(qkvg, head_norm):
    b, lp, _ = qkvg.shape
    nc = lp // CHUNK
    return pl.pallas_call(
        _ret_kernel,
        grid=(b, nc),
        in_specs=[
            pl.BlockSpec((1, CHUNK, RET_QK), lambda i, c: (i, c, 0)),
            pl.BlockSpec((1, CHUNK, RET_QK), lambda i, c: (i, c, 1)),
            pl.BlockSpec((1, CHUNK, RET_V), lambda i, c: (i, c, 1)),
            pl.BlockSpec((1, CHUNK, RET_V), lambda i, c: (i, c, 2)),
            pl.BlockSpec((1, RET_V), lambda i, c: (0, 0)),
        ],
        out_specs=pl.BlockSpec((1, CHUNK, RET_V), lambda i, c: (i, c, 0)),
        out_shape=jax.ShapeDtypeStruct((b, lp, RET_V), BF16),
        scratch_shapes=[pltpu.VMEM((RET_HEADS, RET_QK_DIM, RET_V_DIM), F32)],
        compiler_params=_cparams("parallel", "arbitrary"),
        name="retention",
    )(qkvg, qkvg, qkvg, qkvg, head_norm)


def _ssm_in_kernel(xn_ref, w_ref, cw_ref, cb_ref, o_ref, dt_ref, tail_ref, acc_ref, *, rows_per_seq):
    i = pl.program_id(0)
    tm = xn_ref.shape[0]

    @pl.when(i == 0)
    def _():
        tail_ref[...] = jnp.zeros_like(tail_ref)

    step = acc_ref.shape[2]
    n_steps = SSM_MAIN // step
    rb = SSM_IN_ROWS

    pos = lax.rem(i * tm, rows_per_seq) + lax.broadcasted_iota(I32, (tm, 1), 0)
    is_pad = jnp.logical_or(pos < PAD, jnp.logical_and(pos >= rows_per_seq, pos < rows_per_seq + PAD))

    def project(j):
        c0 = j * step
        acc = _dot(xn_ref[...], w_ref[:, c0:c0 + step])
        acc_ref[j % 2, 8:, :] = acc if c0 < SSM_INNER else jnp.where(is_pad, 0.0, acc)

    project(0)
    for j in range(n_steps):
        if j + 1 < n_steps:
            project(j + 1)
        c0 = j * step
        s = j % 2
        if c0 < SSM_INNER:
            for r0 in range(0, tm, rb):
                o_ref[r0:r0 + rb, c0:c0 + step] = _silu(acc_ref[s, 8 + r0:8 + r0 + rb, :]).astype(BF16)
            continue
        cc = c0 - SSM_INNER
        acc_ref[s, 0:8, :] = tail_ref[:, cc:cc + step]
        tail_ref[:, cc:cc + step] = acc_ref[s, tm:tm + 8, :]
        taps = [cw_ref[k:k + 1, cc:cc + step] for k in range(SSM_CONV)]
        bias = cb_ref[:, cc:cc + step]
        for r0 in range(0, tm, rb):
            ext = acc_ref[s, r0:r0 + rb + 8, :]
            back1 = pltpu.roll(ext, 1, axis=0)
            back23 = pltpu.roll(taps[1] * ext + taps[0] * back1, 2, axis=0)
            conv = bias + taps[3] * ext + taps[2] * back1 + back23
            o_ref[r0:r0 + rb, c0:c0 + step] = _silu(conv[8:]).astype(BF16)
    dt_ref[...] = _dot(xn_ref[...], w_ref[:, SSM_MAIN:SSM_MAIN + LANES])


def _ssm_in(xn, w, conv_w, conv_b, rows_per_seq):
    t, d = xn.shape
    tm = TOKEN_TILE
    assert tm <= rows_per_seq
    const = lambda r, c: pl.BlockSpec((r, c), lambda i: (0, 0))
    return pl.pallas_call(
        functools.partial(_ssm_in_kernel, rows_per_seq=rows_per_seq),
        grid=(t // tm,),
        in_specs=[
            pl.BlockSpec((tm, d), lambda i: (i, 0)),
            const(d, SSM_MAIN + LANES), const(SSM_CONV, SSM_CONV_DIM), const(1, SSM_CONV_DIM),
        ],
        out_specs=[
            pl.BlockSpec((tm, SSM_MAIN), lambda i: (i, 0)),
            pl.BlockSpec((tm, LANES), lambda i: (i, 0)),
        ],
        out_shape=[jax.ShapeDtypeStruct((t, SSM_MAIN), BF16), jax.ShapeDtypeStruct((t, LANES), F32)],
        scratch_shapes=[pltpu.VMEM((8, SSM_CONV_DIM), F32), pltpu.VMEM((2, tm + 8, SSM_IN_STEP), F32)],
        compiler_params=_cparams("arbitrary"),
        name="ssm_in_proj",
    )(xn, w, conv_w, conv_b)


def _ssd_kernel(zg_ref, x_ref, bc_ref, dt_ref, dtb_ref, alog_ref, dsk_ref, gn_ref, o_ref, state_ref):
    c = pl.program_id(1)

    @pl.when(c == 0)
    def _():
        state_ref[...] = jnp.zeros_like(state_ref)

    row = lax.broadcasted_iota(I32, (CHUNK, CHUNK), 0)
    col = lax.broadcasted_iota(I32, (CHUNK, CHUNK), 1)
    causal_bias = jnp.where(row >= col, 0.0, -jnp.inf)
    low_half = col < SSM_HEAD_DIM
    valid = row[:, :1] >= jnp.where(c > 0, 0, PAD)

    dtr = dt_ref[0] + dtb_ref[...]
    dtv = jnp.maximum(dtr, 0.0) + jnp.log(1.0 + jnp.exp(-jnp.abs(dtr)))
    dtv = jnp.where(valid, dtv, 0.0)
    da = dtv * (-jnp.exp(alog_ref[...]))
    tril = jnp.where(row >= col, 1.0, 0.0)
    acs = jnp.dot(tril, da, preferred_element_type=F32, precision=lax.Precision.HIGHEST)
    last = acs[CHUNK - 1:CHUNK, :]
    acs2 = acs * LOG2E
    src_t = (jnp.log2(dtv) - acs2).T
    w_t = (dtv * jnp.exp(last - acs)).T
    elast = jnp.exp(last)

    def pair_row(v, h0):
        return jnp.where(low_half[:1], v[:, h0:h0 + 1], v[:, h0 + 1:h0 + 2])

    groups = range(SSM_GROUPS)
    gsl = [slice(gi * SSM_GROUP_W, (gi + 1) * SSM_GROUP_W) for gi in groups]
    bgs = [bc_ref[0, :, gi * SSM_STATE:(gi + 1) * SSM_STATE] for gi in groups]
    cgs = [bc_ref[0, :, SSM_BC + gi * SSM_STATE:SSM_BC + (gi + 1) * SSM_STATE] for gi in groups]
    sts = [state_ref[gi] for gi in groups]
    cbs = [_dot_nt(cgs[gi], bgs[gi]) for gi in groups]
    yoffs = [_dot(cgs[gi], sts[gi].astype(BF16)) for gi in groups]
    bts = [bgs[gi].astype(F32).T for gi in groups]

    outs = []
    eexp = []
    for gi in groups:
        for pp in range(SSM_HPG // 2):
            h0 = gi * SSM_HPG + 2 * pp
            top, bot, ecols = [], [], []
            for hd in (h0, h0 + 1):
                acs_b = jnp.broadcast_to(acs2[:, hd:hd + 1], (CHUNK, CHUNK))
                top.append(cbs[gi] * jnp.exp2(acs_b + (src_t[hd:hd + 1, :] + causal_bias)))
                bot.append(bts[gi] * w_t[hd:hd + 1, :])
                ecols.append(jnp.exp2(acs_b))
            lhs = jnp.concatenate([jnp.concatenate(top, axis=1), jnp.concatenate(bot, axis=1)], axis=0)
            xp = x_ref[0, :, h0 * SSM_HEAD_DIM:(h0 + 2) * SSM_HEAD_DIM]
            zero = jnp.zeros_like(xp)
            rhs = jnp.concatenate([jnp.where(low_half, xp, zero), jnp.where(low_half, zero, xp)], axis=0)
            outs.append(_dot(lhs.astype(BF16), rhs))
            eexp.append(jnp.where(low_half, ecols[0], ecols[1]))

    for gi in groups:
        pairs = [gi * (SSM_HPG // 2) + pp for pp in range(SSM_HPG // 2)]
        heads = [gi * SSM_HPG + 2 * pp for pp in range(SSM_HPG // 2)]
        lexp = jnp.concatenate([pair_row(elast, h0) for h0 in heads], axis=1)
        state_ref[gi] = sts[gi] * lexp + jnp.concatenate([outs[p][CHUNK:] for p in pairs], axis=1)
        xg = x_ref[0, :, gsl[gi]].astype(F32)
        y = jnp.concatenate([outs[p][:CHUNK] for p in pairs], axis=1)
        y = y + yoffs[gi] * jnp.concatenate([eexp[p] for p in pairs], axis=1)
        y = y + dsk_ref[:, gsl[gi]] * xg
        y = y * zg_ref[0, :, gsl[gi]].astype(F32)
        o_ref[0, :, gsl[gi]] = _rms(y, gn_ref[:, gsl[gi]]).astype(BF16)


def _ssd(zxbc, dt, dt_bias, a_log, d_skip, gate_norm):
    b, lp, _ = zxbc.shape
    nc = lp // CHUNK
    wide = lambda j: pl.BlockSpec((1, CHUNK, SSM_INNER), lambda i, c: (i, c, j))
    const = lambda r, w: pl.BlockSpec((r, w), lambda i, c: (0, 0))
    return pl.pallas_call(
        _ssd_kernel,
        grid=(b, nc),
        in_specs=[
            wide(0), wide(1), wide(2),
            pl.BlockSpec((1, CHUNK, LANES), lambda i, c: (i, c, 0)),
            const(1, LANES), const(1, LANES), const(1, SSM_INNER), const(1, SSM_INNER),
        ],
        out_specs=pl.BlockSpec((1, CHUNK, SSM_INNER), lambda i, c: (i, c, 0)),
        out_shape=jax.ShapeDtypeStruct((b, lp, SSM_INNER), BF16),
        scratch_shapes=[pltpu.VMEM((SSM_GROUPS, SSM_STATE, SSM_GROUP_W), F32)],
        compiler_params=_cparams("parallel", "arbitrary"),
        name="ssd",
    )(zxbc, zxbc, zxbc, dt, dt_bias, a_log, d_skip, gate_norm)


def _out_proj_kernel(h_ref, y_ref, w_ref, g_ref, wr_ref, br_ref, ho_ref, lg_ref):
    hn = h_ref[...] + _dot(y_ref[...], w_ref[...])
    ho_ref[...] = hn
    u = _rms(hn, g_ref[...])
    u_hi = u.astype(BF16)
    u_lo = (u - u_hi.astype(F32)).astype(BF16)
    p = _dot(u_hi, wr_ref[...])
    logits = p[:, :LANES] + (p[:, LANES:] + _dot(u_lo, wr_ref[:, :LANES])) + br_ref[...]
    lg_ref[...] = logits.T[:ROUTER_ROWS]


def _out_proj(h, y, w, gain, wr_t, br):
    t, d = h.shape
    tm = TOKEN_TILE
    kin = y.shape[1]
    return pl.pallas_call(
        _out_proj_kernel,
        grid=(t // tm,),
        in_specs=[
            pl.BlockSpec((tm, d), lambda i: (i, 0)),
            pl.BlockSpec((tm, kin), lambda i: (i, 0)),
            pl.BlockSpec((kin, d), lambda i: (0, 0)),
            pl.BlockSpec((1, d), lambda i: (0, 0)),
            pl.BlockSpec((d, 2 * LANES), lambda i: (0, 0)),
            pl.BlockSpec((1, LANES), lambda i: (0, 0)),
        ],
        out_specs=[
            pl.BlockSpec((tm, d), lambda i: (i, 0)),
            pl.BlockSpec((ROUTER_ROWS, tm), lambda i: (0, i)),
        ],
        out_shape=[jax.ShapeDtypeStruct((t, d), F32), jax.ShapeDtypeStruct((ROUTER_ROWS, t), F32)],
        compiler_params=_cparams("parallel"),
        name="out_proj_router",
    )(h, y, w, gain, wr_t, br)


def _first_argmax(v, n):
    ridx = lax.broadcasted_iota(I32, v.shape, 0).astype(F32)
    vmax = jnp.max(v, axis=0, keepdims=True)
    idx = jnp.min(jnp.where(v == vmax, ridx, float(n)), axis=0, keepdims=True)
    return vmax, idx.astype(I32)


def _route_kernel(lg_ref, bkt_ref, gate_ref, rank_ref, cnt_ref, carry_ref):
    @pl.when(pl.program_id(0) == 0)
    def _():
        carry_ref[...] = jnp.zeros_like(carry_ref)

    logits = lg_ref[...]
    tm = logits.shape[1]
    gl = logits[0:MOE_GROUPS]
    gmax, gsel = _first_argmax(gl, MOE_GROUPS)
    p_group = 1.0 / jnp.sum(jnp.exp(gl - gmax), axis=0, keepdims=True)
    el = logits[MOE_GROUPS:MOE_GROUPS + MOE_EPG]
    for gg in range(1, MOE_GROUPS):
        el = jnp.where(gsel == gg, logits[MOE_GROUPS + gg * MOE_EPG:MOE_GROUPS + (gg + 1) * MOE_EPG], el)
    ex = jnp.exp(el - jnp.max(el, axis=0, keepdims=True))
    p = ex / jnp.sum(ex, axis=0, keepdims=True)
    p1, i1 = _first_argmax(p, MOE_EPG)
    ridx8 = lax.broadcasted_iota(I32, p.shape, 0)
    p2, i2 = _first_argmax(jnp.where(ridx8 == i1, -1.0, p), MOE_EPG)
    denom = p1 + p2
    gate1 = p_group * p1 / denom
    gate2 = p_group * p2 / denom
    first_low = i1 < i2
    gate_ref[0:1, :] = jnp.where(first_low, gate1, gate2)
    gate_ref[1:2, :] = jnp.where(first_low, gate2, gate1)
    lo = jnp.minimum(i1, i2).astype(F32)
    hi = jnp.maximum(i1, i2).astype(F32)
    pair = lo * (2.0 * MOE_EPG - 1.0 - lo) * 0.5 + (hi - lo - 1.0)
    bucket = gsel * MOE_PAIRS + pair.astype(I32)
    bkt_ref[...] = bucket

    ridx = lax.broadcasted_iota(I32, (MOE_BUCKET_ROWS, tm), 0)
    onehot = jnp.where(ridx == bucket, 1.0, 0.0)
    before = jnp.where(lax.broadcasted_iota(I32, (tm, tm), 0) < lax.broadcasted_iota(I32, (tm, tm), 1), 1.0, 0.0)
    prefix = _dot(onehot.astype(BF16), before.astype(BF16))
    carry = carry_ref[...]
    rank_ref[...] = jnp.sum(onehot * (carry + prefix), axis=0, keepdims=True).astype(I32)
    carry = carry + jnp.sum(onehot, axis=1, keepdims=True)
    carry_ref[...] = carry
    cnt_ref[...] = jnp.broadcast_to(carry, cnt_ref.shape).astype(I32)


def _route(logits_t):
    t = logits_t.shape[1]
    tm = TOKEN_TILE
    rows = lambda r: pl.BlockSpec((r, tm), lambda i: (0, i))
    return pl.pallas_call(
        _route_kernel,
        grid=(t // tm,),
        in_specs=[rows(ROUTER_ROWS)],
        out_specs=[rows(1), rows(2), rows(1), pl.BlockSpec((MOE_BUCKET_ROWS, LANES), lambda i: (0, 0))],
        out_shape=[jax.ShapeDtypeStruct((1, t), I32), jax.ShapeDtypeStruct((2, t), F32),
                   jax.ShapeDtypeStruct((1, t), I32), jax.ShapeDtypeStruct((MOE_BUCKET_ROWS, LANES), I32)],
        scratch_shapes=[pltpu.VMEM((MOE_BUCKET_ROWS, 1), F32)],
        compiler_params=_cparams("arbitrary"),
        name="route_rank",
    )(logits_t)


def _dispatch_kernel(seg_ref, dest_ref, gate_ref, h_ref, g_ref, xs_ref, u_ref, zero_ref, sems):
    i = pl.program_id(0)
    n = pl.num_programs(0)
    slot = lax.rem(i, 2)
    tm, d = h_ref.shape

    def wait_rows(s):
        pltpu.make_async_copy(u_ref.at[s], xs_ref.at[pl.ds(0, tm)], sems.at[s]).wait()

    @pl.when(i == 0)
    def _():
        zero_ref[...] = jnp.zeros_like(zero_ref)

        def fill_block(start):
            fill = pltpu.make_async_copy(
                zero_ref, xs_ref.at[pl.ds(pl.multiple_of(start, MOE_BLOCK), MOE_BLOCK)], sems.at[0])
            fill.start()
            fill.wait()

        def fill_segment(e, carry):
            @pl.when(seg_ref[MOE_BUCKET_ROWS + e] > 0)
            def _():
                fill_block(seg_ref[e] - MOE_BLOCK)
            return carry

        lax.fori_loop(0, MOE_BUCKETS, fill_segment, 0)

        def fill_tail(blk, carry):
            fill_block(blk * MOE_BLOCK)
            return carry

        lax.fori_loop(seg_ref[MOE_BUCKET_ROWS - 1] // MOE_BLOCK, xs_ref.shape[0] // MOE_BLOCK, fill_tail, 0)

    @pl.when(i >= 2)
    def _():
        wait_rows(slot)

    u = _rms(h_ref[...], g_ref[...])
    half = d // 2
    as_bits = lambda v: lax.bitcast_convert_type(v.astype(BF16).astype(F32), jnp.uint32)
    words = (as_bits(u[:, half:]) & jnp.uint32(0xFFFF0000)) | (as_bits(u[:, :half]) >> 16)
    gates = jnp.concatenate([gate_ref[...], jnp.zeros((LANES - gate_ref.shape[0], tm), F32)], axis=0).T
    packed = jnp.concatenate([words, lax.bitcast_convert_type(gates, jnp.uint32),
                              jnp.zeros((tm, half - LANES), jnp.uint32)], axis=1)
    u_ref[slot] = packed.reshape(u_ref.shape[1:])

    def issue(q, carry):
        for k in range(ROW_DMA_UNROLL):
            t = q * ROW_DMA_UNROLL + k
            pltpu.make_async_copy(
                u_ref.at[slot, t], xs_ref.at[dest_ref[0, 0, t]], sems.at[slot]).start(priority=k % 2)
        return carry

    lax.fori_loop(0, tm // ROW_DMA_UNROLL, issue, 0)

    @pl.when(i == n - 1)
    def _():
        wait_rows(slot)

        @pl.when(n >= 2)
        def _():
            wait_rows(1 - slot)


def _dispatch(seg, dest, gate, h, gain, rows):
    t, d = h.shape
    tm = TOKEN_TILE
    sub = d // LANES
    return pl.pallas_call(
        _dispatch_kernel,
        grid_spec=pltpu.PrefetchScalarGridSpec(
            num_scalar_prefetch=1,
            grid=(t // tm,),
            in_specs=[
                pl.BlockSpec((1, 1, tm), lambda i, seg: (i, 0, 0), memory_space=pltpu.SMEM),
                pl.BlockSpec((8, tm), lambda i, seg: (0, i)),
                pl.BlockSpec((tm, d), lambda i, seg: (i, 0)),
                pl.BlockSpec((1, d), lambda i, seg: (0, 0)),
            ],
            out_specs=pl.BlockSpec(memory_space=pl.ANY),
            scratch_shapes=[pltpu.VMEM((2, tm, sub, LANES), jnp.uint32), pltpu.VMEM((MOE_BLOCK, sub, LANES), jnp.uint32),
                            pltpu.SemaphoreType.DMA((2,))],
        ),
        out_shape=jax.ShapeDtypeStruct((rows, sub, LANES), jnp.uint32),
        compiler_params=_cparams("arbitrary"),
        name="moe_dispatch",
    )(seg, dest, gate, h, gain)


def _expert_kernel(ea_ref, eb_ref, nu_ref, xs_ref, w1a_ref, w3a_ref, w2a_ref, w1b_ref, w3b_ref, w2b_ref, ys_ref):
    used = pl.program_id(0) < nu_ref[0]

    @pl.when(used)
    def _():
        rows, sub, lanes = xs_ref.shape
        d = sub * lanes
        packed = xs_ref[...].reshape(rows, d)
        words = packed[:, :d // 2]
        lo = lax.bitcast_convert_type(words << 16, F32).astype(BF16)
        hi = lax.bitcast_convert_type(words & jnp.uint32(0xFFFF0000), F32).astype(BF16)
        x = jnp.concatenate([lo, hi], axis=1)
        gates = lax.bitcast_convert_type(packed[:, d // 2:d // 2 + LANES], F32)

        def ffn(w1_ref, w3_ref, w2_ref):
            hid = _silu(_dot(x, w1_ref[0, 0].astype(BF16))) * _dot(x, w3_ref[0, 0].astype(BF16))
            return _dot(hid.astype(BF16), w2_ref[0, 0].astype(BF16))

        y = ffn(w1a_ref, w3a_ref, w2a_ref) * gates[:, 0:1] + ffn(w1b_ref, w3b_ref, w2b_ref) * gates[:, 1:2]
        ys_ref[...] = y.reshape(rows, sub, lanes)

    @pl.when(jnp.logical_not(used))
    def _():
        ys_ref[...] = jnp.zeros_like(ys_ref)


def _experts(block_ea, block_eb, n_used, xs, w1, w3, w2, layer):
    rows, sub, lanes = xs.shape
    d = sub * lanes
    nb = rows // MOE_BLOCK
    w_in = lambda tbl: pl.BlockSpec((1, 1, d, MOE_FF), lambda i, ea, eb, nu: (layer, (ea, eb)[tbl][i], 0, 0))
    w_out = lambda tbl: pl.BlockSpec((1, 1, MOE_FF, d), lambda i, ea, eb, nu: (layer, (ea, eb)[tbl][i], 0, 0))
    return pl.pallas_call(
        _expert_kernel,
        grid_spec=pltpu.PrefetchScalarGridSpec(
            num_scalar_prefetch=3,
            grid=(nb,),
            in_specs=[
                pl.BlockSpec((MOE_BLOCK, sub, lanes), lambda i, ea, eb, nu: (jnp.minimum(i, nu[0] - 1), 0, 0)),
                w_in(0), w_in(0), w_out(0), w_in(1), w_in(1), w_out(1),
            ],
            out_specs=pl.BlockSpec((MOE_BLOCK, sub, lanes), lambda i, ea, eb, nu: (i, 0, 0)),
        ),
        out_shape=jax.ShapeDtypeStruct((rows, sub, lanes), F32),
        compiler_params=_cparams("arbitrary"),
        name="moe_experts",
    )(block_ea, block_eb, n_used, xs, w1, w3, w2, w1, w3, w2)


def _issue_row_gather(idx_ref, src_ref, dst_ref, sem, rows):
    def issue(q, carry):
        for k in range(ROW_DMA_UNROLL):
            r = q * ROW_DMA_UNROLL + k
            pltpu.make_async_copy(src_ref.at[idx_ref[0, 0, r]], dst_ref.at[r], sem).start(priority=k % 2)
        return carry

    lax.fori_loop(0, rows // ROW_DMA_UNROLL, issue, 0)


def _combine_kernel(dest_ref, next_ref, h_ref, g_ref, ys_ref, ho_ref, xn_ref, ybuf, sems):
    i = pl.program_id(0)
    slot = lax.rem(i, 2)
    tm, d = h_ref.shape

    @pl.when(i == 0)
    def _():
        _issue_row_gather(dest_ref, ys_ref, ybuf.at[slot], sems.at[slot], tm)

    @pl.when(i + 1 < pl.num_programs(0))
    def _():
        _issue_row_gather(next_ref, ys_ref, ybuf.at[1 - slot], sems.at[1 - slot], tm)

    pltpu.make_async_copy(ys_ref.at[pl.ds(0, tm)], ybuf.at[slot], sems.at[slot]).wait()
    h = h_ref[...] + ybuf[slot].reshape(tm, d)
    ho_ref[...] = h
    xn_ref[...] = _rms(h, g_ref[...]).astype(BF16)


def _combine_norm(dest, ys, h, gain):
    n, _, tm = dest.shape
    t, d = h.shape
    smem = lambda fn: pl.BlockSpec((1, 1, tm), fn, memory_space=pltpu.SMEM)
    return pl.pallas_call(
        _combine_kernel,
        grid=(n,),
        in_specs=[
            smem(lambda i: (i, 0, 0)),
            smem(lambda i: (jnp.minimum(i + 1, n - 1), 0, 0)),
            pl.BlockSpec((tm, d), lambda i: (i, 0)),
            pl.BlockSpec((1, d), lambda i: (0, 0)),
            pl.BlockSpec(memory_space=pl.ANY),
        ],
        out_specs=[pl.BlockSpec((tm, d), lambda i: (i, 0)), pl.BlockSpec((tm, d), lambda i: (i, 0))],
        out_shape=[jax.ShapeDtypeStruct((t, d), F32), jax.ShapeDtypeStruct((t, d), BF16)],
        scratch_shapes=[pltpu.VMEM((2, tm, d // LANES, LANES), F32), pltpu.SemaphoreType.DMA((2,))],
        compiler_params=_cparams("arbitrary"),
        name="moe_combine",
    )(dest, dest, h, gain, ys)


def _final_kernel(dest_ref, next_ref, h_ref, g_ref, ys_ref, o_ref, ybuf, sems):
    step = pl.program_id(0) * pl.num_programs(1) + pl.program_id(1)
    last = pl.num_programs(0) * pl.num_programs(1) - 1
    slot = lax.rem(step, 2)
    rows, d = h_ref.shape[1:]

    @pl.when(step == 0)
    def _():
        _issue_row_gather(dest_ref, ys_ref, ybuf.at[slot], sems.at[slot], rows)

    @pl.when(step < last)
    def _():
        _issue_row_gather(next_ref, ys_ref, ybuf.at[1 - slot], sems.at[1 - slot], rows)

    pltpu.make_async_copy(ys_ref.at[pl.ds(0, rows)], ybuf.at[slot], sems.at[slot]).wait()
    o_ref[0] = _rms(h_ref[0] + ybuf[slot].reshape(rows, d), g_ref[...])


def _final_norm(h, dest, ys, gain, seq):
    b, lp, d = h.shape
    nc = lp // CHUNK
    ns = seq // CHUNK

    def chunk_of(step):
        return (step // ns) * nc + lax.rem(step, ns) + 1

    smem = lambda fn: pl.BlockSpec((1, 1, CHUNK), fn, memory_space=pltpu.SMEM)
    return pl.pallas_call(
        _final_kernel,
        grid=(b, ns),
        in_specs=[
            smem(lambda i, c: (chunk_of(i * ns + c), 0, 0)),
            smem(lambda i, c: (chunk_of(jnp.minimum(i * ns + c + 1, b * ns - 1)), 0, 0)),
            pl.BlockSpec((1, CHUNK, d), lambda i, c: (i, c + 1, 0)),
            pl.BlockSpec((1, d), lambda i, c: (0, 0)),
            pl.BlockSpec(memory_space=pl.ANY),
        ],
        out_specs=pl.BlockSpec((1, CHUNK, d), lambda i, c: (i, c, 0)),
        out_shape=jax.ShapeDtypeStruct((b, seq, d), F32),
        scratch_shapes=[pltpu.VMEM((2, CHUNK, d // LANES, LANES), F32), pltpu.SemaphoreType.DMA((2,))],
        compiler_params=_cparams("arbitrary", "arbitrary"),
        name="final_norm",
    )(dest, dest, h, gain, ys)


def _bucket_experts():
    lo, hi = [], []
    for g in range(MOE_GROUPS):
        for a in range(MOE_EPG):
            for c in range(a + 1, MOE_EPG):
                lo.append(g * MOE_EPG + a)
                hi.append(g * MOE_EPG + c)
    fill = MOE_BUCKET_ROWS - len(lo)
    return jnp.array(lo + [lo[-1]] * fill, I32), jnp.array(hi + [hi[-1]] * fill, I32)


def _moe_layer(h, y_mix, w_out, norm_gain, wg, bg, we, be, w1, w3, w2, layer):
    t, d = h.shape
    n_logits = MOE_GROUPS + MOE_EXPERTS
    wr = jnp.concatenate([wg, we.reshape(d, MOE_EXPERTS), jnp.zeros((d, LANES - n_logits), F32)], axis=1)
    br = jnp.concatenate([bg, be.reshape(-1), jnp.zeros((LANES - n_logits,), F32)])
    wr_hi = wr.astype(BF16)
    wr_split = jnp.concatenate([wr_hi, (wr - wr_hi.astype(F32)).astype(BF16)], axis=1)
    h, logits_t = _out_proj(h, y_mix, w_out.astype(BF16), norm_gain[None, :], wr_split, br[None, :])
    bucket, gate, rank, counts = _route(logits_t)

    counts = counts[:, 0]
    padded = (counts + MOE_BLOCK - 1) // MOE_BLOCK * MOE_BLOCK
    pend = jnp.cumsum(padded)
    pstart = pend - padded
    ids = jnp.arange(MOE_BUCKET_ROWS, dtype=I32)
    dest = rank[0] + jnp.sum(jnp.where(bucket == ids[:, None], pstart[:, None], 0), axis=0)
    n_blocks = -(-t // MOE_BLOCK) + MOE_BUCKETS
    block_start = jnp.arange(n_blocks, dtype=I32) * MOE_BLOCK
    block_bucket = jnp.minimum(jnp.sum((pend[None, :] <= block_start[:, None]).astype(I32), axis=1),
                               MOE_BUCKET_ROWS - 1)
    n_used = (pend[-1:] // MOE_BLOCK).astype(I32)
    lo, hi = _bucket_experts()
    gate8 = jnp.concatenate([gate, jnp.zeros((6, t), F32)], axis=0)

    xs = _dispatch(jnp.concatenate([pend, padded]).astype(I32), dest.reshape(t // TOKEN_TILE, 1, TOKEN_TILE), gate8,
                   h, norm_gain[None, :], n_blocks * MOE_BLOCK)
    ys = _experts(lo[block_bucket], hi[block_bucket], n_used, xs, w1, w3, w2, layer)
    return h, dest, ys


def kernel(x, meta, norm_mix, norm_ffn, norm_final, ret_w_in, ret_w_out, ret_norm, ssm_w_in, ssm_conv_w,
           ssm_conv_b, ssm_dt_bias, ssm_a_log, ssm_d, ssm_norm, ssm_w_out, moe_wg, moe_bg, moe_we, moe_be,
           moe_w1, moe_w3, moe_w2):
    b, seq, d = x.shape
    lp = CHUNK + seq
    t = b * lp
    depth = norm_mix.shape[0]

    half = RET_QK_DIM // 2
    inv_freq = 1.0 / (RET_ROPE_BASE ** jnp.linspace(0.0, 1.0, half, dtype=F32))
    ang = (jnp.arange(lp, dtype=F32) - PAD)[:, None] * inv_freq[None, :]
    cos, sin = jnp.cos(ang), jnp.sin(ang)
    perm = jnp.concatenate([jnp.arange(0, RET_QK_DIM, 2), jnp.arange(1, RET_QK_DIM, 2)])
    qk_perm = (jnp.arange(2 * RET_HEADS)[:, None] * RET_QK_DIM + perm[None, :]).reshape(-1)

    h = None
    moe = None
    for i in range(depth):
        j = i // 2
        if moe is not None:
            h, xn = _combine_norm(moe[0].reshape(t // TOKEN_TILE, 1, TOKEN_TILE), moe[1], h, norm_mix[i][None, :])
        if i % 2 == 0:
            w_in = jnp.concatenate([ret_w_in[j][:, qk_perm], ret_w_in[j][:, 2 * RET_QK:]], axis=1).astype(BF16)
            if i == 0:
                h, qkvg = _ret_in_first(x, meta, norm_mix[i][None, :], cos, sin, w_in)
                h = h.reshape(t, d)
            else:
                qkvg = _ret_in(h.reshape(b, lp, d), norm_mix[i][None, :], cos, sin, w_in)
            y = _retention(qkvg, ret_norm[j][None, :])
            w_out = ret_w_out[j]
        else:
            w_in = jnp.concatenate([ssm_w_in[j], jnp.zeros((d, LANES - SSM_HEADS), F32)], axis=1).astype(BF16)
            zxbc, dt = _ssm_in(xn, w_in, ssm_conv_w[j], ssm_conv_b[j][None, :], lp)
            lane_pad = lambda v: jnp.concatenate([v, jnp.zeros((LANES - SSM_HEADS,), F32)])[None, :]
            y = _ssd(zxbc.reshape(b, lp, SSM_MAIN), dt.reshape(b, lp, LANES), lane_pad(ssm_dt_bias[j]),
                     lane_pad(ssm_a_log[j]), jnp.repeat(ssm_d[j], SSM_HEAD_DIM)[None, :], ssm_norm[j][None, :])
            w_out = ssm_w_out[j]
        h, *moe = _moe_layer(h, y.reshape(t, -1), w_out, norm_ffn[i], moe_wg[i], moe_bg[i], moe_we[i],
                             moe_be[i], moe_w1, moe_w3, moe_w2, i)
    return _final_norm(h.reshape(b, lp, d), moe[0].reshape(t // CHUNK, 1, CHUNK), moe[1], norm_final[None, :], seq)
```

```python
import functools
import math

import jax
import jax.numpy as jnp
from jax import lax
from jax.experimental import pallas as pl
from jax.experimental.pallas import tpu as pltpu

F32 = jnp.float32
BF16 = jnp.bfloat16
I32 = jnp.int32

D_MODEL = 1024
N_META = 16
CHUNK = 128
PAD = CHUNK - N_META
EPS = 1e-6

RET_HEADS = 4
RET_QK_DIM = 256
RET_V_DIM = 512
RET_QK = RET_HEADS * RET_QK_DIM
RET_V = RET_HEADS * RET_V_DIM
RET_IN = 2 * RET_QK + 2 * RET_V
RET_ROPE_BASE = 10000.0

SSM_INNER = 2048
SSM_HEAD_DIM = 64
SSM_HEADS = 32
SSM_GROUPS = 8
SSM_HPG = 4
SSM_STATE = 128
SSM_CONV = 4
SSM_BC = SSM_GROUPS * SSM_STATE
SSM_CONV_DIM = SSM_INNER + 2 * SSM_BC
SSM_MAIN = SSM_INNER + SSM_CONV_DIM
LANES = 128
SSM_GROUP_W = SSM_HPG * SSM_HEAD_DIM

MOE_GROUPS = 4
MOE_EPG = 8
MOE_EXPERTS = 32
MOE_FF = 512
MOE_BLOCK = 256
MOE_PAIRS = MOE_EPG * (MOE_EPG - 1) // 2
MOE_BUCKETS = MOE_GROUPS * MOE_PAIRS
MOE_BUCKET_ROWS = 128
ROUTER_ROWS = 40

ROW_DMA_UNROLL = 8
SSM_IN_STEP = 256
SSM_IN_ROWS = 64
LOG2E = math.log2(math.e)
TOKEN_TILE = 512
VMEM_LIMIT = 56 * 1024 * 1024


def _cparams(*sem):
    return pltpu.CompilerParams(dimension_semantics=sem, vmem_limit_bytes=VMEM_LIMIT)


def _rms(x, gain):
    ms = jnp.mean(x * x, axis=-1, keepdims=True)
    return x * lax.rsqrt(ms + EPS) * gain


def _silu(x):
    hx = 0.5 * x
    return hx + hx * jnp.tanh(hx)


def _dot(a, b):
    return jnp.dot(a, b, preferred_element_type=F32)


def _dot_nt(a, b, precision=None):
    return lax.dot_general(a, b, (((1,), (1,)), ((), ())), preferred_element_type=F32, precision=precision)


def _dot_tn(a, b):
    return lax.dot_general(a, b, (((0,), (0,)), ((), ())), preferred_element_type=F32)


def _row_tile(rows, cap=544):
    best = 16
    for t in range(16, cap + 1, 16):
        if rows % t == 0:
            best = t
    return best


def _ret_in_kernel(h_ref, g_ref, cos_ref, sin_ref, w_ref, o_ref):
    _ret_project(h_ref[0], g_ref, cos_ref, sin_ref, w_ref, o_ref)


def _ret_in_first_kernel(meta_ref, g_ref, cos_ref, sin_ref, w_ref, x_ref, ho_ref, o_ref, hbuf, sems):
    nj = pl.num_programs(1)
    step = pl.program_id(0) * nj + pl.program_id(1)
    last = pl.num_programs(0) * nj - 1
    slot = lax.rem(step, 2)
    tm = hbuf.shape[1]

    def rows_copy(s, slot_):
        i, j = s // nj, lax.rem(s, nj)
        first = pltpu.make_async_copy(
            x_ref.at[i, pl.ds(0, tm - CHUNK)], hbuf.at[slot_, pl.ds(CHUNK, tm - CHUNK)], sems.at[slot_])
        start = pl.multiple_of(jnp.maximum(j * tm - CHUNK, 0), 8)
        other = pltpu.make_async_copy(x_ref.at[i, pl.ds(start, tm)], hbuf.at[slot_], sems.at[slot_])
        return j == 0, first, other

    def fetch(s, slot_):
        is_first, first, other = rows_copy(s, slot_)
        pl.when(is_first)(first.start)
        pl.when(jnp.logical_not(is_first))(other.start)

    @pl.when(step == 0)
    def _():
        fetch(step, slot)

    @pl.when(step < last)
    def _():
        fetch(step + 1, 1 - slot)

    is_first, first, other = rows_copy(step, slot)
    pl.when(is_first)(first.wait)
    pl.when(jnp.logical_not(is_first))(other.wait)

    @pl.when(is_first)
    def _():
        hbuf[slot, 0:PAD, :] = jnp.zeros((PAD, hbuf.shape[2]), F32)
        hbuf[slot, PAD:CHUNK, :] = meta_ref[...]

    h = hbuf[slot]
    ho_ref[0] = h
    _ret_project(h, g_ref, cos_ref, sin_ref, w_ref, o_ref)


def _ret_project(h, g_ref, cos_ref, sin_ref, w_ref, o_ref):
    xn = _rms(h, g_ref[...]).astype(BF16)
    cos = cos_ref[...]
    sin = sin_ref[...]
    half = RET_QK_DIM // 2
    for j in range(2 * RET_HEADS):
        c0 = j * RET_QK_DIM
        acc = _dot(xn, w_ref[:, c0:c0 + RET_QK_DIM])
        if j >= RET_HEADS:
            acc = acc * (RET_QK_DIM ** -0.5)
        e = acc[:, :half]
        o = acc[:, half:]
        o_ref[0, :, c0:c0 + half] = (e * cos - o * sin).astype(BF16)
        o_ref[0, :, c0 + half:c0 + RET_QK_DIM] = (o * cos + e * sin).astype(BF16)
    step = 1024
    for c0 in range(2 * RET_QK, RET_IN, step):
        o_ref[0, :, c0:c0 + step] = _dot(xn, w_ref[:, c0:c0 + step]).astype(BF16)


def _ret_in_first(x, meta, gain, cos, sin, w):
    b, seq, d = x.shape
    lp = CHUNK + seq
    tm = _row_tile(lp)
    assert tm > CHUNK
    const = lambda r, c: pl.BlockSpec((r, c), lambda i, j: (0, 0))
    return pl.pallas_call(
        _ret_in_first_kernel,
        grid=(b, lp // tm),
        in_specs=[
            const(N_META, d), const(1, d),
            pl.BlockSpec((tm, RET_QK_DIM // 2), lambda i, j: (j, 0)),
            pl.BlockSpec((tm, RET_QK_DIM // 2), lambda i, j: (j, 0)),
            const(d, RET_IN),
            pl.BlockSpec(memory_space=pl.ANY),
        ],
        out_specs=[pl.BlockSpec((1, tm, d), lambda i, j: (i, j, 0)), pl.BlockSpec((1, tm, RET_IN), lambda i, j: (i, j, 0))],
        out_shape=[jax.ShapeDtypeStruct((b, lp, d), F32), jax.ShapeDtypeStruct((b, lp, RET_IN), BF16)],
        scratch_shapes=[pltpu.VMEM((2, tm, d), F32), pltpu.SemaphoreType.DMA((2,))],
        compiler_params=_cparams("arbitrary", "arbitrary"),
        name="ret_in_proj",
    )(meta, gain, cos, sin, w, x)


def _ret_in(h, gain, cos, sin, w):
    b, lp, d = h.shape
    tm = _row_tile(lp)
    return pl.pallas_call(
        _ret_in_kernel,
        grid=(b, lp // tm),
        in_specs=[
            pl.BlockSpec((1, tm, d), lambda i, j: (i, j, 0)),
            pl.BlockSpec((1, d), lambda i, j: (0, 0)),
            pl.BlockSpec((tm, RET_QK_DIM // 2), lambda i, j: (j, 0)),
            pl.BlockSpec((tm, RET_QK_DIM // 2), lambda i, j: (j, 0)),
            pl.BlockSpec((d, RET_IN), lambda i, j: (0, 0)),
        ],
        out_specs=pl.BlockSpec((1, tm, RET_IN), lambda i, j: (i, j, 0)),
        out_shape=jax.ShapeDtypeStruct((b, lp, RET_IN), BF16),
        compiler_params=_cparams("parallel", "parallel"),
        name="ret_in_proj",
    )(h, gain, cos, sin, w)


def _ret_kernel(q_ref, k_ref, v_ref, g_ref, hn_ref, o_ref, state_ref):
    @pl.when(pl.program_id(1) == 0)
    def _():
        state_ref[...] = jnp.zeros_like(state_ref)

    row = lax.broadcasted_iota(I32, (CHUNK, CHUNK), 0).astype(F32)
    col = lax.broadcasted_iota(I32, (CHUNK, CHUNK), 1).astype(F32)
    diff = row - col
    ridx = row[:, :1]
    for hh in range(RET_HEADS):
        lg = math.log(1.0 - 2.0 ** (-5.0 - hh))
        intra = jnp.where(diff >= 0, jnp.exp(lg * jnp.maximum(diff, 0.0)), 0.0)
        qdec = jnp.exp(lg * (ridx + 1.0))
        kdec = jnp.exp(lg * (CHUNK - 1.0 - ridx))
        cdec = math.exp(lg * CHUNK)
        qh = q_ref[0, :, hh * RET_QK_DIM:(hh + 1) * RET_QK_DIM]
        kh = k_ref[0, :, hh * RET_QK_DIM:(hh + 1) * RET_QK_DIM]
        vh = v_ref[0, :, hh * RET_V_DIM:(hh + 1) * RET_V_DIM]
        scores = _dot_nt(qh, kh) * intra
        st = state_ref[hh]
        y = _dot(scores.astype(BF16), vh) + _dot(qh, st.astype(BF16)) * qdec
        kd = (kh.astype(F32) * kdec).astype(BF16)
        state_ref[hh] = st * cdec + _dot_tn(kd, vh)
        yn = _rms(y, hn_ref[:, hh * RET_V_DIM:(hh + 1) * RET_V_DIM])
        gh = g_ref[0, :, hh * RET_V_DIM:(hh + 1) * RET_V_DIM].astype(F32)
        o_ref[0, :, hh * RET_V_DIM:(hh + 1) * RET_V_DIM] = (_silu(gh) * yn).astype(BF16)


def _retention(qkvg, head_norm):
    b, lp, _ = qkvg.shape
    nc = lp // CHUNK
    return pl.pallas_call(
        _ret_kernel,
        grid=(b, nc),
        in_specs=[
            pl.BlockSpec((1, CHUNK, RET_QK), lambda i, c: (i, c, 0)),
            pl.BlockSpec((1, CHUNK, RET_QK), lambda i, c: (i, c, 1)),
            pl.BlockSpec((1, CHUNK, RET_V), lambda i, c: (i, c, 1)),
            pl.BlockSpec((1, CHUNK, RET_V), lambda i, c: (i, c, 2)),
            pl.BlockSpec((1, RET_V), lambda i, c: (0, 0)),
        ],
        out_specs=pl.BlockSpec((1, CHUNK, RET_V), lambda i, c: (i, c, 0)),
        out_shape=jax.ShapeDtypeStruct((b, lp, RET_V), BF16),
        scratch_shapes=[pltpu.VMEM((RET_HEADS, RET_QK_DIM, RET_V_DIM), F32)],
        compiler_params=_cparams("parallel", "arbitrary"),
        name="retention",
    )(qkvg, qkvg, qkvg, qkvg, head_norm)


def _ssm_in_kernel(xn_ref, w_ref, cw_ref, cb_ref, o_ref, dt_ref, tail_ref, acc_ref, *, rows_per_seq):
    i = pl.program_id(0)
    tm = xn_ref.shape[0]

    @pl.when(i == 0)
    def _():
        tail_ref[...] = jnp.zeros_like(tail_ref)

    step = acc_ref.shape[2]
    n_steps = SSM_MAIN // step
    rb = SSM_IN_ROWS

    pos = lax.rem(i * tm, rows_per_seq) + lax.broadcasted_iota(I32, (tm, 1), 0)
    is_pad = jnp.logical_or(pos < PAD, jnp.logical_and(pos >= rows_per_seq, pos < rows_per_seq + PAD))

    def project(j):
        c0 = j * step
        acc = _dot(xn_ref[...], w_ref[:, c0:c0 + step])
        acc_ref[j % 2, 8:, :] = acc if c0 < SSM_INNER else jnp.where(is_pad, 0.0, acc)

    project(0)
    for j in range(n_steps):
        if j + 1 < n_steps:
            project(j + 1)
        c0 = j * step
        s = j % 2
        if c0 < SSM_INNER:
            for r0 in range(0, tm, rb):
                o_ref[r0:r0 + rb, c0:c0 + step] = _silu(acc_ref[s, 8 + r0:8 + r0 + rb, :]).astype(BF16)
            continue
        cc = c0 - SSM_INNER
        acc_ref[s, 0:8, :] = tail_ref[:, cc:cc + step]
        tail_ref[:, cc:cc + step] = acc_ref[s, tm:tm + 8, :]
        taps = [cw_ref[k:k + 1, cc:cc + step] for k in range(SSM_CONV)]
        bias = cb_ref[:, cc:cc + step]
        for r0 in range(0, tm, rb):
            ext = acc_ref[s, r0:r0 + rb + 8, :]
            back1 = pltpu.roll(ext, 1, axis=0)
            back23 = pltpu.roll(taps[1] * ext + taps[0] * back1, 2, axis=0)
            conv = bias + taps[3] * ext + taps[2] * back1 + back23
            o_ref[r0:r0 + rb, c0:c0 + step] = _silu(conv[8:]).astype(BF16)
    dt_ref[...] = _dot(xn_ref[...], w_ref[:, SSM_MAIN:SSM_MAIN + LANES])


def _ssm_in(xn, w, conv_w, conv_b, rows_per_seq):
    t, d = xn.shape
    tm = TOKEN_TILE
    assert tm <= rows_per_seq
    const = lambda r, c: pl.BlockSpec((r, c), lambda i: (0, 0))
    return pl.pallas_call(
        functools.partial(_ssm_in_kernel, rows_per_seq=rows_per_seq),
        grid=(t // tm,),
        in_specs=[
            pl.BlockSpec((tm, d), lambda i: (i, 0)),
            const(d, SSM_MAIN + LANES), const(SSM_CONV, SSM_CONV_DIM), const(1, SSM_CONV_DIM),
        ],
        out_specs=[
            pl.BlockSpec((tm, SSM_MAIN), lambda i: (i, 0)),
            pl.BlockSpec((tm, LANES), lambda i: (i, 0)),
        ],
        out_shape=[jax.ShapeDtypeStruct((t, SSM_MAIN), BF16), jax.ShapeDtypeStruct((t, LANES), F32)],
        scratch_shapes=[pltpu.VMEM((8, SSM_CONV_DIM), F32), pltpu.VMEM((2, tm + 8, SSM_IN_STEP), F32)],
        compiler_params=_cparams("arbitrary"),
        name="ssm_in_proj",
    )(xn, w, conv_w, conv_b)


def _ssd_kernel(zg_ref, x_ref, bc_ref, dt_ref, dtb_ref, alog_ref, dsk_ref, gn_ref, o_ref, state_ref):
    c = pl.program_id(1)

    @pl.when(c == 0)
    def _():
        state_ref[...] = jnp.zeros_like(state_ref)

    row = lax.broadcasted_iota(I32, (CHUNK, CHUNK), 0)
    col = lax.broadcasted_iota(I32, (CHUNK, CHUNK), 1)
    causal_bias = jnp.where(row >= col, 0.0, -jnp.inf)
    low_half = col < SSM_HEAD_DIM
    valid = row[:, :1] >= jnp.where(c > 0, 0, PAD)

    dtr = dt_ref[0] + dtb_ref[...]
    dtv = jnp.maximum(dtr, 0.0) + jnp.log(1.0 + jnp.exp(-jnp.abs(dtr)))
    dtv = jnp.where(valid, dtv, 0.0)
    da = dtv * (-jnp.exp(alog_ref[...]))
    tril = jnp.where(row >= col, 1.0, 0.0)
    acs = jnp.dot(tril, da, preferred_element_type=F32, precision=lax.Precision.HIGHEST)
    last = acs[CHUNK - 1:CHUNK, :]
    acs2 = acs * LOG2E
    src_t = (jnp.log2(dtv) - acs2).T
    w_t = (dtv * jnp.exp(last - acs)).T
    elast = jnp.exp(last)

    def pair_row(v, h0):
        return jnp.where(low_half[:1], v[:, h0:h0 + 1], v[:, h0 + 1:h0 + 2])

    groups = range(SSM_GROUPS)
    gsl = [slice(gi * SSM_GROUP_W, (gi + 1) * SSM_GROUP_W) for gi in groups]
    bgs = [bc_ref[0, :, gi * SSM_STATE:(gi + 1) * SSM_STATE] for gi in groups]
    cgs = [bc_ref[0, :, SSM_BC + gi * SSM_STATE:SSM_BC + (gi + 1) * SSM_STATE] for gi in groups]
    sts = [state_ref[gi] for gi in groups]
    cbs = [_dot_nt(cgs[gi], bgs[gi]) for gi in groups]
    yoffs = [_dot(cgs[gi], sts[gi].astype(BF16)) for gi in groups]
    bts = [bgs[gi].astype(F32).T for gi in groups]

    outs = []
    eexp = []
    for gi in groups:
        for pp in range(SSM_HPG // 2):
            h0 = gi * SSM_HPG + 2 * pp
            top, bot, ecols = [], [], []
            for hd in (h0, h0 + 1):
                acs_b = jnp.broadcast_to(acs2[:, hd:hd + 1], (CHUNK, CHUNK))
                top.append(cbs[gi] * jnp.exp2(acs_b + (src_t[hd:hd + 1, :] + causal_bias)))
                bot.append(bts[gi] * w_t[hd:hd + 1, :])
                ecols.append(jnp.exp2(acs_b))
            lhs = jnp.concatenate([jnp.concatenate(top, axis=1), jnp.concatenate(bot, axis=1)], axis=0)
            xp = x_ref[0, :, h0 * SSM_HEAD_DIM:(h0 + 2) * SSM_HEAD_DIM]
            zero = jnp.zeros_like(xp)
            rhs = jnp.concatenate([jnp.where(low_half, xp, zero), jnp.where(low_half, zero, xp)], axis=0)
            outs.append(_dot(lhs.astype(BF16), rhs))
            eexp.append(jnp.where(low_half, ecols[0], ecols[1]))

    for gi in groups:
        pairs = [gi * (SSM_HPG // 2) + pp for pp in range(SSM_HPG // 2)]
        heads = [gi * SSM_HPG + 2 * pp for pp in range(SSM_HPG // 2)]
        lexp = jnp.concatenate([pair_row(elast, h0) for h0 in heads], axis=1)
        state_ref[gi] = sts[gi] * lexp + jnp.concatenate([outs[p][CHUNK:] for p in pairs], axis=1)
        xg = x_ref[0, :, gsl[gi]].astype(F32)
        y = jnp.concatenate([outs[p][:CHUNK] for p in pairs], axis=1)
        y = y + yoffs[gi] * jnp.concatenate([eexp[p] for p in pairs], axis=1)
        y = y + dsk_ref[:, gsl[gi]] * xg
        y = y * zg_ref[0, :, gsl[gi]].astype(F32)
        o_ref[0, :, gsl[gi]] = _rms(y, gn_ref[:, gsl[gi]]).astype(BF16)


def _ssd(zxbc, dt, dt_bias, a_log, d_skip, gate_norm):
    b, lp, _ = zxbc.shape
    nc = lp // CHUNK
    wide = lambda j: pl.BlockSpec((1, CHUNK, SSM_INNER), lambda i, c: (i, c, j))
    const = lambda r, w: pl.BlockSpec((r, w), lambda i, c: (0, 0))
    return pl.pallas_call(
        _ssd_kernel,
        grid=(b, nc),
        in_specs=[
            wide(0), wide(1), wide(2),
            pl.BlockSpec((1, CHUNK, LANES), lambda i, c: (i, c, 0)),
            const(1, LANES), const(1, LANES), const(1, SSM_INNER), const(1, SSM_INNER),
        ],
        out_specs=pl.BlockSpec((1, CHUNK, SSM_INNER), lambda i, c: (i, c, 0)),
        out_shape=jax.ShapeDtypeStruct((b, lp, SSM_INNER), BF16),
        scratch_shapes=[pltpu.VMEM((SSM_GROUPS, SSM_STATE, SSM_GROUP_W), F32)],
        compiler_params=_cparams("parallel", "arbitrary"),
        name="ssd",
    )(zxbc, zxbc, zxbc, dt, dt_bias, a_log, d_skip, gate_norm)


def _out_proj_kernel(h_ref, y_ref, w_ref, g_ref, wr_ref, br_ref, ho_ref, lg_ref):
    hn = h_ref[...] + _dot(y_ref[...], w_ref[...])
    ho_ref[...] = hn
    u = _rms(hn, g_ref[...])
    u_hi = u.astype(BF16)
    u_lo = (u - u_hi.astype(F32)).astype(BF16)
    p = _dot(u_hi, wr_ref[...])
    logits = p[:, :LANES] + (p[:, LANES:] + _dot(u_lo, wr_ref[:, :LANES])) + br_ref[...]
    lg_ref[...] = logits.T[:ROUTER_ROWS]


def _out_proj(h, y, w, gain, wr_t, br):
    t, d = h.shape
    tm = TOKEN_TILE
    kin = y.shape[1]
    return pl.pallas_call(
        _out_proj_kernel,
        grid=(t // tm,),
        in_specs=[
            pl.BlockSpec((tm, d), lambda i: (i, 0)),
            pl.BlockSpec((tm, kin), lambda i: (i, 0)),
            pl.BlockSpec((kin, d), lambda i: (0, 0)),
            pl.BlockSpec((1, d), lambda i: (0, 0)),
            pl.BlockSpec((d, 2 * LANES), lambda i: (0, 0)),
            pl.BlockSpec((1, LANES), lambda i: (0, 0)),
        ],
        out_specs=[
            pl.BlockSpec((tm, d), lambda i: (i, 0)),
            pl.BlockSpec((ROUTER_ROWS, tm), lambda i: (0, i)),
        ],
        out_shape=[jax.ShapeDtypeStruct((t, d), F32), jax.ShapeDtypeStruct((ROUTER_ROWS, t), F32)],
        compiler_params=_cparams("parallel"),
        name="out_proj_router",
    )(h, y, w, gain, wr_t, br)


def _first_argmax(v, n):
    ridx = lax.broadcasted_iota(I32, v.shape, 0).astype(F32)
    vmax = jnp.max(v, axis=0, keepdims=True)
    idx = jnp.min(jnp.where(v == vmax, ridx, float(n)), axis=0, keepdims=True)
    return vmax, idx.astype(I32)


def _route_kernel(lg_ref, bkt_ref, gate_ref, rank_ref, cnt_ref, carry_ref):
    @pl.when(pl.program_id(0) == 0)
    def _():
        carry_ref[...] = jnp.zeros_like(carry_ref)

    logits = lg_ref[...]
    tm = logits.shape[1]
    gl = logits[0:MOE_GROUPS]
    gmax, gsel = _first_argmax(gl, MOE_GROUPS)
    p_group = 1.0 / jnp.sum(jnp.exp(gl - gmax), axis=0, keepdims=True)
    el = logits[MOE_GROUPS:MOE_GROUPS + MOE_EPG]
    for gg in range(1, MOE_GROUPS):
        el = jnp.where(gsel == gg, logits[MOE_GROUPS + gg * MOE_EPG:MOE_GROUPS + (gg + 1) * MOE_EPG], el)
    ex = jnp.exp(el - jnp.max(el, axis=0, keepdims=True))
    p = ex / jnp.sum(ex, axis=0, keepdims=True)
    p1, i1 = _first_argmax(p, MOE_EPG)
    ridx8 = lax.broadcasted_iota(I32, p.shape, 0)
    p2, i2 = _first_argmax(jnp.where(ridx8 == i1, -1.0, p), MOE_EPG)
    denom = p1 + p2
    gate1 = p_group * p1 / denom
    gate2 = p_group * p2 / denom
    first_low = i1 < i2
    gate_ref[0:1, :] = jnp.where(first_low, gate1, gate2)
    gate_ref[1:2, :] = jnp.where(first_low, gate2, gate1)
    lo = jnp.minimum(i1, i2).astype(F32)
    hi = jnp.maximum(i1, i2).astype(F32)
    pair = lo * (2.0 * MOE_EPG - 1.0 - lo) * 0.5 + (hi - lo - 1.0)
    bucket = gsel * MOE_PAIRS + pair.astype(I32)
    bkt_ref[...] = bucket

    ridx = lax.broadcasted_iota(I32, (MOE_BUCKET_ROWS, tm), 0)
    onehot = jnp.where(ridx == bucket, 1.0, 0.0)
    before = jnp.where(lax.broadcasted_iota(I32, (tm, tm), 0) < lax.broadcasted_iota(I32, (tm, tm), 1), 1.0, 0.0)
    prefix = _dot(onehot.astype(BF16), before.astype(BF16))
    carry = carry_ref[...]
    rank_ref[...] = jnp.sum(onehot * (carry + prefix), axis=0, keepdims=True).astype(I32)
    carry = carry + jnp.sum(onehot, axis=1, keepdims=True)
    carry_ref[...] = carry
    cnt_ref[...] = jnp.broadcast_to(carry, cnt_ref.shape).astype(I32)


def _route(logits_t):
    t = logits_t.shape[1]
    tm = TOKEN_TILE
    rows = lambda r: pl.BlockSpec((r, tm), lambda i: (0, i))
    return pl.pallas_call(
        _route_kernel,
        grid=(t // tm,),
        in_specs=[rows(ROUTER_ROWS)],
        out_specs=[rows(1), rows(2), rows(1), pl.BlockSpec((MOE_BUCKET_ROWS, LANES), lambda i: (0, 0))],
        out_shape=[jax.ShapeDtypeStruct((1, t), I32), jax.ShapeDtypeStruct((2, t), F32),
                   jax.ShapeDtypeStruct((1, t), I32), jax.ShapeDtypeStruct((MOE_BUCKET_ROWS, LANES), I32)],
        scratch_shapes=[pltpu.VMEM((MOE_BUCKET_ROWS, 1), F32)],
        compiler_params=_cparams("arbitrary"),
        name="route_rank",
    )(logits_t)


def _dispatch_kernel(seg_ref, dest_ref, gate_ref, h_ref, g_ref, xs_ref, u_ref, zero_ref, sems):
    i = pl.program_id(0)
    n = pl.num_programs(0)
    slot = lax.rem(i, 2)
    tm, d = h_ref.shape

    def wait_rows(s):
        pltpu.make_async_copy(u_ref.at[s], xs_ref.at[pl.ds(0, tm)], sems.at[s]).wait()

    @pl.when(i == 0)
    def _():
        zero_ref[...] = jnp.zeros_like(zero_ref)

        def fill_block(start):
            fill = pltpu.make_async_copy(
                zero_ref, xs_ref.at[pl.ds(pl.multiple_of(start, MOE_BLOCK), MOE_BLOCK)], sems.at[0])
            fill.start()
            fill.wait()

        def fill_segment(e, carry):
            @pl.when(seg_ref[MOE_BUCKET_ROWS + e] > 0)
            def _():
                fill_block(seg_ref[e] - MOE_BLOCK)
            return carry

        lax.fori_loop(0, MOE_BUCKETS, fill_segment, 0)

        def fill_tail(blk, carry):
            fill_block(blk * MOE_BLOCK)
            return carry

        lax.fori_loop(seg_ref[MOE_BUCKET_ROWS - 1] // MOE_BLOCK, xs_ref.shape[0] // MOE_BLOCK, fill_tail, 0)

    @pl.when(i >= 2)
    def _():
        wait_rows(slot)

    u = _rms(h_ref[...], g_ref[...])
    half = d // 2
    as_bits = lambda v: lax.bitcast_convert_type(v.astype(BF16).astype(F32), jnp.uint32)
    words = (as_bits(u[:, half:]) & jnp.uint32(0xFFFF0000)) | (as_bits(u[:, :half]) >> 16)
    gates = jnp.concatenate([gate_ref[...], jnp.zeros((LANES - gate_ref.shape[0], tm), F32)], axis=0).T
    packed = jnp.concatenate([words, lax.bitcast_convert_type(gates, jnp.uint32),
                              jnp.zeros((tm, half - LANES), jnp.uint32)], axis=1)
    u_ref[slot] = packed.reshape(u_ref.shape[1:])

    def issue(q, carry):
        for k in range(ROW_DMA_UNROLL):
            t = q * ROW_DMA_UNROLL + k
            pltpu.make_async_copy(
                u_ref.at[slot, t], xs_ref.at[dest_ref[0, 0, t]], sems.at[slot]).start(priority=k % 2)
        return carry

    lax.fori_loop(0, tm // ROW_DMA_UNROLL, issue, 0)

    @pl.when(i == n - 1)
    def _():
        wait_rows(slot)

        @pl.when(n >= 2)
        def _():
            wait_rows(1 - slot)


def _dispatch(seg, dest, gate, h, gain, rows):
    t, d = h.shape
    tm = TOKEN_TILE
    sub = d // LANES
    return pl.pallas_call(
        _dispatch_kernel,
        grid_spec=pltpu.PrefetchScalarGridSpec(
            num_scalar_prefetch=1,
            grid=(t // tm,),
            in_specs=[
                pl.BlockSpec((1, 1, tm), lambda i, seg: (i, 0, 0), memory_space=pltpu.SMEM),
                pl.BlockSpec((8, tm), lambda i, seg: (0, i)),
                pl.BlockSpec((tm, d), lambda i, seg: (i, 0)),
                pl.BlockSpec((1, d), lambda i, seg: (0, 0)),
            ],
            out_specs=pl.BlockSpec(memory_space=pl.ANY),
            scratch_shapes=[pltpu.VMEM((2, tm, sub, LANES), jnp.uint32), pltpu.VMEM((MOE_BLOCK, sub, LANES), jnp.uint32),
                            pltpu.SemaphoreType.DMA((2,))],
        ),
        out_shape=jax.ShapeDtypeStruct((rows, sub, LANES), jnp.uint32),
        compiler_params=_cparams("arbitrary"),
        name="moe_dispatch",
    )(seg, dest, gate, h, gain)


def _expert_kernel(ea_ref, eb_ref, nu_ref, xs_ref, w1a_ref, w3a_ref, w2a_ref, w1b_ref, w3b_ref, w2b_ref, ys_ref):
    used = pl.program_id(0) < nu_ref[0]

    @pl.when(used)
    def _():
        rows, sub, lanes = xs_ref.shape
        d = sub * lanes
        packed = xs_ref[...].reshape(rows, d)
        words = packed[:, :d // 2]
        lo = lax.bitcast_convert_type(words << 16, F32).astype(BF16)
        hi = lax.bitcast_convert_type(words & jnp.uint32(0xFFFF0000), F32).astype(BF16)
        x = jnp.concatenate([lo, hi], axis=1)
        gates = lax.bitcast_convert_type(packed[:, d // 2:d // 2 + LANES], F32)

        def ffn(w1_ref, w3_ref, w2_ref):
            hid = _silu(_dot(x, w1_ref[0, 0].astype(BF16))) * _dot(x, w3_ref[0, 0].astype(BF16))
            return _dot(hid.astype(BF16), w2_ref[0, 0].astype(BF16))

        y = ffn(w1a_ref, w3a_ref, w2a_ref) * gates[:, 0:1] + ffn(w1b_ref, w3b_ref, w2b_ref) * gates[:, 1:2]
        ys_ref[...] = y.reshape(rows, sub, lanes)

    @pl.when(jnp.logical_not(used))
    def _():
        ys_ref[...] = jnp.zeros_like(ys_ref)


def _experts(block_ea, block_eb, n_used, xs, w1, w3, w2, layer):
    rows, sub, lanes = xs.shape
    d = sub * lanes
    nb = rows // MOE_BLOCK
    w_in = lambda tbl: pl.BlockSpec((1, 1, d, MOE_FF), lambda i, ea, eb, nu: (layer, (ea, eb)[tbl][i], 0, 0))
    w_out = lambda tbl: pl.BlockSpec((1, 1, MOE_FF, d), lambda i, ea, eb, nu: (layer, (ea, eb)[tbl][i], 0, 0))
    return pl.pallas_call(
        _expert_kernel,
        grid_spec=pltpu.PrefetchScalarGridSpec(
            num_scalar_prefetch=3,
            grid=(nb,),
            in_specs=[
                pl.BlockSpec((MOE_BLOCK, sub, lanes), lambda i, ea, eb, nu: (jnp.minimum(i, nu[0] - 1), 0, 0)),
                w_in(0), w_in(0), w_out(0), w_in(1), w_in(1), w_out(1),
            ],
            out_specs=pl.BlockSpec((MOE_BLOCK, sub, lanes), lambda i, ea, eb, nu: (i, 0, 0)),
        ),
        out_shape=jax.ShapeDtypeStruct((rows, sub, lanes), F32),
        compiler_params=_cparams("arbitrary"),
        name="moe_experts",
    )(block_ea, block_eb, n_used, xs, w1, w3, w2, w1, w3, w2)


def _issue_row_gather(idx_ref, src_ref, dst_ref, sem, rows):
    def issue(q, carry):
        for k in range(ROW_DMA_UNROLL):
            r = q * ROW_DMA_UNROLL + k
            pltpu.make_async_copy(src_ref.at[idx_ref[0, 0, r]], dst_ref.at[r], sem).start(priority=k % 2)
        return carry

    lax.fori_loop(0, rows // ROW_DMA_UNROLL, issue, 0)


def _combine_kernel(dest_ref, next_ref, h_ref, g_ref, ys_ref, ho_ref, xn_ref, ybuf, sems):
    i = pl.program_id(0)
    slot = lax.rem(i, 2)
    tm, d = h_ref.shape

    @pl.when(i == 0)
    def _():
        _issue_row_gather(dest_ref, ys_ref, ybuf.at[slot], sems.at[slot], tm)

    @pl.when(i + 1 < pl.num_programs(0))
    def _():
        _issue_row_gather(next_ref, ys_ref, ybuf.at[1 - slot], sems.at[1 - slot], tm)

    pltpu.make_async_copy(ys_ref.at[pl.ds(0, tm)], ybuf.at[slot], sems.at[slot]).wait()
    h = h_ref[...] + ybuf[slot].reshape(tm, d)
    ho_ref[...] = h
    xn_ref[...] = _rms(h, g_ref[...]).astype(BF16)


def _combine_norm(dest, ys, h, gain):
    n, _, tm = dest.shape
    t, d = h.shape
    smem = lambda fn: pl.BlockSpec((1, 1, tm), fn, memory_space=pltpu.SMEM)
    return pl.pallas_call(
        _combine_kernel,
        grid=(n,),
        in_specs=[
            smem(lambda i: (i, 0, 0)),
            smem(lambda i: (jnp.minimum(i + 1, n - 1), 0, 0)),
            pl.BlockSpec((tm, d), lambda i: (i, 0)),
            pl.BlockSpec((1, d), lambda i: (0, 0)),
            pl.BlockSpec(memory_space=pl.ANY),
        ],
        out_specs=[pl.BlockSpec((tm, d), lambda i: (i, 0)), pl.BlockSpec((tm, d), lambda i: (i, 0))],
        out_shape=[jax.ShapeDtypeStruct((t, d), F32), jax.ShapeDtypeStruct((t, d), BF16)],
        scratch_shapes=[pltpu.VMEM((2, tm, d // LANES, LANES), F32), pltpu.SemaphoreType.DMA((2,))],
        compiler_params=_cparams("arbitrary"),
        name="moe_combine",
    )(dest, dest, h, gain, ys)


def _final_kernel(*refs, per):
    cur, nxt, hs = refs[:per], refs[per:2 * per], refs[2 * per:3 * per]
    g_ref, ys_ref, o_ref, ybuf, sems = refs[3 * per:]
    step = pl.program_id(0) * pl.num_programs(1) + pl.program_id(1)
    last = pl.num_programs(0) * pl.num_programs(1) - 1
    slot = lax.rem(step, 2)
    d = o_ref.shape[2]

    def gather(idx_refs, s):
        for p, idx_ref in enumerate(idx_refs):
            _issue_row_gather(idx_ref, ys_ref, ybuf.at[s, pl.ds(p * CHUNK, CHUNK)], sems.at[s], CHUNK)

    @pl.when(step == 0)
    def _():
        gather(cur, slot)

    @pl.when(step < last)
    def _():
        gather(nxt, 1 - slot)

    pltpu.make_async_copy(ys_ref.at[pl.ds(0, per * CHUNK)], ybuf.at[slot], sems.at[slot]).wait()
    h = jnp.concatenate([h_ref[0] for h_ref in hs], axis=0)
    o_ref[0] = _rms(h + ybuf[slot].reshape(per * CHUNK, d), g_ref[...])


def _final_norm(h, dest, ys, gain, seq):
    b, lp, d = h.shape
    nc = lp // CHUNK
    ns = seq // CHUNK
    per = 2 if ns % 2 == 0 else 1
    steps = ns // per

    def chunk_of(step, p):
        return (step // steps) * nc + lax.rem(step, steps) * per + p + 1

    smem = lambda fn: pl.BlockSpec((1, 1, CHUNK), fn, memory_space=pltpu.SMEM)
    cur = [smem(lambda i, c, p=p: (chunk_of(i * steps + c, p), 0, 0)) for p in range(per)]
    nxt = [smem(lambda i, c, p=p: (chunk_of(jnp.minimum(i * steps + c + 1, b * steps - 1), p), 0, 0))
           for p in range(per)]
    hs = [pl.BlockSpec((1, CHUNK, d), lambda i, c, p=p: (i, c * per + p + 1, 0)) for p in range(per)]
    return pl.pallas_call(
        functools.partial(_final_kernel, per=per),
        grid=(b, steps),
        in_specs=cur + nxt + hs + [pl.BlockSpec((1, d), lambda i, c: (0, 0)), pl.BlockSpec(memory_space=pl.ANY)],
        out_specs=pl.BlockSpec((1, per * CHUNK, d), lambda i, c: (i, c, 0)),
        out_shape=jax.ShapeDtypeStruct((b, seq, d), F32),
        scratch_shapes=[pltpu.VMEM((2, per * CHUNK, d // LANES, LANES), F32), pltpu.SemaphoreType.DMA((2,))],
        compiler_params=_cparams("arbitrary", "arbitrary"),
        name="final_norm",
    )(*([dest] * (2 * per)), *([h] * per), gain, ys)


def _bucket_experts():
    lo, hi = [], []
    for g in range(MOE_GROUPS):
        for a in range(MOE_EPG):
            for c in range(a + 1, MOE_EPG):
                lo.append(g * MOE_EPG + a)
                hi.append(g * MOE_EPG + c)
    fill = MOE_BUCKET_ROWS - len(lo)
    return jnp.array(lo + [lo[-1]] * fill, I32), jnp.array(hi + [hi[-1]] * fill, I32)


def _moe_layer(h, y_mix, w_out, norm_gain, wg, bg, we, be, w1, w3, w2, layer):
    t, d = h.shape
    n_logits = MOE_GROUPS + MOE_EXPERTS
    wr = jnp.concatenate([wg, we.reshape(d, MOE_EXPERTS), jnp.zeros((d, LANES - n_logits), F32)], axis=1)
    br = jnp.concatenate([bg, be.reshape(-1), jnp.zeros((LANES - n_logits,), F32)])
    wr_hi = wr.astype(BF16)
    wr_split = jnp.concatenate([wr_hi, (wr - wr_hi.astype(F32)).astype(BF16)], axis=1)
    h, logits_t = _out_proj(h, y_mix, w_out.astype(BF16), norm_gain[None, :], wr_split, br[None, :])
    bucket, gate, rank, counts = _route(logits_t)

    counts = counts[:, 0]
    padded = (counts + MOE_BLOCK - 1) // MOE_BLOCK * MOE_BLOCK
    pend = jnp.cumsum(padded)
    pstart = pend - padded
    ids = jnp.arange(MOE_BUCKET_ROWS, dtype=I32)
    dest = rank[0] + jnp.sum(jnp.where(bucket == ids[:, None], pstart[:, None], 0), axis=0)
    n_blocks = -(-t // MOE_BLOCK) + MOE_BUCKETS
    block_start = jnp.arange(n_blocks, dtype=I32) * MOE_BLOCK
    block_bucket = jnp.minimum(jnp.sum((pend[None, :] <= block_start[:, None]).astype(I32), axis=1),
                               MOE_BUCKET_ROWS - 1)
    n_used = (pend[-1:] // MOE_BLOCK).astype(I32)
    lo, hi = _bucket_experts()
    gate8 = jnp.concatenate([gate, jnp.zeros((6, t), F32)], axis=0)

    xs = _dispatch(jnp.concatenate([pend, padded]).astype(I32), dest.reshape(t // TOKEN_TILE, 1, TOKEN_TILE), gate8,
                   h, norm_gain[None, :], n_blocks * MOE_BLOCK)
    ys = _experts(lo[block_bucket], hi[block_bucket], n_used, xs, w1, w3, w2, layer)
    return h, dest, ys


def kernel(x, meta, norm_mix, norm_ffn, norm_final, ret_w_in, ret_w_out, ret_norm, ssm_w_in, ssm_conv_w,
           ssm_conv_b, ssm_dt_bias, ssm_a_log, ssm_d, ssm_norm, ssm_w_out, moe_wg, moe_bg, moe_we, moe_be,
           moe_w1, moe_w3, moe_w2):
    b, seq, d = x.shape
    lp = CHUNK + seq
    t = b * lp
    depth = norm_mix.shape[0]

    half = RET_QK_DIM // 2
    inv_freq = 1.0 / (RET_ROPE_BASE ** jnp.linspace(0.0, 1.0, half, dtype=F32))
    ang = (jnp.arange(lp, dtype=F32) - PAD)[:, None] * inv_freq[None, :]
    cos, sin = jnp.cos(ang), jnp.sin(ang)
    perm = jnp.concatenate([jnp.arange(0, RET_QK_DIM, 2), jnp.arange(1, RET_QK_DIM, 2)])
    qk_perm = (jnp.arange(2 * RET_HEADS)[:, None] * RET_QK_DIM + perm[None, :]).reshape(-1)

    h = None
    moe = None
    for i in range(depth):
        j = i // 2
        if moe is not None:
            h, xn = _combine_norm(moe[0].reshape(t // TOKEN_TILE, 1, TOKEN_TILE), moe[1], h, norm_mix[i][None, :])
        if i % 2 == 0:
            w_in = jnp.concatenate([ret_w_in[j][:, qk_perm], ret_w_in[j][:, 2 * RET_QK:]], axis=1).astype(BF16)
            if i == 0:
                h, qkvg = _ret_in_first(x, meta, norm_mix[i][None, :], cos, sin, w_in)
                h = h.reshape(t, d)
            else:
                qkvg = _ret_in(h.reshape(b, lp, d), norm_mix[i][None, :], cos, sin, w_in)
            y = _retention(qkvg, ret_norm[j][None, :])
            w_out = ret_w_out[j]
        else:
            w_in = jnp.concatenate([ssm_w_in[j], jnp.zeros((d, LANES - SSM_HEADS), F32)], axis=1).astype(BF16)
            zxbc, dt = _ssm_in(xn, w_in, ssm_conv_w[j], ssm_conv_b[j][None, :], lp)
            lane_pad = lambda v: jnp.concatenate([v, jnp.zeros((LANES - SSM_HEADS,), F32)])[None, :]
            y = _ssd(zxbc.reshape(b, lp, SSM_MAIN), dt.reshape(b, lp, LANES), lane_pad(ssm_dt_bias[j]),
                     lane_pad(ssm_a_log[j]), jnp.repeat(ssm_d[j], SSM_HEAD_DIM)[None, :], ssm_norm[j][None, :])
            w_out = ssm_w_out[j]
        h, *moe = _moe_layer(h, y.reshape(t, -1), w_out, norm_ffn[i], moe_wg[i], moe_bg[i], moe_we[i],
                             moe_be[i], moe_w1, moe_w3, moe_w2, i)
    return _final_norm(h.reshape(b, lp, d), moe[0].reshape(t // CHUNK, 1, CHUNK), moe[1], norm_final[None, :], seq)
```

```python
import functools
import math

import jax
import jax.numpy as jnp
from jax import lax
from jax.experimental import pallas as pl
from jax.experimental.pallas import tpu as pltpu

F32 = jnp.float32
BF16 = jnp.bfloat16
I32 = jnp.int32

D_MODEL = 1024
N_META = 16
CHUNK = 128
PAD = CHUNK - N_META
EPS = 1e-6

RET_HEADS = 4
RET_QK_DIM = 256
RET_V_DIM = 512
RET_QK = RET_HEADS * RET_QK_DIM
RET_V = RET_HEADS * RET_V_DIM
RET_IN = 2 * RET_QK + 2 * RET_V
RET_ROPE_BASE = 10000.0

SSM_INNER = 2048
SSM_HEAD_DIM = 64
SSM_HEADS = 32
SSM_GROUPS = 8
SSM_HPG = 4
SSM_STATE = 128
SSM_CONV = 4
SSM_BC = SSM_GROUPS * SSM_STATE
SSM_CONV_DIM = SSM_INNER + 2 * SSM_BC
SSM_MAIN = SSM_INNER + SSM_CONV_DIM
LANES = 128
SUBLANES = 8
SSM_GROUP_W = SSM_HPG * SSM_HEAD_DIM

MOE_GROUPS = 4
MOE_EPG = 8
MOE_EXPERTS = 32
MOE_FF = 512
MOE_BLOCK = 256
MOE_PAIRS = MOE_EPG * (MOE_EPG - 1) // 2
MOE_BUCKETS = MOE_GROUPS * MOE_PAIRS
MOE_BUCKET_ROWS = 128
ROUTER_ROWS = 40

ROW_DMA_UNROLL = 16
SSM_IN_STEP = 256
SSM_IN_ROWS = 64
LOG2E = math.log2(math.e)
TOKEN_TILE = 512
VMEM_LIMIT = 56 * 1024 * 1024


def _cparams(*sem):
    return pltpu.CompilerParams(dimension_semantics=sem, vmem_limit_bytes=VMEM_LIMIT)


def _rms(x, gain):
    ms = jnp.mean(x * x, axis=-1, keepdims=True)
    return x * lax.rsqrt(ms + EPS) * gain


def _silu(x):
    hx = 0.5 * x
    return hx + hx * jnp.tanh(hx)


def _dot(a, b):
    return jnp.dot(a, b, preferred_element_type=F32)


def _dot_nt(a, b, precision=None):
    return lax.dot_general(a, b, (((1,), (1,)), ((), ())), preferred_element_type=F32, precision=precision)


def _dot_tn(a, b):
    return lax.dot_general(a, b, (((0,), (0,)), ((), ())), preferred_element_type=F32)


def _row_tile(rows, cap=544):
    best = 16
    for t in range(16, cap + 1, 16):
        if rows % t == 0:
            best = t
    return best


def _ret_in_kernel(h_ref, g_ref, cos_ref, sin_ref, w_ref, o_ref):
    _ret_project(h_ref[0], g_ref, cos_ref, sin_ref, w_ref, o_ref)


def _ret_in_first_kernel(meta_ref, g_ref, cos_ref, sin_ref, w_ref, x_ref, ho_ref, o_ref, hbuf, sems):
    nj = pl.num_programs(1)
    step = pl.program_id(0) * nj + pl.program_id(1)
    last = pl.num_programs(0) * nj - 1
    slot = lax.rem(step, 2)
    tm = hbuf.shape[1]

    def rows_copy(s, slot_):
        i, j = s // nj, lax.rem(s, nj)
        first = pltpu.make_async_copy(
            x_ref.at[i, pl.ds(0, tm - CHUNK)], hbuf.at[slot_, pl.ds(CHUNK, tm - CHUNK)], sems.at[slot_])
        start = pl.multiple_of(jnp.maximum(j * tm - CHUNK, 0), SUBLANES)
        other = pltpu.make_async_copy(x_ref.at[i, pl.ds(start, tm)], hbuf.at[slot_], sems.at[slot_])
        return j == 0, first, other

    def fetch(s, slot_):
        is_first, first, other = rows_copy(s, slot_)
        pl.when(is_first)(first.start)
        pl.when(jnp.logical_not(is_first))(other.start)

    @pl.when(step == 0)
    def _():
        fetch(step, slot)

    @pl.when(step < last)
    def _():
        fetch(step + 1, 1 - slot)

    is_first, first, other = rows_copy(step, slot)
    pl.when(is_first)(first.wait)
    pl.when(jnp.logical_not(is_first))(other.wait)

    @pl.when(is_first)
    def _():
        hbuf[slot, 0:PAD, :] = jnp.zeros((PAD, hbuf.shape[2]), F32)
        hbuf[slot, PAD:CHUNK, :] = meta_ref[...]

    h = hbuf[slot]
    ho_ref[0] = h
    _ret_project(h, g_ref, cos_ref, sin_ref, w_ref, o_ref)


def _ret_project(h, g_ref, cos_ref, sin_ref, w_ref, o_ref):
    xn = _rms(h, g_ref[...]).astype(BF16)
    cos = cos_ref[...]
    sin = sin_ref[...]
    half = RET_QK_DIM // 2
    for j in range(2 * RET_HEADS):
        c0 = j * RET_QK_DIM
        acc = _dot(xn, w_ref[:, c0:c0 + RET_QK_DIM])
        if j >= RET_HEADS:
            acc = acc * (RET_QK_DIM ** -0.5)
        e = acc[:, :half]
        o = acc[:, half:]
        o_ref[0, :, c0:c0 + half] = (e * cos - o * sin).astype(BF16)
        o_ref[0, :, c0 + half:c0 + RET_QK_DIM] = (o * cos + e * sin).astype(BF16)
    step = 1024
    for c0 in range(2 * RET_QK, RET_IN, step):
        o_ref[0, :, c0:c0 + step] = _dot(xn, w_ref[:, c0:c0 + step]).astype(BF16)


def _ret_in_first(x, meta, gain, cos, sin, w):
    b, seq, d = x.shape
    lp = CHUNK + seq
    tm = _row_tile(lp)
    assert tm > CHUNK
    const = lambda r, c: pl.BlockSpec((r, c), lambda i, j: (0, 0))
    return pl.pallas_call(
        _ret_in_first_kernel,
        grid=(b, lp // tm),
        in_specs=[
            const(N_META, d), const(1, d),
            pl.BlockSpec((tm, RET_QK_DIM // 2), lambda i, j: (j, 0)),
            pl.BlockSpec((tm, RET_QK_DIM // 2), lambda i, j: (j, 0)),
            const(d, RET_IN),
            pl.BlockSpec(memory_space=pl.ANY),
        ],
        out_specs=[pl.BlockSpec((1, tm, d), lambda i, j: (i, j, 0)), pl.BlockSpec((1, tm, RET_IN), lambda i, j: (i, j, 0))],
        out_shape=[jax.ShapeDtypeStruct((b, lp, d), F32), jax.ShapeDtypeStruct((b, lp, RET_IN), BF16)],
        scratch_shapes=[pltpu.VMEM((2, tm, d), F32), pltpu.SemaphoreType.DMA((2,))],
        compiler_params=_cparams("arbitrary", "arbitrary"),
        name="ret_in_proj",
    )(meta, gain, cos, sin, w, x)


def _ret_in(h, gain, cos, sin, w):
    b, lp, d = h.shape
    tm = _row_tile(lp)
    return pl.pallas_call(
        _ret_in_kernel,
        grid=(b, lp // tm),
        in_specs=[
            pl.BlockSpec((1, tm, d), lambda i, j: (i, j, 0)),
            pl.BlockSpec((1, d), lambda i, j: (0, 0)),
            pl.BlockSpec((tm, RET_QK_DIM // 2), lambda i, j: (j, 0)),
            pl.BlockSpec((tm, RET_QK_DIM // 2), lambda i, j: (j, 0)),
            pl.BlockSpec((d, RET_IN), lambda i, j: (0, 0)),
        ],
        out_specs=pl.BlockSpec((1, tm, RET_IN), lambda i, j: (i, j, 0)),
        out_shape=jax.ShapeDtypeStruct((b, lp, RET_IN), BF16),
        compiler_params=_cparams("parallel", "parallel"),
        name="ret_in_proj",
    )(h, gain, cos, sin, w)


def _ret_kernel(q_ref, k_ref, v_ref, g_ref, hn_ref, o_ref, state_ref):
    @pl.when(pl.program_id(1) == 0)
    def _():
        state_ref[...] = jnp.zeros_like(state_ref)

    row = lax.broadcasted_iota(I32, (CHUNK, CHUNK), 0).astype(F32)
    col = lax.broadcasted_iota(I32, (CHUNK, CHUNK), 1).astype(F32)
    diff = row - col
    ridx = row[:, :1]
    for hh in range(RET_HEADS):
        lg = math.log(1.0 - 2.0 ** (-5.0 - hh))
        intra = jnp.where(diff >= 0, jnp.exp(lg * jnp.maximum(diff, 0.0)), 0.0)
        qdec = jnp.exp(lg * (ridx + 1.0))
        kdec = jnp.exp(lg * (CHUNK - 1.0 - ridx))
        cdec = math.exp(lg * CHUNK)
        qh = q_ref[0, :, hh * RET_QK_DIM:(hh + 1) * RET_QK_DIM]
        kh = k_ref[0, :, hh * RET_QK_DIM:(hh + 1) * RET_QK_DIM]
        vh = v_ref[0, :, hh * RET_V_DIM:(hh + 1) * RET_V_DIM]
        scores = _dot_nt(qh, kh) * intra
        st = state_ref[hh]
        y = _dot(scores.astype(BF16), vh) + _dot(qh, st.astype(BF16)) * qdec
        kd = (kh.astype(F32) * kdec).astype(BF16)
        state_ref[hh] = st * cdec + _dot_tn(kd, vh)
        yn = _rms(y, hn_ref[:, hh * RET_V_DIM:(hh + 1) * RET_V_DIM])
        gh = g_ref[0, :, hh * RET_V_DIM:(hh + 1) * RET_V_DIM].astype(F32)
        o_ref[0, :, hh * RET_V_DIM:(hh + 1) * RET_V_DIM] = (_silu(gh) * yn).astype(BF16)


def _retention(qkvg, head_norm):
    b, lp, _ = qkvg.shape
    nc = lp // CHUNK
    return pl.pallas_call(
        _ret_kernel,
        grid=(b, nc),
        in_specs=[
            pl.BlockSpec((1, CHUNK, RET_QK), lambda i, c: (i, c, 0)),
            pl.BlockSpec((1, CHUNK, RET_QK), lambda i, c: (i, c, 1)),
            pl.BlockSpec((1, CHUNK, RET_V), lambda i, c: (i, c, 1)),
            pl.BlockSpec((1, CHUNK, RET_V), lambda i, c: (i, c, 2)),
            pl.BlockSpec((1, RET_V), lambda i, c: (0, 0)),
        ],
        out_specs=pl.BlockSpec((1, CHUNK, RET_V), lambda i, c: (i, c, 0)),
        out_shape=jax.ShapeDtypeStruct((b, lp, RET_V), BF16),
        scratch_shapes=[pltpu.VMEM((RET_HEADS, RET_QK_DIM, RET_V_DIM), F32)],
        compiler_params=_cparams("parallel", "arbitrary"),
        name="retention",
    )(qkvg, qkvg, qkvg, qkvg, head_norm)


def _ssm_in_kernel(xn_ref, w_ref, cw_ref, cb_ref, o_ref, dt_ref, tail_ref, acc_ref, *, rows_per_seq):
    i = pl.program_id(0)
    tm = xn_ref.shape[0]

    @pl.when(i == 0)
    def _():
        tail_ref[...] = jnp.zeros_like(tail_ref)

    step = acc_ref.shape[2]
    n_steps = SSM_MAIN // step
    rb = SSM_IN_ROWS

    pos = lax.rem(i * tm, rows_per_seq) + lax.broadcasted_iota(I32, (tm, 1), 0)
    is_pad = jnp.logical_or(pos < PAD, jnp.logical_and(pos >= rows_per_seq, pos < rows_per_seq + PAD))

    halo = SUBLANES

    def project(j):
        c0 = j * step
        acc = _dot(xn_ref[...], w_ref[:, c0:c0 + step])
        acc_ref[j % 2, halo:, :] = acc if c0 < SSM_INNER else jnp.where(is_pad, 0.0, acc)

    project(0)
    for j in range(n_steps):
        if j + 1 < n_steps:
            project(j + 1)
        c0 = j * step
        s = j % 2
        if c0 < SSM_INNER:
            for r0 in range(0, tm, rb):
                o_ref[r0:r0 + rb, c0:c0 + step] = _silu(acc_ref[s, halo + r0:halo + r0 + rb, :]).astype(BF16)
            continue
        cc = c0 - SSM_INNER
        acc_ref[s, 0:halo, :] = tail_ref[:, cc:cc + step]
        tail_ref[:, cc:cc + step] = acc_ref[s, tm:tm + halo, :]
        taps = [cw_ref[k:k + 1, cc:cc + step] for k in range(SSM_CONV)]
        bias = cb_ref[:, cc:cc + step]
        for r0 in range(0, tm, rb):
            ext = acc_ref[s, r0:r0 + rb + halo, :]
            back1 = pltpu.roll(ext, 1, axis=0)
            back23 = pltpu.roll(taps[1] * ext + taps[0] * back1, 2, axis=0)
            conv = bias + taps[3] * ext + taps[2] * back1 + back23
            o_ref[r0:r0 + rb, c0:c0 + step] = _silu(conv[halo:]).astype(BF16)
    dt_ref[...] = _dot(xn_ref[...], w_ref[:, SSM_MAIN:SSM_MAIN + LANES])


def _ssm_in(xn, w, conv_w, conv_b, rows_per_seq):
    t, d = xn.shape
    tm = TOKEN_TILE
    assert tm <= rows_per_seq
    const = lambda r, c: pl.BlockSpec((r, c), lambda i: (0, 0))
    return pl.pallas_call(
        functools.partial(_ssm_in_kernel, rows_per_seq=rows_per_seq),
        grid=(t // tm,),
        in_specs=[
            pl.BlockSpec((tm, d), lambda i: (i, 0)),
            const(d, SSM_MAIN + LANES), const(SSM_CONV, SSM_CONV_DIM), const(1, SSM_CONV_DIM),
        ],
        out_specs=[
            pl.BlockSpec((tm, SSM_MAIN), lambda i: (i, 0)),
            pl.BlockSpec((tm, LANES), lambda i: (i, 0)),
        ],
        out_shape=[jax.ShapeDtypeStruct((t, SSM_MAIN), BF16), jax.ShapeDtypeStruct((t, LANES), F32)],
        scratch_shapes=[pltpu.VMEM((SUBLANES, SSM_CONV_DIM), F32), pltpu.VMEM((2, tm + SUBLANES, SSM_IN_STEP), F32)],
        compiler_params=_cparams("arbitrary"),
        name="ssm_in_proj",
    )(xn, w, conv_w, conv_b)


def _ssd_kernel(zg_ref, x_ref, bc_ref, dt_ref, dtb_ref, alog_ref, dsk_ref, gn_ref, o_ref, state_ref):
    c = pl.program_id(1)

    @pl.when(c == 0)
    def _():
        state_ref[...] = jnp.zeros_like(state_ref)

    row = lax.broadcasted_iota(I32, (CHUNK, CHUNK), 0)
    col = lax.broadcasted_iota(I32, (CHUNK, CHUNK), 1)
    causal_bias = jnp.where(row >= col, 0.0, -jnp.inf)
    low_half = col < SSM_HEAD_DIM
    valid = row[:, :1] >= jnp.where(c > 0, 0, PAD)

    dtr = dt_ref[0] + dtb_ref[...]
    dtv = jnp.maximum(dtr, 0.0) + jnp.log(1.0 + jnp.exp(-jnp.abs(dtr)))
    dtv = jnp.where(valid, dtv, 0.0)
    da = dtv * (-jnp.exp(alog_ref[...]))
    tril = jnp.where(row >= col, 1.0, 0.0)
    acs = jnp.dot(tril, da, preferred_element_type=F32, precision=lax.Precision.HIGHEST)
    last = acs[CHUNK - 1:CHUNK, :]
    acs2 = acs * LOG2E
    src_t = (jnp.log2(dtv) - acs2).T
    w_t = (dtv * jnp.exp(last - acs)).T
    elast = jnp.exp(last)

    def pair_row(v, h0):
        return jnp.where(low_half[:1], v[:, h0:h0 + 1], v[:, h0 + 1:h0 + 2])

    groups = range(SSM_GROUPS)
    gsl = [slice(gi * SSM_GROUP_W, (gi + 1) * SSM_GROUP_W) for gi in groups]
    bgs = [bc_ref[0, :, gi * SSM_STATE:(gi + 1) * SSM_STATE] for gi in groups]
    cgs = [bc_ref[0, :, SSM_BC + gi * SSM_STATE:SSM_BC + (gi + 1) * SSM_STATE] for gi in groups]
    sts = [state_ref[gi] for gi in groups]
    cbs = [_dot_nt(cgs[gi], bgs[gi]) for gi in groups]
    yoffs = [_dot(cgs[gi], sts[gi].astype(BF16)) for gi in groups]
    bts = [bgs[gi].astype(F32).T for gi in groups]

    outs = []
    eexp = []
    for gi in groups:
        for pp in range(SSM_HPG // 2):
            h0 = gi * SSM_HPG + 2 * pp
            top, bot, ecols = [], [], []
            for hd in (h0, h0 + 1):
                acs_b = jnp.broadcast_to(acs2[:, hd:hd + 1], (CHUNK, CHUNK))
                top.append(cbs[gi] * jnp.exp2(acs_b + (src_t[hd:hd + 1, :] + causal_bias)))
                bot.append(bts[gi] * w_t[hd:hd + 1, :])
                ecols.append(jnp.exp2(acs_b))
            lhs = jnp.concatenate([jnp.concatenate(top, axis=1), jnp.concatenate(bot, axis=1)], axis=0)
            xp = x_ref[0, :, h0 * SSM_HEAD_DIM:(h0 + 2) * SSM_HEAD_DIM]
            zero = jnp.zeros_like(xp)
            rhs = jnp.concatenate([jnp.where(low_half, xp, zero), jnp.where(low_half, zero, xp)], axis=0)
            outs.append(_dot(lhs.astype(BF16), rhs))
            eexp.append(jnp.where(low_half, ecols[0], ecols[1]))

    for gi in groups:
        pairs = [gi * (SSM_HPG // 2) + pp for pp in range(SSM_HPG // 2)]
        heads = [gi * SSM_HPG + 2 * pp for pp in range(SSM_HPG // 2)]
        lexp = jnp.concatenate([pair_row(elast, h0) for h0 in heads], axis=1)
        state_ref[gi] = sts[gi] * lexp + jnp.concatenate([outs[p][CHUNK:] for p in pairs], axis=1)
        xg = x_ref[0, :, gsl[gi]].astype(F32)
        y = jnp.concatenate([outs[p][:CHUNK] for p in pairs], axis=1)
        y = y + yoffs[gi] * jnp.concatenate([eexp[p] for p in pairs], axis=1)
        y = y + dsk_ref[:, gsl[gi]] * xg
        y = y * zg_ref[0, :, gsl[gi]].astype(F32)
        o_ref[0, :, gsl[gi]] = _rms(y, gn_ref[:, gsl[gi]]).astype(BF16)


def _ssd(zxbc, dt, dt_bias, a_log, d_skip, gate_norm):
    b, lp, _ = zxbc.shape
    nc = lp // CHUNK
    wide = lambda j: pl.BlockSpec((1, CHUNK, SSM_INNER), lambda i, c: (i, c, j))
    const = lambda r, w: pl.BlockSpec((r, w), lambda i, c: (0, 0))
    return pl.pallas_call(
        _ssd_kernel,
        grid=(b, nc),
        in_specs=[
            wide(0), wide(1), wide(2),
            pl.BlockSpec((1, CHUNK, LANES), lambda i, c: (i, c, 0)),
            const(1, LANES), const(1, LANES), const(1, SSM_INNER), const(1, SSM_INNER),
        ],
        out_specs=pl.BlockSpec((1, CHUNK, SSM_INNER), lambda i, c: (i, c, 0)),
        out_shape=jax.ShapeDtypeStruct((b, lp, SSM_INNER), BF16),
        scratch_shapes=[pltpu.VMEM((SSM_GROUPS, SSM_STATE, SSM_GROUP_W), F32)],
        compiler_params=_cparams("parallel", "arbitrary"),
        name="ssd",
    )(zxbc, zxbc, zxbc, dt, dt_bias, a_log, d_skip, gate_norm)


def _out_proj_kernel(h_ref, y_ref, w_ref, g_ref, wr_ref, br_ref, ho_ref, lg_ref):
    hn = h_ref[...] + _dot(y_ref[...], w_ref[...])
    ho_ref[...] = hn
    u = _rms(hn, g_ref[...])
    u_hi = u.astype(BF16)
    u_lo = (u - u_hi.astype(F32)).astype(BF16)
    p = _dot(u_hi, wr_ref[...])
    logits = p[:, :LANES] + (p[:, LANES:] + _dot(u_lo, wr_ref[:, :LANES])) + br_ref[...]
    lg_ref[...] = logits.T[:ROUTER_ROWS]


def _out_proj(h, y, w, gain, wr_t, br):
    t, d = h.shape
    tm = TOKEN_TILE
    kin = y.shape[1]
    return pl.pallas_call(
        _out_proj_kernel,
        grid=(t // tm,),
        in_specs=[
            pl.BlockSpec((tm, d), lambda i: (i, 0)),
            pl.BlockSpec((tm, kin), lambda i: (i, 0)),
            pl.BlockSpec((kin, d), lambda i: (0, 0)),
            pl.BlockSpec((1, d), lambda i: (0, 0)),
            pl.BlockSpec((d, 2 * LANES), lambda i: (0, 0)),
            pl.BlockSpec((1, LANES), lambda i: (0, 0)),
        ],
        out_specs=[
            pl.BlockSpec((tm, d), lambda i: (i, 0)),
            pl.BlockSpec((ROUTER_ROWS, tm), lambda i: (0, i)),
        ],
        out_shape=[jax.ShapeDtypeStruct((t, d), F32), jax.ShapeDtypeStruct((ROUTER_ROWS, t), F32)],
        compiler_params=_cparams("parallel"),
        name="out_proj_router",
    )(h, y, w, gain, wr_t, br)


def _first_argmax(v, n):
    ridx = lax.broadcasted_iota(I32, v.shape, 0).astype(F32)
    vmax = jnp.max(v, axis=0, keepdims=True)
    idx = jnp.min(jnp.where(v == vmax, ridx, float(n)), axis=0, keepdims=True)
    return vmax, idx.astype(I32)


def _route_kernel(lg_ref, bkt_ref, gate_ref, rank_ref, cnt_ref, carry_ref):
    @pl.when(pl.program_id(0) == 0)
    def _():
        carry_ref[...] = jnp.zeros_like(carry_ref)

    logits = lg_ref[...]
    tm = logits.shape[1]
    gl = logits[0:MOE_GROUPS]
    gmax, gsel = _first_argmax(gl, MOE_GROUPS)
    p_group = 1.0 / jnp.sum(jnp.exp(gl - gmax), axis=0, keepdims=True)
    el = logits[MOE_GROUPS:MOE_GROUPS + MOE_EPG]
    for gg in range(1, MOE_GROUPS):
        el = jnp.where(gsel == gg, logits[MOE_GROUPS + gg * MOE_EPG:MOE_GROUPS + (gg + 1) * MOE_EPG], el)
    ex = jnp.exp(el - jnp.max(el, axis=0, keepdims=True))
    p = ex / jnp.sum(ex, axis=0, keepdims=True)
    p1, i1 = _first_argmax(p, MOE_EPG)
    ridx8 = lax.broadcasted_iota(I32, p.shape, 0)
    p2, i2 = _first_argmax(jnp.where(ridx8 == i1, -1.0, p), MOE_EPG)
    denom = p1 + p2
    gate1 = p_group * p1 / denom
    gate2 = p_group * p2 / denom
    first_low = i1 < i2
    gate_ref[0:1, :] = jnp.where(first_low, gate1, gate2)
    gate_ref[1:2, :] = jnp.where(first_low, gate2, gate1)
    lo = jnp.minimum(i1, i2).astype(F32)
    hi = jnp.maximum(i1, i2).astype(F32)
    pair = lo * (2.0 * MOE_EPG - 1.0 - lo) * 0.5 + (hi - lo - 1.0)
    bucket = gsel * MOE_PAIRS + pair.astype(I32)
    bkt_ref[...] = bucket

    ridx = lax.broadcasted_iota(I32, (MOE_BUCKET_ROWS, tm), 0)
    onehot = jnp.where(ridx == bucket, 1.0, 0.0)
    before = jnp.where(lax.broadcasted_iota(I32, (tm, tm), 0) < lax.broadcasted_iota(I32, (tm, tm), 1), 1.0, 0.0)
    prefix = _dot(onehot.astype(BF16), before.astype(BF16))
    carry = carry_ref[...]
    rank_ref[...] = jnp.sum(onehot * (carry + prefix), axis=0, keepdims=True).astype(I32)
    carry = carry + jnp.sum(onehot, axis=1, keepdims=True)
    carry_ref[...] = carry
    cnt_ref[...] = jnp.broadcast_to(carry, cnt_ref.shape).astype(I32)


def _route(logits_t):
    t = logits_t.shape[1]
    tm = TOKEN_TILE
    rows = lambda r: pl.BlockSpec((r, tm), lambda i: (0, i))
    return pl.pallas_call(
        _route_kernel,
        grid=(t // tm,),
        in_specs=[rows(ROUTER_ROWS)],
        out_specs=[rows(1), rows(2), rows(1), pl.BlockSpec((MOE_BUCKET_ROWS, LANES), lambda i: (0, 0))],
        out_shape=[jax.ShapeDtypeStruct((1, t), I32), jax.ShapeDtypeStruct((2, t), F32),
                   jax.ShapeDtypeStruct((1, t), I32), jax.ShapeDtypeStruct((MOE_BUCKET_ROWS, LANES), I32)],
        scratch_shapes=[pltpu.VMEM((MOE_BUCKET_ROWS, 1), F32)],
        compiler_params=_cparams("arbitrary"),
        name="route_rank",
    )(logits_t)


def _dispatch_kernel(seg_ref, dest_ref, gate_ref, h_ref, g_ref, xs_ref, u_ref, zero_ref, sems):
    i = pl.program_id(0)
    n = pl.num_programs(0)
    slot = lax.rem(i, 2)
    tm, d = h_ref.shape

    def wait_rows(s):
        pltpu.make_async_copy(u_ref.at[s], xs_ref.at[pl.ds(0, tm)], sems.at[s]).wait()

    @pl.when(i == 0)
    def _():
        zero_ref[...] = jnp.zeros_like(zero_ref)

        def fill_block(start):
            fill = pltpu.make_async_copy(
                zero_ref, xs_ref.at[pl.ds(pl.multiple_of(start, MOE_BLOCK), MOE_BLOCK)], sems.at[0])
            fill.start()
            fill.wait()

        def fill_segment(e, carry):
            @pl.when(seg_ref[MOE_BUCKET_ROWS + e] > 0)
            def _():
                fill_block(seg_ref[e] - MOE_BLOCK)
            return carry

        lax.fori_loop(0, MOE_BUCKETS, fill_segment, 0)

        def fill_tail(blk, carry):
            fill_block(blk * MOE_BLOCK)
            return carry

        lax.fori_loop(seg_ref[MOE_BUCKET_ROWS - 1] // MOE_BLOCK, xs_ref.shape[0] // MOE_BLOCK, fill_tail, 0)

    @pl.when(i >= 2)
    def _():
        wait_rows(slot)

    u = _rms(h_ref[...], g_ref[...])
    half = d // 2
    as_bits = lambda v: lax.bitcast_convert_type(v.astype(BF16).astype(F32), jnp.uint32)
    words = (as_bits(u[:, half:]) & jnp.uint32(0xFFFF0000)) | (as_bits(u[:, :half]) >> 16)
    gates = jnp.concatenate([gate_ref[...], jnp.zeros((LANES - gate_ref.shape[0], tm), F32)], axis=0).T
    packed = jnp.concatenate([words, lax.bitcast_convert_type(gates, jnp.uint32),
                              jnp.zeros((tm, half - LANES), jnp.uint32)], axis=1)
    u_ref[slot] = packed.reshape(u_ref.shape[1:])

    def issue(q, carry):
        for k in range(ROW_DMA_UNROLL):
            t = q * ROW_DMA_UNROLL + k
            pltpu.make_async_copy(
                u_ref.at[slot, t], xs_ref.at[dest_ref[0, 0, t]], sems.at[slot]).start(priority=k % 2)
        return carry

    lax.fori_loop(0, tm // ROW_DMA_UNROLL, issue, 0)

    @pl.when(i == n - 1)
    def _():
        wait_rows(slot)

        @pl.when(n >= 2)
        def _():
            wait_rows(1 - slot)


def _dispatch(seg, dest, gate, h, gain, rows):
    t, d = h.shape
    tm = TOKEN_TILE
    sub = d // LANES
    return pl.pallas_call(
        _dispatch_kernel,
        grid_spec=pltpu.PrefetchScalarGridSpec(
            num_scalar_prefetch=1,
            grid=(t // tm,),
            in_specs=[
                pl.BlockSpec((1, 1, tm), lambda i, seg: (i, 0, 0), memory_space=pltpu.SMEM),
                pl.BlockSpec((SUBLANES, tm), lambda i, seg: (0, i)),
                pl.BlockSpec((tm, d), lambda i, seg: (i, 0)),
                pl.BlockSpec((1, d), lambda i, seg: (0, 0)),
            ],
            out_specs=pl.BlockSpec(memory_space=pl.ANY),
            scratch_shapes=[pltpu.VMEM((2, tm, sub, LANES), jnp.uint32), pltpu.VMEM((MOE_BLOCK, sub, LANES), jnp.uint32),
                            pltpu.SemaphoreType.DMA((2,))],
        ),
        out_shape=jax.ShapeDtypeStruct((rows, sub, LANES), jnp.uint32),
        compiler_params=_cparams("arbitrary"),
        name="moe_dispatch",
    )(seg, dest, gate, h, gain)


def _expert_kernel(ea_ref, eb_ref, nu_ref, xs_ref, w1a_ref, w3a_ref, w2a_ref, w1b_ref, w3b_ref, w2b_ref, ys_ref):
    used = pl.program_id(0) < nu_ref[0]

    @pl.when(used)
    def _():
        rows, sub, lanes = xs_ref.shape
        d = sub * lanes
        packed = xs_ref[...].reshape(rows, d)
        words = packed[:, :d // 2]
        lo = lax.bitcast_convert_type(words << 16, F32).astype(BF16)
        hi = lax.bitcast_convert_type(words & jnp.uint32(0xFFFF0000), F32).astype(BF16)
        x = jnp.concatenate([lo, hi], axis=1)
        gates = lax.bitcast_convert_type(packed[:, d // 2:d // 2 + LANES], F32)

        def ffn(w1_ref, w3_ref, w2_ref):
            hid = _silu(_dot(x, w1_ref[0, 0].astype(BF16))) * _dot(x, w3_ref[0, 0].astype(BF16))
            return _dot(hid.astype(BF16), w2_ref[0, 0].astype(BF16))

        y = ffn(w1a_ref, w3a_ref, w2a_ref) * gates[:, 0:1] + ffn(w1b_ref, w3b_ref, w2b_ref) * gates[:, 1:2]
        ys_ref[...] = y.reshape(rows, sub, lanes)

    @pl.when(jnp.logical_not(used))
    def _():
        ys_ref[...] = jnp.zeros_like(ys_ref)


def _experts(block_ea, block_eb, n_used, xs, w1, w3, w2, layer):
    rows, sub, lanes = xs.shape
    d = sub * lanes
    nb = rows // MOE_BLOCK
    w_in = lambda tbl: pl.BlockSpec((1, 1, d, MOE_FF), lambda i, ea, eb, nu: (layer, (ea, eb)[tbl][i], 0, 0))
    w_out = lambda tbl: pl.BlockSpec((1, 1, MOE_FF, d), lambda i, ea, eb, nu: (layer, (ea, eb)[tbl][i], 0, 0))
    return pl.pallas_call(
        _expert_kernel,
        grid_spec=pltpu.PrefetchScalarGridSpec(
            num_scalar_prefetch=3,
            grid=(nb,),
            in_specs=[
                pl.BlockSpec((MOE_BLOCK, sub, lanes), lambda i, ea, eb, nu: (jnp.minimum(i, nu[0] - 1), 0, 0)),
                w_in(0), w_in(0), w_out(0), w_in(1), w_in(1), w_out(1),
            ],
            out_specs=pl.BlockSpec((MOE_BLOCK, sub, lanes), lambda i, ea, eb, nu: (i, 0, 0)),
        ),
        out_shape=jax.ShapeDtypeStruct((rows, sub, lanes), F32),
        compiler_params=_cparams("arbitrary"),
        name="moe_experts",
    )(block_ea, block_eb, n_used, xs, w1, w3, w2, w1, w3, w2)


def _issue_row_gather(idx_ref, src_ref, dst_ref, sem, rows):
    def issue(q, carry):
        for k in range(ROW_DMA_UNROLL):
            r = q * ROW_DMA_UNROLL + k
            pltpu.make_async_copy(src_ref.at[idx_ref[0, 0, r]], dst_ref.at[r], sem).start(priority=k % 2)
        return carry

    lax.fori_loop(0, rows // ROW_DMA_UNROLL, issue, 0)


def _combine_kernel(dest_ref, next_ref, h_ref, g_ref, ys_ref, ho_ref, xn_ref, ybuf, sems):
    i = pl.program_id(0)
    slot = lax.rem(i, 2)
    tm, d = h_ref.shape

    @pl.when(i == 0)
    def _():
        _issue_row_gather(dest_ref, ys_ref, ybuf.at[slot], sems.at[slot], tm)

    @pl.when(i + 1 < pl.num_programs(0))
    def _():
        _issue_row_gather(next_ref, ys_ref, ybuf.at[1 - slot], sems.at[1 - slot], tm)

    pltpu.make_async_copy(ys_ref.at[pl.ds(0, tm)], ybuf.at[slot], sems.at[slot]).wait()
    h = h_ref[...] + ybuf[slot].reshape(tm, d)
    ho_ref[...] = h
    xn_ref[...] = _rms(h, g_ref[...]).astype(BF16)


def _combine_norm(dest, ys, h, gain):
    n, _, tm = dest.shape
    t, d = h.shape
    smem = lambda fn: pl.BlockSpec((1, 1, tm), fn, memory_space=pltpu.SMEM)
    return pl.pallas_call(
        _combine_kernel,
        grid=(n,),
        in_specs=[
            smem(lambda i: (i, 0, 0)),
            smem(lambda i: (jnp.minimum(i + 1, n - 1), 0, 0)),
            pl.BlockSpec((tm, d), lambda i: (i, 0)),
            pl.BlockSpec((1, d), lambda i: (0, 0)),
            pl.BlockSpec(memory_space=pl.ANY),
        ],
        out_specs=[pl.BlockSpec((tm, d), lambda i: (i, 0)), pl.BlockSpec((tm, d), lambda i: (i, 0))],
        out_shape=[jax.ShapeDtypeStruct((t, d), F32), jax.ShapeDtypeStruct((t, d), BF16)],
        scratch_shapes=[pltpu.VMEM((2, tm, d // LANES, LANES), F32), pltpu.SemaphoreType.DMA((2,))],
        compiler_params=_cparams("arbitrary"),
        name="moe_combine",
    )(dest, dest, h, gain, ys)


def _final_kernel(*refs, per):
    cur, nxt, hs = refs[:per], refs[per:2 * per], refs[2 * per:3 * per]
    g_ref, ys_ref, o_ref, ybuf, sems = refs[3 * per:]
    step = pl.program_id(0) * pl.num_programs(1) + pl.program_id(1)
    last = pl.num_programs(0) * pl.num_programs(1) - 1
    slot = lax.rem(step, 2)
    d = o_ref.shape[2]

    def gather(idx_refs, s):
        for p, idx_ref in enumerate(idx_refs):
            _issue_row_gather(idx_ref, ys_ref, ybuf.at[s, pl.ds(p * CHUNK, CHUNK)], sems.at[s], CHUNK)

    @pl.when(step == 0)
    def _():
        gather(cur, slot)

    @pl.when(step < last)
    def _():
        gather(nxt, 1 - slot)

    pltpu.make_async_copy(ys_ref.at[pl.ds(0, per * CHUNK)], ybuf.at[slot], sems.at[slot]).wait()
    h = jnp.concatenate([h_ref[0] for h_ref in hs], axis=0)
    o_ref[0] = _rms(h + ybuf[slot].reshape(per * CHUNK, d), g_ref[...])


def _final_norm(h, dest, ys, gain, seq):
    b, lp, d = h.shape
    nc = lp // CHUNK
    ns = seq // CHUNK
    per = 2 if ns % 2 == 0 else 1
    steps = ns // per

    def chunk_of(step, p):
        return (step // steps) * nc + lax.rem(step, steps) * per + p + 1

    smem = lambda fn: pl.BlockSpec((1, 1, CHUNK), fn, memory_space=pltpu.SMEM)
    cur = [smem(lambda i, c, p=p: (chunk_of(i * steps + c, p), 0, 0)) for p in range(per)]
    nxt = [smem(lambda i, c, p=p: (chunk_of(jnp.minimum(i * steps + c + 1, b * steps - 1), p), 0, 0))
           for p in range(per)]
    hs = [pl.BlockSpec((1, CHUNK, d), lambda i, c, p=p: (i, c * per + p + 1, 0)) for p in range(per)]
    return pl.pallas_call(
        functools.partial(_final_kernel, per=per),
        grid=(b, steps),
        in_specs=cur + nxt + hs + [pl.BlockSpec((1, d), lambda i, c: (0, 0)), pl.BlockSpec(memory_space=pl.ANY)],
        out_specs=pl.BlockSpec((1, per * CHUNK, d), lambda i, c: (i, c, 0)),
        out_shape=jax.ShapeDtypeStruct((b, seq, d), F32),
        scratch_shapes=[pltpu.VMEM((2, per * CHUNK, d // LANES, LANES), F32), pltpu.SemaphoreType.DMA((2,))],
        compiler_params=_cparams("arbitrary", "arbitrary"),
        name="final_norm",
    )(*([dest] * (2 * per)), *([h] * per), gain, ys)


def _bucket_experts():
    lo, hi = [], []
    for g in range(MOE_GROUPS):
        for a in range(MOE_EPG):
            for c in range(a + 1, MOE_EPG):
                lo.append(g * MOE_EPG + a)
                hi.append(g * MOE_EPG + c)
    fill = MOE_BUCKET_ROWS - len(lo)
    return jnp.array(lo + [lo[-1]] * fill, I32), jnp.array(hi + [hi[-1]] * fill, I32)


def _moe_layer(h, y_mix, w_out, norm_gain, wg, bg, we, be, w1, w3, w2, layer):
    t, d = h.shape
    n_logits = MOE_GROUPS + MOE_EXPERTS
    wr = jnp.concatenate([wg, we.reshape(d, MOE_EXPERTS), jnp.zeros((d, LANES - n_logits), F32)], axis=1)
    br = jnp.concatenate([bg, be.reshape(-1), jnp.zeros((LANES - n_logits,), F32)])
    wr_hi = wr.astype(BF16)
    wr_split = jnp.concatenate([wr_hi, (wr - wr_hi.astype(F32)).astype(BF16)], axis=1)
    h, logits_t = _out_proj(h, y_mix, w_out.astype(BF16), norm_gain[None, :], wr_split, br[None, :])
    bucket, gate, rank, counts = _route(logits_t)

    counts = counts[:, 0]
    padded = (counts + MOE_BLOCK - 1) // MOE_BLOCK * MOE_BLOCK
    pend = jnp.cumsum(padded)
    pstart = pend - padded
    ids = jnp.arange(MOE_BUCKET_ROWS, dtype=I32)
    dest = rank[0] + jnp.sum(jnp.where(bucket == ids[:, None], pstart[:, None], 0), axis=0)
    n_blocks = -(-t // MOE_BLOCK) + MOE_BUCKETS
    block_start = jnp.arange(n_blocks, dtype=I32) * MOE_BLOCK
    block_bucket = jnp.minimum(jnp.sum((pend[None, :] <= block_start[:, None]).astype(I32), axis=1),
                               MOE_BUCKET_ROWS - 1)
    n_used = (pend[-1:] // MOE_BLOCK).astype(I32)
    lo, hi = _bucket_experts()
    gate8 = jnp.concatenate([gate, jnp.zeros((SUBLANES - gate.shape[0], t), F32)], axis=0)

    xs = _dispatch(jnp.concatenate([pend, padded]).astype(I32), dest.reshape(t // TOKEN_TILE, 1, TOKEN_TILE), gate8,
                   h, norm_gain[None, :], n_blocks * MOE_BLOCK)
    ys = _experts(lo[block_bucket], hi[block_bucket], n_used, xs, w1, w3, w2, layer)
    return h, dest, ys


def kernel(x, meta, norm_mix, norm_ffn, norm_final, ret_w_in, ret_w_out, ret_norm, ssm_w_in, ssm_conv_w,
           ssm_conv_b, ssm_dt_bias, ssm_a_log, ssm_d, ssm_norm, ssm_w_out, moe_wg, moe_bg, moe_we, moe_be,
           moe_w1, moe_w3, moe_w2):
    b, seq, d = x.shape
    lp = CHUNK + seq
    t = b * lp
    depth = norm_mix.shape[0]

    half = RET_QK_DIM // 2
    inv_freq = 1.0 / (RET_ROPE_BASE ** jnp.linspace(0.0, 1.0, half, dtype=F32))
    ang = (jnp.arange(lp, dtype=F32) - PAD)[:, None] * inv_freq[None, :]
    cos, sin = jnp.cos(ang), jnp.sin(ang)
    perm = jnp.concatenate([jnp.arange(0, RET_QK_DIM, 2), jnp.arange(1, RET_QK_DIM, 2)])
    qk_perm = (jnp.arange(2 * RET_HEADS)[:, None] * RET_QK_DIM + perm[None, :]).reshape(-1)

    h = None
    moe = None
    for i in range(depth):
        j = i // 2
        if moe is not None:
            h, xn = _combine_norm(moe[0].reshape(t // TOKEN_TILE, 1, TOKEN_TILE), moe[1], h, norm_mix[i][None, :])
        if i % 2 == 0:
            w_in = jnp.concatenate([ret_w_in[j][:, qk_perm], ret_w_in[j][:, 2 * RET_QK:]], axis=1).astype(BF16)
            if i == 0:
                h, qkvg = _ret_in_first(x, meta, norm_mix[i][None, :], cos, sin, w_in)
                h = h.reshape(t, d)
            else:
                qkvg = _ret_in(h.reshape(b, lp, d), norm_mix[i][None, :], cos, sin, w_in)
            y = _retention(qkvg, ret_norm[j][None, :])
            w_out = ret_w_out[j]
        else:
            w_in = jnp.concatenate([ssm_w_in[j], jnp.zeros((d, LANES - SSM_HEADS), F32)], axis=1).astype(BF16)
            zxbc, dt = _ssm_in(xn, w_in, ssm_conv_w[j], ssm_conv_b[j][None, :], lp)
            lane_pad = lambda v: jnp.concatenate([v, jnp.zeros((LANES - SSM_HEADS,), F32)])[None, :]
            y = _ssd(zxbc.reshape(b, lp, SSM_MAIN), dt.reshape(b, lp, LANES), lane_pad(ssm_dt_bias[j]),
                     lane_pad(ssm_a_log[j]), jnp.repeat(ssm_d[j], SSM_HEAD_DIM)[None, :], ssm_norm[j][None, :])
            w_out = ssm_w_out[j]
        h, *moe = _moe_layer(h, y.reshape(t, -1), w_out, norm_ffn[i], moe_wg[i], moe_bg[i], moe_we[i],
                             moe_be[i], moe_w1, moe_w3, moe_w2, i)
    return _final_norm(h.reshape(b, lp, d), moe[0].reshape(t // CHUNK, 1, CHUNK), moe[1], norm_final[None, :], seq)
```

```python
import functools
import math

import jax
import jax.numpy as jnp
from jax import lax
from jax.experimental import pallas as pl
from jax.experimental.pallas import tpu as pltpu

F32 = jnp.float32
BF16 = jnp.bfloat16
I32 = jnp.int32

D_MODEL = 1024
N_META = 16
CHUNK = 128
PAD = CHUNK - N_META
EPS = 1e-6

RET_HEADS = 4
RET_QK_DIM = 256
RET_V_DIM = 512
RET_QK = RET_HEADS * RET_QK_DIM
RET_V = RET_HEADS * RET_V_DIM
RET_IN = 2 * RET_QK + 2 * RET_V
RET_ROPE_BASE = 10000.0

SSM_INNER = 2048
SSM_HEAD_DIM = 64
SSM_HEADS = 32
SSM_GROUPS = 8
SSM_HPG = 4
SSM_STATE = 128
SSM_CONV = 4
SSM_BC = SSM_GROUPS * SSM_STATE
SSM_CONV_DIM = SSM_INNER + 2 * SSM_BC
SSM_MAIN = SSM_INNER + SSM_CONV_DIM
LANES = 128
SUBLANES = 8
SSM_GROUP_W = SSM_HPG * SSM_HEAD_DIM

MOE_GROUPS = 4
MOE_EPG = 8
MOE_EXPERTS = 32
MOE_FF = 512
MOE_BLOCK = 256
MOE_PAIRS = MOE_EPG * (MOE_EPG - 1) // 2
MOE_BUCKETS = MOE_GROUPS * MOE_PAIRS
MOE_BUCKET_ROWS = 128
ROUTER_ROWS = 40

ROW_DMA_UNROLL = 16
SSM_IN_STEP = 256
SSM_IN_ROWS = 64
LOG2E = math.log2(math.e)
TOKEN_TILE = 512
ROW_COPY_TILE = 1024
VMEM_LIMIT = 56 * 1024 * 1024


def _cparams(*sem):
    return pltpu.CompilerParams(dimension_semantics=sem, vmem_limit_bytes=VMEM_LIMIT)


def _rms(x, gain):
    ms = jnp.mean(x * x, axis=-1, keepdims=True)
    return x * lax.rsqrt(ms + EPS) * gain


def _silu(x):
    hx = 0.5 * x
    return hx + hx * jnp.tanh(hx)


def _dot(a, b):
    return jnp.dot(a, b, preferred_element_type=F32)


def _dot_nt(a, b, precision=None):
    return lax.dot_general(a, b, (((1,), (1,)), ((), ())), preferred_element_type=F32, precision=precision)


def _dot_tn(a, b):
    return lax.dot_general(a, b, (((0,), (0,)), ((), ())), preferred_element_type=F32)


def _row_tile(rows, cap=544):
    best = 16
    for t in range(16, cap + 1, 16):
        if rows % t == 0:
            best = t
    return best


def _ret_in_kernel(h_ref, g_ref, cos_ref, sin_ref, w_ref, o_ref):
    _ret_project(h_ref[0], g_ref, cos_ref, sin_ref, w_ref, o_ref)


def _ret_in_first_kernel(meta_ref, g_ref, cos_ref, sin_ref, w_ref, x_ref, ho_ref, o_ref, hbuf, sems):
    nj = pl.num_programs(1)
    step = pl.program_id(0) * nj + pl.program_id(1)
    last = pl.num_programs(0) * nj - 1
    slot = lax.rem(step, 2)
    tm = hbuf.shape[1]

    def rows_copy(s, slot_):
        i, j = s // nj, lax.rem(s, nj)
        first = pltpu.make_async_copy(
            x_ref.at[i, pl.ds(0, tm - CHUNK)], hbuf.at[slot_, pl.ds(CHUNK, tm - CHUNK)], sems.at[slot_])
        start = pl.multiple_of(jnp.maximum(j * tm - CHUNK, 0), SUBLANES)
        other = pltpu.make_async_copy(x_ref.at[i, pl.ds(start, tm)], hbuf.at[slot_], sems.at[slot_])
        return j == 0, first, other

    def fetch(s, slot_):
        is_first, first, other = rows_copy(s, slot_)
        pl.when(is_first)(first.start)
        pl.when(jnp.logical_not(is_first))(other.start)

    @pl.when(step == 0)
    def _():
        fetch(step, slot)

    @pl.when(step < last)
    def _():
        fetch(step + 1, 1 - slot)

    is_first, first, other = rows_copy(step, slot)
    pl.when(is_first)(first.wait)
    pl.when(jnp.logical_not(is_first))(other.wait)

    @pl.when(is_first)
    def _():
        hbuf[slot, 0:PAD, :] = jnp.zeros((PAD, hbuf.shape[2]), F32)
        hbuf[slot, PAD:CHUNK, :] = meta_ref[...]

    h = hbuf[slot]
    ho_ref[0] = h
    _ret_project(h, g_ref, cos_ref, sin_ref, w_ref, o_ref)


def _ret_project(h, g_ref, cos_ref, sin_ref, w_ref, o_ref):
    xn = _rms(h, g_ref[...]).astype(BF16)
    cos = cos_ref[...]
    sin = sin_ref[...]
    half = RET_QK_DIM // 2
    for j in range(2 * RET_HEADS):
        c0 = j * RET_QK_DIM
        acc = _dot(xn, w_ref[:, c0:c0 + RET_QK_DIM])
        if j >= RET_HEADS:
            acc = acc * (RET_QK_DIM ** -0.5)
        e = acc[:, :half]
        o = acc[:, half:]
        o_ref[0, :, c0:c0 + half] = (e * cos - o * sin).astype(BF16)
        o_ref[0, :, c0 + half:c0 + RET_QK_DIM] = (o * cos + e * sin).astype(BF16)
    step = 1024
    for c0 in range(2 * RET_QK, RET_IN, step):
        o_ref[0, :, c0:c0 + step] = _dot(xn, w_ref[:, c0:c0 + step]).astype(BF16)


def _ret_in_first(x, meta, gain, cos, sin, w):
    b, seq, d = x.shape
    lp = CHUNK + seq
    tm = _row_tile(lp)
    assert tm > CHUNK
    const = lambda r, c: pl.BlockSpec((r, c), lambda i, j: (0, 0))
    return pl.pallas_call(
        _ret_in_first_kernel,
        grid=(b, lp // tm),
        in_specs=[
            const(N_META, d), const(1, d),
            pl.BlockSpec((tm, RET_QK_DIM // 2), lambda i, j: (j, 0)),
            pl.BlockSpec((tm, RET_QK_DIM // 2), lambda i, j: (j, 0)),
            const(d, RET_IN),
            pl.BlockSpec(memory_space=pl.ANY),
        ],
        out_specs=[pl.BlockSpec((1, tm, d), lambda i, j: (i, j, 0)), pl.BlockSpec((1, tm, RET_IN), lambda i, j: (i, j, 0))],
        out_shape=[jax.ShapeDtypeStruct((b, lp, d), F32), jax.ShapeDtypeStruct((b, lp, RET_IN), BF16)],
        scratch_shapes=[pltpu.VMEM((2, tm, d), F32), pltpu.SemaphoreType.DMA((2,))],
        compiler_params=_cparams("arbitrary", "arbitrary"),
        name="ret_in_proj",
    )(meta, gain, cos, sin, w, x)


def _ret_in(h, gain, cos, sin, w):
    b, lp, d = h.shape
    tm = _row_tile(lp)
    return pl.pallas_call(
        _ret_in_kernel,
        grid=(b, lp // tm),
        in_specs=[
            pl.BlockSpec((1, tm, d), lambda i, j: (i, j, 0)),
            pl.BlockSpec((1, d), lambda i, j: (0, 0)),
            pl.BlockSpec((tm, RET_QK_DIM // 2), lambda i, j: (j, 0)),
            pl.BlockSpec((tm, RET_QK_DIM // 2), lambda i, j: (j, 0)),
            pl.BlockSpec((d, RET_IN), lambda i, j: (0, 0)),
        ],
        out_specs=pl.BlockSpec((1, tm, RET_IN), lambda i, j: (i, j, 0)),
        out_shape=jax.ShapeDtypeStruct((b, lp, RET_IN), BF16),
        compiler_params=_cparams("parallel", "parallel"),
        name="ret_in_proj",
    )(h, gain, cos, sin, w)


def _ret_kernel(q_ref, k_ref, v_ref, g_ref, hn_ref, o_ref, state_ref):
    @pl.when(pl.program_id(1) == 0)
    def _():
        state_ref[...] = jnp.zeros_like(state_ref)

    row = lax.broadcasted_iota(I32, (CHUNK, CHUNK), 0).astype(F32)
    col = lax.broadcasted_iota(I32, (CHUNK, CHUNK), 1).astype(F32)
    diff = row - col
    ridx = row[:, :1]
    for hh in range(RET_HEADS):
        lg = math.log(1.0 - 2.0 ** (-5.0 - hh))
        intra = jnp.where(diff >= 0, jnp.exp(lg * jnp.maximum(diff, 0.0)), 0.0)
        qdec = jnp.exp(lg * (ridx + 1.0))
        kdec = jnp.exp(lg * (CHUNK - 1.0 - ridx))
        cdec = math.exp(lg * CHUNK)
        qh = q_ref[0, :, hh * RET_QK_DIM:(hh + 1) * RET_QK_DIM]
        kh = k_ref[0, :, hh * RET_QK_DIM:(hh + 1) * RET_QK_DIM]
        vh = v_ref[0, :, hh * RET_V_DIM:(hh + 1) * RET_V_DIM]
        scores = _dot_nt(qh, kh) * intra
        st = state_ref[hh]
        y = _dot(scores.astype(BF16), vh) + _dot(qh, st.astype(BF16)) * qdec
        kd = (kh.astype(F32) * kdec).astype(BF16)
        state_ref[hh] = st * cdec + _dot_tn(kd, vh)
        yn = _rms(y, hn_ref[:, hh * RET_V_DIM:(hh + 1) * RET_V_DIM])
        gh = g_ref[0, :, hh * RET_V_DIM:(hh + 1) * RET_V_DIM].astype(F32)
        o_ref[0, :, hh * RET_V_DIM:(hh + 1) * RET_V_DIM] = (_silu(gh) * yn).astype(BF16)


def _retention(qkvg, head_norm):
    b, lp, _ = qkvg.shape
    nc = lp // CHUNK
    return pl.pallas_call(
        _ret_kernel,
        grid=(b, nc),
        in_specs=[
            pl.BlockSpec((1, CHUNK, RET_QK), lambda i, c: (i, c, 0)),
            pl.BlockSpec((1, CHUNK, RET_QK), lambda i, c: (i, c, 1)),
            pl.BlockSpec((1, CHUNK, RET_V), lambda i, c: (i, c, 1)),
            pl.BlockSpec((1, CHUNK, RET_V), lambda i, c: (i, c, 2)),
            pl.BlockSpec((1, RET_V), lambda i, c: (0, 0)),
        ],
        out_specs=pl.BlockSpec((1, CHUNK, RET_V), lambda i, c: (i, c, 0)),
        out_shape=jax.ShapeDtypeStruct((b, lp, RET_V), BF16),
        scratch_shapes=[pltpu.VMEM((RET_HEADS, RET_QK_DIM, RET_V_DIM), F32)],
        compiler_params=_cparams("parallel", "arbitrary"),
        name="retention",
    )(qkvg, qkvg, qkvg, qkvg, head_norm)


def _ssm_in_kernel(xn_ref, w_ref, cw_ref, cb_ref, o_ref, dt_ref, tail_ref, acc_ref, *, rows_per_seq):
    i = pl.program_id(0)
    tm = xn_ref.shape[0]

    @pl.when(i == 0)
    def _():
        tail_ref[...] = jnp.zeros_like(tail_ref)

    step = acc_ref.shape[2]
    n_steps = SSM_MAIN // step
    rb = SSM_IN_ROWS

    pos = lax.rem(i * tm, rows_per_seq) + lax.broadcasted_iota(I32, (tm, 1), 0)
    is_pad = jnp.logical_or(pos < PAD, jnp.logical_and(pos >= rows_per_seq, pos < rows_per_seq + PAD))

    halo = SUBLANES

    def project(j):
        c0 = j * step
        acc = _dot(xn_ref[...], w_ref[:, c0:c0 + step])
        acc_ref[j % 2, halo:, :] = acc if c0 < SSM_INNER else jnp.where(is_pad, 0.0, acc)

    project(0)
    for j in range(n_steps):
        if j + 1 < n_steps:
            project(j + 1)
        c0 = j * step
        s = j % 2
        if c0 < SSM_INNER:
            for r0 in range(0, tm, rb):
                o_ref[r0:r0 + rb, c0:c0 + step] = _silu(acc_ref[s, halo + r0:halo + r0 + rb, :]).astype(BF16)
            continue
        cc = c0 - SSM_INNER
        acc_ref[s, 0:halo, :] = tail_ref[:, cc:cc + step]
        tail_ref[:, cc:cc + step] = acc_ref[s, tm:tm + halo, :]
        taps = [cw_ref[k:k + 1, cc:cc + step] for k in range(SSM_CONV)]
        bias = cb_ref[:, cc:cc + step]
        for r0 in range(0, tm, rb):
            ext = acc_ref[s, r0:r0 + rb + halo, :]
            back1 = pltpu.roll(ext, 1, axis=0)
            back23 = pltpu.roll(taps[1] * ext + taps[0] * back1, 2, axis=0)
            conv = bias + taps[3] * ext + taps[2] * back1 + back23
            o_ref[r0:r0 + rb, c0:c0 + step] = _silu(conv[halo:]).astype(BF16)
    dt_ref[...] = _dot(xn_ref[...], w_ref[:, SSM_MAIN:SSM_MAIN + LANES])


def _ssm_in(xn, w, conv_w, conv_b, rows_per_seq):
    t, d = xn.shape
    tm = TOKEN_TILE
    assert tm <= rows_per_seq
    const = lambda r, c: pl.BlockSpec((r, c), lambda i: (0, 0))
    return pl.pallas_call(
        functools.partial(_ssm_in_kernel, rows_per_seq=rows_per_seq),
        grid=(t // tm,),
        in_specs=[
            pl.BlockSpec((tm, d), lambda i: (i, 0)),
            const(d, SSM_MAIN + LANES), const(SSM_CONV, SSM_CONV_DIM), const(1, SSM_CONV_DIM),
        ],
        out_specs=[
            pl.BlockSpec((tm, SSM_MAIN), lambda i: (i, 0)),
            pl.BlockSpec((tm, LANES), lambda i: (i, 0)),
        ],
        out_shape=[jax.ShapeDtypeStruct((t, SSM_MAIN), BF16), jax.ShapeDtypeStruct((t, LANES), F32)],
        scratch_shapes=[pltpu.VMEM((SUBLANES, SSM_CONV_DIM), F32), pltpu.VMEM((2, tm + SUBLANES, SSM_IN_STEP), F32)],
        compiler_params=_cparams("arbitrary"),
        name="ssm_in_proj",
    )(xn, w, conv_w, conv_b)


def _ssd_kernel(zg_ref, x_ref, bc_ref, dt_ref, dtb_ref, alog_ref, dsk_ref, gn_ref, o_ref, state_ref):
    c = pl.program_id(1)

    @pl.when(c == 0)
    def _():
        state_ref[...] = jnp.zeros_like(state_ref)

    row = lax.broadcasted_iota(I32, (CHUNK, CHUNK), 0)
    col = lax.broadcasted_iota(I32, (CHUNK, CHUNK), 1)
    causal_bias = jnp.where(row >= col, 0.0, -jnp.inf)
    low_half = col < SSM_HEAD_DIM
    valid = row[:, :1] >= jnp.where(c > 0, 0, PAD)

    dtr = dt_ref[0] + dtb_ref[...]
    dtv = jnp.maximum(dtr, 0.0) + jnp.log(1.0 + jnp.exp(-jnp.abs(dtr)))
    dtv = jnp.where(valid, dtv, 0.0)
    da = dtv * (-jnp.exp(alog_ref[...]))
    tril = jnp.where(row >= col, 1.0, 0.0)
    acs = jnp.dot(tril, da, preferred_element_type=F32, precision=lax.Precision.HIGHEST)
    last = acs[CHUNK - 1:CHUNK, :]
    acs2 = acs * LOG2E
    src_t = (jnp.log2(dtv) - acs2).T
    w_t = (dtv * jnp.exp(last - acs)).T
    elast = jnp.exp(last)

    def pair_row(v, h0):
        return jnp.where(low_half[:1], v[:, h0:h0 + 1], v[:, h0 + 1:h0 + 2])

    groups = range(SSM_GROUPS)
    gsl = [slice(gi * SSM_GROUP_W, (gi + 1) * SSM_GROUP_W) for gi in groups]
    bgs = [bc_ref[0, :, gi * SSM_STATE:(gi + 1) * SSM_STATE] for gi in groups]
    cgs = [bc_ref[0, :, SSM_BC + gi * SSM_STATE:SSM_BC + (gi + 1) * SSM_STATE] for gi in groups]
    sts = [state_ref[gi] for gi in groups]
    cbs = [_dot_nt(cgs[gi], bgs[gi]) for gi in groups]
    yoffs = [_dot(cgs[gi], sts[gi].astype(BF16)) for gi in groups]
    bts = [bgs[gi].astype(F32).T for gi in groups]

    outs = []
    eexp = []
    for gi in groups:
        for pp in range(SSM_HPG // 2):
            h0 = gi * SSM_HPG + 2 * pp
            top, bot, ecols = [], [], []
            for hd in (h0, h0 + 1):
                acs_b = jnp.broadcast_to(acs2[:, hd:hd + 1], (CHUNK, CHUNK))
                top.append(cbs[gi] * jnp.exp2(acs_b + (src_t[hd:hd + 1, :] + causal_bias)))
                bot.append(bts[gi] * w_t[hd:hd + 1, :])
                ecols.append(jnp.exp2(acs_b))
            lhs = jnp.concatenate([jnp.concatenate(top, axis=1), jnp.concatenate(bot, axis=1)], axis=0)
            xp = x_ref[0, :, h0 * SSM_HEAD_DIM:(h0 + 2) * SSM_HEAD_DIM]
            zero = jnp.zeros_like(xp)
            rhs = jnp.concatenate([jnp.where(low_half, xp, zero), jnp.where(low_half, zero, xp)], axis=0)
            outs.append(_dot(lhs.astype(BF16), rhs))
            eexp.append(jnp.where(low_half, ecols[0], ecols[1]))

    for gi in groups:
        pairs = [gi * (SSM_HPG // 2) + pp for pp in range(SSM_HPG // 2)]
        heads = [gi * SSM_HPG + 2 * pp for pp in range(SSM_HPG // 2)]
        lexp = jnp.concatenate([pair_row(elast, h0) for h0 in heads], axis=1)
        state_ref[gi] = sts[gi] * lexp + jnp.concatenate([outs[p][CHUNK:] for p in pairs], axis=1)
        xg = x_ref[0, :, gsl[gi]].astype(F32)
        y = jnp.concatenate([outs[p][:CHUNK] for p in pairs], axis=1)
        y = y + yoffs[gi] * jnp.concatenate([eexp[p] for p in pairs], axis=1)
        y = y + dsk_ref[:, gsl[gi]] * xg
        y = y * zg_ref[0, :, gsl[gi]].astype(F32)
        o_ref[0, :, gsl[gi]] = _rms(y, gn_ref[:, gsl[gi]]).astype(BF16)


def _ssd(zxbc, dt, dt_bias, a_log, d_skip, gate_norm):
    b, lp, _ = zxbc.shape
    nc = lp // CHUNK
    wide = lambda j: pl.BlockSpec((1, CHUNK, SSM_INNER), lambda i, c: (i, c, j))
    const = lambda r, w: pl.BlockSpec((r, w), lambda i, c: (0, 0))
    return pl.pallas_call(
        _ssd_kernel,
        grid=(b, nc),
        in_specs=[
            wide(0), wide(1), wide(2),
            pl.BlockSpec((1, CHUNK, LANES), lambda i, c: (i, c, 0)),
            const(1, LANES), const(1, LANES), const(1, SSM_INNER), const(1, SSM_INNER),
        ],
        out_specs=pl.BlockSpec((1, CHUNK, SSM_INNER), lambda i, c: (i, c, 0)),
        out_shape=jax.ShapeDtypeStruct((b, lp, SSM_INNER), BF16),
        scratch_shapes=[pltpu.VMEM((SSM_GROUPS, SSM_STATE, SSM_GROUP_W), F32)],
        compiler_params=_cparams("parallel", "arbitrary"),
        name="ssd",
    )(zxbc, zxbc, zxbc, dt, dt_bias, a_log, d_skip, gate_norm)


def _out_proj_kernel(h_ref, y_ref, w_ref, g_ref, wr_ref, br_ref, ho_ref, lg_ref):
    hn = h_ref[...] + _dot(y_ref[...], w_ref[...])
    ho_ref[...] = hn
    u = _rms(hn, g_ref[...])
    u_hi = u.astype(BF16)
    u_lo = (u - u_hi.astype(F32)).astype(BF16)
    p = _dot(u_hi, wr_ref[...])
    logits = p[:, :LANES] + (p[:, LANES:] + _dot(u_lo, wr_ref[:, :LANES])) + br_ref[...]
    lg_ref[...] = logits.T[:ROUTER_ROWS]


def _out_proj(h, y, w, gain, wr_t, br):
    t, d = h.shape
    tm = TOKEN_TILE
    kin = y.shape[1]
    return pl.pallas_call(
        _out_proj_kernel,
        grid=(t // tm,),
        in_specs=[
            pl.BlockSpec((tm, d), lambda i: (i, 0)),
            pl.BlockSpec((tm, kin), lambda i: (i, 0)),
            pl.BlockSpec((kin, d), lambda i: (0, 0)),
            pl.BlockSpec((1, d), lambda i: (0, 0)),
            pl.BlockSpec((d, 2 * LANES), lambda i: (0, 0)),
            pl.BlockSpec((1, LANES), lambda i: (0, 0)),
        ],
        out_specs=[
            pl.BlockSpec((tm, d), lambda i: (i, 0)),
            pl.BlockSpec((ROUTER_ROWS, tm), lambda i: (0, i)),
        ],
        out_shape=[jax.ShapeDtypeStruct((t, d), F32), jax.ShapeDtypeStruct((ROUTER_ROWS, t), F32)],
        compiler_params=_cparams("parallel"),
        name="out_proj_router",
    )(h, y, w, gain, wr_t, br)


def _first_argmax(v, n):
    ridx = lax.broadcasted_iota(I32, v.shape, 0).astype(F32)
    vmax = jnp.max(v, axis=0, keepdims=True)
    idx = jnp.min(jnp.where(v == vmax, ridx, float(n)), axis=0, keepdims=True)
    return vmax, idx.astype(I32)


def _route_kernel(lg_ref, bkt_ref, gate_ref, rank_ref, cnt_ref, carry_ref):
    @pl.when(pl.program_id(0) == 0)
    def _():
        carry_ref[...] = jnp.zeros_like(carry_ref)

    logits = lg_ref[...]
    tm = logits.shape[1]
    gl = logits[0:MOE_GROUPS]
    gmax, gsel = _first_argmax(gl, MOE_GROUPS)
    p_group = 1.0 / jnp.sum(jnp.exp(gl - gmax), axis=0, keepdims=True)
    el = logits[MOE_GROUPS:MOE_GROUPS + MOE_EPG]
    for gg in range(1, MOE_GROUPS):
        el = jnp.where(gsel == gg, logits[MOE_GROUPS + gg * MOE_EPG:MOE_GROUPS + (gg + 1) * MOE_EPG], el)
    ex = jnp.exp(el - jnp.max(el, axis=0, keepdims=True))
    p = ex / jnp.sum(ex, axis=0, keepdims=True)
    p1, i1 = _first_argmax(p, MOE_EPG)
    ridx8 = lax.broadcasted_iota(I32, p.shape, 0)
    p2, i2 = _first_argmax(jnp.where(ridx8 == i1, -1.0, p), MOE_EPG)
    denom = p1 + p2
    gate1 = p_group * p1 / denom
    gate2 = p_group * p2 / denom
    first_low = i1 < i2
    gate_ref[0:1, :] = jnp.where(first_low, gate1, gate2)
    gate_ref[1:2, :] = jnp.where(first_low, gate2, gate1)
    lo = jnp.minimum(i1, i2).astype(F32)
    hi = jnp.maximum(i1, i2).astype(F32)
    pair = lo * (2.0 * MOE_EPG - 1.0 - lo) * 0.5 + (hi - lo - 1.0)
    bucket = gsel * MOE_PAIRS + pair.astype(I32)
    bkt_ref[...] = bucket

    ridx = lax.broadcasted_iota(I32, (MOE_BUCKET_ROWS, tm), 0)
    onehot = jnp.where(ridx == bucket, 1.0, 0.0)
    before = jnp.where(lax.broadcasted_iota(I32, (tm, tm), 0) < lax.broadcasted_iota(I32, (tm, tm), 1), 1.0, 0.0)
    prefix = _dot(onehot.astype(BF16), before.astype(BF16))
    carry = carry_ref[...]
    rank_ref[...] = jnp.sum(onehot * (carry + prefix), axis=0, keepdims=True).astype(I32)
    carry = carry + jnp.sum(onehot, axis=1, keepdims=True)
    carry_ref[...] = carry
    cnt_ref[...] = jnp.broadcast_to(carry, cnt_ref.shape).astype(I32)


def _route(logits_t):
    t = logits_t.shape[1]
    tm = TOKEN_TILE
    rows = lambda r: pl.BlockSpec((r, tm), lambda i: (0, i))
    return pl.pallas_call(
        _route_kernel,
        grid=(t // tm,),
        in_specs=[rows(ROUTER_ROWS)],
        out_specs=[rows(1), rows(2), rows(1), pl.BlockSpec((MOE_BUCKET_ROWS, LANES), lambda i: (0, 0))],
        out_shape=[jax.ShapeDtypeStruct((1, t), I32), jax.ShapeDtypeStruct((2, t), F32),
                   jax.ShapeDtypeStruct((1, t), I32), jax.ShapeDtypeStruct((MOE_BUCKET_ROWS, LANES), I32)],
        scratch_shapes=[pltpu.VMEM((MOE_BUCKET_ROWS, 1), F32)],
        compiler_params=_cparams("arbitrary"),
        name="route_rank",
    )(logits_t)


def _dispatch_kernel(seg_ref, dest_ref, gate_ref, h_ref, g_ref, xs_ref, u_ref, zero_ref, sems):
    i = pl.program_id(0)
    n = pl.num_programs(0)
    slot = lax.rem(i, 2)
    tm, d = h_ref.shape

    def wait_rows(s):
        pltpu.make_async_copy(u_ref.at[s], xs_ref.at[pl.ds(0, tm)], sems.at[s]).wait()

    @pl.when(i == 0)
    def _():
        zero_ref[...] = jnp.zeros_like(zero_ref)

        def fill_block(start):
            fill = pltpu.make_async_copy(
                zero_ref, xs_ref.at[pl.ds(pl.multiple_of(start, MOE_BLOCK), MOE_BLOCK)], sems.at[0])
            fill.start()
            fill.wait()

        def fill_segment(e, carry):
            @pl.when(seg_ref[MOE_BUCKET_ROWS + e] > 0)
            def _():
                fill_block(seg_ref[e] - MOE_BLOCK)
            return carry

        lax.fori_loop(0, MOE_BUCKETS, fill_segment, 0)

        def fill_tail(blk, carry):
            fill_block(blk * MOE_BLOCK)
            return carry

        lax.fori_loop(seg_ref[MOE_BUCKET_ROWS - 1] // MOE_BLOCK, xs_ref.shape[0] // MOE_BLOCK, fill_tail, 0)

    @pl.when(i >= 2)
    def _():
        wait_rows(slot)

    u = _rms(h_ref[...], g_ref[...])
    half = d // 2
    as_bits = lambda v: lax.bitcast_convert_type(v.astype(BF16).astype(F32), jnp.uint32)
    words = (as_bits(u[:, half:]) & jnp.uint32(0xFFFF0000)) | (as_bits(u[:, :half]) >> 16)
    gates = jnp.concatenate([gate_ref[...], jnp.zeros((LANES - gate_ref.shape[0], tm), F32)], axis=0).T
    packed = jnp.concatenate([words, lax.bitcast_convert_type(gates, jnp.uint32),
                              jnp.zeros((tm, half - LANES), jnp.uint32)], axis=1)
    u_ref[slot] = packed.reshape(u_ref.shape[1:])

    def issue(q, carry):
        for k in range(ROW_DMA_UNROLL):
            t = q * ROW_DMA_UNROLL + k
            pltpu.make_async_copy(
                u_ref.at[slot, t], xs_ref.at[dest_ref[0, 0, t]], sems.at[slot]).start(priority=k % 2)
        return carry

    lax.fori_loop(0, tm // ROW_DMA_UNROLL, issue, 0)

    @pl.when(i == n - 1)
    def _():
        wait_rows(slot)

        @pl.when(n >= 2)
        def _():
            wait_rows(1 - slot)


def _dispatch(seg, dest, gate, h, gain, rows):
    t, d = h.shape
    tm = dest.shape[2]
    sub = d // LANES
    return pl.pallas_call(
        _dispatch_kernel,
        grid_spec=pltpu.PrefetchScalarGridSpec(
            num_scalar_prefetch=1,
            grid=(t // tm,),
            in_specs=[
                pl.BlockSpec((1, 1, tm), lambda i, seg: (i, 0, 0), memory_space=pltpu.SMEM),
                pl.BlockSpec((SUBLANES, tm), lambda i, seg: (0, i)),
                pl.BlockSpec((tm, d), lambda i, seg: (i, 0)),
                pl.BlockSpec((1, d), lambda i, seg: (0, 0)),
            ],
            out_specs=pl.BlockSpec(memory_space=pl.ANY),
            scratch_shapes=[pltpu.VMEM((2, tm, sub, LANES), jnp.uint32), pltpu.VMEM((MOE_BLOCK, sub, LANES), jnp.uint32),
                            pltpu.SemaphoreType.DMA((2,))],
        ),
        out_shape=jax.ShapeDtypeStruct((rows, sub, LANES), jnp.uint32),
        compiler_params=_cparams("arbitrary"),
        name="moe_dispatch",
    )(seg, dest, gate, h, gain)


def _expert_kernel(ea_ref, eb_ref, nu_ref, xs_ref, w1a_ref, w3a_ref, w2a_ref, w1b_ref, w3b_ref, w2b_ref, ys_ref):
    used = pl.program_id(0) < nu_ref[0]

    @pl.when(used)
    def _():
        rows, sub, lanes = xs_ref.shape
        d = sub * lanes
        packed = xs_ref[...].reshape(rows, d)
        words = packed[:, :d // 2]
        lo = lax.bitcast_convert_type(words << 16, F32).astype(BF16)
        hi = lax.bitcast_convert_type(words & jnp.uint32(0xFFFF0000), F32).astype(BF16)
        x = jnp.concatenate([lo, hi], axis=1)
        gates = lax.bitcast_convert_type(packed[:, d // 2:d // 2 + LANES], F32)

        def ffn(w1_ref, w3_ref, w2_ref):
            hid = _silu(_dot(x, w1_ref[0, 0].astype(BF16))) * _dot(x, w3_ref[0, 0].astype(BF16))
            return _dot(hid.astype(BF16), w2_ref[0, 0].astype(BF16))

        y = ffn(w1a_ref, w3a_ref, w2a_ref) * gates[:, 0:1] + ffn(w1b_ref, w3b_ref, w2b_ref) * gates[:, 1:2]
        ys_ref[...] = y.reshape(rows, sub, lanes)

    @pl.when(jnp.logical_not(used))
    def _():
        ys_ref[...] = jnp.zeros_like(ys_ref)


def _experts(block_ea, block_eb, n_used, xs, w1, w3, w2, layer):
    rows, sub, lanes = xs.shape
    d = sub * lanes
    nb = rows // MOE_BLOCK
    w_in = lambda tbl: pl.BlockSpec((1, 1, d, MOE_FF), lambda i, ea, eb, nu: (layer, (ea, eb)[tbl][i], 0, 0))
    w_out = lambda tbl: pl.BlockSpec((1, 1, MOE_FF, d), lambda i, ea, eb, nu: (layer, (ea, eb)[tbl][i], 0, 0))
    return pl.pallas_call(
        _expert_kernel,
        grid_spec=pltpu.PrefetchScalarGridSpec(
            num_scalar_prefetch=3,
            grid=(nb,),
            in_specs=[
                pl.BlockSpec((MOE_BLOCK, sub, lanes), lambda i, ea, eb, nu: (jnp.minimum(i, nu[0] - 1), 0, 0)),
                w_in(0), w_in(0), w_out(0), w_in(1), w_in(1), w_out(1),
            ],
            out_specs=pl.BlockSpec((MOE_BLOCK, sub, lanes), lambda i, ea, eb, nu: (i, 0, 0)),
        ),
        out_shape=jax.ShapeDtypeStruct((rows, sub, lanes), F32),
        compiler_params=_cparams("arbitrary"),
        name="moe_experts",
    )(block_ea, block_eb, n_used, xs, w1, w3, w2, w1, w3, w2)


def _issue_row_gather(idx_ref, src_ref, dst_ref, sem, rows):
    def issue(q, carry):
        for k in range(ROW_DMA_UNROLL):
            r = q * ROW_DMA_UNROLL + k
            pltpu.make_async_copy(src_ref.at[idx_ref[0, 0, r]], dst_ref.at[r], sem).start(priority=k % 2)
        return carry

    lax.fori_loop(0, rows // ROW_DMA_UNROLL, issue, 0)


def _combine_kernel(dest_ref, next_ref, h_ref, g_ref, ys_ref, ho_ref, xn_ref, ybuf, sems):
    i = pl.program_id(0)
    slot = lax.rem(i, 2)
    tm, d = h_ref.shape

    @pl.when(i == 0)
    def _():
        _issue_row_gather(dest_ref, ys_ref, ybuf.at[slot], sems.at[slot], tm)

    @pl.when(i + 1 < pl.num_programs(0))
    def _():
        _issue_row_gather(next_ref, ys_ref, ybuf.at[1 - slot], sems.at[1 - slot], tm)

    pltpu.make_async_copy(ys_ref.at[pl.ds(0, tm)], ybuf.at[slot], sems.at[slot]).wait()
    h = h_ref[...] + ybuf[slot].reshape(tm, d)
    ho_ref[...] = h
    xn_ref[...] = _rms(h, g_ref[...]).astype(BF16)


def _combine_norm(dest, ys, h, gain):
    n, _, tm = dest.shape
    t, d = h.shape
    smem = lambda fn: pl.BlockSpec((1, 1, tm), fn, memory_space=pltpu.SMEM)
    return pl.pallas_call(
        _combine_kernel,
        grid=(n,),
        in_specs=[
            smem(lambda i: (i, 0, 0)),
            smem(lambda i: (jnp.minimum(i + 1, n - 1), 0, 0)),
            pl.BlockSpec((tm, d), lambda i: (i, 0)),
            pl.BlockSpec((1, d), lambda i: (0, 0)),
            pl.BlockSpec(memory_space=pl.ANY),
        ],
        out_specs=[pl.BlockSpec((tm, d), lambda i: (i, 0)), pl.BlockSpec((tm, d), lambda i: (i, 0))],
        out_shape=[jax.ShapeDtypeStruct((t, d), F32), jax.ShapeDtypeStruct((t, d), BF16)],
        scratch_shapes=[pltpu.VMEM((2, tm, d // LANES, LANES), F32), pltpu.SemaphoreType.DMA((2,))],
        compiler_params=_cparams("arbitrary"),
        name="moe_combine",
    )(dest, dest, h, gain, ys)


def _final_kernel(*refs, per):
    cur, nxt, hs = refs[:per], refs[per:2 * per], refs[2 * per:3 * per]
    g_ref, ys_ref, o_ref, ybuf, sems = refs[3 * per:]
    step = pl.program_id(0) * pl.num_programs(1) + pl.program_id(1)
    last = pl.num_programs(0) * pl.num_programs(1) - 1
    slot = lax.rem(step, 2)
    d = o_ref.shape[2]

    def gather(idx_refs, s):
        for p, idx_ref in enumerate(idx_refs):
            _issue_row_gather(idx_ref, ys_ref, ybuf.at[s, pl.ds(p * CHUNK, CHUNK)], sems.at[s], CHUNK)

    @pl.when(step == 0)
    def _():
        gather(cur, slot)

    @pl.when(step < last)
    def _():
        gather(nxt, 1 - slot)

    pltpu.make_async_copy(ys_ref.at[pl.ds(0, per * CHUNK)], ybuf.at[slot], sems.at[slot]).wait()
    h = jnp.concatenate([h_ref[0] for h_ref in hs], axis=0)
    o_ref[0] = _rms(h + ybuf[slot].reshape(per * CHUNK, d), g_ref[...])


def _final_norm(h, dest, ys, gain, seq):
    b, lp, d = h.shape
    nc = lp // CHUNK
    ns = seq // CHUNK
    per = 4 if ns % 4 == 0 else 2 if ns % 2 == 0 else 1
    steps = ns // per

    def chunk_of(step, p):
        return (step // steps) * nc + lax.rem(step, steps) * per + p + 1

    smem = lambda fn: pl.BlockSpec((1, 1, CHUNK), fn, memory_space=pltpu.SMEM)
    cur = [smem(lambda i, c, p=p: (chunk_of(i * steps + c, p), 0, 0)) for p in range(per)]
    nxt = [smem(lambda i, c, p=p: (chunk_of(jnp.minimum(i * steps + c + 1, b * steps - 1), p), 0, 0))
           for p in range(per)]
    hs = [pl.BlockSpec((1, CHUNK, d), lambda i, c, p=p: (i, c * per + p + 1, 0)) for p in range(per)]
    return pl.pallas_call(
        functools.partial(_final_kernel, per=per),
        grid=(b, steps),
        in_specs=cur + nxt + hs + [pl.BlockSpec((1, d), lambda i, c: (0, 0)), pl.BlockSpec(memory_space=pl.ANY)],
        out_specs=pl.BlockSpec((1, per * CHUNK, d), lambda i, c: (i, c, 0)),
        out_shape=jax.ShapeDtypeStruct((b, seq, d), F32),
        scratch_shapes=[pltpu.VMEM((2, per * CHUNK, d // LANES, LANES), F32), pltpu.SemaphoreType.DMA((2,))],
        compiler_params=_cparams("arbitrary", "arbitrary"),
        name="final_norm",
    )(*([dest] * (2 * per)), *([h] * per), gain, ys)


def _row_tiles(dest):
    t = dest.shape[0]
    tile = ROW_COPY_TILE if t % ROW_COPY_TILE == 0 else TOKEN_TILE
    return dest.reshape(t // tile, 1, tile)


def _bucket_experts():
    lo, hi = [], []
    for g in range(MOE_GROUPS):
        for a in range(MOE_EPG):
            for c in range(a + 1, MOE_EPG):
                lo.append(g * MOE_EPG + a)
                hi.append(g * MOE_EPG + c)
    fill = MOE_BUCKET_ROWS - len(lo)
    return jnp.array(lo + [lo[-1]] * fill, I32), jnp.array(hi + [hi[-1]] * fill, I32)


def _moe_layer(h, y_mix, w_out, norm_gain, wg, bg, we, be, w1, w3, w2, layer):
    t, d = h.shape
    n_logits = MOE_GROUPS + MOE_EXPERTS
    wr = jnp.concatenate([wg, we.reshape(d, MOE_EXPERTS), jnp.zeros((d, LANES - n_logits), F32)], axis=1)
    br = jnp.concatenate([bg, be.reshape(-1), jnp.zeros((LANES - n_logits,), F32)])
    wr_hi = wr.astype(BF16)
    wr_split = jnp.concatenate([wr_hi, (wr - wr_hi.astype(F32)).astype(BF16)], axis=1)
    h, logits_t = _out_proj(h, y_mix, w_out.astype(BF16), norm_gain[None, :], wr_split, br[None, :])
    bucket, gate, rank, counts = _route(logits_t)

    counts = counts[:, 0]
    padded = (counts + MOE_BLOCK - 1) // MOE_BLOCK * MOE_BLOCK
    pend = jnp.cumsum(padded)
    pstart = pend - padded
    ids = jnp.arange(MOE_BUCKET_ROWS, dtype=I32)
    dest = rank[0] + jnp.sum(jnp.where(bucket == ids[:, None], pstart[:, None], 0), axis=0)
    n_blocks = -(-t // MOE_BLOCK) + MOE_BUCKETS
    block_start = jnp.arange(n_blocks, dtype=I32) * MOE_BLOCK
    block_bucket = jnp.minimum(jnp.sum((pend[None, :] <= block_start[:, None]).astype(I32), axis=1),
                               MOE_BUCKET_ROWS - 1)
    n_used = (pend[-1:] // MOE_BLOCK).astype(I32)
    lo, hi = _bucket_experts()
    gate8 = jnp.concatenate([gate, jnp.zeros((SUBLANES - gate.shape[0], t), F32)], axis=0)

    xs = _dispatch(jnp.concatenate([pend, padded]).astype(I32), _row_tiles(dest), gate8, h, norm_gain[None, :],
                   n_blocks * MOE_BLOCK)
    ys = _experts(lo[block_bucket], hi[block_bucket], n_used, xs, w1, w3, w2, layer)
    return h, dest, ys


def kernel(x, meta, norm_mix, norm_ffn, norm_final, ret_w_in, ret_w_out, ret_norm, ssm_w_in, ssm_conv_w,
           ssm_conv_b, ssm_dt_bias, ssm_a_log, ssm_d, ssm_norm, ssm_w_out, moe_wg, moe_bg, moe_we, moe_be,
           moe_w1, moe_w3, moe_w2):
    b, seq, d = x.shape
    lp = CHUNK + seq
    t = b * lp
    depth = norm_mix.shape[0]

    half = RET_QK_DIM // 2
    inv_freq = 1.0 / (RET_ROPE_BASE ** jnp.linspace(0.0, 1.0, half, dtype=F32))
    ang = (jnp.arange(lp, dtype=F32) - PAD)[:, None] * inv_freq[None, :]
    cos, sin = jnp.cos(ang), jnp.sin(ang)
    perm = jnp.concatenate([jnp.arange(0, RET_QK_DIM, 2), jnp.arange(1, RET_QK_DIM, 2)])
    qk_perm = (jnp.arange(2 * RET_HEADS)[:, None] * RET_QK_DIM + perm[None, :]).reshape(-1)

    h = None
    moe = None
    for i in range(depth):
        j = i // 2
        if moe is not None:
            h, xn = _combine_norm(_row_tiles(moe[0]), moe[1], h, norm_mix[i][None, :])
        if i % 2 == 0:
            w_in = jnp.concatenate([ret_w_in[j][:, qk_perm], ret_w_in[j][:, 2 * RET_QK:]], axis=1).astype(BF16)
            if i == 0:
                h, qkvg = _ret_in_first(x, meta, norm_mix[i][None, :], cos, sin, w_in)
                h = h.reshape(t, d)
            else:
                qkvg = _ret_in(h.reshape(b, lp, d), norm_mix[i][None, :], cos, sin, w_in)
            y = _retention(qkvg, ret_norm[j][None, :])
            w_out = ret_w_out[j]
        else:
            w_in = jnp.concatenate([ssm_w_in[j], jnp.zeros((d, LANES - SSM_HEADS), F32)], axis=1).astype(BF16)
            zxbc, dt = _ssm_in(xn, w_in, ssm_conv_w[j], ssm_conv_b[j][None, :], lp)
            lane_pad = lambda v: jnp.concatenate([v, jnp.zeros((LANES - SSM_HEADS,), F32)])[None, :]
            y = _ssd(zxbc.reshape(b, lp, SSM_MAIN), dt.reshape(b, lp, LANES), lane_pad(ssm_dt_bias[j]),
                     lane_pad(ssm_a_log[j]), jnp.repeat(ssm_d[j], SSM_HEAD_DIM)[None, :], ssm_norm[j][None, :])
            w_out = ssm_w_out[j]
        h, *moe = _moe_layer(h, y.reshape(t, -1), w_out, norm_ffn[i], moe_wg[i], moe_bg[i], moe_we[i],
                             moe_be[i], moe_w1, moe_w3, moe_w2, i)
    return _final_norm(h.reshape(b, lp, d), moe[0].reshape(t // CHUNK, 1, CHUNK), moe[1], norm_final[None, :], seq)
```

```python
import functools
import math

import jax
import jax.numpy as jnp
from jax import lax
from jax.experimental import pallas as pl
from jax.experimental.pallas import tpu as pltpu

F32 = jnp.float32
BF16 = jnp.bfloat16
I32 = jnp.int32

D_MODEL = 1024
N_META = 16
CHUNK = 128
PAD = CHUNK - N_META
EPS = 1e-6

RET_HEADS = 4
RET_QK_DIM = 256
RET_V_DIM = 512
RET_QK = RET_HEADS * RET_QK_DIM
RET_V = RET_HEADS * RET_V_DIM
RET_IN = 2 * RET_QK + 2 * RET_V
RET_ROPE_BASE = 10000.0

SSM_INNER = 2048
SSM_HEAD_DIM = 64
SSM_HEADS = 32
SSM_GROUPS = 8
SSM_HPG = 4
SSM_STATE = 128
SSM_CONV = 4
SSM_BC = SSM_GROUPS * SSM_STATE
SSM_CONV_DIM = SSM_INNER + 2 * SSM_BC
SSM_MAIN = SSM_INNER + SSM_CONV_DIM
LANES = 128
SUBLANES = 8
SSM_GROUP_W = SSM_HPG * SSM_HEAD_DIM

MOE_GROUPS = 4
MOE_EPG = 8
MOE_EXPERTS = 32
MOE_FF = 512
MOE_BLOCK = 256
MOE_PAIRS = MOE_EPG * (MOE_EPG - 1) // 2
MOE_BUCKETS = MOE_GROUPS * MOE_PAIRS
MOE_BUCKET_ROWS = 128
ROUTER_ROWS = 40

ROW_DMA_UNROLL = 16
SSM_IN_STEP = 256
SSM_IN_ROWS = 64
LOG2E = math.log2(math.e)
TOKEN_TILE = 512
ROW_COPY_TILE = 1024
VMEM_LIMIT = 56 * 1024 * 1024


def _cparams(*sem):
    return pltpu.CompilerParams(dimension_semantics=sem, vmem_limit_bytes=VMEM_LIMIT)


def _rms(x, gain):
    ms = jnp.mean(x * x, axis=-1, keepdims=True)
    return x * lax.rsqrt(ms + EPS) * gain


def _silu(x):
    hx = 0.5 * x
    return hx + hx * jnp.tanh(hx)


def _dot(a, b):
    return jnp.dot(a, b, preferred_element_type=F32)


def _dot_nt(a, b, precision=None):
    return lax.dot_general(a, b, (((1,), (1,)), ((), ())), preferred_element_type=F32, precision=precision)


def _dot_tn(a, b):
    return lax.dot_general(a, b, (((0,), (0,)), ((), ())), preferred_element_type=F32)


def _row_tile(rows, cap=544):
    best = 16
    for t in range(16, cap + 1, 16):
        if rows % t == 0:
            best = t
    return best


def _ret_in_kernel(h_ref, g_ref, cos_ref, sin_ref, w_ref, o_ref):
    _ret_project(h_ref[0], g_ref, cos_ref, sin_ref, w_ref, o_ref)


def _ret_in_first_kernel(meta_ref, g_ref, cos_ref, sin_ref, w_ref, x_ref, ho_ref, o_ref, hbuf, sems):
    nj = pl.num_programs(1)
    step = pl.program_id(0) * nj + pl.program_id(1)
    last = pl.num_programs(0) * nj - 1
    slot = lax.rem(step, 2)
    tm = hbuf.shape[1]

    def rows_copy(s, slot_):
        i, j = s // nj, lax.rem(s, nj)
        first = pltpu.make_async_copy(
            x_ref.at[i, pl.ds(0, tm - CHUNK)], hbuf.at[slot_, pl.ds(CHUNK, tm - CHUNK)], sems.at[slot_])
        start = pl.multiple_of(jnp.maximum(j * tm - CHUNK, 0), SUBLANES)
        other = pltpu.make_async_copy(x_ref.at[i, pl.ds(start, tm)], hbuf.at[slot_], sems.at[slot_])
        return j == 0, first, other

    def fetch(s, slot_):
        is_first, first, other = rows_copy(s, slot_)
        pl.when(is_first)(first.start)
        pl.when(jnp.logical_not(is_first))(other.start)

    @pl.when(step == 0)
    def _():
        fetch(step, slot)

    @pl.when(step < last)
    def _():
        fetch(step + 1, 1 - slot)

    is_first, first, other = rows_copy(step, slot)
    pl.when(is_first)(first.wait)
    pl.when(jnp.logical_not(is_first))(other.wait)

    @pl.when(is_first)
    def _():
        hbuf[slot, 0:PAD, :] = jnp.zeros((PAD, hbuf.shape[2]), F32)
        hbuf[slot, PAD:CHUNK, :] = meta_ref[...]

    h = hbuf[slot]
    ho_ref[0] = h
    _ret_project(h, g_ref, cos_ref, sin_ref, w_ref, o_ref)


def _ret_project(h, g_ref, cos_ref, sin_ref, w_ref, o_ref):
    xn = _rms(h, g_ref[...]).astype(BF16)
    cos = cos_ref[...]
    sin = sin_ref[...]
    half = RET_QK_DIM // 2
    for j in range(2 * RET_HEADS):
        c0 = j * RET_QK_DIM
        acc = _dot(xn, w_ref[:, c0:c0 + RET_QK_DIM])
        if j >= RET_HEADS:
            acc = acc * (RET_QK_DIM ** -0.5)
        e = acc[:, :half]
        o = acc[:, half:]
        o_ref[0, :, c0:c0 + half] = (e * cos - o * sin).astype(BF16)
        o_ref[0, :, c0 + half:c0 + RET_QK_DIM] = (o * cos + e * sin).astype(BF16)
    step = 1024
    for c0 in range(2 * RET_QK, RET_IN, step):
        o_ref[0, :, c0:c0 + step] = _dot(xn, w_ref[:, c0:c0 + step]).astype(BF16)


def _ret_in_first(x, meta, gain, cos, sin, w):
    b, seq, d = x.shape
    lp = CHUNK + seq
    tm = _row_tile(lp)
    assert tm > CHUNK
    const = lambda r, c: pl.BlockSpec((r, c), lambda i, j: (0, 0))
    return pl.pallas_call(
        _ret_in_first_kernel,
        grid=(b, lp // tm),
        in_specs=[
            const(N_META, d), const(1, d),
            pl.BlockSpec((tm, RET_QK_DIM // 2), lambda i, j: (j, 0)),
            pl.BlockSpec((tm, RET_QK_DIM // 2), lambda i, j: (j, 0)),
            const(d, RET_IN),
            pl.BlockSpec(memory_space=pl.ANY),
        ],
        out_specs=[pl.BlockSpec((1, tm, d), lambda i, j: (i, j, 0)), pl.BlockSpec((1, tm, RET_IN), lambda i, j: (i, j, 0))],
        out_shape=[jax.ShapeDtypeStruct((b, lp, d), F32), jax.ShapeDtypeStruct((b, lp, RET_IN), BF16)],
        scratch_shapes=[pltpu.VMEM((2, tm, d), F32), pltpu.SemaphoreType.DMA((2,))],
        compiler_params=_cparams("arbitrary", "arbitrary"),
        name="ret_in_proj",
    )(meta, gain, cos, sin, w, x)


def _ret_in(h, gain, cos, sin, w):
    b, lp, d = h.shape
    tm = _row_tile(lp)
    return pl.pallas_call(
        _ret_in_kernel,
        grid=(b, lp // tm),
        in_specs=[
            pl.BlockSpec((1, tm, d), lambda i, j: (i, j, 0)),
            pl.BlockSpec((1, d), lambda i, j: (0, 0)),
            pl.BlockSpec((tm, RET_QK_DIM // 2), lambda i, j: (j, 0)),
            pl.BlockSpec((tm, RET_QK_DIM // 2), lambda i, j: (j, 0)),
            pl.BlockSpec((d, RET_IN), lambda i, j: (0, 0)),
        ],
        out_specs=pl.BlockSpec((1, tm, RET_IN), lambda i, j: (i, j, 0)),
        out_shape=jax.ShapeDtypeStruct((b, lp, RET_IN), BF16),
        compiler_params=_cparams("parallel", "parallel"),
        name="ret_in_proj",
    )(h, gain, cos, sin, w)


def _ret_kernel(q_ref, k_ref, v_ref, g_ref, hn_ref, o_ref, state_ref):
    @pl.when(pl.program_id(1) == 0)
    def _():
        state_ref[...] = jnp.zeros_like(state_ref)

    row = lax.broadcasted_iota(I32, (CHUNK, CHUNK), 0).astype(F32)
    col = lax.broadcasted_iota(I32, (CHUNK, CHUNK), 1).astype(F32)
    diff = row - col
    ridx = row[:, :1]
    for hh in range(RET_HEADS):
        lg = math.log(1.0 - 2.0 ** (-5.0 - hh))
        intra = jnp.where(diff >= 0, jnp.exp(lg * jnp.maximum(diff, 0.0)), 0.0)
        qdec = jnp.exp(lg * (ridx + 1.0))
        kdec = jnp.exp(lg * (CHUNK - 1.0 - ridx))
        cdec = math.exp(lg * CHUNK)
        qh = q_ref[0, :, hh * RET_QK_DIM:(hh + 1) * RET_QK_DIM]
        kh = k_ref[0, :, hh * RET_QK_DIM:(hh + 1) * RET_QK_DIM]
        vh = v_ref[0, :, hh * RET_V_DIM:(hh + 1) * RET_V_DIM]
        scores = _dot_nt(qh, kh) * intra
        st = state_ref[hh]
        y = _dot(scores.astype(BF16), vh) + _dot(qh, st.astype(BF16)) * qdec
        kd = (kh.astype(F32) * kdec).astype(BF16)
        state_ref[hh] = st * cdec + _dot_tn(kd, vh)
        yn = _rms(y, hn_ref[:, hh * RET_V_DIM:(hh + 1) * RET_V_DIM])
        gh = g_ref[0, :, hh * RET_V_DIM:(hh + 1) * RET_V_DIM].astype(F32)
        o_ref[0, :, hh * RET_V_DIM:(hh + 1) * RET_V_DIM] = (_silu(gh) * yn).astype(BF16)


def _retention(qkvg, head_norm):
    b, lp, _ = qkvg.shape
    nc = lp // CHUNK
    return pl.pallas_call(
        _ret_kernel,
        grid=(b, nc),
        in_specs=[
            pl.BlockSpec((1, CHUNK, RET_QK), lambda i, c: (i, c, 0)),
            pl.BlockSpec((1, CHUNK, RET_QK), lambda i, c: (i, c, 1)),
            pl.BlockSpec((1, CHUNK, RET_V), lambda i, c: (i, c, 1)),
            pl.BlockSpec((1, CHUNK, RET_V), lambda i, c: (i, c, 2)),
            pl.BlockSpec((1, RET_V), lambda i, c: (0, 0)),
        ],
        out_specs=pl.BlockSpec((1, CHUNK, RET_V), lambda i, c: (i, c, 0)),
        out_shape=jax.ShapeDtypeStruct((b, lp, RET_V), BF16),
        scratch_shapes=[pltpu.VMEM((RET_HEADS, RET_QK_DIM, RET_V_DIM), F32)],
        compiler_params=_cparams("parallel", "arbitrary"),
        name="retention",
    )(qkvg, qkvg, qkvg, qkvg, head_norm)


def _ssm_in_kernel(xn_ref, w_ref, cw_ref, cb_ref, o_ref, dt_ref, tail_ref, acc_ref, *, rows_per_seq):
    i = pl.program_id(0)
    tm = xn_ref.shape[0]

    @pl.when(i == 0)
    def _():
        tail_ref[...] = jnp.zeros_like(tail_ref)

    step = acc_ref.shape[2]
    n_steps = SSM_MAIN // step
    rb = SSM_IN_ROWS

    pos = lax.rem(i * tm, rows_per_seq) + lax.broadcasted_iota(I32, (tm, 1), 0)
    is_pad = jnp.logical_or(pos < PAD, jnp.logical_and(pos >= rows_per_seq, pos < rows_per_seq + PAD))

    halo = SUBLANES

    def project(j):
        c0 = j * step
        acc = _dot(xn_ref[...], w_ref[:, c0:c0 + step])
        acc_ref[j % 2, halo:, :] = acc if c0 < SSM_INNER else jnp.where(is_pad, 0.0, acc)

    project(0)
    for j in range(n_steps):
        if j + 1 < n_steps:
            project(j + 1)
        c0 = j * step
        s = j % 2
        if c0 < SSM_INNER:
            for r0 in range(0, tm, rb):
                o_ref[r0:r0 + rb, c0:c0 + step] = _silu(acc_ref[s, halo + r0:halo + r0 + rb, :]).astype(BF16)
            continue
        cc = c0 - SSM_INNER
        acc_ref[s, 0:halo, :] = tail_ref[:, cc:cc + step]
        tail_ref[:, cc:cc + step] = acc_ref[s, tm:tm + halo, :]
        taps = [cw_ref[k:k + 1, cc:cc + step] for k in range(SSM_CONV)]
        bias = cb_ref[:, cc:cc + step]
        for r0 in range(0, tm, rb):
            ext = acc_ref[s, r0:r0 + rb + halo, :]
            back1 = pltpu.roll(ext, 1, axis=0)
            back23 = pltpu.roll(taps[1] * ext + taps[0] * back1, 2, axis=0)
            conv = bias + taps[3] * ext + taps[2] * back1 + back23
            o_ref[r0:r0 + rb, c0:c0 + step] = _silu(conv[halo:]).astype(BF16)
    dt_ref[...] = _dot(xn_ref[...], w_ref[:, SSM_MAIN:SSM_MAIN + LANES])


def _ssm_in(xn, w, conv_w, conv_b, rows_per_seq):
    t, d = xn.shape
    tm = TOKEN_TILE
    assert tm <= rows_per_seq
    const = lambda r, c: pl.BlockSpec((r, c), lambda i: (0, 0))
    return pl.pallas_call(
        functools.partial(_ssm_in_kernel, rows_per_seq=rows_per_seq),
        grid=(t // tm,),
        in_specs=[
            pl.BlockSpec((tm, d), lambda i: (i, 0)),
            const(d, SSM_MAIN + LANES), const(SSM_CONV, SSM_CONV_DIM), const(1, SSM_CONV_DIM),
        ],
        out_specs=[
            pl.BlockSpec((tm, SSM_MAIN), lambda i: (i, 0)),
            pl.BlockSpec((tm, LANES), lambda i: (i, 0)),
        ],
        out_shape=[jax.ShapeDtypeStruct((t, SSM_MAIN), BF16), jax.ShapeDtypeStruct((t, LANES), F32)],
        scratch_shapes=[pltpu.VMEM((SUBLANES, SSM_CONV_DIM), F32), pltpu.VMEM((2, tm + SUBLANES, SSM_IN_STEP), F32)],
        compiler_params=_cparams("arbitrary"),
        name="ssm_in_proj",
    )(xn, w, conv_w, conv_b)


def _ssd_kernel(zg_ref, x_ref, bc_ref, dt_ref, dtb_ref, alog_ref, dsk_ref, gn_ref, o_ref, state_ref):
    c = pl.program_id(1)

    @pl.when(c == 0)
    def _():
        state_ref[...] = jnp.zeros_like(state_ref)

    row = lax.broadcasted_iota(I32, (CHUNK, CHUNK), 0)
    col = lax.broadcasted_iota(I32, (CHUNK, CHUNK), 1)
    causal_bias = jnp.where(row >= col, 0.0, -jnp.inf)
    low_half = col < SSM_HEAD_DIM
    valid = row[:, :1] >= jnp.where(c > 0, 0, PAD)

    dtr = dt_ref[0] + dtb_ref[...]
    dtv = jnp.maximum(dtr, 0.0) + jnp.log(1.0 + jnp.exp(-jnp.abs(dtr)))
    dtv = jnp.where(valid, dtv, 0.0)
    da = dtv * (-jnp.exp(alog_ref[...]))
    tril = jnp.where(row >= col, 1.0, 0.0)
    acs = jnp.dot(tril, da, preferred_element_type=F32, precision=lax.Precision.HIGHEST)
    last = acs[CHUNK - 1:CHUNK, :]
    acs2 = acs * LOG2E
    src_t = (jnp.log2(dtv) - acs2).T
    w_t = (dtv * jnp.exp(last - acs)).T
    elast = jnp.exp(last)

    def pair_row(v, h0):
        return jnp.where(low_half[:1], v[:, h0:h0 + 1], v[:, h0 + 1:h0 + 2])

    groups = range(SSM_GROUPS)
    gsl = [slice(gi * SSM_GROUP_W, (gi + 1) * SSM_GROUP_W) for gi in groups]
    bgs = [bc_ref[0, :, gi * SSM_STATE:(gi + 1) * SSM_STATE] for gi in groups]
    cgs = [bc_ref[0, :, SSM_BC + gi * SSM_STATE:SSM_BC + (gi + 1) * SSM_STATE] for gi in groups]
    sts = [state_ref[gi] for gi in groups]
    cbs = [_dot_nt(cgs[gi], bgs[gi]) for gi in groups]
    yoffs = [_dot(cgs[gi], sts[gi].astype(BF16)) for gi in groups]
    bts = [bgs[gi].astype(F32).T for gi in groups]

    outs = []
    eexp = []
    for gi in groups:
        for pp in range(SSM_HPG // 2):
            h0 = gi * SSM_HPG + 2 * pp
            top, bot, ecols = [], [], []
            for hd in (h0, h0 + 1):
                acs_b = jnp.broadcast_to(acs2[:, hd:hd + 1], (CHUNK, CHUNK))
                top.append(cbs[gi] * jnp.exp2(acs_b + (src_t[hd:hd + 1, :] + causal_bias)))
                bot.append(bts[gi] * w_t[hd:hd + 1, :])
                ecols.append(jnp.exp2(acs_b))
            lhs = jnp.concatenate([jnp.concatenate(top, axis=1), jnp.concatenate(bot, axis=1)], axis=0)
            xp = x_ref[0, :, h0 * SSM_HEAD_DIM:(h0 + 2) * SSM_HEAD_DIM]
            zero = jnp.zeros_like(xp)
            rhs = jnp.concatenate([jnp.where(low_half, xp, zero), jnp.where(low_half, zero, xp)], axis=0)
            outs.append(_dot(lhs.astype(BF16), rhs))
            eexp.append(jnp.where(low_half, ecols[0], ecols[1]))

    for gi in groups:
        pairs = [gi * (SSM_HPG // 2) + pp for pp in range(SSM_HPG // 2)]
        heads = [gi * SSM_HPG + 2 * pp for pp in range(SSM_HPG // 2)]
        lexp = jnp.concatenate([pair_row(elast, h0) for h0 in heads], axis=1)
        state_ref[gi] = sts[gi] * lexp + jnp.concatenate([outs[p][CHUNK:] for p in pairs], axis=1)
        xg = x_ref[0, :, gsl[gi]].astype(F32)
        y = jnp.concatenate([outs[p][:CHUNK] for p in pairs], axis=1)
        y = y + yoffs[gi] * jnp.concatenate([eexp[p] for p in pairs], axis=1)
        y = y + dsk_ref[:, gsl[gi]] * xg
        y = y * zg_ref[0, :, gsl[gi]].astype(F32)
        o_ref[0, :, gsl[gi]] = _rms(y, gn_ref[:, gsl[gi]]).astype(BF16)


def _ssd(zxbc, dt, dt_bias, a_log, d_skip, gate_norm):
    b, lp, _ = zxbc.shape
    nc = lp // CHUNK
    wide = lambda j: pl.BlockSpec((1, CHUNK, SSM_INNER), lambda i, c: (i, c, j))
    const = lambda r, w: pl.BlockSpec((r, w), lambda i, c: (0, 0))
    return pl.pallas_call(
        _ssd_kernel,
        grid=(b, nc),
        in_specs=[
            wide(0), wide(1), wide(2),
            pl.BlockSpec((1, CHUNK, LANES), lambda i, c: (i, c, 0)),
            const(1, LANES), const(1, LANES), const(1, SSM_INNER), const(1, SSM_INNER),
        ],
        out_specs=pl.BlockSpec((1, CHUNK, SSM_INNER), lambda i, c: (i, c, 0)),
        out_shape=jax.ShapeDtypeStruct((b, lp, SSM_INNER), BF16),
        scratch_shapes=[pltpu.VMEM((SSM_GROUPS, SSM_STATE, SSM_GROUP_W), F32)],
        compiler_params=_cparams("parallel", "arbitrary"),
        name="ssd",
    )(zxbc, zxbc, zxbc, dt, dt_bias, a_log, d_skip, gate_norm)


def _out_proj_kernel(h_ref, y_ref, w_ref, g_ref, wr_ref, br_ref, ho_ref, lg_ref):
    hn = h_ref[...] + _dot(y_ref[...], w_ref[...])
    ho_ref[...] = hn
    u = _rms(hn, g_ref[...])
    u_hi = u.astype(BF16)
    u_lo = (u - u_hi.astype(F32)).astype(BF16)
    p = _dot(u_hi, wr_ref[...])
    logits = p[:, :LANES] + (p[:, LANES:] + _dot(u_lo, wr_ref[:, :LANES])) + br_ref[...]
    lg_ref[...] = logits.T[:ROUTER_ROWS]


def _out_proj(h, y, w, gain, wr_t, br):
    t, d = h.shape
    tm = TOKEN_TILE
    kin = y.shape[1]
    return pl.pallas_call(
        _out_proj_kernel,
        grid=(t // tm,),
        in_specs=[
            pl.BlockSpec((tm, d), lambda i: (i, 0)),
            pl.BlockSpec((tm, kin), lambda i: (i, 0)),
            pl.BlockSpec((kin, d), lambda i: (0, 0)),
            pl.BlockSpec((1, d), lambda i: (0, 0)),
            pl.BlockSpec((d, 2 * LANES), lambda i: (0, 0)),
            pl.BlockSpec((1, LANES), lambda i: (0, 0)),
        ],
        out_specs=[
            pl.BlockSpec((tm, d), lambda i: (i, 0)),
            pl.BlockSpec((ROUTER_ROWS, tm), lambda i: (0, i)),
        ],
        out_shape=[jax.ShapeDtypeStruct((t, d), F32), jax.ShapeDtypeStruct((ROUTER_ROWS, t), F32)],
        compiler_params=_cparams("parallel"),
        name="out_proj_router",
    )(h, y, w, gain, wr_t, br)


def _first_argmax(v, n):
    ridx = lax.broadcasted_iota(I32, v.shape, 0).astype(F32)
    vmax = jnp.max(v, axis=0, keepdims=True)
    idx = jnp.min(jnp.where(v == vmax, ridx, float(n)), axis=0, keepdims=True)
    return vmax, idx.astype(I32)


def _route_kernel(lg_ref, bkt_ref, gate_ref, rank_ref, cnt_ref, carry_ref):
    @pl.when(pl.program_id(0) == 0)
    def _():
        carry_ref[...] = jnp.zeros_like(carry_ref)

    logits = lg_ref[...]
    tm = logits.shape[1]
    gl = logits[0:MOE_GROUPS]
    gmax, gsel = _first_argmax(gl, MOE_GROUPS)
    p_group = 1.0 / jnp.sum(jnp.exp(gl - gmax), axis=0, keepdims=True)
    el = logits[MOE_GROUPS:MOE_GROUPS + MOE_EPG]
    for gg in range(1, MOE_GROUPS):
        el = jnp.where(gsel == gg, logits[MOE_GROUPS + gg * MOE_EPG:MOE_GROUPS + (gg + 1) * MOE_EPG], el)
    ex = jnp.exp(el - jnp.max(el, axis=0, keepdims=True))
    p = ex / jnp.sum(ex, axis=0, keepdims=True)
    p1, i1 = _first_argmax(p, MOE_EPG)
    ridx8 = lax.broadcasted_iota(I32, p.shape, 0)
    p2, i2 = _first_argmax(jnp.where(ridx8 == i1, -1.0, p), MOE_EPG)
    denom = p1 + p2
    gate1 = p_group * p1 / denom
    gate2 = p_group * p2 / denom
    first_low = i1 < i2
    gate_ref[0:1, :] = jnp.where(first_low, gate1, gate2)
    gate_ref[1:2, :] = jnp.where(first_low, gate2, gate1)
    lo = jnp.minimum(i1, i2).astype(F32)
    hi = jnp.maximum(i1, i2).astype(F32)
    pair = lo * (2.0 * MOE_EPG - 1.0 - lo) * 0.5 + (hi - lo - 1.0)
    bucket = gsel * MOE_PAIRS + pair.astype(I32)
    bkt_ref[...] = bucket

    ridx = lax.broadcasted_iota(I32, (MOE_BUCKET_ROWS, tm), 0)
    onehot = jnp.where(ridx == bucket, 1.0, 0.0)
    before = jnp.where(lax.broadcasted_iota(I32, (tm, tm), 0) < lax.broadcasted_iota(I32, (tm, tm), 1), 1.0, 0.0)
    prefix = _dot(onehot.astype(BF16), before.astype(BF16))
    carry = carry_ref[...]
    rank_ref[...] = jnp.sum(onehot * (carry + prefix), axis=0, keepdims=True).astype(I32)
    carry = carry + jnp.sum(onehot, axis=1, keepdims=True)
    carry_ref[...] = carry
    cnt_ref[...] = jnp.broadcast_to(carry, cnt_ref.shape).astype(I32)


def _route(logits_t):
    t = logits_t.shape[1]
    tm = TOKEN_TILE
    rows = lambda r: pl.BlockSpec((r, tm), lambda i: (0, i))
    return pl.pallas_call(
        _route_kernel,
        grid=(t // tm,),
        in_specs=[rows(ROUTER_ROWS)],
        out_specs=[rows(1), rows(2), rows(1), pl.BlockSpec((MOE_BUCKET_ROWS, LANES), lambda i: (0, 0))],
        out_shape=[jax.ShapeDtypeStruct((1, t), I32), jax.ShapeDtypeStruct((2, t), F32),
                   jax.ShapeDtypeStruct((1, t), I32), jax.ShapeDtypeStruct((MOE_BUCKET_ROWS, LANES), I32)],
        scratch_shapes=[pltpu.VMEM((MOE_BUCKET_ROWS, 1), F32)],
        compiler_params=_cparams("arbitrary"),
        name="route_rank",
    )(logits_t)


def _dispatch_kernel(seg_ref, dest_ref, gate_ref, h_ref, g_ref, xs_ref, u_ref, zero_ref, sems):
    i = pl.program_id(0)
    n = pl.num_programs(0)
    slot = lax.rem(i, 2)
    tm, d = h_ref.shape

    def wait_rows(s):
        pltpu.make_async_copy(u_ref.at[s], xs_ref.at[pl.ds(0, tm)], sems.at[s]).wait()

    @pl.when(i == 0)
    def _():
        zero_ref[...] = jnp.zeros_like(zero_ref)

        def fill_block(start):
            fill = pltpu.make_async_copy(
                zero_ref, xs_ref.at[pl.ds(pl.multiple_of(start, MOE_BLOCK), MOE_BLOCK)], sems.at[0])
            fill.start()
            fill.wait()

        def fill_segment(e, carry):
            @pl.when(seg_ref[MOE_BUCKET_ROWS + e] > 0)
            def _():
                fill_block(seg_ref[e] - MOE_BLOCK)
            return carry

        lax.fori_loop(0, MOE_BUCKETS, fill_segment, 0)

        def fill_tail(blk, carry):
            fill_block(blk * MOE_BLOCK)
            return carry

        lax.fori_loop(seg_ref[MOE_BUCKET_ROWS - 1] // MOE_BLOCK, xs_ref.shape[0] // MOE_BLOCK, fill_tail, 0)

    @pl.when(i >= 2)
    def _():
        wait_rows(slot)

    u = _rms(h_ref[...], g_ref[...])
    half = d // 2
    as_bits = lambda v: lax.bitcast_convert_type(v.astype(BF16).astype(F32), jnp.uint32)
    words = (as_bits(u[:, half:]) & jnp.uint32(0xFFFF0000)) | (as_bits(u[:, :half]) >> 16)
    gates = jnp.concatenate([gate_ref[...], jnp.zeros((LANES - gate_ref.shape[0], tm), F32)], axis=0).T
    packed = jnp.concatenate([words, lax.bitcast_convert_type(gates, jnp.uint32),
                              jnp.zeros((tm, half - LANES), jnp.uint32)], axis=1)
    u_ref[slot] = packed.reshape(u_ref.shape[1:])

    def issue(q, carry):
        for k in range(ROW_DMA_UNROLL):
            t = q * ROW_DMA_UNROLL + k
            pltpu.make_async_copy(
                u_ref.at[slot, t], xs_ref.at[dest_ref[0, 0, t]], sems.at[slot]).start(priority=k % 2)
        return carry

    lax.fori_loop(0, tm // ROW_DMA_UNROLL, issue, 0)

    @pl.when(i == n - 1)
    def _():
        wait_rows(slot)

        @pl.when(n >= 2)
        def _():
            wait_rows(1 - slot)


def _dispatch(seg, dest, gate, h, gain, rows):
    t, d = h.shape
    tm = dest.shape[2]
    sub = d // LANES
    return pl.pallas_call(
        _dispatch_kernel,
        grid_spec=pltpu.PrefetchScalarGridSpec(
            num_scalar_prefetch=1,
            grid=(t // tm,),
            in_specs=[
                pl.BlockSpec((1, 1, tm), lambda i, seg: (i, 0, 0), memory_space=pltpu.SMEM),
                pl.BlockSpec((SUBLANES, tm), lambda i, seg: (0, i)),
                pl.BlockSpec((tm, d), lambda i, seg: (i, 0)),
                pl.BlockSpec((1, d), lambda i, seg: (0, 0)),
            ],
            out_specs=pl.BlockSpec(memory_space=pl.ANY),
            scratch_shapes=[pltpu.VMEM((2, tm, sub, LANES), jnp.uint32), pltpu.VMEM((MOE_BLOCK, sub, LANES), jnp.uint32),
                            pltpu.SemaphoreType.DMA((2,))],
        ),
        out_shape=jax.ShapeDtypeStruct((rows, sub, LANES), jnp.uint32),
        compiler_params=_cparams("arbitrary"),
        name="moe_dispatch",
    )(seg, dest, gate, h, gain)


def _expert_kernel(ea_ref, eb_ref, nu_ref, xs_ref, w1a_ref, w3a_ref, w2a_ref, w1b_ref, w3b_ref, w2b_ref, ys_ref):
    used = pl.program_id(0) < nu_ref[0]

    @pl.when(used)
    def _():
        rows, sub, lanes = xs_ref.shape
        d = sub * lanes
        packed = xs_ref[...].reshape(rows, d)
        words = packed[:, :d // 2]
        lo = lax.bitcast_convert_type(words << 16, F32).astype(BF16)
        hi = lax.bitcast_convert_type(words & jnp.uint32(0xFFFF0000), F32).astype(BF16)
        x = jnp.concatenate([lo, hi], axis=1)
        gates = lax.bitcast_convert_type(packed[:, d // 2:d // 2 + LANES], F32)

        def ffn(w1_ref, w3_ref, w2_ref):
            hid = _silu(_dot(x, w1_ref[0, 0].astype(BF16))) * _dot(x, w3_ref[0, 0].astype(BF16))
            return _dot(hid.astype(BF16), w2_ref[0, 0].astype(BF16))

        y = ffn(w1a_ref, w3a_ref, w2a_ref) * gates[:, 0:1] + ffn(w1b_ref, w3b_ref, w2b_ref) * gates[:, 1:2]
        ys_ref[...] = y.reshape(rows, sub, lanes)

    @pl.when(jnp.logical_not(used))
    def _():
        ys_ref[...] = jnp.zeros_like(ys_ref)


def _experts(block_ea, block_eb, n_used, xs, w1, w3, w2, layer):
    rows, sub, lanes = xs.shape
    d = sub * lanes
    nb = rows // MOE_BLOCK
    w_in = lambda tbl: pl.BlockSpec((1, 1, d, MOE_FF), lambda i, ea, eb, nu: (layer, (ea, eb)[tbl][i], 0, 0))
    w_out = lambda tbl: pl.BlockSpec((1, 1, MOE_FF, d), lambda i, ea, eb, nu: (layer, (ea, eb)[tbl][i], 0, 0))
    return pl.pallas_call(
        _expert_kernel,
        grid_spec=pltpu.PrefetchScalarGridSpec(
            num_scalar_prefetch=3,
            grid=(nb,),
            in_specs=[
                pl.BlockSpec((MOE_BLOCK, sub, lanes), lambda i, ea, eb, nu: (jnp.minimum(i, nu[0] - 1), 0, 0)),
                w_in(0), w_in(0), w_out(0), w_in(1), w_in(1), w_out(1),
            ],
            out_specs=pl.BlockSpec((MOE_BLOCK, sub, lanes), lambda i, ea, eb, nu: (i, 0, 0)),
        ),
        out_shape=jax.ShapeDtypeStruct((rows, sub, lanes), F32),
        compiler_params=_cparams("arbitrary"),
        name="moe_experts",
    )(block_ea, block_eb, n_used, xs, w1, w3, w2, w1, w3, w2)


def _issue_row_gather(idx_ref, src_ref, dst_ref, sem, rows):
    def issue(q, carry):
        for k in range(ROW_DMA_UNROLL):
            r = q * ROW_DMA_UNROLL + k
            pltpu.make_async_copy(src_ref.at[idx_ref[0, 0, r]], dst_ref.at[r], sem).start(priority=k % 2)
        return carry

    lax.fori_loop(0, rows // ROW_DMA_UNROLL, issue, 0)


def _combine_kernel(dest_ref, next_ref, h_ref, g_ref, ys_ref, ho_ref, xn_ref, ybuf, sems):
    i = pl.program_id(0)
    slot = lax.rem(i, 2)
    tm, d = h_ref.shape

    @pl.when(i == 0)
    def _():
        _issue_row_gather(dest_ref, ys_ref, ybuf.at[slot], sems.at[slot], tm)

    @pl.when(i + 1 < pl.num_programs(0))
    def _():
        _issue_row_gather(next_ref, ys_ref, ybuf.at[1 - slot], sems.at[1 - slot], tm)

    pltpu.make_async_copy(ys_ref.at[pl.ds(0, tm)], ybuf.at[slot], sems.at[slot]).wait()
    h = h_ref[...] + ybuf[slot].reshape(tm, d)
    ho_ref[...] = h
    xn_ref[...] = _rms(h, g_ref[...]).astype(BF16)


def _combine_norm(dest, ys, h, gain):
    n, _, tm = dest.shape
    t, d = h.shape
    smem = lambda fn: pl.BlockSpec((1, 1, tm), fn, memory_space=pltpu.SMEM)
    return pl.pallas_call(
        _combine_kernel,
        grid=(n,),
        in_specs=[
            smem(lambda i: (i, 0, 0)),
            smem(lambda i: (jnp.minimum(i + 1, n - 1), 0, 0)),
            pl.BlockSpec((tm, d), lambda i: (i, 0)),
            pl.BlockSpec((1, d), lambda i: (0, 0)),
            pl.BlockSpec(memory_space=pl.ANY),
        ],
        out_specs=[pl.BlockSpec((tm, d), lambda i: (i, 0)), pl.BlockSpec((tm, d), lambda i: (i, 0))],
        out_shape=[jax.ShapeDtypeStruct((t, d), F32), jax.ShapeDtypeStruct((t, d), BF16)],
        scratch_shapes=[pltpu.VMEM((2, tm, d // LANES, LANES), F32), pltpu.SemaphoreType.DMA((2,))],
        compiler_params=_cparams("arbitrary"),
        name="moe_combine",
    )(dest, dest, h, gain, ys)


def _final_kernel(*refs, per):
    cur, nxt, hs = refs[:per], refs[per:2 * per], refs[2 * per:3 * per]
    g_ref, ys_ref, o_ref, ybuf, sems = refs[3 * per:]
    step = pl.program_id(0) * pl.num_programs(1) + pl.program_id(1)
    last = pl.num_programs(0) * pl.num_programs(1) - 1
    slot = lax.rem(step, 2)
    d = o_ref.shape[2]

    def gather(idx_refs, s):
        for p, idx_ref in enumerate(idx_refs):
            _issue_row_gather(idx_ref, ys_ref, ybuf.at[s, pl.ds(p * CHUNK, CHUNK)], sems.at[s], CHUNK)

    @pl.when(step == 0)
    def _():
        gather(cur, slot)

    @pl.when(step < last)
    def _():
        gather(nxt, 1 - slot)

    pltpu.make_async_copy(ys_ref.at[pl.ds(0, per * CHUNK)], ybuf.at[slot], sems.at[slot]).wait()
    h = jnp.concatenate([h_ref[0] for h_ref in hs], axis=0)
    o_ref[0] = _rms(h + ybuf[slot].reshape(per * CHUNK, d), g_ref[...])


def _final_norm(h, dest, ys, gain, seq):
    b, lp, d = h.shape
    nc = lp // CHUNK
    ns = seq // CHUNK
    per = 4 if ns % 4 == 0 else 2 if ns % 2 == 0 else 1
    steps = ns // per

    def chunk_of(step, p):
        return (step // steps) * nc + lax.rem(step, steps) * per + p + 1

    smem = lambda fn: pl.BlockSpec((1, 1, CHUNK), fn, memory_space=pltpu.SMEM)
    cur = [smem(lambda i, c, p=p: (chunk_of(i * steps + c, p), 0, 0)) for p in range(per)]
    nxt = [smem(lambda i, c, p=p: (chunk_of(jnp.minimum(i * steps + c + 1, b * steps - 1), p), 0, 0))
           for p in range(per)]
    hs = [pl.BlockSpec((1, CHUNK, d), lambda i, c, p=p: (i, c * per + p + 1, 0)) for p in range(per)]
    return pl.pallas_call(
        functools.partial(_final_kernel, per=per),
        grid=(b, steps),
        in_specs=cur + nxt + hs + [pl.BlockSpec((1, d), lambda i, c: (0, 0)), pl.BlockSpec(memory_space=pl.ANY)],
        out_specs=pl.BlockSpec((1, per * CHUNK, d), lambda i, c: (i, c, 0)),
        out_shape=jax.ShapeDtypeStruct((b, seq, d), F32),
        scratch_shapes=[pltpu.VMEM((2, per * CHUNK, d // LANES, LANES), F32), pltpu.SemaphoreType.DMA((2,))],
        compiler_params=_cparams("arbitrary", "arbitrary"),
        name="final_norm",
    )(*([dest] * (2 * per)), *([h] * per), gain, ys)


def _row_tiles(dest):
    t = dest.shape[0]
    tile = ROW_COPY_TILE if t % ROW_COPY_TILE == 0 else TOKEN_TILE
    return dest.reshape(t // tile, 1, tile)


def _bucket_experts():
    lo, hi = [], []
    for g in range(MOE_GROUPS):
        for a in range(MOE_EPG):
            for c in range(a + 1, MOE_EPG):
                lo.append(g * MOE_EPG + a)
                hi.append(g * MOE_EPG + c)
    fill = MOE_BUCKET_ROWS - len(lo)
    return jnp.array(lo + [lo[-1]] * fill, I32), jnp.array(hi + [hi[-1]] * fill, I32)


def _moe_layer(h, y_mix, w_out, norm_gain, wg, bg, we, be, w1, w3, w2, layer):
    t, d = h.shape
    n_logits = MOE_GROUPS + MOE_EXPERTS
    wr = jnp.concatenate([wg, we.reshape(d, MOE_EXPERTS), jnp.zeros((d, LANES - n_logits), F32)], axis=1)
    br = jnp.concatenate([bg, be.reshape(-1), jnp.zeros((LANES - n_logits,), F32)])
    wr_hi = wr.astype(BF16)
    wr_split = jnp.concatenate([wr_hi, (wr - wr_hi.astype(F32)).astype(BF16)], axis=1)
    h, logits_t = _out_proj(h, y_mix, w_out.astype(BF16), norm_gain[None, :], wr_split, br[None, :])
    bucket, gate, rank, counts = _route(logits_t)

    counts = counts[:, 0]
    padded = (counts + MOE_BLOCK - 1) // MOE_BLOCK * MOE_BLOCK
    pend = jnp.cumsum(padded)
    pstart = pend - padded
    ids = jnp.arange(MOE_BUCKET_ROWS, dtype=I32)
    dest = rank[0] + jnp.sum(jnp.where(bucket == ids[:, None], pstart[:, None], 0), axis=0)
    n_blocks = -(-t // MOE_BLOCK) + MOE_BUCKETS
    block_start = jnp.arange(n_blocks, dtype=I32) * MOE_BLOCK
    block_bucket = jnp.minimum(jnp.sum((pend[None, :] <= block_start[:, None]).astype(I32), axis=1),
                               MOE_BUCKET_ROWS - 1)
    n_used = (pend[-1:] // MOE_BLOCK).astype(I32)
    lo, hi = _bucket_experts()
    gate8 = jnp.concatenate([gate, jnp.zeros((SUBLANES - gate.shape[0], t), F32)], axis=0)

    xs = _dispatch(jnp.concatenate([pend, padded]).astype(I32), _row_tiles(dest), gate8, h, norm_gain[None, :],
                   n_blocks * MOE_BLOCK)
    ys = _experts(lo[block_bucket], hi[block_bucket], n_used, xs, w1, w3, w2, layer)
    return h, dest, ys


def kernel(x, meta, norm_mix, norm_ffn, norm_final, ret_w_in, ret_w_out, ret_norm, ssm_w_in, ssm_conv_w,
           ssm_conv_b, ssm_dt_bias, ssm_a_log, ssm_d, ssm_norm, ssm_w_out, moe_wg, moe_bg, moe_we, moe_be,
           moe_w1, moe_w3, moe_w2):
    b, seq, d = x.shape
    lp = CHUNK + seq
    t = b * lp
    depth = norm_mix.shape[0]

    half = RET_QK_DIM // 2
    inv_freq = 1.0 / (RET_ROPE_BASE ** jnp.linspace(0.0, 1.0, half, dtype=F32))
    ang = (jnp.arange(lp, dtype=F32) - PAD)[:, None] * inv_freq[None, :]
    cos, sin = jnp.cos(ang), jnp.sin(ang)
    perm = jnp.concatenate([jnp.arange(0, RET_QK_DIM, 2), jnp.arange(1, RET_QK_DIM, 2)])
    qk_perm = (jnp.arange(2 * RET_HEADS)[:, None] * RET_QK_DIM + perm[None, :]).reshape(-1)

    h = None
    moe = None
    for i in range(depth):
        j = i // 2
        if moe is not None:
            h, xn = _combine_norm(moe[0].reshape(t // TOKEN_TILE, 1, TOKEN_TILE), moe[1], h, norm_mix[i][None, :])
        if i % 2 == 0:
            w_in = jnp.concatenate([ret_w_in[j][:, qk_perm], ret_w_in[j][:, 2 * RET_QK:]], axis=1).astype(BF16)
            if i == 0:
                h, qkvg = _ret_in_first(x, meta, norm_mix[i][None, :], cos, sin, w_in)
                h = h.reshape(t, d)
            else:
                qkvg = _ret_in(h.reshape(b, lp, d), norm_mix[i][None, :], cos, sin, w_in)
            y = _retention(qkvg, ret_norm[j][None, :])
            w_out = ret_w_out[j]
        else:
            w_in = jnp.concatenate([ssm_w_in[j], jnp.zeros((d, LANES - SSM_HEADS), F32)], axis=1).astype(BF16)
            zxbc, dt = _ssm_in(xn, w_in, ssm_conv_w[j], ssm_conv_b[j][None, :], lp)
            lane_pad = lambda v: jnp.concatenate([v, jnp.zeros((LANES - SSM_HEADS,), F32)])[None, :]
            y = _ssd(zxbc.reshape(b, lp, SSM_MAIN), dt.reshape(b, lp, LANES), lane_pad(ssm_dt_bias[j]),
                     lane_pad(ssm_a_log[j]), jnp.repeat(ssm_d[j], SSM_HEAD_DIM)[None, :], ssm_norm[j][None, :])
            w_out = ssm_w_out[j]
        h, *moe = _moe_layer(h, y.reshape(t, -1), w_out, norm_ffn[i], moe_wg[i], moe_bg[i], moe_we[i],
                             moe_be[i], moe_w1, moe_w3, moe_w2, i)
    return _final_norm(h.reshape(b, lp, d), moe[0].reshape(t // CHUNK, 1, CHUNK), moe[1], norm_final[None, :], seq)
```
